```python
import jax
import jax.numpy as jnp
from jax import lax
import numpy as np

D_MODEL = 1024
BATCH = 4
SEQ = 8192
DEPTH = 4

GRID_W = 64
CTX_LEN = 256

MLA_HEADS = 4
MLA_NOPE = 128
MLA_ROPE = 64
MLA_V = 128
MLA_Q_LORA = 384
MLA_KV_LORA = 256
MLA_QK = MLA_NOPE + MLA_ROPE
MLA_SCALE = MLA_QK ** -0.5
ROPE_BASE = 10000.0
ROPE_AXIS_FREQS = MLA_ROPE // 4
Q_BLOCK = 128

SG_GROUPS = 4
SG_WIDTH = 256
SG_CHUNK = 128

RET_HEADS = 4
RET_QK = 32
RET_V = 64
RET_CHUNK = 128
RET_ROPE_BASE = 10000.0

N_EXPERTS = 32
TOP_K = 4
D_EXPERT = 1024
SWIGLU_LIMIT = 7.0
SWIGLU_ALPHA = 1.702
MOE_BLOCK = 256

MLA_OUT = MLA_HEADS * MLA_V
RET_OUT = RET_HEADS * RET_V
MIX_WIDTH = MLA_OUT + SG_WIDTH + RET_OUT
IN_SIZES = (MLA_Q_LORA, MLA_KV_LORA, MLA_ROPE, SG_WIDTH, SG_WIDTH,
            RET_HEADS * RET_QK, RET_HEADS * RET_QK, RET_OUT, RET_OUT)
IN_WIDTH = sum(IN_SIZES)
N_MOD = 6
LN_EPS = 1e-5
RMS_EPS = 1e-6
DEEPNORM_ALPHA = (2 * DEPTH) ** 0.25
DEEPNORM_BETA = (8 * DEPTH) ** -0.25

kernel_name = "hybrid_mla_sgmlp_retention_moe_dit"

F32 = jnp.float32


def _layer_norm(x, g=None, b=None):
    xf = x.astype(F32)
    xc = xf - jnp.mean(xf, axis=-1, keepdims=True)
    y = xc * lax.rsqrt(jnp.mean(xc * xc, axis=-1, keepdims=True) + LN_EPS)
    if g is not None:
        y = y * g.astype(F32) + b.astype(F32)
    return y.astype(x.dtype)


def _rms_norm(x, g):
    xf = x.astype(F32)
    y = xf * lax.rsqrt(jnp.mean(xf * xf, axis=-1, keepdims=True) + RMS_EPS) * g.astype(F32)
    return y.astype(x.dtype)


def _split_cols(a, sizes):
    out, off = [], 0
    for s in sizes:
        out.append(a[..., off:off + s])
        off += s
    return out


def _axial_tables(seq_len):
    rows = seq_len // GRID_W
    row = jnp.broadcast_to(jnp.arange(rows, dtype=F32)[:, None], (rows, GRID_W)).reshape(-1)
    col = jnp.broadcast_to(jnp.arange(GRID_W, dtype=F32)[None, :], (rows, GRID_W)).reshape(-1)
    inv = ROPE_BASE ** (-jnp.arange(ROPE_AXIS_FREQS, dtype=F32) / ROPE_AXIS_FREQS)
    ang = jnp.stack([row[:, None] * inv, col[:, None] * inv], axis=1)
    return jnp.cos(ang), jnp.sin(ang)


def _axial_rope(x, cos, sin):
    xr = x.reshape(x.shape[:-1] + (2, 2, ROPE_AXIS_FREQS)).astype(F32)
    x1, x2 = xr[..., 0, :], xr[..., 1, :]
    out = jnp.stack([x1 * cos - x2 * sin, x2 * cos + x1 * sin], axis=-2)
    return out.reshape(x.shape).astype(x.dtype)


def _ret_rope(x, pos):
    half = RET_QK // 2
    inv = 1.0 / (RET_ROPE_BASE ** jnp.linspace(0.0, 1.0, half, dtype=F32))
    ang = pos[:, None] * inv
    c, s = jnp.cos(ang)[:, None, :], jnp.sin(ang)[:, None, :]
    xf = x.astype(F32)
    x1, x2 = xf[..., :half], xf[..., half:]
    return jnp.concatenate([x1 * c - x2 * s, x2 * c + x1 * s], axis=-1).astype(x.dtype)


def _attend(q, k, v):
    s = jnp.einsum('bqhd,bkhd->bhqk', q, k).astype(F32) * MLA_SCALE
    p = jax.nn.softmax(s, axis=-1).astype(v.dtype)
    return jnp.einsum('bhqk,bkhd->bqhd', p, v)


def _mla_qkv(cq, ckv, krope, q_norm_g, kv_norm_g, w_uq, w_ukv, cos, sin, with_q):
    b, t = ckv.shape[:2]
    kv = (_rms_norm(ckv, kv_norm_g) @ w_ukv).reshape(b, t, MLA_HEADS, MLA_NOPE + MLA_V)
    k_nope, v = kv[..., :MLA_NOPE], kv[..., MLA_NOPE:]
    if cos is not None:
        krope = _axial_rope(krope, cos, sin)
    k = jnp.concatenate(
        [k_nope, jnp.broadcast_to(krope[:, :, None, :], (b, t, MLA_HEADS, MLA_ROPE))], axis=-1)
    q = None
    if with_q:
        q = (_rms_norm(cq, q_norm_g) @ w_uq).reshape(b, t, MLA_HEADS, MLA_QK)
        if cos is not None:
            q = jnp.concatenate(
                [q[..., :MLA_NOPE], _axial_rope(q[..., MLA_NOPE:], cos[:, None], sin[:, None])], axis=-1)
    return q, k, v


def mla_mixer(lat3, ctx3, q_norm_g, kv_norm_g, w_uq, w_ukv, cos, sin, with_ctx_out):
    q, k, v = _mla_qkv(*lat3, q_norm_g, kv_norm_g, w_uq, w_ukv, cos, sin, True)
    qc, kc, vc = _mla_qkv(*ctx3, q_norm_g, kv_norm_g, w_uq, w_ukv, None, None, with_ctx_out)
    k_all = jnp.concatenate([kc, k], axis=1)
    v_all = jnp.concatenate([vc, v], axis=1)
    b, s = q.shape[:2]
    nq = s // Q_BLOCK
    qb = q.reshape(b, nq, Q_BLOCK, MLA_HEADS, MLA_QK).swapaxes(0, 1)
    ob = lax.map(lambda blk: _attend(blk, k_all, v_all), qb)
    out = ob.swapaxes(0, 1).reshape(b, s, MLA_OUT)
    out_c = _attend(qc, kc, vc).reshape(b, -1, MLA_OUT) if with_ctx_out else None
    return out, out_c


def spatial_gate(u, v, norm_g, norm_b, w_s, b_s):
    b, t, _ = u.shape
    u = jax.nn.gelu(u, approximate=False)
    v = _layer_norm(jax.nn.gelu(v, approximate=False), norm_g, norm_b)
    vr = v.reshape(b, t // SG_CHUNK, SG_CHUNK, SG_GROUPS, SG_WIDTH // SG_GROUPS)
    mixed = jnp.einsum('gpq,bnqgc->bnpgc', w_s, vr) + b_s.T[:, :, None]
    return u * mixed.reshape(b, t, SG_WIDTH)


def _retention_chunks(q, k, v, log_gamma, state0, inclusive, emit):
    b, h, t, _ = q.shape
    dv = v.shape[-1]
    n = t // RET_CHUNK

    def to_chunks(a):
        return a.reshape(b, h, n, RET_CHUNK, a.shape[-1]).transpose(2, 0, 1, 3, 4)

    idx = jnp.arange(RET_CHUNK, dtype=F32)
    diff = idx[:, None] - idx[None, :]
    mask = (diff >= 0) if inclusive else (diff > 0)
    inner_decay = jnp.where(mask, jnp.exp(log_gamma[:, None, None] * jnp.where(mask, diff, 0.0)), 0.0)
    q_decay = jnp.exp(log_gamma[:, None] * (idx + 1.0))[None, :, :, None]
    k_decay = jnp.exp(log_gamma[:, None] * (RET_CHUNK - 1.0 - idx))[None, :, :, None]
    chunk_decay = jnp.exp(log_gamma * RET_CHUNK)[None, :, None, None]

    def step(state, blk):
        qb, kb, vb = (a.astype(F32) for a in blk)
        new_state = state * chunk_decay + jnp.einsum('bhjd,bhjv->bhdv', kb * k_decay, vb)
        if not emit:
            return new_state, None
        inner = jnp.einsum('bhij,bhjv->bhiv', jnp.einsum('bhid,bhjd->bhij', qb, kb) * inner_decay, vb)
        cross = jnp.einsum('bhid,bhdv->bhiv', qb, state) * q_decay
        return new_state, inner + cross

    state, out = lax.scan(step, state0, (to_chunks(q), to_chunks(k), to_chunks(v)))
    if emit:
        out = out.transpose(1, 2, 0, 3, 4).reshape(b, h, t, dv)
    return state, out


def _ret_heads(q, k, v, pos):
    b, t = q.shape[:2]
    q = _ret_rope(q.reshape(b, t, RET_HEADS, RET_QK), pos) * (RET_QK ** -0.5)
    k = _ret_rope(k.reshape(b, t, RET_HEADS, RET_QK), pos)
    v = v.reshape(b, t, RET_HEADS, RET_V)
    return q.transpose(0, 2, 1, 3), k.transpose(0, 2, 1, 3), v.transpose(0, 2, 1, 3)


def _ret_out(o, g):
    b, h, t, dv = o.shape
    o = _layer_norm(o.transpose(0, 2, 1, 3))
    return (o.reshape(b, t, h * dv) * jax.nn.silu(g.astype(F32))).astype(g.dtype)


def retention_mixer(lat4, ctx4, logit_fwd, logit_bwd, pos_lat, pos_ctx, with_ctx_out):
    q, k, v, g = lat4
    qc, kc, vc, gc = ctx4
    ql, kl, vl = _ret_heads(q, k, v, pos_lat)
    qcc, kcc, vcc = _ret_heads(qc, kc, vc, pos_ctx)
    lg_f = jax.nn.log_sigmoid(logit_fwd.astype(F32))
    lg_b = jax.nn.log_sigmoid(logit_bwd.astype(F32))
    zero = jnp.zeros((ql.shape[0], RET_HEADS, RET_QK, RET_V), F32)

    def flip(a):
        return a[:, :, ::-1]

    st_f, oc_f = _retention_chunks(qcc, kcc, vcc, lg_f, zero, True, with_ctx_out)
    st_b, oc_b = _retention_chunks(flip(qcc), flip(kcc), flip(vcc), lg_b, zero, False, with_ctx_out)
    _, ol_f = _retention_chunks(ql, kl, vl, lg_f, st_f, True, True)
    _, ol_b = _retention_chunks(flip(ql), flip(kl), flip(vl), lg_b, st_b, False, True)
    out = _ret_out(ol_f + flip(ol_b), g)
    out_c = _ret_out(oc_f + flip(oc_b), gc) if with_ctx_out else None
    return out, out_c


def _clamped_swiglu(h):
    x_glu = jnp.minimum(h[..., ::2], SWIGLU_LIMIT)
    x_lin = jnp.clip(h[..., 1::2], -SWIGLU_LIMIT, SWIGLU_LIMIT)
    return x_glu * jax.nn.sigmoid(SWIGLU_ALPHA * x_glu) * (x_lin + 1.0)


def moe_ffn(tokens, router_w, router_b, w_gu, b_gu, w_down, b_down):
    n, d = tokens.shape
    logits = (tokens @ router_w + router_b).astype(F32)
    top_val, top_idx = lax.top_k(logits, TOP_K)
    gates = jax.nn.softmax(top_val, axis=-1)
    n_assign = n * TOP_K
    flat_e = top_idx.reshape(-1)
    flat_tok = jnp.repeat(jnp.arange(n, dtype=jnp.int32), TOP_K)
    order = jnp.argsort(flat_e)
    sorted_e = flat_e[order]
    counts = jnp.bincount(flat_e, length=N_EXPERTS)
    starts = jnp.cumsum(counts) - counts
    padded = (counts + MOE_BLOCK - 1) // MOE_BLOCK * MOE_BLOCK
    pad_end = jnp.cumsum(padded)
    dest = (pad_end - padded)[sorted_e] + jnp.arange(n_assign, dtype=jnp.int32) - starts[sorted_e]
    n_blocks = -(-(n_assign + N_EXPERTS * (MOE_BLOCK - 1)) // MOE_BLOCK)
    cap = n_blocks * MOE_BLOCK
    tok_buf = jnp.full((cap,), n, jnp.int32).at[dest].set(flat_tok[order])
    gate_buf = jnp.zeros((cap,), F32).at[dest].set(gates.reshape(-1)[order])
    block_e = jnp.minimum(
        jnp.searchsorted(pad_end, jnp.arange(n_blocks, dtype=jnp.int32) * MOE_BLOCK, side='right'),
        N_EXPERTS - 1)
    x_pad = jnp.concatenate([tokens, jnp.zeros((1, d), tokens.dtype)], axis=0)

    def expert_block(args):
        idx, e = args
        h = x_pad[idx] @ w_gu[e] + b_gu[e]
        return _clamped_swiglu(h) @ w_down[e] + b_down[e]

    y_blocks = lax.map(expert_block, (tok_buf.reshape(n_blocks, MOE_BLOCK), block_e))
    y = jnp.zeros((n + 1, d), F32).at[tok_buf].add(
        y_blocks.reshape(cap, d).astype(F32) * gate_buf[:, None])
    return y[:n].astype(tokens.dtype)


def hybrid_layer(x, ctx, silu_c, silu_cc, p, rope_cos, rope_sin, pos_lat, pos_ctx, with_ctx_out):
    b, s, d = x.shape
    mod = (silu_c @ p['ada_w'] + p['ada_b']).reshape(b, N_MOD, 1, d)
    mod_c = (silu_cc @ p['ada_w'] + p['ada_b']).reshape(N_MOD, 1, 1, d)
    sh1, sc1, g1, sh2, sc2, g2 = (mod[:, i] for i in range(N_MOD))
    csh1, csc1, cg1, csh2, csc2, cg2 = (mod_c[i] for i in range(N_MOD))

    lat = _split_cols((x * (1.0 + sc1) + sh1) @ p['w_in'], IN_SIZES)
    cpr = _split_cols((ctx * (1.0 + csc1) + csh1) @ p['w_in'], IN_SIZES)

    mla_l, mla_c = mla_mixer(lat[0:3], cpr[0:3], p['mla_q_norm_g'], p['mla_kv_norm_g'],
                             p['mla_w_uq'], p['mla_w_ukv'], rope_cos, rope_sin, with_ctx_out)
    sg_l = spatial_gate(lat[3], lat[4], p['sg_norm_g'], p['sg_norm_b'], p['sg_w'], p['sg_b'])
    ret_l, ret_c = retention_mixer(lat[5:9], cpr[5:9], p['ret_decay_fwd'], p['ret_decay_bwd'],
                                   pos_lat, pos_ctx, with_ctx_out)
    y = jnp.concatenate([mla_l, sg_l, ret_l], axis=-1) @ p['w_o']
    x = _layer_norm(DEEPNORM_ALPHA * x + g1 * y, p['ln1_g'], p['ln1_b'])
    h2 = x * (1.0 + sc2) + sh2
    moe_args = (p['router_w'], p['router_b'], p['w_gate_up'], p['b_gate_up'], p['w_down'], p['b_down'])

    if not with_ctx_out:
        f = moe_ffn(h2.reshape(b * s, d), *moe_args).reshape(b, s, d)
        return _layer_norm(DEEPNORM_ALPHA * x + g2 * f, p['ln2_g'], p['ln2_b']), None

    sg_c = spatial_gate(cpr[3], cpr[4], p['sg_norm_g'], p['sg_norm_b'], p['sg_w'], p['sg_b'])
    yc = jnp.concatenate([mla_c, sg_c, ret_c], axis=-1) @ p['w_o']
    ctx = _layer_norm(DEEPNORM_ALPHA * ctx + cg1 * yc, p['ln1_g'], p['ln1_b'])
    h2c = ctx * (1.0 + csc2) + csh2
    lc = ctx.shape[1]
    f = moe_ffn(jnp.concatenate([h2c.reshape(b * lc, d), h2.reshape(b * s, d)], axis=0), *moe_args)
    f_c = f[:b * lc].reshape(b, lc, d)
    f_l = f[b * lc:].reshape(b, s, d)
    x = _layer_norm(DEEPNORM_ALPHA * x + g2 * f_l, p['ln2_g'], p['ln2_b'])
    ctx = _layer_norm(DEEPNORM_ALPHA * ctx + cg2 * f_c, p['ln2_g'], p['ln2_b'])
    return x, ctx


def setup_inputs(seed: int = 0) -> dict:
    key = jax.random.key(seed)
    k = jax.random.split(key, 28)
    L, D = DEPTH, D_MODEL

    def nrm(i, shape, scale):
        return jax.random.normal(k[i], shape, F32) * scale

    gamma0 = 1.0 - 2.0 ** (-5.0 - np.arange(RET_HEADS, dtype=np.float32))
    logit0 = jnp.asarray(np.log(gamma0 / (1.0 - gamma0)), F32)
    return {
        'x': nrm(0, (BATCH, SEQ, D), 1.0),
        'c': nrm(1, (BATCH, D), 1.0),
        'ctx': nrm(2, (BATCH, CTX_LEN, D), 1.0),
        'c_ctx': nrm(3, (D,), 1.0),
        'ada_w': nrm(4, (L, D, N_MOD * D), D ** -0.5),
        'ada_b': nrm(5, (L, N_MOD * D), 0.02),
        'w_in': nrm(6, (L, D, IN_WIDTH), D ** -0.5),
        'mla_q_norm_g': 1.0 + nrm(7, (L, MLA_Q_LORA), 0.02),
        'mla_kv_norm_g': 1.0 + nrm(8, (L, MLA_KV_LORA), 0.02),
        'mla_w_uq': nrm(9, (L, MLA_Q_LORA, MLA_HEADS * MLA_QK), MLA_Q_LORA ** -0.5),
        'mla_w_ukv': nrm(10, (L, MLA_KV_LORA, MLA_HEADS * (MLA_NOPE + MLA_V)), MLA_KV_LORA ** -0.5),
        'sg_norm_g': 1.0 + nrm(11, (L, SG_WIDTH), 0.02),
        'sg_norm_b': nrm(12, (L, SG_WIDTH), 0.02),
        'sg_w': nrm(13, (L, SG_GROUPS, SG_CHUNK, SG_CHUNK), SG_CHUNK ** -0.5),
        'sg_b': 1.0 + nrm(14, (L, SG_GROUPS, SG_CHUNK), 0.02),
        'ret_decay_fwd': logit0 + nrm(15, (L, RET_HEADS), 0.05),
        'ret_decay_bwd': logit0 + nrm(16, (L, RET_HEADS), 0.05),
        'w_o': nrm(17, (L, MIX_WIDTH, D), DEEPNORM_BETA * MIX_WIDTH ** -0.5),
        'ln1_g': 1.0 + nrm(18, (L, D), 0.02),
        'ln1_b': nrm(19, (L, D), 0.02),
        'router_w': nrm(20, (L, D, N_EXPERTS), D ** -0.5),
        'router_b': nrm(21, (L, N_EXPERTS), 0.01),
        'w_gate_up': nrm(22, (L, N_EXPERTS, D, 2 * D_EXPERT), D ** -0.5),
        'b_gate_up': nrm(23, (L, N_EXPERTS, 2 * D_EXPERT), 0.02),
        'w_down': nrm(24, (L, N_EXPERTS, D_EXPERT, D), DEEPNORM_BETA * D_EXPERT ** -0.5),
        'b_down': nrm(25, (L, N_EXPERTS, D), 0.02),
        'ln2_g': 1.0 + nrm(26, (L, D), 0.02),
        'ln2_b': nrm(27, (L, D), 0.02),
    }


def reference(x, c, ctx, c_ctx, ada_w, ada_b, w_in, mla_q_norm_g, mla_kv_norm_g, mla_w_uq,
              mla_w_ukv, sg_norm_g, sg_norm_b, sg_w, sg_b, ret_decay_fwd, ret_decay_bwd, w_o,
              ln1_g, ln1_b, router_w, router_b, w_gate_up, b_gate_up, w_down, b_down, ln2_g, ln2_b):
    s = x.shape[1]
    lc = ctx.shape[1]
    rope_cos, rope_sin = _axial_tables(s)
    pos_ctx = jnp.arange(lc, dtype=F32)
    pos_lat = lc + jnp.arange(s, dtype=F32)
    silu_c = jax.nn.silu(c)
    silu_cc = jax.nn.silu(c_ctx)
    for i in range(DEPTH):
        p = {
            'ada_w': ada_w[i], 'ada_b': ada_b[i], 'w_in': w_in[i],
            'mla_q_norm_g': mla_q_norm_g[i], 'mla_kv_norm_g': mla_kv_norm_g[i],
            'mla_w_uq': mla_w_uq[i], 'mla_w_ukv': mla_w_ukv[i],
            'sg_norm_g': sg_norm_g[i], 'sg_norm_b': sg_norm_b[i], 'sg_w': sg_w[i], 'sg_b': sg_b[i],
            'ret_decay_fwd': ret_decay_fwd[i], 'ret_decay_bwd': ret_decay_bwd[i],
            'w_o': w_o[i], 'ln1_g': ln1_g[i], 'ln1_b': ln1_b[i],
            'router_w': router_w[i], 'router_b': router_b[i],
            'w_gate_up': w_gate_up[i], 'b_gate_up': b_gate_up[i],
            'w_down': w_down[i], 'b_down': b_down[i], 'ln2_g': ln2_g[i], 'ln2_b': ln2_b[i],
        }
        x, ctx = hybrid_layer(x, ctx, silu_c, silu_cc, p, rope_cos, rope_sin, pos_lat, pos_ctx,
                              i < DEPTH - 1)
    return x
```

```python
import functools

import numpy as np
import jax
import jax.numpy as jnp
from jax import lax
from jax.experimental import pallas as pl
from jax.experimental.pallas import tpu as pltpu

F32 = jnp.float32
BF16 = jnp.bfloat16
MXU_DT = BF16

D_MODEL = 1024
DEPTH = 4
GRID_W = 64
MLA_HEADS = 4
MLA_NOPE = 128
MLA_ROPE = 64
MLA_V = 128
MLA_Q_LORA = 384
MLA_KV_LORA = 256
MLA_QK = MLA_NOPE + MLA_ROPE
MLA_SCALE = MLA_QK ** -0.5
ROPE_BASE = 10000.0
ROPE_AXIS_FREQS = MLA_ROPE // 4
SG_GROUPS = 4
SG_WIDTH = 256
SG_CHUNK = 128
RET_HEADS = 4
RET_QK = 32
RET_V = 64
RET_CHUNK = 128
RET_ROPE_BASE = 10000.0
N_EXPERTS = 32
TOP_K = 4
D_EXPERT = 1024
SWIGLU_LIMIT = 7.0
SWIGLU_ALPHA = 1.702
N_MOD = 6
LN_EPS = 1e-5
RMS_EPS = 1e-6
DEEPNORM_ALPHA = (2 * DEPTH) ** 0.25
MLA_OUT = MLA_HEADS * MLA_V
RET_OUT = RET_HEADS * RET_V

TM = 256
CHUNK = 128
MOE_BM = 256
ATT_TQ = 512
ATT_TK = 1024
VMEM_LIMIT = 48 * 2 ** 20

_O_CQ, _O_CKV, _O_SGU, _O_SGV = 0, 384, 640, 896
_O_RQ, _O_RK, _O_RQS, _O_RKS, _O_RV, _O_RG, _O_KR = 1152, 1280, 1408, 1536, 1664, 1920, 2176
IN_P = 2304
_T_QC, _T_QS, _T_KCS, _T_RQC, _T_RQS, _T_RKC, _T_RKS = 0, 256, 512, 640, 768, 896, 1024
TAB_W = 1152


def _cparams(sem):
    return pltpu.CompilerParams(dimension_semantics=sem, vmem_limit_bytes=VMEM_LIMIT)


def _dot(a, b):
    return jnp.dot(a, b, preferred_element_type=F32)


def _dot_nt(a, b):
    return lax.dot_general(a, b, (((1,), (1,)), ((), ())), preferred_element_type=F32)


def _mx(a):
    return a.astype(MXU_DT)


def _swap16(j):
    return (j // 32) * 32 + ((j % 32) + 16) % 32


_ERF_ALPHA = (-2.72614225801306e-10, 2.77068142495902e-08, -2.10102402082508e-06,
              -5.69250639462346e-05, -7.34990630326855e-04, -2.95459980854025e-03,
              -1.60960333262415e-02)
_ERF_BETA = (-1.45660718464996e-05, -2.13374055278905e-04, -1.68282697438203e-03,
             -7.37332916720468e-03, -1.42647390514189e-02)


def _erf(x):
    x = jnp.clip(x, -4.0, 4.0)
    x2 = x * x
    p = jnp.full_like(x, _ERF_ALPHA[0])
    for c in _ERF_ALPHA[1:]:
        p = p * x2 + c
    q = jnp.full_like(x, _ERF_BETA[0])
    for c in _ERF_BETA[1:]:
        q = q * x2 + c
    return x * p / q


def _gelu(x):
    return 0.5 * x * (1.0 + _erf(x * 0.7071067811865476))


def _sigmoid(x):
    return 1.0 / (1.0 + jnp.exp(-x))


def _ln(x):
    xc = x - jnp.mean(x, axis=-1, keepdims=True)
    return xc * lax.rsqrt(jnp.mean(xc * xc, axis=-1, keepdims=True) + LN_EPS)


def _lane_group(shape, width):
    return lax.broadcasted_iota(jnp.int32, shape, len(shape) - 1) // width


def _ada_kernel(c_ref, w_ref, b_ref, o_ref):
    c = c_ref[...]
    o_ref[0] = _dot(c * _sigmoid(c), w_ref[0]) + b_ref[0]


def ada_modulation(c_rows, ada_w, ada_b):
    nl, d, n = ada_w.shape
    tn = 1536
    return pl.pallas_call(
        _ada_kernel,
        grid=(nl, n // tn),
        in_specs=[pl.BlockSpec((8, d), lambda l, j: (0, 0)),
                  pl.BlockSpec((1, d, tn), lambda l, j: (l, 0, j)),
                  pl.BlockSpec((1, 1, tn), lambda l, j: (l, 0, j))],
        out_specs=pl.BlockSpec((1, 8, tn), lambda l, j: (l, 0, j)),
        out_shape=jax.ShapeDtypeStruct((nl, 8, n), F32),
        compiler_params=_cparams(("arbitrary", "arbitrary")),
        name="ada_modulation",
    )(c_rows, ada_w, ada_b.reshape(nl, 1, n))


def _inproj_kernel(x_ref, mod_ref, tab_ref, w_in_ref, qg_ref, kvg_ref, w_uq_ref, w_ukv_ref,
                   sgg_ref, sgb_ref, sgw_ref, sgbias_ref, kdec_ref, bd_ref,
                   q_ref, k_ref, v_ref, sg_ref, retp_ref, rv_ref, a_ref):
    x = x_ref[0]
    mod = mod_ref[0, 0]
    h = x * (1.0 + mod[1:2]) + mod[0:1]
    p = _dot(_mx(h), w_in_ref[...])
    tab = tab_ref[...]

    cq = p[:, _O_CQ:_O_CQ + MLA_Q_LORA]
    cq = cq * lax.rsqrt(jnp.mean(cq * cq, axis=-1, keepdims=True) + RMS_EPS) * qg_ref[...]
    qa = _dot(_mx(cq), w_uq_ref[...])
    rot = (qa[:, 512:768] * tab[:, _T_QC:_T_QC + 256]
           + qa[:, 768:1024] * tab[:, _T_QS:_T_QS + 256])
    for hh in range(MLA_HEADS):
        q_ref[0, hh, :, 0:128] = (qa[:, 128 * hh:128 * hh + 128] * MLA_SCALE).astype(q_ref.dtype)
        g = hh // 2
        q_ref[0, hh, :, 128:256] = rot[:, 128 * g:128 * g + 128].astype(q_ref.dtype)

    ckv = p[:, _O_CKV:_O_CKV + MLA_KV_LORA]
    ckv = ckv * lax.rsqrt(jnp.mean(ckv * ckv, axis=-1, keepdims=True) + RMS_EPS) * kvg_ref[...]
    kv = _dot(_mx(ckv), w_ukv_ref[...])
    t = p[:, _O_KR:_O_KR + 128] * tab[:, _T_KCS:_T_KCS + 128]
    u = t + pltpu.roll(t, 64, axis=1)
    low = lax.broadcasted_iota(jnp.int32, u.shape, 1) < 64
    kx = (jnp.where(low, u, 0.0), jnp.where(low, 0.0, u))
    for hh in range(MLA_HEADS):
        k_ref[0, hh, :, 0:128] = kv[:, 256 * hh:256 * hh + 128].astype(k_ref.dtype)
        k_ref[0, hh, :, 128:256] = kx[hh % 2].astype(k_ref.dtype)
        v_ref[0, hh] = kv[:, 256 * hh + 128:256 * hh + 256].astype(v_ref.dtype)

    gu = _gelu(p[:, _O_SGU:_O_SGU + SG_WIDTH])
    gv = _ln(_gelu(p[:, _O_SGV:_O_SGV + SG_WIDTH])) * sgg_ref[...] + sgb_ref[...]
    gvm = _mx(gv)
    grp = _lane_group((CHUNK, SG_WIDTH), SG_WIDTH // SG_GROUPS)
    for c in range(TM // CHUNK):
        rows = slice(c * CHUNK, (c + 1) * CHUNK)
        res = _dot(sgw_ref[...], gvm[rows])
        mixed = sgbias_ref[...]
        for g in range(SG_GROUPS):
            mixed = mixed + jnp.where(grp == g, res[g * CHUNK:(g + 1) * CHUNK], 0.0)
        sg_ref[0, rows, :] = (gu[rows] * mixed).astype(sg_ref.dtype)

    rq = (p[:, _O_RQ:_O_RQ + 128] * tab[:, _T_RQC:_T_RQC + 128]
          + p[:, _O_RQS:_O_RQS + 128] * tab[:, _T_RQS:_T_RQS + 128])
    rk = (p[:, _O_RK:_O_RK + 128] * tab[:, _T_RKC:_T_RKC + 128]
          + p[:, _O_RKS:_O_RKS + 128] * tab[:, _T_RKS:_T_RKS + 128])
    rv = p[:, _O_RV:_O_RV + RET_OUT]
    retp_ref[0, :, 0:128] = rq
    retp_ref[0, :, 128:256] = rk
    retp_ref[0, :, 256:512] = p[:, _O_RG:_O_RG + RET_OUT]
    rvm = _mx(rv)
    rv_ref[0] = rvm.astype(rv_ref.dtype)
    bd = bd_ref[...]
    for c in range(TM // CHUNK):
        rows = slice(c * CHUNK, (c + 1) * CHUNK)
        for d in range(2):
            kd_t = _mx((rk[rows] * kdec_ref[d]).T)
            a_ref[0, c, d] = _dot(kd_t, rvm[rows]) * bd


def input_projection(x_all, mod_tab, tab, lw, n_lat_tiles):
    b, t, d = x_all.shape
    nt = t // TM
    nc = t // CHUNK
    cpt = TM // CHUNK
    const2 = lambda bi, j: (0, 0)
    const3 = lambda bi, j: (0, 0, 0)
    out_shape = (
        jax.ShapeDtypeStruct((b, MLA_HEADS, t, 256), MXU_DT),
        jax.ShapeDtypeStruct((b, MLA_HEADS, t, 256), MXU_DT),
        jax.ShapeDtypeStruct((b, MLA_HEADS, t, MLA_V), MXU_DT),
        jax.ShapeDtypeStruct((b, t, SG_WIDTH), MXU_DT),
        jax.ShapeDtypeStruct((b, t, 512), F32),
        jax.ShapeDtypeStruct((b, t, RET_OUT), MXU_DT),
        jax.ShapeDtypeStruct((b, nc, 2, 128, RET_OUT), F32),
    )
    head_spec = lambda w: pl.BlockSpec((1, MLA_HEADS, TM, w), lambda bi, j: (bi, 0, j, 0))
    tok_spec = lambda w: pl.BlockSpec((1, TM, w), lambda bi, j: (bi, j, 0))
    return pl.pallas_call(
        _inproj_kernel,
        grid=(b, nt),
        in_specs=[
            tok_spec(d),
            pl.BlockSpec((1, 1, N_MOD, d), lambda bi, j: (bi, j // n_lat_tiles, 0, 0)),
            pl.BlockSpec((TM, TAB_W), lambda bi, j: (j, 0)),
            pl.BlockSpec((d, IN_P), const2),
            pl.BlockSpec((1, MLA_Q_LORA), const2),
            pl.BlockSpec((1, MLA_KV_LORA), const2),
            pl.BlockSpec((MLA_Q_LORA, 1024), const2),
            pl.BlockSpec((MLA_KV_LORA, 1024), const2),
            pl.BlockSpec((1, SG_WIDTH), const2),
            pl.BlockSpec((1, SG_WIDTH), const2),
            pl.BlockSpec((SG_GROUPS * CHUNK, CHUNK), const2),
            pl.BlockSpec((CHUNK, SG_WIDTH), const2),
            pl.BlockSpec((2, CHUNK, 128), const3),
            pl.BlockSpec((128, RET_OUT), const2),
        ],
        out_specs=(head_spec(256), head_spec(256), head_spec(MLA_V), tok_spec(SG_WIDTH),
                   tok_spec(512), tok_spec(RET_OUT),
                   pl.BlockSpec((1, cpt, 2, 128, RET_OUT), lambda bi, j: (bi, j, 0, 0, 0))),
        out_shape=out_shape,
        compiler_params=_cparams(("parallel", "parallel")),
        name="input_projection",
    )(x_all, mod_tab, tab, lw["w_in"], lw["q_g"], lw["kv_g"], lw["w_uq"], lw["w_ukv"],
      lw["sg_g"], lw["sg_b"], lw["sg_w"], lw["sg_bias"], lw["kdec"], lw["bd"])


def _attn_kernel(q_ref, k_ref, v_ref, o_ref, *, n_main, tk, tail):
    q = q_ref[0, 0]
    tq = q.shape[0]

    def step(carry, start, size):
        m, l, acc = carry
        kk = k_ref[0, 0, pl.ds(start, size), :]
        vv = v_ref[0, 0, pl.ds(start, size), :]
        s = _dot_nt(q, kk)
        m_new = jnp.maximum(m, jnp.max(s, axis=-1, keepdims=True))
        alpha = jnp.exp(m - m_new)
        p = jnp.exp(s - m_new)
        l = alpha * l + jnp.sum(p, axis=-1, keepdims=True)
        acc = alpha * acc + _dot(_mx(p), vv)
        return m_new, l, acc

    carry = (jnp.full((tq, 1), -1e30, F32), jnp.zeros((tq, 1), F32), jnp.zeros((tq, MLA_V), F32))
    if n_main:
        carry = lax.fori_loop(
            0, n_main, lambda i, c: step(c, pl.multiple_of(i * tk, tk), tk), carry)
    if tail:
        carry = step(carry, n_main * tk, tail)
    _, l, acc = carry
    o_ref[0] = (acc / l).astype(o_ref.dtype)


def mla_attention(q, k, v, s_len, lc):
    b, hn, t, _ = q.shape
    tq = min(ATT_TQ, s_len)
    tk = min(ATT_TK, s_len)
    kv_full = lambda w: pl.BlockSpec((1, 1, t, w), lambda bi, hi, i: (bi, hi, 0, 0))
    out_lat = pl.pallas_call(
        functools.partial(_attn_kernel, n_main=s_len // tk, tk=tk, tail=lc),
        grid=(b, hn, s_len // tq),
        in_specs=[pl.BlockSpec((1, 1, tq, 256), lambda bi, hi, i: (bi, hi, i, 0)),
                  kv_full(256), kv_full(MLA_V)],
        out_specs=pl.BlockSpec((1, tq, MLA_V), lambda bi, hi, i: (bi, i, hi)),
        out_shape=jax.ShapeDtypeStruct((b, t, MLA_OUT), MXU_DT),
        compiler_params=_cparams(("parallel", "parallel", "arbitrary")),
        name="mla_attention_latent",
    )(q, k, v)
    cblk = s_len // lc

    def _ctx_kernel(q_ref, k_ref, v_ref, prev_ref, o_ref):
        del prev_ref
        _attn_kernel(q_ref, k_ref, v_ref, o_ref, n_main=0, tk=tk, tail=lc)

    ctx_spec = lambda w: pl.BlockSpec((1, 1, lc, w), lambda bi, hi: (bi, hi, cblk, 0))
    return pl.pallas_call(
        _ctx_kernel,
        grid=(b, hn),
        in_specs=[ctx_spec(256), ctx_spec(256), ctx_spec(MLA_V), pl.BlockSpec(memory_space=pl.ANY)],
        out_specs=pl.BlockSpec((1, lc, MLA_V), lambda bi, hi: (bi, cblk, hi)),
        out_shape=jax.ShapeDtypeStruct((b, t, MLA_OUT), MXU_DT),
        input_output_aliases={3: 0},
        compiler_params=_cparams(("parallel", "parallel")),
        name="mla_attention_context",
    )(q, k, v, out_lat)


def _ret_scan_kernel(af_ref, ab_ref, cd_ref, sf_ref, sb_ref, st_ref):
    @pl.when(pl.program_id(1) == 0)
    def _():
        st_ref[...] = jnp.zeros_like(st_ref)

    sf_ref[0, 0] = st_ref[0]
    sb_ref[0, 0] = st_ref[1]
    st_ref[0] = st_ref[0] * cd_ref[0] + af_ref[0, 0, 0]
    st_ref[1] = st_ref[1] * cd_ref[1] + ab_ref[0, 0, 0]


def retention_scan(a, cd, n_lat_chunks):
    b, nc = a.shape[:2]
    ncc = nc - n_lat_chunks

    def fwd_chunk(n):
        return jnp.where(n < ncc, n_lat_chunks + n, n - ncc)

    def bwd_chunk(n):
        return jnp.where(n < ncc, nc - 1 - n, n_lat_chunks - 1 - (n - ncc))

    blk = (1, 1, 1, 128, RET_OUT)
    sblk = (1, 1, 128, RET_OUT)
    return pl.pallas_call(
        _ret_scan_kernel,
        grid=(b, nc),
        in_specs=[pl.BlockSpec(blk, lambda bi, n: (bi, fwd_chunk(n), 0, 0, 0)),
                  pl.BlockSpec(blk, lambda bi, n: (bi, bwd_chunk(n), 1, 0, 0)),
                  pl.BlockSpec((2, 1, RET_OUT), lambda bi, n: (0, 0, 0))],
        out_specs=(pl.BlockSpec(sblk, lambda bi, n: (bi, fwd_chunk(n), 0, 0)),
                   pl.BlockSpec(sblk, lambda bi, n: (bi, bwd_chunk(n), 0, 0))),
        out_shape=(jax.ShapeDtypeStruct((b, nc, 128, RET_OUT), F32),
                   jax.ShapeDtypeStruct((b, nc, 128, RET_OUT), F32)),
        scratch_shapes=[pltpu.VMEM((2, 128, RET_OUT), F32)],
        compiler_params=_cparams(("parallel", "arbitrary")),
        name="retention_scan",
    )(a, a, cd)


def _split_dot(x, ones_bd):
    hi = x.astype(BF16)
    lo = (x - hi.astype(F32)).astype(BF16)
    return _dot(hi, ones_bd) + _dot(lo, ones_bd)


def _outproj_kernel(x_ref, mla_ref, sg_ref, retp_ref, rv_ref, sf_ref, sb_ref, mod_ref,
                    m_ref, qdec_ref, seg_ref, w_o_ref, lng_ref, lnb_ref, rw_ref, rb_ref, tri_ref,
                    x1_ref, h2_ref, idx_ref, gate_ref, rank_ref, cnt_ref, carry_ref):
    first = jnp.logical_and(pl.program_id(0) == 0, pl.program_id(1) == 0)

    @pl.when(first)
    def _():
        carry_ref[...] = jnp.zeros_like(carry_ref)

    g32 = _lane_group((CHUNK, 128), RET_QK)
    g64 = _lane_group((CHUNK, RET_OUT), RET_V)
    seg = seg_ref[...]
    ret_rows = []
    for c in range(TM // CHUNK):
        rows = slice(c * CHUNK, (c + 1) * CHUNK)
        rq = retp_ref[0, rows, 0:128]
        rk = _mx(retp_ref[0, rows, 128:256])
        rg = retp_ref[0, rows, 256:512]
        rv = rv_ref[0, rows, :]
        o = (_dot(_mx(rq * qdec_ref[0]), _mx(sf_ref[0, c]))
             + _dot(_mx(rq * qdec_ref[1]), _mx(sb_ref[0, c])))
        for hh in range(RET_HEADS):
            s = _dot_nt(_mx(jnp.where(g32 == hh, rq, 0.0)), rk)
            oh = _dot(_mx(s * m_ref[hh]), rv)
            o = o + jnp.where(g64 == hh, oh, 0.0)
        mean = _split_dot(o, seg) * (1.0 / RET_V)
        oc = o - mean
        var = _split_dot(oc * oc, seg) * (1.0 / RET_V)
        ret_rows.append(oc * lax.rsqrt(var + LN_EPS) * (rg * _sigmoid(rg)))
    ret = jnp.concatenate(ret_rows, axis=0)

    y = (_dot(mla_ref[0], w_o_ref[0:MLA_OUT, :])
         + _dot(sg_ref[0], w_o_ref[MLA_OUT:MLA_OUT + SG_WIDTH, :])
         + _dot(_mx(ret), w_o_ref[MLA_OUT + SG_WIDTH:, :]))
    mod = mod_ref[0, 0]
    x1 = _ln(DEEPNORM_ALPHA * x_ref[0] + mod[2:3] * y) * lng_ref[...] + lnb_ref[...]
    x1_ref[0] = x1
    h2 = x1 * (1.0 + mod[4:5]) + mod[3:4]
    h2_hi = h2.astype(BF16)
    h2_ref[0] = h2_hi.astype(h2_ref.dtype) if MXU_DT == BF16 else h2
    h2_lo = (h2 - h2_hi.astype(F32)).astype(BF16)

    r2 = _dot_nt(rw_ref[...], h2_hi)
    logits = (r2[0:N_EXPERTS] + r2[N_EXPERTS:2 * N_EXPERTS]
              + _dot_nt(rw_ref[0:N_EXPERTS, :], h2_lo) + rb_ref[...])
    e_iota = lax.broadcasted_iota(jnp.int32, logits.shape, 0).astype(F32)
    work = logits
    vals, idxs = [], []
    for _ in range(TOP_K):
        mval = jnp.max(work, axis=0, keepdims=True)
        midx = jnp.min(jnp.where(work == mval, e_iota, float(N_EXPERTS)), axis=0, keepdims=True)
        vals.append(mval)
        idxs.append(midx)
        work = jnp.where(e_iota == midx, -jnp.inf, work)
    ex = [jnp.exp(vv - vals[0]) for vv in vals]
    den = ex[0] + ex[1] + ex[2] + ex[3]
    onehot = jnp.zeros_like(logits)
    for kk in range(TOP_K):
        onehot = onehot + jnp.where(e_iota == idxs[kk], 1.0, 0.0)
    base = carry_ref[:, 0:1] + _dot(onehot.astype(BF16), tri_ref[...])
    for kk in range(TOP_K):
        gate_ref[0, 0, kk:kk + 1, :] = ex[kk] / den
        idx_ref[0, 0, kk:kk + 1, :] = idxs[kk].astype(jnp.int32)
        rk_k = jnp.sum(jnp.where(e_iota == idxs[kk], base, 0.0), axis=0, keepdims=True)
        rank_ref[0, 0, kk:kk + 1, :] = rk_k.astype(jnp.int32)
    carry_ref[...] = carry_ref[...] + jnp.sum(onehot, axis=1, keepdims=True)
    cnt_ref[...] = carry_ref[...].astype(jnp.int32)


def output_projection(x_all, mla, sg, retp, rv, sf, sb, mod_tab, lw, n_lat_tiles):
    b, t, d = x_all.shape
    nt = t // TM
    cpt = TM // CHUNK
    const2 = lambda bi, j: (0, 0)
    const3 = lambda bi, j: (0, 0, 0)
    tok_spec = lambda w: pl.BlockSpec((1, TM, w), lambda bi, j: (bi, j, 0))
    st_spec = pl.BlockSpec((1, cpt, 128, RET_OUT), lambda bi, j: (bi, j, 0, 0))
    route_spec = pl.BlockSpec((1, 1, TOP_K, TM), lambda bi, j: (bi, j, 0, 0))
    route_shape = lambda dt: jax.ShapeDtypeStruct((b, nt, TOP_K, TM), dt)
    return pl.pallas_call(
        _outproj_kernel,
        grid=(b, nt),
        in_specs=[
            tok_spec(d), tok_spec(MLA_OUT), tok_spec(SG_WIDTH), tok_spec(512), tok_spec(RET_OUT),
            st_spec, st_spec,
            pl.BlockSpec((1, 1, N_MOD, d), lambda bi, j: (bi, j // n_lat_tiles, 0, 0)),
            pl.BlockSpec((RET_HEADS, CHUNK, CHUNK), const3),
            pl.BlockSpec((2, CHUNK, 128), const3),
            pl.BlockSpec((RET_OUT, RET_OUT), const2),
            pl.BlockSpec((d, d), const2),
            pl.BlockSpec((1, d), const2),
            pl.BlockSpec((1, d), const2),
            pl.BlockSpec((2 * N_EXPERTS, d), const2),
            pl.BlockSpec((N_EXPERTS, 1), const2),
            pl.BlockSpec((TM, TM), const2),
        ],
        out_specs=(tok_spec(d), tok_spec(d), route_spec, route_spec, route_spec,
                   pl.BlockSpec((N_EXPERTS, 128), const2)),
        out_shape=(jax.ShapeDtypeStruct((b, t, d), F32),
                   jax.ShapeDtypeStruct((b, t, d), MXU_DT),
                   route_shape(jnp.int32), route_shape(F32), route_shape(jnp.int32),
                   jax.ShapeDtypeStruct((N_EXPERTS, 128), jnp.int32)),
        scratch_shapes=[pltpu.VMEM((N_EXPERTS, 128), F32)],
        compiler_params=_cparams(("arbitrary", "arbitrary")),
        name="output_projection",
    )(x_all, mla, sg, retp, rv, sf, sb, mod_tab, lw["ret_m"], lw["qdec"], lw["seg"], lw["w_o"],
      lw["ln1_g"], lw["ln1_b"], lw["router_w"], lw["router_b"], lw["tri"])


def _expert_kernel(be_ref, nu_ref, x_ref, wg_ref, wl_ref, bg_ref, bl_ref, wd_ref, bd_ref, y_ref):
    del be_ref
    i = pl.program_id(0)

    @pl.when(i < nu_ref[0])
    def _():
        xb = x_ref[...]
        glu = jnp.minimum(_dot(xb, wg_ref[0]) + bg_ref[0], SWIGLU_LIMIT)
        lin = jnp.clip(_dot(xb, wl_ref[0]) + bl_ref[0], -SWIGLU_LIMIT, SWIGLU_LIMIT)
        act = glu * _sigmoid(SWIGLU_ALPHA * glu) * (lin + 1.0)
        y_ref[...] = _dot(_mx(act), wd_ref[0]) + bd_ref[0]

    @pl.when(i >= nu_ref[0])
    def _():
        y_ref[...] = jnp.zeros_like(y_ref)


def expert_ffn(xg, block_e, n_used, lw):
    cap, d = xg.shape
    nb = cap // MOE_BM
    de = D_EXPERT
    xmap = lambda i, be, nu: (jnp.minimum(i, nu[0] - 1), 0)
    wmap = lambda i, be, nu: (be[i], 0, 0)
    grid_spec = pltpu.PrefetchScalarGridSpec(
        num_scalar_prefetch=2,
        grid=(nb,),
        in_specs=[pl.BlockSpec((MOE_BM, d), xmap),
                  pl.BlockSpec((1, d, de), wmap), pl.BlockSpec((1, d, de), wmap),
                  pl.BlockSpec((1, 1, de), wmap), pl.BlockSpec((1, 1, de), wmap),
                  pl.BlockSpec((1, de, d), wmap), pl.BlockSpec((1, 1, d), wmap)],
        out_specs=pl.BlockSpec((MOE_BM, d), lambda i, be, nu: (i, 0)),
    )
    return pl.pallas_call(
        _expert_kernel,
        grid_spec=grid_spec,
        out_shape=jax.ShapeDtypeStruct((cap, d), F32),
        compiler_params=_cparams(("arbitrary",)),
        name="expert_ffn",
    )(block_e, n_used, xg, lw["w_glu"], lw["w_lin"], lw["b_glu"], lw["b_lin"], lw["w_down"], lw["b_down"])


def _ln2_kernel(x_ref, f_ref, mod_ref, g_ref, b_ref, o_ref):
    mod = mod_ref[0, 0]
    o_ref[0] = _ln(DEEPNORM_ALPHA * x_ref[0] + mod[5:6] * f_ref[0]) * g_ref[...] + b_ref[...]


def deepnorm2(x1, f, mod_tab, lw, n_lat_tiles):
    b, t, d = x1.shape
    tok = pl.BlockSpec((1, TM, d), lambda bi, j: (bi, j, 0))
    vec = pl.BlockSpec((1, d), lambda bi, j: (0, 0))
    return pl.pallas_call(
        _ln2_kernel,
        grid=(b, t // TM),
        in_specs=[tok, tok, pl.BlockSpec((1, 1, N_MOD, d), lambda bi, j: (bi, j // n_lat_tiles, 0, 0)), vec, vec],
        out_specs=tok,
        out_shape=jax.ShapeDtypeStruct((b, t, d), F32),
        compiler_params=_cparams(("parallel", "parallel")),
        name="deepnorm2",
    )(x1, f, mod_tab, lw["ln2_g"], lw["ln2_b"])


def _rotation_tables(s_len, lc):
    rows = s_len // GRID_W
    row = jnp.broadcast_to(jnp.arange(rows, dtype=F32)[:, None], (rows, GRID_W)).reshape(-1)
    col = jnp.broadcast_to(jnp.arange(GRID_W, dtype=F32)[None, :], (rows, GRID_W)).reshape(-1)
    inv = ROPE_BASE ** (-jnp.arange(ROPE_AXIS_FREQS, dtype=F32) / ROPE_AXIS_FREQS)
    ar, ac = row[:, None] * inv, col[:, None] * inv
    c64 = jnp.concatenate([jnp.cos(ar), jnp.cos(ar), jnp.cos(ac), jnp.cos(ac)], axis=1)
    s64 = jnp.concatenate([-jnp.sin(ar), jnp.sin(ar), -jnp.sin(ac), jnp.sin(ac)], axis=1)
    c64 = jnp.concatenate([c64, jnp.ones((lc, 64), F32)], axis=0)
    s64 = jnp.concatenate([s64, jnp.zeros((lc, 64), F32)], axis=0)
    half = RET_QK // 2
    pos = jnp.concatenate([lc + jnp.arange(s_len, dtype=F32), jnp.arange(lc, dtype=F32)])
    inv_r = 1.0 / (RET_ROPE_BASE ** jnp.linspace(0.0, 1.0, half, dtype=F32))
    ang = pos[:, None] * inv_r
    rc = jnp.tile(jnp.concatenate([jnp.cos(ang), jnp.cos(ang)], axis=1), (1, RET_HEADS))
    rs = jnp.tile(jnp.concatenate([-jnp.sin(ang), jnp.sin(ang)], axis=1), (1, RET_HEADS))
    qs = RET_QK ** -0.5
    return jnp.concatenate([
        jnp.tile(c64, (1, MLA_HEADS)) * MLA_SCALE, jnp.tile(s64, (1, MLA_HEADS)) * MLA_SCALE,
        c64, s64, rc * qs, rs * qs, rc, rs], axis=1)


def _in_perm():
    a = np.arange
    return np.concatenate([
        a(0, 640), a(704, 1216), a(1216, 1344), a(1344, 1472),
        1216 + _swap16(a(128)), 1344 + _swap16(a(128)), a(1472, 1984),
        640 + a(64), 640 + _swap16(a(64))])


def _uq_perm():
    a = np.arange
    nope = [h * MLA_QK + a(MLA_NOPE) for h in range(MLA_HEADS)]
    rope = [h * MLA_QK + MLA_NOPE + a(MLA_ROPE) for h in range(MLA_HEADS)]
    part = [h * MLA_QK + MLA_NOPE + _swap16(a(MLA_ROPE)) for h in range(MLA_HEADS)]
    return np.concatenate(nope + rope + part)


def _layer_weights(p):
    nl = p["w_in"].shape[0]
    lgf = jax.nn.log_sigmoid(p["ret_decay_fwd"].astype(F32))
    lgb = jax.nn.log_sigmoid(p["ret_decay_bwd"].astype(F32))
    h128 = np.arange(128) // RET_QK
    h256 = np.arange(RET_OUT) // RET_V
    a = jnp.arange(CHUNK, dtype=F32)[None, :, None]
    lf, lb = lgf[:, h128][:, None, :], lgb[:, h128][:, None, :]
    kdec = jnp.stack([jnp.exp(lf * (CHUNK - 1.0 - a)), jnp.exp(lb * a)], axis=1)
    qdec = jnp.stack([jnp.exp(lf * (a + 1.0)), jnp.exp(lb * (CHUNK - a))], axis=1)
    i = jnp.arange(CHUNK, dtype=F32)[:, None]
    j = jnp.arange(CHUNK, dtype=F32)[None, :]
    dif = (i - j)[None, None]
    ret_m = jnp.where(dif >= 0, jnp.exp(lgf[:, :, None, None] * jnp.maximum(dif, 0.0)),
                      jnp.exp(lgb[:, :, None, None] * jnp.maximum(-dif, 0.0)))
    cd = jnp.stack([jnp.exp(lgf[:, h256] * CHUNK), jnp.exp(lgb[:, h256] * CHUNK)], axis=1)[:, :, None, :]
    bd = (h128[:, None] == h256[None, :]).astype(np.float32)
    seg = (h256[:, None] == h256[None, :]).astype(np.float32)
    tri = (np.arange(TM)[:, None] < np.arange(TM)[None, :]).astype(np.float32)
    rw_t = jnp.swapaxes(p["router_w"], 1, 2)
    rw_hi = rw_t.astype(BF16)
    rw_lo = (rw_t - rw_hi.astype(F32)).astype(BF16)
    sg_bias = jnp.repeat(jnp.swapaxes(p["sg_b"], 1, 2), SG_WIDTH // SG_GROUPS, axis=2)
    wgu = p["w_gate_up"]
    bgu = p["b_gate_up"]
    return {
        "w_in": p["w_in"][:, :, _in_perm()].astype(MXU_DT),
        "q_g": p["mla_q_norm_g"][:, None, :], "kv_g": p["mla_kv_norm_g"][:, None, :],
        "w_uq": p["mla_w_uq"][:, :, _uq_perm()].astype(MXU_DT),
        "w_ukv": p["mla_w_ukv"].astype(MXU_DT),
        "sg_g": p["sg_norm_g"][:, None, :], "sg_b": p["sg_norm_b"][:, None, :],
        "sg_w": p["sg_w"].reshape(nl, SG_GROUPS * CHUNK, CHUNK).astype(MXU_DT),
        "sg_bias": sg_bias,
        "kdec": kdec, "qdec": qdec, "ret_m": ret_m, "cd": cd,
        "bd": jnp.broadcast_to(jnp.asarray(bd), (nl,) + bd.shape),
        "seg": jnp.broadcast_to(jnp.asarray(seg, BF16), (nl,) + seg.shape),
        "tri": jnp.broadcast_to(jnp.asarray(tri, BF16), (nl,) + tri.shape),
        "w_o": p["w_o"].astype(MXU_DT),
        "ln1_g": p["ln1_g"][:, None, :], "ln1_b": p["ln1_b"][:, None, :],
        "ln2_g": p["ln2_g"][:, None, :], "ln2_b": p["ln2_b"][:, None, :],
        "router_w": jnp.concatenate([rw_hi, rw_lo], axis=1),
        "router_b": p["router_b"][:, :, None],
        "w_glu": wgu[..., 0::2].astype(MXU_DT), "w_lin": wgu[..., 1::2].astype(MXU_DT),
        "b_glu": bgu[:, :, None, 0::2], "b_lin": bgu[:, :, None, 1::2],
        "w_down": p["w_down"].astype(MXU_DT), "b_down": p["b_down"][:, :, None, :],
    }


def _route(idx, rank, counts):
    n_assign = idx.shape[0] * TOP_K
    nb = -(-(n_assign + N_EXPERTS * (MOE_BM - 1)) // MOE_BM)
    padded = (counts + MOE_BM - 1) // MOE_BM * MOE_BM
    pad_end = jnp.cumsum(padded)
    dest = (pad_end - padded)[idx] + rank
    block_e = jnp.minimum(
        jnp.searchsorted(pad_end, jnp.arange(nb, dtype=jnp.int32) * MOE_BM, side="right"),
        N_EXPERTS - 1).astype(jnp.int32)
    n_used = (pad_end[-1] // MOE_BM).astype(jnp.int32).reshape(1)
    return dest.astype(jnp.int32), block_e, n_used, nb


def _dispatch(h2, dest, nb):
    n, d = h2.shape
    tok = jnp.repeat(jnp.arange(n, dtype=jnp.int32), TOP_K)
    tok_buf = jnp.full((nb * MOE_BM,), n, jnp.int32).at[dest.reshape(-1)].set(tok)
    return jnp.concatenate([h2, jnp.zeros((1, d), h2.dtype)], axis=0)[tok_buf]


def _combine(y, dest, gates):
    return jnp.sum(y[dest] * gates[..., None], axis=1)


def kernel(x, c, ctx, c_ctx, ada_w, ada_b, w_in, mla_q_norm_g, mla_kv_norm_g, mla_w_uq, mla_w_ukv,
           sg_norm_g, sg_norm_b, sg_w, sg_b, ret_decay_fwd, ret_decay_bwd, w_o, ln1_g, ln1_b,
           router_w, router_b, w_gate_up, b_gate_up, w_down, b_down, ln2_g, ln2_b):
    b, s_len, d = x.shape
    lc = ctx.shape[1]
    assert d == D_MODEL and lc % TM == 0 and s_len % lc == 0 and s_len % GRID_W == 0
    assert b + 1 <= 8
    t = s_len + lc
    n_lat_tiles = s_len // TM
    params = dict(w_in=w_in, mla_q_norm_g=mla_q_norm_g, mla_kv_norm_g=mla_kv_norm_g, mla_w_uq=mla_w_uq,
                  mla_w_ukv=mla_w_ukv, sg_norm_g=sg_norm_g, sg_norm_b=sg_norm_b, sg_w=sg_w, sg_b=sg_b,
                  ret_decay_fwd=ret_decay_fwd, ret_decay_bwd=ret_decay_bwd, w_o=w_o, ln1_g=ln1_g,
                  ln1_b=ln1_b, router_w=router_w, router_b=router_b, w_gate_up=w_gate_up,
                  b_gate_up=b_gate_up, w_down=w_down, b_down=b_down, ln2_g=ln2_g, ln2_b=ln2_b)
    lws = _layer_weights(params)
    tab = _rotation_tables(s_len, lc)

    c_rows = jnp.concatenate([c, c_ctx[None, :], jnp.zeros((8 - b - 1, d), F32)], axis=0)
    mod = ada_modulation(c_rows, ada_w, ada_b).reshape(DEPTH, 8, N_MOD, d)
    mod_tab = jnp.stack([mod[:, :b], jnp.broadcast_to(mod[:, b:b + 1], (DEPTH, b, N_MOD, d))], axis=2)

    x_all = jnp.concatenate([x, ctx], axis=1)
    for li in range(DEPTH):
        lw = {k: v[li] for k, v in lws.items()}
        mt = mod_tab[li]
        q, k, v, sg, retp, rv, a = input_projection(x_all, mt, tab, lw, n_lat_tiles)
        mla = mla_attention(q, k, v, s_len, lc)
        sf, sb = retention_scan(a, lw["cd"], s_len // CHUNK)
        x1, h2, idx, gates, rank, cnt = output_projection(
            x_all, mla, sg, retp, rv, sf, sb, mt, lw, n_lat_tiles)
        to_tok = lambda z: z.transpose(0, 1, 3, 2).reshape(b * t, TOP_K)
        idx, gates, rank = to_tok(idx), to_tok(gates), to_tok(rank)
        dest, block_e, n_used, nb = _route(idx, rank, cnt[:, 0])
        xg = _dispatch(h2.reshape(b * t, d), dest, nb)
        y = expert_ffn(xg, block_e, n_used, lw)
        f = _combine(y, dest, gates).reshape(b, t, d)
        x_all = deepnorm2(x1, f, mt, lw, n_lat_tiles)
    return x_all[:, :s_len]
```

```python
import functools

import numpy as np
import jax
import jax.numpy as jnp
from jax import lax
from jax.experimental import pallas as pl
from jax.experimental.pallas import tpu as pltpu

F32 = jnp.float32
BF16 = jnp.bfloat16
MXU_DT = BF16

D_MODEL = 1024
DEPTH = 4
GRID_W = 64
MLA_HEADS = 4
MLA_NOPE = 128
MLA_ROPE = 64
MLA_V = 128
MLA_Q_LORA = 384
MLA_KV_LORA = 256
MLA_QK = MLA_NOPE + MLA_ROPE
MLA_SCALE = MLA_QK ** -0.5
ROPE_BASE = 10000.0
ROPE_AXIS_FREQS = MLA_ROPE // 4
SG_GROUPS = 4
SG_WIDTH = 256
SG_CHUNK = 128
RET_HEADS = 4
RET_QK = 32
RET_V = 64
RET_CHUNK = 128
RET_ROPE_BASE = 10000.0
N_EXPERTS = 32
TOP_K = 4
D_EXPERT = 1024
SWIGLU_LIMIT = 7.0
SWIGLU_ALPHA = 1.702
N_MOD = 6
LN_EPS = 1e-5
RMS_EPS = 1e-6
DEEPNORM_ALPHA = (2 * DEPTH) ** 0.25
MLA_OUT = MLA_HEADS * MLA_V
RET_OUT = RET_HEADS * RET_V

TM = 256
CHUNK = 128
MOE_BM = 256
ATT_TQ = 512
ATT_TK = 1024
VMEM_LIMIT = 48 * 2 ** 20
EXPERT_VMEM_LIMIT = 56 * 2 ** 20

_O_CQ, _O_CKV, _O_SGU, _O_SGV = 0, 384, 640, 896
_O_RQ, _O_RK, _O_RQS, _O_RKS, _O_RV, _O_RG, _O_KR = 1152, 1280, 1408, 1536, 1664, 1920, 2176
IN_P = 2304
_T_QC, _T_QS, _T_KCS, _T_RQC, _T_RQS, _T_RKC, _T_RKS = 0, 256, 512, 640, 768, 896, 1024
TAB_W = 1152


def _cparams(sem):
    return pltpu.CompilerParams(dimension_semantics=sem, vmem_limit_bytes=VMEM_LIMIT)


def _dot(a, b):
    return jnp.dot(a, b, preferred_element_type=F32)


def _dot_nt(a, b):
    return lax.dot_general(a, b, (((1,), (1,)), ((), ())), preferred_element_type=F32)


def _mx(a):
    return a.astype(MXU_DT)


def _swap16(j):
    return (j // 32) * 32 + ((j % 32) + 16) % 32


_ERF_ALPHA = (-2.72614225801306e-10, 2.77068142495902e-08, -2.10102402082508e-06,
              -5.69250639462346e-05, -7.34990630326855e-04, -2.95459980854025e-03,
              -1.60960333262415e-02)
_ERF_BETA = (-1.45660718464996e-05, -2.13374055278905e-04, -1.68282697438203e-03,
             -7.37332916720468e-03, -1.42647390514189e-02)


def _erf(x):
    x = jnp.clip(x, -4.0, 4.0)
    x2 = x * x
    p = jnp.full_like(x, _ERF_ALPHA[0])
    for c in _ERF_ALPHA[1:]:
        p = p * x2 + c
    q = jnp.full_like(x, _ERF_BETA[0])
    for c in _ERF_BETA[1:]:
        q = q * x2 + c
    return x * p / q


def _gelu(x):
    return 0.5 * x * (1.0 + _erf(x * 0.7071067811865476))


def _sigmoid(x):
    return 1.0 / (1.0 + jnp.exp(-x))


def _ln(x):
    xc = x - jnp.mean(x, axis=-1, keepdims=True)
    return xc * lax.rsqrt(jnp.mean(xc * xc, axis=-1, keepdims=True) + LN_EPS)


def _lane_group(shape, width):
    return lax.broadcasted_iota(jnp.int32, shape, len(shape) - 1) // width


def _ada_kernel(c_ref, w_ref, b_ref, o_ref):
    c = c_ref[...]
    o_ref[0] = _dot(c * _sigmoid(c), w_ref[0]) + b_ref[0]


def ada_modulation(c_rows, ada_w, ada_b):
    nl, d, n = ada_w.shape
    tn = 1536
    return pl.pallas_call(
        _ada_kernel,
        grid=(nl, n // tn),
        in_specs=[pl.BlockSpec((8, d), lambda l, j: (0, 0)),
                  pl.BlockSpec((1, d, tn), lambda l, j: (l, 0, j)),
                  pl.BlockSpec((1, 1, tn), lambda l, j: (l, 0, j))],
        out_specs=pl.BlockSpec((1, 8, tn), lambda l, j: (l, 0, j)),
        out_shape=jax.ShapeDtypeStruct((nl, 8, n), F32),
        compiler_params=_cparams(("arbitrary", "arbitrary")),
        name="ada_modulation",
    )(c_rows, ada_w, ada_b.reshape(nl, 1, n))


def _inproj_kernel(x_ref, mod_ref, tab_ref, w_in_ref, qg_ref, kvg_ref, w_uq_ref, w_ukv_ref,
                   sgg_ref, sgb_ref, sgw_ref, sgbias_ref, kdec_ref, bd_ref,
                   q_ref, k_ref, v_ref, sg_ref, retp_ref, rv_ref, a_ref):
    x = x_ref[0]
    mod = mod_ref[0, 0]
    h = x * (1.0 + mod[1:2]) + mod[0:1]
    p = _dot(_mx(h), w_in_ref[...])
    tab = tab_ref[...]

    cq = p[:, _O_CQ:_O_CQ + MLA_Q_LORA]
    cq = cq * lax.rsqrt(jnp.mean(cq * cq, axis=-1, keepdims=True) + RMS_EPS) * qg_ref[...]
    qa = _dot(_mx(cq), w_uq_ref[...])
    rot = (qa[:, 512:768] * tab[:, _T_QC:_T_QC + 256]
           + qa[:, 768:1024] * tab[:, _T_QS:_T_QS + 256])
    for hh in range(MLA_HEADS):
        q_ref[0, hh, :, 0:128] = (qa[:, 128 * hh:128 * hh + 128] * MLA_SCALE).astype(q_ref.dtype)
        g = hh // 2
        q_ref[0, hh, :, 128:256] = rot[:, 128 * g:128 * g + 128].astype(q_ref.dtype)

    ckv = p[:, _O_CKV:_O_CKV + MLA_KV_LORA]
    ckv = ckv * lax.rsqrt(jnp.mean(ckv * ckv, axis=-1, keepdims=True) + RMS_EPS) * kvg_ref[...]
    kv = _dot(_mx(ckv), w_ukv_ref[...])
    t = p[:, _O_KR:_O_KR + 128] * tab[:, _T_KCS:_T_KCS + 128]
    u = t + pltpu.roll(t, 64, axis=1)
    low = lax.broadcasted_iota(jnp.int32, u.shape, 1) < 64
    kx = (jnp.where(low, u, 0.0), jnp.where(low, 0.0, u))
    for hh in range(MLA_HEADS):
        k_ref[0, hh, :, 0:128] = kv[:, 256 * hh:256 * hh + 128].astype(k_ref.dtype)
        k_ref[0, hh, :, 128:256] = kx[hh % 2].astype(k_ref.dtype)
        v_ref[0, hh] = kv[:, 256 * hh + 128:256 * hh + 256].astype(v_ref.dtype)

    gu = _gelu(p[:, _O_SGU:_O_SGU + SG_WIDTH])
    gv = _ln(_gelu(p[:, _O_SGV:_O_SGV + SG_WIDTH])) * sgg_ref[...] + sgb_ref[...]
    gvm = _mx(gv)
    grp = _lane_group((CHUNK, SG_WIDTH), SG_WIDTH // SG_GROUPS)
    for c in range(TM // CHUNK):
        rows = slice(c * CHUNK, (c + 1) * CHUNK)
        res = _dot(sgw_ref[...], gvm[rows])
        mixed = sgbias_ref[...]
        for g in range(SG_GROUPS):
            mixed = mixed + jnp.where(grp == g, res[g * CHUNK:(g + 1) * CHUNK], 0.0)
        sg_ref[0, rows, :] = (gu[rows] * mixed).astype(sg_ref.dtype)

    rq = (p[:, _O_RQ:_O_RQ + 128] * tab[:, _T_RQC:_T_RQC + 128]
          + p[:, _O_RQS:_O_RQS + 128] * tab[:, _T_RQS:_T_RQS + 128])
    rk = (p[:, _O_RK:_O_RK + 128] * tab[:, _T_RKC:_T_RKC + 128]
          + p[:, _O_RKS:_O_RKS + 128] * tab[:, _T_RKS:_T_RKS + 128])
    rv = p[:, _O_RV:_O_RV + RET_OUT]
    retp_ref[0, :, 0:128] = rq
    retp_ref[0, :, 128:256] = rk
    retp_ref[0, :, 256:512] = p[:, _O_RG:_O_RG + RET_OUT]
    rvm = _mx(rv)
    rv_ref[0] = rvm.astype(rv_ref.dtype)
    bd = bd_ref[...]
    for c in range(TM // CHUNK):
        rows = slice(c * CHUNK, (c + 1) * CHUNK)
        for d in range(2):
            kd_t = _mx((rk[rows] * kdec_ref[d]).T)
            a_ref[0, c, d] = _dot(kd_t, rvm[rows]) * bd


def input_projection(x_all, mod_tab, tab, lw, n_lat_tiles):
    b, t, d = x_all.shape
    nt = t // TM
    nc = t // CHUNK
    cpt = TM // CHUNK
    const2 = lambda bi, j: (0, 0)
    const3 = lambda bi, j: (0, 0, 0)
    out_shape = (
        jax.ShapeDtypeStruct((b, MLA_HEADS, t, 256), MXU_DT),
        jax.ShapeDtypeStruct((b, MLA_HEADS, t, 256), MXU_DT),
        jax.ShapeDtypeStruct((b, MLA_HEADS, t, MLA_V), MXU_DT),
        jax.ShapeDtypeStruct((b, t, SG_WIDTH), MXU_DT),
        jax.ShapeDtypeStruct((b, t, 512), F32),
        jax.ShapeDtypeStruct((b, t, RET_OUT), MXU_DT),
        jax.ShapeDtypeStruct((b, nc, 2, 128, RET_OUT), F32),
    )
    head_spec = lambda w: pl.BlockSpec((1, MLA_HEADS, TM, w), lambda bi, j: (bi, 0, j, 0))
    tok_spec = lambda w: pl.BlockSpec((1, TM, w), lambda bi, j: (bi, j, 0))
    return pl.pallas_call(
        _inproj_kernel,
        grid=(b, nt),
        in_specs=[
            tok_spec(d),
            pl.BlockSpec((1, 1, N_MOD, d), lambda bi, j: (bi, j // n_lat_tiles, 0, 0)),
            pl.BlockSpec((TM, TAB_W), lambda bi, j: (j, 0)),
            pl.BlockSpec((d, IN_P), const2),
            pl.BlockSpec((1, MLA_Q_LORA), const2),
            pl.BlockSpec((1, MLA_KV_LORA), const2),
            pl.BlockSpec((MLA_Q_LORA, 1024), const2),
            pl.BlockSpec((MLA_KV_LORA, 1024), const2),
            pl.BlockSpec((1, SG_WIDTH), const2),
            pl.BlockSpec((1, SG_WIDTH), const2),
            pl.BlockSpec((SG_GROUPS * CHUNK, CHUNK), const2),
            pl.BlockSpec((CHUNK, SG_WIDTH), const2),
            pl.BlockSpec((2, CHUNK, 128), const3),
            pl.BlockSpec((128, RET_OUT), const2),
        ],
        out_specs=(head_spec(256), head_spec(256), head_spec(MLA_V), tok_spec(SG_WIDTH),
                   tok_spec(512), tok_spec(RET_OUT),
                   pl.BlockSpec((1, cpt, 2, 128, RET_OUT), lambda bi, j: (bi, j, 0, 0, 0))),
        out_shape=out_shape,
        compiler_params=_cparams(("parallel", "parallel")),
        name="input_projection",
    )(x_all, mod_tab, tab, lw["w_in"], lw["q_g"], lw["kv_g"], lw["w_uq"], lw["w_ukv"],
      lw["sg_g"], lw["sg_b"], lw["sg_w"], lw["sg_bias"], lw["kdec"], lw["bd"])


def _attn_kernel(q_ref, k_ref, v_ref, o_ref, *, n_main, tk, tail):
    q = q_ref[0, 0]
    tq = q.shape[0]
    chunks = [(i * tk, tk) for i in range(n_main)] + ([(n_main * tk, tail)] if tail else [])

    def scores(ci):
        start, size = chunks[ci]
        return _dot_nt(q, k_ref[0, 0, start:start + size, :])

    m = jnp.full((tq, 1), -1e30, F32)
    l = jnp.zeros((tq, 1), F32)
    acc = jnp.zeros((tq, MLA_V), F32)
    s_next = scores(0)
    for ci, (start, size) in enumerate(chunks):
        s = s_next
        if ci + 1 < len(chunks):
            s_next = scores(ci + 1)
        m_new = jnp.maximum(m, jnp.max(s, axis=-1, keepdims=True))
        alpha = jnp.exp(m - m_new)
        p = jnp.exp(s - m_new)
        l = alpha * l + jnp.sum(p, axis=-1, keepdims=True)
        acc = alpha * acc + _dot(_mx(p), v_ref[0, 0, start:start + size, :])
        m = m_new
    o_ref[0] = (acc / l).astype(o_ref.dtype)


def mla_attention(q, k, v, s_len, lc):
    b, hn, t, _ = q.shape
    tq = min(ATT_TQ, s_len)
    tk = min(ATT_TK, s_len)
    kv_full = lambda w: pl.BlockSpec((1, 1, t, w), lambda bi, hi, i: (bi, hi, 0, 0))
    out_lat = pl.pallas_call(
        functools.partial(_attn_kernel, n_main=s_len // tk, tk=tk, tail=lc),
        grid=(b, hn, s_len // tq),
        in_specs=[pl.BlockSpec((1, 1, tq, 256), lambda bi, hi, i: (bi, hi, i, 0)),
                  kv_full(256), kv_full(MLA_V)],
        out_specs=pl.BlockSpec((1, tq, MLA_V), lambda bi, hi, i: (bi, i, hi)),
        out_shape=jax.ShapeDtypeStruct((b, t, MLA_OUT), MXU_DT),
        compiler_params=_cparams(("parallel", "parallel", "arbitrary")),
        name="mla_attention_latent",
    )(q, k, v)
    cblk = s_len // lc

    def _ctx_kernel(q_ref, k_ref, v_ref, prev_ref, o_ref):
        del prev_ref
        _attn_kernel(q_ref, k_ref, v_ref, o_ref, n_main=0, tk=tk, tail=lc)

    ctx_spec = lambda w: pl.BlockSpec((1, 1, lc, w), lambda bi, hi: (bi, hi, cblk, 0))
    return pl.pallas_call(
        _ctx_kernel,
        grid=(b, hn),
        in_specs=[ctx_spec(256), ctx_spec(256), ctx_spec(MLA_V), pl.BlockSpec(memory_space=pl.ANY)],
        out_specs=pl.BlockSpec((1, lc, MLA_V), lambda bi, hi: (bi, cblk, hi)),
        out_shape=jax.ShapeDtypeStruct((b, t, MLA_OUT), MXU_DT),
        input_output_aliases={3: 0},
        compiler_params=_cparams(("parallel", "parallel")),
        name="mla_attention_context",
    )(q, k, v, out_lat)


def _ret_scan_kernel(af_ref, ab_ref, cd_ref, sf_ref, sb_ref, st_ref):
    @pl.when(pl.program_id(1) == 0)
    def _():
        st_ref[...] = jnp.zeros_like(st_ref)

    sf_ref[0, 0] = st_ref[0]
    sb_ref[0, 0] = st_ref[1]
    st_ref[0] = st_ref[0] * cd_ref[0] + af_ref[0, 0, 0]
    st_ref[1] = st_ref[1] * cd_ref[1] + ab_ref[0, 0, 0]


def retention_scan(a, cd, n_lat_chunks):
    b, nc = a.shape[:2]
    ncc = nc - n_lat_chunks

    def fwd_chunk(n):
        return jnp.where(n < ncc, n_lat_chunks + n, n - ncc)

    def bwd_chunk(n):
        return jnp.where(n < ncc, nc - 1 - n, n_lat_chunks - 1 - (n - ncc))

    blk = (1, 1, 1, 128, RET_OUT)
    sblk = (1, 1, 128, RET_OUT)
    return pl.pallas_call(
        _ret_scan_kernel,
        grid=(b, nc),
        in_specs=[pl.BlockSpec(blk, lambda bi, n: (bi, fwd_chunk(n), 0, 0, 0)),
                  pl.BlockSpec(blk, lambda bi, n: (bi, bwd_chunk(n), 1, 0, 0)),
                  pl.BlockSpec((2, 1, RET_OUT), lambda bi, n: (0, 0, 0))],
        out_specs=(pl.BlockSpec(sblk, lambda bi, n: (bi, fwd_chunk(n), 0, 0)),
                   pl.BlockSpec(sblk, lambda bi, n: (bi, bwd_chunk(n), 0, 0))),
        out_shape=(jax.ShapeDtypeStruct((b, nc, 128, RET_OUT), F32),
                   jax.ShapeDtypeStruct((b, nc, 128, RET_OUT), F32)),
        scratch_shapes=[pltpu.VMEM((2, 128, RET_OUT), F32)],
        compiler_params=_cparams(("parallel", "arbitrary")),
        name="retention_scan",
    )(a, a, cd)


def _split_dot(x, ones_bd):
    hi = x.astype(BF16)
    lo = (x - hi.astype(F32)).astype(BF16)
    return _dot(hi, ones_bd) + _dot(lo, ones_bd)


def _outproj_kernel(x_ref, mla_ref, sg_ref, retp_ref, rv_ref, sf_ref, sb_ref, mod_ref,
                    m_ref, qdec_ref, seg_ref, w_o_ref, lng_ref, lnb_ref, rw_ref, rb_ref, tri_ref,
                    x1_ref, h2_ref, idx_ref, gate_ref, rank_ref, cnt_ref, carry_ref):
    first = jnp.logical_and(pl.program_id(0) == 0, pl.program_id(1) == 0)

    @pl.when(first)
    def _():
        carry_ref[...] = jnp.zeros_like(carry_ref)

    g32 = _lane_group((CHUNK, 128), RET_QK)
    g64 = _lane_group((CHUNK, RET_OUT), RET_V)
    seg = seg_ref[...]
    ret_rows = []
    for c in range(TM // CHUNK):
        rows = slice(c * CHUNK, (c + 1) * CHUNK)
        rq = retp_ref[0, rows, 0:128]
        rk = _mx(retp_ref[0, rows, 128:256])
        rg = retp_ref[0, rows, 256:512]
        rv = rv_ref[0, rows, :]
        o = (_dot(_mx(rq * qdec_ref[0]), _mx(sf_ref[0, c]))
             + _dot(_mx(rq * qdec_ref[1]), _mx(sb_ref[0, c])))
        for hh in range(RET_HEADS):
            s = _dot_nt(_mx(jnp.where(g32 == hh, rq, 0.0)), rk)
            oh = _dot(_mx(s * m_ref[hh]), rv)
            o = o + jnp.where(g64 == hh, oh, 0.0)
        mean = _split_dot(o, seg) * (1.0 / RET_V)
        oc = o - mean
        var = _split_dot(oc * oc, seg) * (1.0 / RET_V)
        ret_rows.append(oc * lax.rsqrt(var + LN_EPS) * (rg * _sigmoid(rg)))
    ret = jnp.concatenate(ret_rows, axis=0)

    y = (_dot(mla_ref[0], w_o_ref[0:MLA_OUT, :])
         + _dot(sg_ref[0], w_o_ref[MLA_OUT:MLA_OUT + SG_WIDTH, :])
         + _dot(_mx(ret), w_o_ref[MLA_OUT + SG_WIDTH:, :]))
    mod = mod_ref[0, 0]
    x1 = _ln(DEEPNORM_ALPHA * x_ref[0] + mod[2:3] * y) * lng_ref[...] + lnb_ref[...]
    x1_ref[0] = x1
    h2 = x1 * (1.0 + mod[4:5]) + mod[3:4]
    h2_hi = h2.astype(BF16)
    h2_ref[0] = h2_hi.astype(h2_ref.dtype) if MXU_DT == BF16 else h2
    h2_lo = (h2 - h2_hi.astype(F32)).astype(BF16)

    r2 = _dot_nt(rw_ref[...], h2_hi)
    logits = (r2[0:N_EXPERTS] + r2[N_EXPERTS:2 * N_EXPERTS]
              + _dot_nt(rw_ref[0:N_EXPERTS, :], h2_lo) + rb_ref[...])
    e_iota = lax.broadcasted_iota(jnp.int32, logits.shape, 0).astype(F32)
    work = logits
    vals, idxs = [], []
    for _ in range(TOP_K):
        mval = jnp.max(work, axis=0, keepdims=True)
        midx = jnp.min(jnp.where(work == mval, e_iota, float(N_EXPERTS)), axis=0, keepdims=True)
        vals.append(mval)
        idxs.append(midx)
        work = jnp.where(e_iota == midx, -jnp.inf, work)
    ex = [jnp.exp(vv - vals[0]) for vv in vals]
    den = ex[0] + ex[1] + ex[2] + ex[3]
    onehot = jnp.zeros_like(logits)
    for kk in range(TOP_K):
        onehot = onehot + jnp.where(e_iota == idxs[kk], 1.0, 0.0)
    base = carry_ref[:, 0:1] + _dot(onehot.astype(BF16), tri_ref[...])
    for kk in range(TOP_K):
        gate_ref[0, 0, kk:kk + 1, :] = ex[kk] / den
        idx_ref[0, 0, kk:kk + 1, :] = idxs[kk].astype(jnp.int32)
        rk_k = jnp.sum(jnp.where(e_iota == idxs[kk], base, 0.0), axis=0, keepdims=True)
        rank_ref[0, 0, kk:kk + 1, :] = rk_k.astype(jnp.int32)
    carry_ref[...] = carry_ref[...] + jnp.sum(onehot, axis=1, keepdims=True)
    cnt_ref[...] = carry_ref[...].astype(jnp.int32)


def output_projection(x_all, mla, sg, retp, rv, sf, sb, mod_tab, lw, n_lat_tiles):
    b, t, d = x_all.shape
    nt = t // TM
    cpt = TM // CHUNK
    const2 = lambda bi, j: (0, 0)
    const3 = lambda bi, j: (0, 0, 0)
    tok_spec = lambda w: pl.BlockSpec((1, TM, w), lambda bi, j: (bi, j, 0))
    st_spec = pl.BlockSpec((1, cpt, 128, RET_OUT), lambda bi, j: (bi, j, 0, 0))
    route_spec = pl.BlockSpec((1, 1, TOP_K, TM), lambda bi, j: (bi, j, 0, 0))
    route_shape = lambda dt: jax.ShapeDtypeStruct((b, nt, TOP_K, TM), dt)
    return pl.pallas_call(
        _outproj_kernel,
        grid=(b, nt),
        in_specs=[
            tok_spec(d), tok_spec(MLA_OUT), tok_spec(SG_WIDTH), tok_spec(512), tok_spec(RET_OUT),
            st_spec, st_spec,
            pl.BlockSpec((1, 1, N_MOD, d), lambda bi, j: (bi, j // n_lat_tiles, 0, 0)),
            pl.BlockSpec((RET_HEADS, CHUNK, CHUNK), const3),
            pl.BlockSpec((2, CHUNK, 128), const3),
            pl.BlockSpec((RET_OUT, RET_OUT), const2),
            pl.BlockSpec((d, d), const2),
            pl.BlockSpec((1, d), const2),
            pl.BlockSpec((1, d), const2),
            pl.BlockSpec((2 * N_EXPERTS, d), const2),
            pl.BlockSpec((N_EXPERTS, 1), const2),
            pl.BlockSpec((TM, TM), const2),
        ],
        out_specs=(tok_spec(d), tok_spec(d), route_spec, route_spec, route_spec,
                   pl.BlockSpec((N_EXPERTS, 128), const2)),
        out_shape=(jax.ShapeDtypeStruct((b, t, d), F32),
                   jax.ShapeDtypeStruct((b, t, d), MXU_DT),
                   route_shape(jnp.int32), route_shape(F32), route_shape(jnp.int32),
                   jax.ShapeDtypeStruct((N_EXPERTS, 128), jnp.int32)),
        scratch_shapes=[pltpu.VMEM((N_EXPERTS, 128), F32)],
        compiler_params=_cparams(("arbitrary", "arbitrary")),
        name="output_projection",
    )(x_all, mla, sg, retp, rv, sf, sb, mod_tab, lw["ret_m"], lw["qdec"], lw["seg"], lw["w_o"],
      lw["ln1_g"], lw["ln1_b"], lw["router_w"], lw["router_b"], lw["tri"])


_DEINT = 256


def _expert_kernel(be_ref, nu_ref, x_ref, wgu_ref, bg_ref, bl_ref, wd_ref, bd_ref, perm_ref, y_ref,
                   wg_s, wl_s, wd_s):
    i = pl.program_id(0)
    active = i < nu_ref[0]
    fresh = jnp.logical_or(i == 0, be_ref[i] != be_ref[jnp.maximum(i - 1, 0)])

    @pl.when(jnp.logical_and(active, fresh))
    def _():
        half = _DEINT // 2
        for c in range(2 * D_EXPERT // _DEINT):
            r = _dot(_mx(wgu_ref[0, :, _DEINT * c:_DEINT * (c + 1)]), perm_ref[...])
            wg_s[:, half * c:half * (c + 1)] = r[:, :half].astype(wg_s.dtype)
            wl_s[:, half * c:half * (c + 1)] = r[:, half:].astype(wl_s.dtype)
        wd_s[...] = wd_ref[0].astype(wd_s.dtype)

    @pl.when(active)
    def _():
        xb = x_ref[...]
        glu = jnp.minimum(_dot(xb, wg_s[...]) + bg_ref[0], SWIGLU_LIMIT)
        lin = jnp.clip(_dot(xb, wl_s[...]) + bl_ref[0], -SWIGLU_LIMIT, SWIGLU_LIMIT)
        act = glu * _sigmoid(SWIGLU_ALPHA * glu) * (lin + 1.0)
        y_ref[...] = _dot(_mx(act), wd_s[...]) + bd_ref[0]

    @pl.when(jnp.logical_not(active))
    def _():
        y_ref[...] = jnp.zeros_like(y_ref)


def expert_ffn(xg, block_e, n_used, lw):
    cap, d = xg.shape
    nb = cap // MOE_BM
    de = D_EXPERT
    xmap = lambda i, be, nu: (jnp.minimum(i, nu[0] - 1), 0)
    wmap = lambda i, be, nu: (be[i], 0, 0)
    grid_spec = pltpu.PrefetchScalarGridSpec(
        num_scalar_prefetch=2,
        grid=(nb,),
        in_specs=[pl.BlockSpec((MOE_BM, d), xmap),
                  pl.BlockSpec((1, d, 2 * de), wmap),
                  pl.BlockSpec((1, 1, de), wmap), pl.BlockSpec((1, 1, de), wmap),
                  pl.BlockSpec((1, de, d), wmap), pl.BlockSpec((1, 1, d), wmap),
                  pl.BlockSpec((_DEINT, _DEINT), lambda i, be, nu: (0, 0))],
        out_specs=pl.BlockSpec((MOE_BM, d), lambda i, be, nu: (i, 0)),
        scratch_shapes=[pltpu.VMEM((d, de), MXU_DT), pltpu.VMEM((d, de), MXU_DT),
                        pltpu.VMEM((de, d), MXU_DT)],
    )
    return pl.pallas_call(
        _expert_kernel,
        grid_spec=grid_spec,
        out_shape=jax.ShapeDtypeStruct((cap, d), F32),
        compiler_params=pltpu.CompilerParams(dimension_semantics=("arbitrary",),
                                             vmem_limit_bytes=EXPERT_VMEM_LIMIT),
        name="expert_ffn",
    )(block_e, n_used, xg, lw["w_gate_up"], lw["b_glu"], lw["b_lin"], lw["w_down"], lw["b_down"],
      lw["deint"])


def _ln2_kernel(x_ref, f_ref, mod_ref, g_ref, b_ref, o_ref):
    mod = mod_ref[0, 0]
    o_ref[0] = _ln(DEEPNORM_ALPHA * x_ref[0] + mod[5:6] * f_ref[0]) * g_ref[...] + b_ref[...]


def deepnorm2(x1, f, mod_tab, lw, n_lat_tiles):
    b, t, d = x1.shape
    tok = pl.BlockSpec((1, TM, d), lambda bi, j: (bi, j, 0))
    vec = pl.BlockSpec((1, d), lambda bi, j: (0, 0))
    return pl.pallas_call(
        _ln2_kernel,
        grid=(b, t // TM),
        in_specs=[tok, tok, pl.BlockSpec((1, 1, N_MOD, d), lambda bi, j: (bi, j // n_lat_tiles, 0, 0)), vec, vec],
        out_specs=tok,
        out_shape=jax.ShapeDtypeStruct((b, t, d), F32),
        compiler_params=_cparams(("parallel", "parallel")),
        name="deepnorm2",
    )(x1, f, mod_tab, lw["ln2_g"], lw["ln2_b"])


def _rotation_tables(s_len, lc):
    rows = s_len // GRID_W
    row = jnp.broadcast_to(jnp.arange(rows, dtype=F32)[:, None], (rows, GRID_W)).reshape(-1)
    col = jnp.broadcast_to(jnp.arange(GRID_W, dtype=F32)[None, :], (rows, GRID_W)).reshape(-1)
    inv = ROPE_BASE ** (-jnp.arange(ROPE_AXIS_FREQS, dtype=F32) / ROPE_AXIS_FREQS)
    ar, ac = row[:, None] * inv, col[:, None] * inv
    c64 = jnp.concatenate([jnp.cos(ar), jnp.cos(ar), jnp.cos(ac), jnp.cos(ac)], axis=1)
    s64 = jnp.concatenate([-jnp.sin(ar), jnp.sin(ar), -jnp.sin(ac), jnp.sin(ac)], axis=1)
    c64 = jnp.concatenate([c64, jnp.ones((lc, 64), F32)], axis=0)
    s64 = jnp.concatenate([s64, jnp.zeros((lc, 64), F32)], axis=0)
    half = RET_QK // 2
    pos = jnp.concatenate([lc + jnp.arange(s_len, dtype=F32), jnp.arange(lc, dtype=F32)])
    inv_r = 1.0 / (RET_ROPE_BASE ** jnp.linspace(0.0, 1.0, half, dtype=F32))
    ang = pos[:, None] * inv_r
    rc = jnp.tile(jnp.concatenate([jnp.cos(ang), jnp.cos(ang)], axis=1), (1, RET_HEADS))
    rs = jnp.tile(jnp.concatenate([-jnp.sin(ang), jnp.sin(ang)], axis=1), (1, RET_HEADS))
    qs = RET_QK ** -0.5
    return jnp.concatenate([
        jnp.tile(c64, (1, MLA_HEADS)) * MLA_SCALE, jnp.tile(s64, (1, MLA_HEADS)) * MLA_SCALE,
        c64, s64, rc * qs, rs * qs, rc, rs], axis=1)


def _in_perm():
    a = np.arange
    return np.concatenate([
        a(0, 640), a(704, 1216), a(1216, 1344), a(1344, 1472),
        1216 + _swap16(a(128)), 1344 + _swap16(a(128)), a(1472, 1984),
        640 + a(64), 640 + _swap16(a(64))])


def _uq_perm():
    a = np.arange
    nope = [h * MLA_QK + a(MLA_NOPE) for h in range(MLA_HEADS)]
    rope = [h * MLA_QK + MLA_NOPE + a(MLA_ROPE) for h in range(MLA_HEADS)]
    part = [h * MLA_QK + MLA_NOPE + _swap16(a(MLA_ROPE)) for h in range(MLA_HEADS)]
    return np.concatenate(nope + rope + part)


def _layer_weights(p):
    nl = p["w_in"].shape[0]
    lgf = jax.nn.log_sigmoid(p["ret_decay_fwd"].astype(F32))
    lgb = jax.nn.log_sigmoid(p["ret_decay_bwd"].astype(F32))
    h128 = np.arange(128) // RET_QK
    h256 = np.arange(RET_OUT) // RET_V
    a = jnp.arange(CHUNK, dtype=F32)[None, :, None]
    lf, lb = lgf[:, h128][:, None, :], lgb[:, h128][:, None, :]
    kdec = jnp.stack([jnp.exp(lf * (CHUNK - 1.0 - a)), jnp.exp(lb * a)], axis=1)
    qdec = jnp.stack([jnp.exp(lf * (a + 1.0)), jnp.exp(lb * (CHUNK - a))], axis=1)
    i = jnp.arange(CHUNK, dtype=F32)[:, None]
    j = jnp.arange(CHUNK, dtype=F32)[None, :]
    dif = (i - j)[None, None]
    ret_m = jnp.where(dif >= 0, jnp.exp(lgf[:, :, None, None] * jnp.maximum(dif, 0.0)),
                      jnp.exp(lgb[:, :, None, None] * jnp.maximum(-dif, 0.0)))
    cd = jnp.stack([jnp.exp(lgf[:, h256] * CHUNK), jnp.exp(lgb[:, h256] * CHUNK)], axis=1)[:, :, None, :]
    bd = (h128[:, None] == h256[None, :]).astype(np.float32)
    seg = (h256[:, None] == h256[None, :]).astype(np.float32)
    tri = (np.arange(TM)[:, None] < np.arange(TM)[None, :]).astype(np.float32)
    jj = np.arange(_DEINT // 2)
    deint = np.zeros((_DEINT, _DEINT), np.float32)
    deint[2 * jj, jj] = 1.0
    deint[2 * jj + 1, _DEINT // 2 + jj] = 1.0
    rw_t = jnp.swapaxes(p["router_w"], 1, 2)
    rw_hi = rw_t.astype(BF16)
    rw_lo = (rw_t - rw_hi.astype(F32)).astype(BF16)
    sg_bias = jnp.repeat(jnp.swapaxes(p["sg_b"], 1, 2), SG_WIDTH // SG_GROUPS, axis=2)
    wgu = p["w_gate_up"]
    bgu = p["b_gate_up"]
    return {
        "w_in": p["w_in"][:, :, _in_perm()].astype(MXU_DT),
        "q_g": p["mla_q_norm_g"][:, None, :], "kv_g": p["mla_kv_norm_g"][:, None, :],
        "w_uq": p["mla_w_uq"][:, :, _uq_perm()].astype(MXU_DT),
        "w_ukv": p["mla_w_ukv"].astype(MXU_DT),
        "sg_g": p["sg_norm_g"][:, None, :], "sg_b": p["sg_norm_b"][:, None, :],
        "sg_w": p["sg_w"].reshape(nl, SG_GROUPS * CHUNK, CHUNK).astype(MXU_DT),
        "sg_bias": sg_bias,
        "kdec": kdec, "qdec": qdec, "ret_m": ret_m, "cd": cd,
        "bd": jnp.broadcast_to(jnp.asarray(bd), (nl,) + bd.shape),
        "seg": jnp.broadcast_to(jnp.asarray(seg, BF16), (nl,) + seg.shape),
        "tri": jnp.broadcast_to(jnp.asarray(tri, BF16), (nl,) + tri.shape),
        "w_o": p["w_o"].astype(MXU_DT),
        "ln1_g": p["ln1_g"][:, None, :], "ln1_b": p["ln1_b"][:, None, :],
        "ln2_g": p["ln2_g"][:, None, :], "ln2_b": p["ln2_b"][:, None, :],
        "router_w": jnp.concatenate([rw_hi, rw_lo], axis=1),
        "router_b": p["router_b"][:, :, None],
        "w_gate_up": wgu, "b_glu": bgu[:, :, None, 0::2], "b_lin": bgu[:, :, None, 1::2],
        "w_down": p["w_down"], "b_down": p["b_down"][:, :, None, :],
        "deint": jnp.broadcast_to(jnp.asarray(deint, MXU_DT), (nl,) + deint.shape),
    }


def _route(idx, rank, counts):
    n_assign = idx.shape[1] * TOP_K
    nb = -(-(n_assign + N_EXPERTS * (MOE_BM - 1)) // MOE_BM)
    padded = (counts + MOE_BM - 1) // MOE_BM * MOE_BM
    pad_end = jnp.cumsum(padded)
    dest = (pad_end - padded)[idx] + rank
    blk_start = jnp.arange(nb, dtype=jnp.int32) * MOE_BM
    block_e = jnp.minimum(jnp.sum((pad_end[None, :] <= blk_start[:, None]).astype(jnp.int32), axis=1),
                          N_EXPERTS - 1)
    n_used = (pad_end[-1] // MOE_BM).astype(jnp.int32).reshape(1)
    return dest.astype(jnp.int32), block_e, n_used, nb


def _dispatch(h2, dest, nb):
    n, d = h2.shape
    tok = jnp.tile(jnp.arange(n, dtype=jnp.int32), TOP_K)
    tok_buf = jnp.zeros((nb * MOE_BM,), jnp.int32).at[dest.reshape(-1)].set(tok, unique_indices=True)
    return h2[tok_buf]


def _combine(y, dest, gates):
    return jnp.sum(y[dest] * gates[..., None], axis=0)


def kernel(x, c, ctx, c_ctx, ada_w, ada_b, w_in, mla_q_norm_g, mla_kv_norm_g, mla_w_uq, mla_w_ukv,
           sg_norm_g, sg_norm_b, sg_w, sg_b, ret_decay_fwd, ret_decay_bwd, w_o, ln1_g, ln1_b,
           router_w, router_b, w_gate_up, b_gate_up, w_down, b_down, ln2_g, ln2_b):
    b, s_len, d = x.shape
    lc = ctx.shape[1]
    assert d == D_MODEL and lc % TM == 0 and s_len % lc == 0 and s_len % GRID_W == 0
    assert b + 1 <= 8
    t = s_len + lc
    n_lat_tiles = s_len // TM
    params = dict(w_in=w_in, mla_q_norm_g=mla_q_norm_g, mla_kv_norm_g=mla_kv_norm_g, mla_w_uq=mla_w_uq,
                  mla_w_ukv=mla_w_ukv, sg_norm_g=sg_norm_g, sg_norm_b=sg_norm_b, sg_w=sg_w, sg_b=sg_b,
                  ret_decay_fwd=ret_decay_fwd, ret_decay_bwd=ret_decay_bwd, w_o=w_o, ln1_g=ln1_g,
                  ln1_b=ln1_b, router_w=router_w, router_b=router_b, w_gate_up=w_gate_up,
                  b_gate_up=b_gate_up, w_down=w_down, b_down=b_down, ln2_g=ln2_g, ln2_b=ln2_b)
    lws = _layer_weights(params)
    tab = _rotation_tables(s_len, lc)

    c_rows = jnp.concatenate([c, c_ctx[None, :], jnp.zeros((8 - b - 1, d), F32)], axis=0)
    mod = ada_modulation(c_rows, ada_w, ada_b).reshape(DEPTH, 8, N_MOD, d)
    mod_tab = jnp.stack([mod[:, :b], jnp.broadcast_to(mod[:, b:b + 1], (DEPTH, b, N_MOD, d))], axis=2)

    x_all = jnp.concatenate([x, ctx], axis=1)
    for li in range(DEPTH):
        lw = {k: v[li] for k, v in lws.items()}
        mt = mod_tab[li]
        q, k, v, sg, retp, rv, a = input_projection(x_all, mt, tab, lw, n_lat_tiles)
        mla = mla_attention(q, k, v, s_len, lc)
        sf, sb = retention_scan(a, lw["cd"], s_len // CHUNK)
        x1, h2, idx, gates, rank, cnt = output_projection(
            x_all, mla, sg, retp, rv, sf, sb, mt, lw, n_lat_tiles)
        to_tok = lambda z: z.transpose(2, 0, 1, 3).reshape(TOP_K, b * t)
        idx, gates, rank = to_tok(idx), to_tok(gates), to_tok(rank)
        dest, block_e, n_used, nb = _route(idx, rank, cnt[:, 0])
        xg = _dispatch(h2.reshape(b * t, d), dest, nb)
        y = expert_ffn(xg, block_e, n_used, lw)
        f = _combine(y, dest, gates).reshape(b, t, d)
        x_all = deepnorm2(x1, f, mt, lw, n_lat_tiles)
    return x_all[:, :s_len]
```

```python
import functools

import numpy as np
import jax
import jax.numpy as jnp
from jax import lax
from jax.experimental import pallas as pl
from jax.experimental.pallas import tpu as pltpu
from jax.experimental.pallas import tpu_sc as plsc

F32 = jnp.float32
BF16 = jnp.bfloat16
MXU_DT = BF16
PACK_ROWS = True

D_MODEL = 1024
DEPTH = 4
GRID_W = 64
MLA_HEADS = 4
MLA_NOPE = 128
MLA_ROPE = 64
MLA_V = 128
MLA_Q_LORA = 384
MLA_KV_LORA = 256
MLA_QK = MLA_NOPE + MLA_ROPE
MLA_SCALE = MLA_QK ** -0.5
ROPE_BASE = 10000.0
ROPE_AXIS_FREQS = MLA_ROPE // 4
SG_GROUPS = 4
SG_WIDTH = 256
SG_CHUNK = 128
RET_HEADS = 4
RET_QK = 32
RET_V = 64
RET_CHUNK = 128
RET_ROPE_BASE = 10000.0
N_EXPERTS = 32
TOP_K = 4
D_EXPERT = 1024
SWIGLU_LIMIT = 7.0
SWIGLU_ALPHA = 1.702
N_MOD = 6
LN_EPS = 1e-5
RMS_EPS = 1e-6
DEEPNORM_ALPHA = (2 * DEPTH) ** 0.25
MLA_OUT = MLA_HEADS * MLA_V
RET_OUT = RET_HEADS * RET_V

TM = 256
CHUNK = 128
MOE_BM = 256
SC_CHUNK = 32
ATT_TQ = 512
ATT_TK = 1024
VMEM_LIMIT = 48 * 2 ** 20
EXPERT_VMEM_LIMIT = 56 * 2 ** 20

_O_CQ, _O_CKV, _O_SGU, _O_SGV = 0, 384, 640, 896
_O_RQ, _O_RK, _O_RQS, _O_RKS, _O_RV, _O_RG, _O_KR = 1152, 1280, 1408, 1536, 1664, 1920, 2176
IN_P = 2304
_T_QC, _T_QS, _T_KCS, _T_RQC, _T_RQS, _T_RKC, _T_RKS = 0, 256, 512, 640, 768, 896, 1024
TAB_W = 1152


def _cparams(sem):
    return pltpu.CompilerParams(dimension_semantics=sem, vmem_limit_bytes=VMEM_LIMIT)


def _dot(a, b):
    return jnp.dot(a, b, preferred_element_type=F32)


def _dot_nt(a, b):
    return lax.dot_general(a, b, (((1,), (1,)), ((), ())), preferred_element_type=F32)


def _mx(a):
    return a.astype(MXU_DT)


def _swap16(j):
    return (j // 32) * 32 + ((j % 32) + 16) % 32


_ERF_ALPHA = (-2.72614225801306e-10, 2.77068142495902e-08, -2.10102402082508e-06,
              -5.69250639462346e-05, -7.34990630326855e-04, -2.95459980854025e-03,
              -1.60960333262415e-02)
_ERF_BETA = (-1.45660718464996e-05, -2.13374055278905e-04, -1.68282697438203e-03,
             -7.37332916720468e-03, -1.42647390514189e-02)


def _erf(x):
    x = jnp.clip(x, -4.0, 4.0)
    x2 = x * x
    p = jnp.full_like(x, _ERF_ALPHA[0])
    for c in _ERF_ALPHA[1:]:
        p = p * x2 + c
    q = jnp.full_like(x, _ERF_BETA[0])
    for c in _ERF_BETA[1:]:
        q = q * x2 + c
    return x * p / q


def _gelu(x):
    return 0.5 * x * (1.0 + _erf(x * 0.7071067811865476))


def _sigmoid(x):
    return 1.0 / (1.0 + jnp.exp(-x))


def _ln(x):
    xc = x - jnp.mean(x, axis=-1, keepdims=True)
    return xc * lax.rsqrt(jnp.mean(xc * xc, axis=-1, keepdims=True) + LN_EPS)


def _lane_group(shape, width):
    return lax.broadcasted_iota(jnp.int32, shape, len(shape) - 1) // width


def _ada_kernel(c_ref, w_ref, b_ref, o_ref):
    c = c_ref[...]
    o_ref[0] = _dot(c * _sigmoid(c), w_ref[0]) + b_ref[0]


def ada_modulation(c_rows, ada_w, ada_b):
    nl, d, n = ada_w.shape
    tn = 1536
    return pl.pallas_call(
        _ada_kernel,
        grid=(nl, n // tn),
        in_specs=[pl.BlockSpec((8, d), lambda l, j: (0, 0)),
                  pl.BlockSpec((1, d, tn), lambda l, j: (l, 0, j)),
                  pl.BlockSpec((1, 1, tn), lambda l, j: (l, 0, j))],
        out_specs=pl.BlockSpec((1, 8, tn), lambda l, j: (l, 0, j)),
        out_shape=jax.ShapeDtypeStruct((nl, 8, n), F32),
        compiler_params=_cparams(("arbitrary", "arbitrary")),
        name="ada_modulation",
    )(c_rows, ada_w, ada_b.reshape(nl, 1, n))


def _inproj_kernel(x_ref, mod_ref, tab_ref, w_in_ref, qg_ref, kvg_ref, w_uq_ref, w_ukv_ref,
                   sgg_ref, sgb_ref, sgw_ref, sgbias_ref, kdec_ref, bd_ref,
                   q_ref, k_ref, v_ref, sg_ref, retp_ref, rv_ref, a_ref):
    x = x_ref[0]
    mod = mod_ref[0, 0]
    h = x * (1.0 + mod[1:2]) + mod[0:1]
    p = _dot(_mx(h), w_in_ref[...])
    tab = tab_ref[...]

    cq = p[:, _O_CQ:_O_CQ + MLA_Q_LORA]
    cq = cq * lax.rsqrt(jnp.mean(cq * cq, axis=-1, keepdims=True) + RMS_EPS) * qg_ref[...]
    qa = _dot(_mx(cq), w_uq_ref[...])
    rot = (qa[:, 512:768] * tab[:, _T_QC:_T_QC + 256]
           + qa[:, 768:1024] * tab[:, _T_QS:_T_QS + 256])
    for hh in range(MLA_HEADS):
        q_ref[0, hh, :, 0:128] = (qa[:, 128 * hh:128 * hh + 128] * MLA_SCALE).astype(q_ref.dtype)
        g = hh // 2
        q_ref[0, hh, :, 128:256] = rot[:, 128 * g:128 * g + 128].astype(q_ref.dtype)

    ckv = p[:, _O_CKV:_O_CKV + MLA_KV_LORA]
    ckv = ckv * lax.rsqrt(jnp.mean(ckv * ckv, axis=-1, keepdims=True) + RMS_EPS) * kvg_ref[...]
    kv = _dot(_mx(ckv), w_ukv_ref[...])
    t = p[:, _O_KR:_O_KR + 128] * tab[:, _T_KCS:_T_KCS + 128]
    u = t + pltpu.roll(t, 64, axis=1)
    low = lax.broadcasted_iota(jnp.int32, u.shape, 1) < 64
    kx = (jnp.where(low, u, 0.0), jnp.where(low, 0.0, u))
    for hh in range(MLA_HEADS):
        k_ref[0, hh, :, 0:128] = kv[:, 256 * hh:256 * hh + 128].astype(k_ref.dtype)
        k_ref[0, hh, :, 128:256] = kx[hh % 2].astype(k_ref.dtype)
        v_ref[0, hh] = kv[:, 256 * hh + 128:256 * hh + 256].astype(v_ref.dtype)

    gu = _gelu(p[:, _O_SGU:_O_SGU + SG_WIDTH])
    gv = _ln(_gelu(p[:, _O_SGV:_O_SGV + SG_WIDTH])) * sgg_ref[...] + sgb_ref[...]
    gvm = _mx(gv)
    grp = _lane_group((CHUNK, SG_WIDTH), SG_WIDTH // SG_GROUPS)
    for c in range(TM // CHUNK):
        rows = slice(c * CHUNK, (c + 1) * CHUNK)
        res = _dot(sgw_ref[...], gvm[rows])
        mixed = sgbias_ref[...]
        for g in range(SG_GROUPS):
            mixed = mixed + jnp.where(grp == g, res[g * CHUNK:(g + 1) * CHUNK], 0.0)
        sg_ref[0, rows, :] = (gu[rows] * mixed).astype(sg_ref.dtype)

    rq = (p[:, _O_RQ:_O_RQ + 128] * tab[:, _T_RQC:_T_RQC + 128]
          + p[:, _O_RQS:_O_RQS + 128] * tab[:, _T_RQS:_T_RQS + 128])
    rk = (p[:, _O_RK:_O_RK + 128] * tab[:, _T_RKC:_T_RKC + 128]
          + p[:, _O_RKS:_O_RKS + 128] * tab[:, _T_RKS:_T_RKS + 128])
    rv = p[:, _O_RV:_O_RV + RET_OUT]
    retp_ref[0, :, 0:128] = rq
    retp_ref[0, :, 128:256] = rk
    retp_ref[0, :, 256:512] = p[:, _O_RG:_O_RG + RET_OUT]
    rvm = _mx(rv)
    rv_ref[0] = rvm.astype(rv_ref.dtype)
    bd = bd_ref[...]
    for c in range(TM // CHUNK):
        rows = slice(c * CHUNK, (c + 1) * CHUNK)
        for d in range(2):
            kd_t = _mx((rk[rows] * kdec_ref[d]).T)
            a_ref[0, c, d] = _dot(kd_t, rvm[rows]) * bd


def input_projection(x_all, mod_tab, tab, lw, n_lat_tiles):
    b, t, d = x_all.shape
    nt = t // TM
    nc = t // CHUNK
    cpt = TM // CHUNK
    const2 = lambda bi, j: (0, 0)
    const3 = lambda bi, j: (0, 0, 0)
    out_shape = (
        jax.ShapeDtypeStruct((b, MLA_HEADS, t, 256), MXU_DT),
        jax.ShapeDtypeStruct((b, MLA_HEADS, t, 256), MXU_DT),
        jax.ShapeDtypeStruct((b, MLA_HEADS, t, MLA_V), MXU_DT),
        jax.ShapeDtypeStruct((b, t, SG_WIDTH), MXU_DT),
        jax.ShapeDtypeStruct((b, t, 512), F32),
        jax.ShapeDtypeStruct((b, t, RET_OUT), MXU_DT),
        jax.ShapeDtypeStruct((b, nc, 2, 128, RET_OUT), F32),
    )
    head_spec = lambda w: pl.BlockSpec((1, MLA_HEADS, TM, w), lambda bi, j: (bi, 0, j, 0))
    tok_spec = lambda w: pl.BlockSpec((1, TM, w), lambda bi, j: (bi, j, 0))
    return pl.pallas_call(
        _inproj_kernel,
        grid=(b, nt),
        in_specs=[
            tok_spec(d),
            pl.BlockSpec((1, 1, N_MOD, d), lambda bi, j: (bi, j // n_lat_tiles, 0, 0)),
            pl.BlockSpec((TM, TAB_W), lambda bi, j: (j, 0)),
            pl.BlockSpec((d, IN_P), const2),
            pl.BlockSpec((1, MLA_Q_LORA), const2),
            pl.BlockSpec((1, MLA_KV_LORA), const2),
            pl.BlockSpec((MLA_Q_LORA, 1024), const2),
            pl.BlockSpec((MLA_KV_LORA, 1024), const2),
            pl.BlockSpec((1, SG_WIDTH), const2),
            pl.BlockSpec((1, SG_WIDTH), const2),
            pl.BlockSpec((SG_GROUPS * CHUNK, CHUNK), const2),
            pl.BlockSpec((CHUNK, SG_WIDTH), const2),
            pl.BlockSpec((2, CHUNK, 128), const3),
            pl.BlockSpec((128, RET_OUT), const2),
        ],
        out_specs=(head_spec(256), head_spec(256), head_spec(MLA_V), tok_spec(SG_WIDTH),
                   tok_spec(512), tok_spec(RET_OUT),
                   pl.BlockSpec((1, cpt, 2, 128, RET_OUT), lambda bi, j: (bi, j, 0, 0, 0))),
        out_shape=out_shape,
        compiler_params=_cparams(("parallel", "parallel")),
        name="input_projection",
    )(x_all, mod_tab, tab, lw["w_in"], lw["q_g"], lw["kv_g"], lw["w_uq"], lw["w_ukv"],
      lw["sg_g"], lw["sg_b"], lw["sg_w"], lw["sg_bias"], lw["kdec"], lw["bd"])


def _attn_kernel(q_ref, k_ref, v_ref, o_ref, *, n_main, tk, tail):
    q = q_ref[0, 0]
    tq = q.shape[0]
    chunks = [(i * tk, tk) for i in range(n_main)] + ([(n_main * tk, tail)] if tail else [])

    def scores(ci):
        start, size = chunks[ci]
        return _dot_nt(q, k_ref[0, 0, start:start + size, :])

    m = jnp.full((tq, 1), -1e30, F32)
    l = jnp.zeros((tq, 1), F32)
    acc = jnp.zeros((tq, MLA_V), F32)
    s_next = scores(0)
    for ci, (start, size) in enumerate(chunks):
        s = s_next
        if ci + 1 < len(chunks):
            s_next = scores(ci + 1)
        m_new = jnp.maximum(m, jnp.max(s, axis=-1, keepdims=True))
        alpha = jnp.exp(m - m_new)
        p = jnp.exp(s - m_new)
        l = alpha * l + jnp.sum(p, axis=-1, keepdims=True)
        acc = alpha * acc + _dot(_mx(p), v_ref[0, 0, start:start + size, :])
        m = m_new
    o_ref[0] = (acc / l).astype(o_ref.dtype)


def mla_attention(q, k, v, s_len, lc):
    b, hn, t, _ = q.shape
    tq = min(ATT_TQ, s_len)
    tk = min(ATT_TK, s_len)
    kv_full = lambda w: pl.BlockSpec((1, 1, t, w), lambda bi, hi, i: (bi, hi, 0, 0))
    out_lat = pl.pallas_call(
        functools.partial(_attn_kernel, n_main=s_len // tk, tk=tk, tail=lc),
        grid=(b, hn, s_len // tq),
        in_specs=[pl.BlockSpec((1, 1, tq, 256), lambda bi, hi, i: (bi, hi, i, 0)),
                  kv_full(256), kv_full(MLA_V)],
        out_specs=pl.BlockSpec((1, tq, MLA_V), lambda bi, hi, i: (bi, i, hi)),
        out_shape=jax.ShapeDtypeStruct((b, t, MLA_OUT), MXU_DT),
        compiler_params=_cparams(("parallel", "parallel", "arbitrary")),
        name="mla_attention_latent",
    )(q, k, v)
    cblk = s_len // lc

    def _ctx_kernel(q_ref, k_ref, v_ref, prev_ref, o_ref):
        del prev_ref
        _attn_kernel(q_ref, k_ref, v_ref, o_ref, n_main=0, tk=tk, tail=lc)

    ctx_spec = lambda w: pl.BlockSpec((1, 1, lc, w), lambda bi, hi: (bi, hi, cblk, 0))
    return pl.pallas_call(
        _ctx_kernel,
        grid=(b, hn),
        in_specs=[ctx_spec(256), ctx_spec(256), ctx_spec(MLA_V), pl.BlockSpec(memory_space=pl.ANY)],
        out_specs=pl.BlockSpec((1, lc, MLA_V), lambda bi, hi: (bi, cblk, hi)),
        out_shape=jax.ShapeDtypeStruct((b, t, MLA_OUT), MXU_DT),
        input_output_aliases={3: 0},
        compiler_params=_cparams(("parallel", "parallel")),
        name="mla_attention_context",
    )(q, k, v, out_lat)


def _ret_scan_kernel(af_ref, ab_ref, cd_ref, sf_ref, sb_ref, st_ref):
    @pl.when(pl.program_id(1) == 0)
    def _():
        st_ref[...] = jnp.zeros_like(st_ref)

    sf_ref[0, 0] = st_ref[0]
    sb_ref[0, 0] = st_ref[1]
    st_ref[0] = st_ref[0] * cd_ref[0] + af_ref[0, 0, 0]
    st_ref[1] = st_ref[1] * cd_ref[1] + ab_ref[0, 0, 0]


def retention_scan(a, cd, n_lat_chunks):
    b, nc = a.shape[:2]
    ncc = nc - n_lat_chunks

    def fwd_chunk(n):
        return jnp.where(n < ncc, n_lat_chunks + n, n - ncc)

    def bwd_chunk(n):
        return jnp.where(n < ncc, nc - 1 - n, n_lat_chunks - 1 - (n - ncc))

    blk = (1, 1, 1, 128, RET_OUT)
    sblk = (1, 1, 128, RET_OUT)
    return pl.pallas_call(
        _ret_scan_kernel,
        grid=(b, nc),
        in_specs=[pl.BlockSpec(blk, lambda bi, n: (bi, fwd_chunk(n), 0, 0, 0)),
                  pl.BlockSpec(blk, lambda bi, n: (bi, bwd_chunk(n), 1, 0, 0)),
                  pl.BlockSpec((2, 1, RET_OUT), lambda bi, n: (0, 0, 0))],
        out_specs=(pl.BlockSpec(sblk, lambda bi, n: (bi, fwd_chunk(n), 0, 0)),
                   pl.BlockSpec(sblk, lambda bi, n: (bi, bwd_chunk(n), 0, 0))),
        out_shape=(jax.ShapeDtypeStruct((b, nc, 128, RET_OUT), F32),
                   jax.ShapeDtypeStruct((b, nc, 128, RET_OUT), F32)),
        scratch_shapes=[pltpu.VMEM((2, 128, RET_OUT), F32)],
        compiler_params=_cparams(("parallel", "arbitrary")),
        name="retention_scan",
    )(a, a, cd)


def _split_dot(x, ones_bd):
    hi = x.astype(BF16)
    lo = (x - hi.astype(F32)).astype(BF16)
    return _dot(hi, ones_bd) + _dot(lo, ones_bd)


def _outproj_kernel(x_ref, mla_ref, sg_ref, retp_ref, rv_ref, sf_ref, sb_ref, mod_ref,
                    m_ref, qdec_ref, seg_ref, w_o_ref, lng_ref, lnb_ref, rw_ref, rb_ref, tri_ref,
                    x1_ref, h2_ref, idx_ref, gate_ref, rank_ref, cnt_ref, carry_ref):
    first = jnp.logical_and(pl.program_id(0) == 0, pl.program_id(1) == 0)

    @pl.when(first)
    def _():
        carry_ref[...] = jnp.zeros_like(carry_ref)

    g32 = _lane_group((CHUNK, 128), RET_QK)
    g64 = _lane_group((CHUNK, RET_OUT), RET_V)
    seg = seg_ref[...]
    ret_rows = []
    for c in range(TM // CHUNK):
        rows = slice(c * CHUNK, (c + 1) * CHUNK)
        rq = retp_ref[0, rows, 0:128]
        rk = _mx(retp_ref[0, rows, 128:256])
        rg = retp_ref[0, rows, 256:512]
        rv = rv_ref[0, rows, :]
        o = (_dot(_mx(rq * qdec_ref[0]), _mx(sf_ref[0, c]))
             + _dot(_mx(rq * qdec_ref[1]), _mx(sb_ref[0, c])))
        for hh in range(RET_HEADS):
            s = _dot_nt(_mx(jnp.where(g32 == hh, rq, 0.0)), rk)
            oh = _dot(_mx(s * m_ref[hh]), rv)
            o = o + jnp.where(g64 == hh, oh, 0.0)
        mean = _split_dot(o, seg) * (1.0 / RET_V)
        oc = o - mean
        var = _split_dot(oc * oc, seg) * (1.0 / RET_V)
        ret_rows.append(oc * lax.rsqrt(var + LN_EPS) * (rg * _sigmoid(rg)))
    ret = jnp.concatenate(ret_rows, axis=0)

    y = (_dot(mla_ref[0], w_o_ref[0:MLA_OUT, :])
         + _dot(sg_ref[0], w_o_ref[MLA_OUT:MLA_OUT + SG_WIDTH, :])
         + _dot(_mx(ret), w_o_ref[MLA_OUT + SG_WIDTH:, :]))
    mod = mod_ref[0, 0]
    x1 = _ln(DEEPNORM_ALPHA * x_ref[0] + mod[2:3] * y) * lng_ref[...] + lnb_ref[...]
    x1_ref[0] = x1
    h2 = x1 * (1.0 + mod[4:5]) + mod[3:4]
    h2_hi = h2.astype(BF16)
    if PACK_ROWS:
        bits = lax.bitcast_convert_type(h2_hi.astype(F32), jnp.uint32)
        half = bits.shape[1] // 2
        h2_ref[0] = bits[:, :half] | (bits[:, half:] >> 16)
    else:
        h2_ref[0] = h2
    h2_lo = (h2 - h2_hi.astype(F32)).astype(BF16)

    r2 = _dot_nt(rw_ref[...], h2_hi)
    logits = (r2[0:N_EXPERTS] + r2[N_EXPERTS:2 * N_EXPERTS]
              + _dot_nt(rw_ref[0:N_EXPERTS, :], h2_lo) + rb_ref[...])
    e_iota = lax.broadcasted_iota(jnp.int32, logits.shape, 0).astype(F32)
    work = logits
    vals, idxs = [], []
    for _ in range(TOP_K):
        mval = jnp.max(work, axis=0, keepdims=True)
        midx = jnp.min(jnp.where(work == mval, e_iota, float(N_EXPERTS)), axis=0, keepdims=True)
        vals.append(mval)
        idxs.append(midx)
        work = jnp.where(e_iota == midx, -jnp.inf, work)
    ex = [jnp.exp(vv - vals[0]) for vv in vals]
    den = ex[0] + ex[1] + ex[2] + ex[3]
    onehot = jnp.zeros_like(logits)
    for kk in range(TOP_K):
        onehot = onehot + jnp.where(e_iota == idxs[kk], 1.0, 0.0)
    base = carry_ref[:, 0:1] + _dot(onehot.astype(BF16), tri_ref[...])
    for kk in range(TOP_K):
        gate_ref[0, 0, kk:kk + 1, :] = ex[kk] / den
        idx_ref[0, 0, kk:kk + 1, :] = idxs[kk].astype(jnp.int32)
        rk_k = jnp.sum(jnp.where(e_iota == idxs[kk], base, 0.0), axis=0, keepdims=True)
        rank_ref[0, 0, kk:kk + 1, :] = rk_k.astype(jnp.int32)
    carry_ref[...] = carry_ref[...] + jnp.sum(onehot, axis=1, keepdims=True)
    cnt_ref[...] = carry_ref[...].astype(jnp.int32)


def output_projection(x_all, mla, sg, retp, rv, sf, sb, mod_tab, lw, n_lat_tiles):
    b, t, d = x_all.shape
    nt = t // TM
    cpt = TM // CHUNK
    const2 = lambda bi, j: (0, 0)
    const3 = lambda bi, j: (0, 0, 0)
    tok_spec = lambda w: pl.BlockSpec((1, TM, w), lambda bi, j: (bi, j, 0))
    st_spec = pl.BlockSpec((1, cpt, 128, RET_OUT), lambda bi, j: (bi, j, 0, 0))
    route_spec = pl.BlockSpec((1, 1, TOP_K, TM), lambda bi, j: (bi, j, 0, 0))
    route_shape = lambda dt: jax.ShapeDtypeStruct((b, nt, TOP_K, TM), dt)
    h2w = d // 2 if PACK_ROWS else d
    return pl.pallas_call(
        _outproj_kernel,
        grid=(b, nt),
        in_specs=[
            tok_spec(d), tok_spec(MLA_OUT), tok_spec(SG_WIDTH), tok_spec(512), tok_spec(RET_OUT),
            st_spec, st_spec,
            pl.BlockSpec((1, 1, N_MOD, d), lambda bi, j: (bi, j // n_lat_tiles, 0, 0)),
            pl.BlockSpec((RET_HEADS, CHUNK, CHUNK), const3),
            pl.BlockSpec((2, CHUNK, 128), const3),
            pl.BlockSpec((RET_OUT, RET_OUT), const2),
            pl.BlockSpec((d, d), const2),
            pl.BlockSpec((1, d), const2),
            pl.BlockSpec((1, d), const2),
            pl.BlockSpec((2 * N_EXPERTS, d), const2),
            pl.BlockSpec((N_EXPERTS, 1), const2),
            pl.BlockSpec((TM, TM), const2),
        ],
        out_specs=(tok_spec(d), tok_spec(h2w), route_spec, route_spec, route_spec,
                   pl.BlockSpec((N_EXPERTS, 128), const2)),
        out_shape=(jax.ShapeDtypeStruct((b, t, d), F32),
                   jax.ShapeDtypeStruct((b, t, h2w), jnp.uint32 if PACK_ROWS else F32),
                   route_shape(jnp.int32), route_shape(F32), route_shape(jnp.int32),
                   jax.ShapeDtypeStruct((N_EXPERTS, 128), jnp.int32)),
        scratch_shapes=[pltpu.VMEM((N_EXPERTS, 128), F32)],
        compiler_params=_cparams(("arbitrary", "arbitrary")),
        name="output_projection",
    )(x_all, mla, sg, retp, rv, sf, sb, mod_tab, lw["ret_m"], lw["qdec"], lw["seg"], lw["w_o"],
      lw["ln1_g"], lw["ln1_b"], lw["router_w"], lw["router_b"], lw["tri"])


_DEINT = 256


def _expert_kernel(be_ref, nu_ref, x_ref, wgu_ref, bg_ref, bl_ref, wd_ref, bd_ref, perm_ref, y_ref,
                   wg_s, wl_s, wd_s):
    i = pl.program_id(0)
    active = i < nu_ref[0]
    fresh = jnp.logical_or(i == 0, be_ref[i] != be_ref[jnp.maximum(i - 1, 0)])

    @pl.when(jnp.logical_and(active, fresh))
    def _():
        half = _DEINT // 2
        for c in range(2 * D_EXPERT // _DEINT):
            r = _dot(_mx(wgu_ref[0, 0, :, _DEINT * c:_DEINT * (c + 1)]), perm_ref[...])
            wg_s[:, half * c:half * (c + 1)] = r[:, :half].astype(wg_s.dtype)
            wl_s[:, half * c:half * (c + 1)] = r[:, half:].astype(wl_s.dtype)
        wd_s[...] = wd_ref[0, 0].astype(wd_s.dtype)

    @pl.when(active)
    def _():
        if PACK_ROWS:
            w = x_ref[...]
            hi = lax.bitcast_convert_type(w & jnp.uint32(0xFFFF0000), F32).astype(BF16)
            lo = lax.bitcast_convert_type(w << 16, F32).astype(BF16)
            xb = jnp.concatenate([hi, lo], axis=1)
        else:
            xb = x_ref[...]
        glu = jnp.minimum(_dot(xb, wg_s[...]) + bg_ref[0], SWIGLU_LIMIT)
        lin = jnp.clip(_dot(xb, wl_s[...]) + bl_ref[0], -SWIGLU_LIMIT, SWIGLU_LIMIT)
        act = glu * _sigmoid(SWIGLU_ALPHA * glu) * (lin + 1.0)
        y_ref[...] = _dot(_mx(act), wd_s[...]) + bd_ref[0]

    @pl.when(jnp.logical_not(active))
    def _():
        y_ref[...] = jnp.zeros_like(y_ref)


def expert_ffn(xg, block_e, n_used, w_gate_up, w_down, li, lw):
    cap, xw = xg.shape
    d = D_MODEL
    nb = cap // MOE_BM
    de = D_EXPERT
    xmap = lambda i, be, nu: (jnp.minimum(i, nu[0] - 1), 0)
    wmap = lambda i, be, nu: (be[i], 0, 0)
    lmap = lambda i, be, nu: (li, be[i], 0, 0)
    grid_spec = pltpu.PrefetchScalarGridSpec(
        num_scalar_prefetch=2,
        grid=(nb,),
        in_specs=[pl.BlockSpec((MOE_BM, xw), xmap),
                  pl.BlockSpec((1, 1, d, 2 * de), lmap),
                  pl.BlockSpec((1, 1, de), wmap), pl.BlockSpec((1, 1, de), wmap),
                  pl.BlockSpec((1, 1, de, d), lmap), pl.BlockSpec((1, 1, d), wmap),
                  pl.BlockSpec((_DEINT, _DEINT), lambda i, be, nu: (0, 0))],
        out_specs=pl.BlockSpec((MOE_BM, d), lambda i, be, nu: (i, 0)),
        scratch_shapes=[pltpu.VMEM((d, de), MXU_DT), pltpu.VMEM((d, de), MXU_DT),
                        pltpu.VMEM((de, d), MXU_DT)],
    )
    return pl.pallas_call(
        _expert_kernel,
        grid_spec=grid_spec,
        out_shape=jax.ShapeDtypeStruct((cap, d), F32),
        compiler_params=pltpu.CompilerParams(dimension_semantics=("arbitrary",),
                                             vmem_limit_bytes=EXPERT_VMEM_LIMIT),
        name="expert_ffn",
    )(block_e, n_used, xg, w_gate_up, lw["b_glu"], lw["b_lin"], w_down, lw["b_down"], lw["deint"])


def _combine_ln2_kernel(x_ref, y_ref, gate_ref, mod_ref, g_ref, b_ref, o_ref):
    gates = gate_ref[0]
    f = gates[:, 0:1] * y_ref[0, 0]
    for kk in range(1, TOP_K):
        f = f + gates[:, kk:kk + 1] * y_ref[kk, 0]
    mod = mod_ref[0, 0]
    o_ref[0] = _ln(DEEPNORM_ALPHA * x_ref[0] + mod[5:6] * f) * g_ref[...] + b_ref[...]


def combine_deepnorm2(x1, yg, gates, mod_tab, lw, n_lat_tiles):
    b, t, d = x1.shape
    tok = pl.BlockSpec((1, TM, d), lambda bi, j: (bi, j, 0))
    vec = pl.BlockSpec((1, d), lambda bi, j: (0, 0))
    return pl.pallas_call(
        _combine_ln2_kernel,
        grid=(b, t // TM),
        in_specs=[tok,
                  pl.BlockSpec((TOP_K, 1, TM, d), lambda bi, j: (0, bi, j, 0)),
                  pl.BlockSpec((1, TM, TOP_K), lambda bi, j: (bi, j, 0)),
                  pl.BlockSpec((1, 1, N_MOD, d), lambda bi, j: (bi, j // n_lat_tiles, 0, 0)), vec, vec],
        out_specs=tok,
        out_shape=jax.ShapeDtypeStruct((b, t, d), F32),
        compiler_params=_cparams(("parallel", "parallel")),
        name="combine_deepnorm2",
    )(x1, yg, gates, mod_tab, lw["ln2_g"], lw["ln2_b"])


def _rotation_tables(s_len, lc):
    rows = s_len // GRID_W
    row = jnp.broadcast_to(jnp.arange(rows, dtype=F32)[:, None], (rows, GRID_W)).reshape(-1)
    col = jnp.broadcast_to(jnp.arange(GRID_W, dtype=F32)[None, :], (rows, GRID_W)).reshape(-1)
    inv = ROPE_BASE ** (-jnp.arange(ROPE_AXIS_FREQS, dtype=F32) / ROPE_AXIS_FREQS)
    ar, ac = row[:, None] * inv, col[:, None] * inv
    c64 = jnp.concatenate([jnp.cos(ar), jnp.cos(ar), jnp.cos(ac), jnp.cos(ac)], axis=1)
    s64 = jnp.concatenate([-jnp.sin(ar), jnp.sin(ar), -jnp.sin(ac), jnp.sin(ac)], axis=1)
    c64 = jnp.concatenate([c64, jnp.ones((lc, 64), F32)], axis=0)
    s64 = jnp.concatenate([s64, jnp.zeros((lc, 64), F32)], axis=0)
    half = RET_QK // 2
    pos = jnp.concatenate([lc + jnp.arange(s_len, dtype=F32), jnp.arange(lc, dtype=F32)])
    inv_r = 1.0 / (RET_ROPE_BASE ** jnp.linspace(0.0, 1.0, half, dtype=F32))
    ang = pos[:, None] * inv_r
    rc = jnp.tile(jnp.concatenate([jnp.cos(ang), jnp.cos(ang)], axis=1), (1, RET_HEADS))
    rs = jnp.tile(jnp.concatenate([-jnp.sin(ang), jnp.sin(ang)], axis=1), (1, RET_HEADS))
    qs = RET_QK ** -0.5
    return jnp.concatenate([
        jnp.tile(c64, (1, MLA_HEADS)) * MLA_SCALE, jnp.tile(s64, (1, MLA_HEADS)) * MLA_SCALE,
        c64, s64, rc * qs, rs * qs, rc, rs], axis=1)


def _in_perm():
    a = np.arange
    return np.concatenate([
        a(0, 640), a(704, 1216), a(1216, 1344), a(1344, 1472),
        1216 + _swap16(a(128)), 1344 + _swap16(a(128)), a(1472, 1984),
        640 + a(64), 640 + _swap16(a(64))])


def _uq_perm():
    a = np.arange
    nope = [h * MLA_QK + a(MLA_NOPE) for h in range(MLA_HEADS)]
    rope = [h * MLA_QK + MLA_NOPE + a(MLA_ROPE) for h in range(MLA_HEADS)]
    part = [h * MLA_QK + MLA_NOPE + _swap16(a(MLA_ROPE)) for h in range(MLA_HEADS)]
    return np.concatenate(nope + rope + part)


def _layer_weights(p):
    nl = p["w_in"].shape[0]
    lgf = jax.nn.log_sigmoid(p["ret_decay_fwd"].astype(F32))
    lgb = jax.nn.log_sigmoid(p["ret_decay_bwd"].astype(F32))
    h128 = np.arange(128) // RET_QK
    h256 = np.arange(RET_OUT) // RET_V
    a = jnp.arange(CHUNK, dtype=F32)[None, :, None]
    lf, lb = lgf[:, h128][:, None, :], lgb[:, h128][:, None, :]
    kdec = jnp.stack([jnp.exp(lf * (CHUNK - 1.0 - a)), jnp.exp(lb * a)], axis=1)
    qdec = jnp.stack([jnp.exp(lf * (a + 1.0)), jnp.exp(lb * (CHUNK - a))], axis=1)
    i = jnp.arange(CHUNK, dtype=F32)[:, None]
    j = jnp.arange(CHUNK, dtype=F32)[None, :]
    dif = (i - j)[None, None]
    ret_m = jnp.where(dif >= 0, jnp.exp(lgf[:, :, None, None] * jnp.maximum(dif, 0.0)),
                      jnp.exp(lgb[:, :, None, None] * jnp.maximum(-dif, 0.0)))
    cd = jnp.stack([jnp.exp(lgf[:, h256] * CHUNK), jnp.exp(lgb[:, h256] * CHUNK)], axis=1)[:, :, None, :]
    bd = (h128[:, None] == h256[None, :]).astype(np.float32)
    seg = (h256[:, None] == h256[None, :]).astype(np.float32)
    tri = (np.arange(TM)[:, None] < np.arange(TM)[None, :]).astype(np.float32)
    jj = np.arange(_DEINT // 2)
    deint = np.zeros((_DEINT, _DEINT), np.float32)
    deint[2 * jj, jj] = 1.0
    deint[2 * jj + 1, _DEINT // 2 + jj] = 1.0
    rw_t = jnp.swapaxes(p["router_w"], 1, 2)
    rw_hi = rw_t.astype(BF16)
    rw_lo = (rw_t - rw_hi.astype(F32)).astype(BF16)
    sg_bias = jnp.repeat(jnp.swapaxes(p["sg_b"], 1, 2), SG_WIDTH // SG_GROUPS, axis=2)
    bgu = p["b_gate_up"]
    return {
        "w_in": p["w_in"][:, :, _in_perm()].astype(MXU_DT),
        "q_g": p["mla_q_norm_g"][:, None, :], "kv_g": p["mla_kv_norm_g"][:, None, :],
        "w_uq": p["mla_w_uq"][:, :, _uq_perm()].astype(MXU_DT),
        "w_ukv": p["mla_w_ukv"].astype(MXU_DT),
        "sg_g": p["sg_norm_g"][:, None, :], "sg_b": p["sg_norm_b"][:, None, :],
        "sg_w": p["sg_w"].reshape(nl, SG_GROUPS * CHUNK, CHUNK).astype(MXU_DT),
        "sg_bias": sg_bias,
        "kdec": kdec, "qdec": qdec, "ret_m": ret_m, "cd": cd,
        "bd": jnp.broadcast_to(jnp.asarray(bd), (nl,) + bd.shape),
        "seg": jnp.broadcast_to(jnp.asarray(seg, BF16), (nl,) + seg.shape),
        "tri": jnp.broadcast_to(jnp.asarray(tri, BF16), (nl,) + tri.shape),
        "w_o": p["w_o"].astype(MXU_DT),
        "ln1_g": p["ln1_g"][:, None, :], "ln1_b": p["ln1_b"][:, None, :],
        "ln2_g": p["ln2_g"][:, None, :], "ln2_b": p["ln2_b"][:, None, :],
        "router_w": jnp.concatenate([rw_hi, rw_lo], axis=1),
        "router_b": p["router_b"][:, :, None],
        "b_glu": bgu[:, :, None, 0::2], "b_lin": bgu[:, :, None, 1::2],
        "b_down": p["b_down"][:, :, None, :],
        "deint": jnp.broadcast_to(jnp.asarray(deint, MXU_DT), (nl,) + deint.shape),
    }


def _route(idx, rank, counts):
    n_assign = idx.shape[1] * TOP_K
    nb = -(-(n_assign + N_EXPERTS * (MOE_BM - 1)) // MOE_BM)
    padded = (counts + MOE_BM - 1) // MOE_BM * MOE_BM
    pad_end = jnp.cumsum(padded)
    pad_start = pad_end - padded
    experts = jnp.arange(N_EXPERTS, dtype=jnp.int32)
    dest = rank + jnp.sum(jnp.where(idx[..., None] == experts, pad_start, 0), axis=-1)
    blk_start = jnp.arange(nb, dtype=jnp.int32) * MOE_BM
    block_e = jnp.minimum(jnp.sum((pad_end[None, :] <= blk_start[:, None]).astype(jnp.int32), axis=1),
                          N_EXPERTS - 1)
    n_used = (pad_end[-1] // MOE_BM).astype(jnp.int32).reshape(1)
    return dest.astype(jnp.int32), block_e, n_used, nb


def _sc_workers():
    info = plsc.get_sparse_core_info()
    return info.num_cores, info.num_cores * info.num_subcores


def sc_dispatch(rows, dest3, cap):
    n, w = rows.shape
    nch, kk, c = dest3.shape
    ncores, nw = _sc_workers()
    per_w = nch // nw
    mesh = plsc.VectorSubcoreMesh(core_axis_name="c", subcore_axis_name="s")

    @functools.partial(
        pl.kernel, mesh=mesh, out_type=jax.ShapeDtypeStruct((cap, w), rows.dtype),
        scratch_types=[pltpu.VMEM((kk, c), jnp.int32), pltpu.VMEM((c, w), rows.dtype)])
    def scatter_rows(h_hbm, d_hbm, o_hbm, idx_v, rows_v):
        wid = lax.axis_index("s") * ncores + lax.axis_index("c")

        @pl.loop(0, per_w)
        def _(j):
            ch = wid * per_w + j
            pltpu.sync_copy(d_hbm.at[ch], idx_v)
            pltpu.sync_copy(h_hbm.at[pl.ds(ch * c, c)], rows_v)
            for q in range(kk):
                pltpu.sync_copy(rows_v, o_hbm.at[idx_v.at[q]])

    return scatter_rows(rows, dest3)


def sc_combine_gather(y, dest3, n):
    cap, d = y.shape
    nch, kk, c = dest3.shape
    ncores, nw = _sc_workers()
    per_w = nch // nw
    mesh = plsc.VectorSubcoreMesh(core_axis_name="c", subcore_axis_name="s")

    @functools.partial(
        pl.kernel, mesh=mesh, out_type=jax.ShapeDtypeStruct((kk, n, d), y.dtype),
        scratch_types=[pltpu.VMEM((kk, c), jnp.int32), pltpu.VMEM((c, d), y.dtype)])
    def gather_rows(y_hbm, d_hbm, o_hbm, idx_v, rows_v):
        wid = lax.axis_index("s") * ncores + lax.axis_index("c")

        @pl.loop(0, per_w)
        def _(j):
            ch = wid * per_w + j
            pltpu.sync_copy(d_hbm.at[ch], idx_v)
            for q in range(kk):
                pltpu.sync_copy(y_hbm.at[idx_v.at[q]], rows_v)
                pltpu.sync_copy(rows_v, o_hbm.at[q, pl.ds(ch * c, c)])

    return gather_rows(y, dest3)


def kernel(x, c, ctx, c_ctx, ada_w, ada_b, w_in, mla_q_norm_g, mla_kv_norm_g, mla_w_uq, mla_w_ukv,
           sg_norm_g, sg_norm_b, sg_w, sg_b, ret_decay_fwd, ret_decay_bwd, w_o, ln1_g, ln1_b,
           router_w, router_b, w_gate_up, b_gate_up, w_down, b_down, ln2_g, ln2_b):
    b, s_len, d = x.shape
    lc = ctx.shape[1]
    assert d == D_MODEL and lc % TM == 0 and s_len % lc == 0 and s_len % GRID_W == 0
    assert b + 1 <= 8
    t = s_len + lc
    n_lat_tiles = s_len // TM
    params = dict(w_in=w_in, mla_q_norm_g=mla_q_norm_g, mla_kv_norm_g=mla_kv_norm_g, mla_w_uq=mla_w_uq,
                  mla_w_ukv=mla_w_ukv, sg_norm_g=sg_norm_g, sg_norm_b=sg_norm_b, sg_w=sg_w, sg_b=sg_b,
                  ret_decay_fwd=ret_decay_fwd, ret_decay_bwd=ret_decay_bwd, w_o=w_o, ln1_g=ln1_g,
                  ln1_b=ln1_b, router_w=router_w, router_b=router_b, w_gate_up=w_gate_up,
                  b_gate_up=b_gate_up, w_down=w_down, b_down=b_down, ln2_g=ln2_g, ln2_b=ln2_b)
    lws = _layer_weights(params)
    tab = _rotation_tables(s_len, lc)

    c_rows = jnp.concatenate([c, c_ctx[None, :], jnp.zeros((8 - b - 1, d), F32)], axis=0)
    mod = ada_modulation(c_rows, ada_w, ada_b).reshape(DEPTH, 8, N_MOD, d)
    mod_tab = jnp.stack([mod[:, :b], jnp.broadcast_to(mod[:, b:b + 1], (DEPTH, b, N_MOD, d))], axis=2)

    x_all = jnp.concatenate([x, ctx], axis=1)
    for li in range(DEPTH):
        lw = {k: v[li] for k, v in lws.items()}
        mt = mod_tab[li]
        q, k, v, sg, retp, rv, a = input_projection(x_all, mt, tab, lw, n_lat_tiles)
        mla = mla_attention(q, k, v, s_len, lc)
        sf, sb = retention_scan(a, lw["cd"], s_len // CHUNK)
        x1, h2, idx, gates, rank, cnt = output_projection(
            x_all, mla, sg, retp, rv, sf, sb, mt, lw, n_lat_tiles)
        to_tok = lambda z: z.transpose(2, 0, 1, 3).reshape(TOP_K, b * t)
        dest, block_e, n_used, nb = _route(to_tok(idx), to_tok(rank), cnt[:, 0])
        dest3 = dest.reshape(TOP_K, (b * t) // SC_CHUNK, SC_CHUNK).transpose(1, 0, 2)
        xg = sc_dispatch(h2.reshape(b * t, h2.shape[-1]), dest3, nb * MOE_BM)
        y = expert_ffn(xg, block_e, n_used, w_gate_up, w_down, li, lw)
        yg = sc_combine_gather(y, dest3, b * t).reshape(TOP_K, b, t, d)
        gates_tok = gates.transpose(0, 1, 3, 2).reshape(b, t, TOP_K)
        x_all = combine_deepnorm2(x1, yg, gates_tok, mt, lw, n_lat_tiles)
    return x_all[:, :s_len]
```

```python
import functools

import numpy as np
import jax
import jax.numpy as jnp
from jax import lax
from jax.experimental import pallas as pl
from jax.experimental.pallas import tpu as pltpu
from jax.experimental.pallas import tpu_sc as plsc

F32 = jnp.float32
BF16 = jnp.bfloat16
MXU_DT = BF16
PACK_ROWS = True

D_MODEL = 1024
DEPTH = 4
GRID_W = 64
MLA_HEADS = 4
MLA_NOPE = 128
MLA_ROPE = 64
MLA_V = 128
MLA_Q_LORA = 384
MLA_KV_LORA = 256
MLA_QK = MLA_NOPE + MLA_ROPE
MLA_SCALE = MLA_QK ** -0.5
ROPE_BASE = 10000.0
ROPE_AXIS_FREQS = MLA_ROPE // 4
SG_GROUPS = 4
SG_WIDTH = 256
SG_CHUNK = 128
RET_HEADS = 4
RET_QK = 32
RET_V = 64
RET_CHUNK = 128
RET_ROPE_BASE = 10000.0
N_EXPERTS = 32
TOP_K = 4
D_EXPERT = 1024
SWIGLU_LIMIT = 7.0
SWIGLU_ALPHA = 1.702
N_MOD = 6
LN_EPS = 1e-5
RMS_EPS = 1e-6
DEEPNORM_ALPHA = (2 * DEPTH) ** 0.25
MLA_OUT = MLA_HEADS * MLA_V
RET_OUT = RET_HEADS * RET_V

TM = 256
CHUNK = 128
MOE_BM = 256
SC_CHUNK = 48
N_STREAMS = 2
ATT_TQ = 512
ATT_TK = 1024
VMEM_LIMIT = 48 * 2 ** 20
EXPERT_VMEM_LIMIT = 56 * 2 ** 20

_O_CQ, _O_CKV, _O_SGU, _O_SGV = 0, 384, 640, 896
_O_RQ, _O_RK, _O_RQS, _O_RKS, _O_RV, _O_RG, _O_KR = 1152, 1280, 1408, 1536, 1664, 1920, 2176
IN_P = 2304
_T_QC, _T_QS, _T_KCS, _T_RQC, _T_RQS, _T_RKC, _T_RKS = 0, 256, 512, 640, 768, 896, 1024
TAB_W = 1152


def _cparams(sem):
    return pltpu.CompilerParams(dimension_semantics=sem, vmem_limit_bytes=VMEM_LIMIT)


def _dot(a, b):
    return jnp.dot(a, b, preferred_element_type=F32)


def _dot_nt(a, b):
    return lax.dot_general(a, b, (((1,), (1,)), ((), ())), preferred_element_type=F32)


def _mx(a):
    return a.astype(MXU_DT)


def _swap16(j):
    return (j // 32) * 32 + ((j % 32) + 16) % 32


_ERF_ALPHA = (-2.72614225801306e-10, 2.77068142495902e-08, -2.10102402082508e-06,
              -5.69250639462346e-05, -7.34990630326855e-04, -2.95459980854025e-03,
              -1.60960333262415e-02)
_ERF_BETA = (-1.45660718464996e-05, -2.13374055278905e-04, -1.68282697438203e-03,
             -7.37332916720468e-03, -1.42647390514189e-02)


def _erf(x):
    x = jnp.clip(x, -4.0, 4.0)
    x2 = x * x
    p = jnp.full_like(x, _ERF_ALPHA[0])
    for c in _ERF_ALPHA[1:]:
        p = p * x2 + c
    q = jnp.full_like(x, _ERF_BETA[0])
    for c in _ERF_BETA[1:]:
        q = q * x2 + c
    return x * p / q


def _gelu(x):
    return 0.5 * x * (1.0 + _erf(x * 0.7071067811865476))


def _sigmoid(x):
    return 1.0 / (1.0 + jnp.exp(-x))


def _ln(x):
    xc = x - jnp.mean(x, axis=-1, keepdims=True)
    return xc * lax.rsqrt(jnp.mean(xc * xc, axis=-1, keepdims=True) + LN_EPS)


def _lane_group(shape, width):
    return lax.broadcasted_iota(jnp.int32, shape, len(shape) - 1) // width


def _ada_kernel(c_ref, w_ref, b_ref, o_ref):
    c = c_ref[...]
    o_ref[0] = _dot(c * _sigmoid(c), w_ref[0]) + b_ref[0]


def ada_modulation(c_rows, ada_w, ada_b):
    nl, d, n = ada_w.shape
    tn = 1536
    return pl.pallas_call(
        _ada_kernel,
        grid=(nl, n // tn),
        in_specs=[pl.BlockSpec((8, d), lambda l, j: (0, 0)),
                  pl.BlockSpec((1, d, tn), lambda l, j: (l, 0, j)),
                  pl.BlockSpec((1, 1, tn), lambda l, j: (l, 0, j))],
        out_specs=pl.BlockSpec((1, 8, tn), lambda l, j: (l, 0, j)),
        out_shape=jax.ShapeDtypeStruct((nl, 8, n), F32),
        compiler_params=_cparams(("arbitrary", "arbitrary")),
        name="ada_modulation",
    )(c_rows, ada_w, ada_b.reshape(nl, 1, n))


def _inproj_kernel(x_ref, mod_ref, tab_ref, w_in_ref, qg_ref, kvg_ref, w_uq_ref, w_ukv_ref,
                   sgg_ref, sgb_ref, sgw_ref, sgbias_ref, kdec_ref, bd_ref,
                   q_ref, k_ref, v_ref, sg_ref, retp_ref, rv_ref, a_ref):
    x = x_ref[0]
    mod = mod_ref[0, 0]
    h = x * (1.0 + mod[1:2]) + mod[0:1]
    p = _dot(_mx(h), w_in_ref[...])
    tab = tab_ref[...]

    cq = p[:, _O_CQ:_O_CQ + MLA_Q_LORA]
    cq = cq * lax.rsqrt(jnp.mean(cq * cq, axis=-1, keepdims=True) + RMS_EPS) * qg_ref[...]
    qa = _dot(_mx(cq), w_uq_ref[...])
    rot = (qa[:, 512:768] * tab[:, _T_QC:_T_QC + 256]
           + qa[:, 768:1024] * tab[:, _T_QS:_T_QS + 256])
    for hh in range(MLA_HEADS):
        q_ref[0, hh, :, 0:128] = (qa[:, 128 * hh:128 * hh + 128] * MLA_SCALE).astype(q_ref.dtype)
        g = hh // 2
        q_ref[0, hh, :, 128:256] = rot[:, 128 * g:128 * g + 128].astype(q_ref.dtype)

    ckv = p[:, _O_CKV:_O_CKV + MLA_KV_LORA]
    ckv = ckv * lax.rsqrt(jnp.mean(ckv * ckv, axis=-1, keepdims=True) + RMS_EPS) * kvg_ref[...]
    kv = _dot(_mx(ckv), w_ukv_ref[...])
    t = p[:, _O_KR:_O_KR + 128] * tab[:, _T_KCS:_T_KCS + 128]
    u = t + pltpu.roll(t, 64, axis=1)
    low = lax.broadcasted_iota(jnp.int32, u.shape, 1) < 64
    kx = (jnp.where(low, u, 0.0), jnp.where(low, 0.0, u))
    ones_col = jnp.where(lax.broadcasted_iota(jnp.int32, u.shape, 1) == 0, 1.0, 0.0).astype(v_ref.dtype)
    for hh in range(MLA_HEADS):
        k_ref[0, hh, :, 0:128] = kv[:, 256 * hh:256 * hh + 128].astype(k_ref.dtype)
        k_ref[0, hh, :, 128:256] = kx[hh % 2].astype(k_ref.dtype)
        v_ref[0, hh, :, 0:128] = kv[:, 256 * hh + 128:256 * hh + 256].astype(v_ref.dtype)
        v_ref[0, hh, :, 128:256] = ones_col

    gu = _gelu(p[:, _O_SGU:_O_SGU + SG_WIDTH])
    gv = _ln(_gelu(p[:, _O_SGV:_O_SGV + SG_WIDTH])) * sgg_ref[...] + sgb_ref[...]
    gvm = _mx(gv)
    grp = _lane_group((CHUNK, SG_WIDTH), SG_WIDTH // SG_GROUPS)
    for c in range(TM // CHUNK):
        rows = slice(c * CHUNK, (c + 1) * CHUNK)
        res = _dot(sgw_ref[...], gvm[rows])
        mixed = sgbias_ref[...]
        for g in range(SG_GROUPS):
            mixed = mixed + jnp.where(grp == g, res[g * CHUNK:(g + 1) * CHUNK], 0.0)
        sg_ref[0, rows, :] = (gu[rows] * mixed).astype(sg_ref.dtype)

    rq = (p[:, _O_RQ:_O_RQ + 128] * tab[:, _T_RQC:_T_RQC + 128]
          + p[:, _O_RQS:_O_RQS + 128] * tab[:, _T_RQS:_T_RQS + 128])
    rk = (p[:, _O_RK:_O_RK + 128] * tab[:, _T_RKC:_T_RKC + 128]
          + p[:, _O_RKS:_O_RKS + 128] * tab[:, _T_RKS:_T_RKS + 128])
    rv = p[:, _O_RV:_O_RV + RET_OUT]
    retp_ref[0, :, 0:128] = rq
    retp_ref[0, :, 128:256] = rk
    retp_ref[0, :, 256:512] = p[:, _O_RG:_O_RG + RET_OUT]
    rvm = _mx(rv)
    rv_ref[0] = rvm.astype(rv_ref.dtype)
    bd = bd_ref[...]
    for c in range(TM // CHUNK):
        rows = slice(c * CHUNK, (c + 1) * CHUNK)
        for d in range(2):
            kd_t = _mx((rk[rows] * kdec_ref[d]).T)
            af = _dot(kd_t, rvm[rows]) * bd
            a_ref[0, c, d] = (af[0:32] + af[32:64]) + (af[64:96] + af[96:128])


def _ordered_after(kernel_fn, pos):
    def wrapped(*refs):
        return kernel_fn(*refs[:pos], *refs[pos + 1:])
    return wrapped


_ORDER_SPEC = pl.BlockSpec(memory_space=pl.ANY)


def input_projection(x_all, mod_tab, tab, lw, n_lat_tiles, after):
    b, t, d = x_all.shape
    nt = t // TM
    nc = t // CHUNK
    cpt = TM // CHUNK
    const2 = lambda bi, j: (0, 0)
    const3 = lambda bi, j: (0, 0, 0)
    out_shape = (
        jax.ShapeDtypeStruct((b, MLA_HEADS, t, 256), MXU_DT),
        jax.ShapeDtypeStruct((b, MLA_HEADS, t, 256), MXU_DT),
        jax.ShapeDtypeStruct((b, MLA_HEADS, t, 256), MXU_DT),
        jax.ShapeDtypeStruct((b, t, SG_WIDTH), MXU_DT),
        jax.ShapeDtypeStruct((b, t, 512), F32),
        jax.ShapeDtypeStruct((b, t, RET_OUT), MXU_DT),
        jax.ShapeDtypeStruct((b, nc, 2, RET_QK, RET_OUT), F32),
    )
    head_spec = lambda w: pl.BlockSpec((1, MLA_HEADS, TM, w), lambda bi, j: (bi, 0, j, 0))
    tok_spec = lambda w: pl.BlockSpec((1, TM, w), lambda bi, j: (bi, j, 0))
    return pl.pallas_call(
        _ordered_after(_inproj_kernel, 14),
        grid=(b, nt),
        in_specs=[
            tok_spec(d),
            pl.BlockSpec((1, 1, N_MOD, d), lambda bi, j: (bi, j // n_lat_tiles, 0, 0)),
            pl.BlockSpec((TM, TAB_W), lambda bi, j: (j, 0)),
            pl.BlockSpec((d, IN_P), const2),
            pl.BlockSpec((1, MLA_Q_LORA), const2),
            pl.BlockSpec((1, MLA_KV_LORA), const2),
            pl.BlockSpec((MLA_Q_LORA, 1024), const2),
            pl.BlockSpec((MLA_KV_LORA, 1024), const2),
            pl.BlockSpec((1, SG_WIDTH), const2),
            pl.BlockSpec((1, SG_WIDTH), const2),
            pl.BlockSpec((SG_GROUPS * CHUNK, CHUNK), const2),
            pl.BlockSpec((CHUNK, SG_WIDTH), const2),
            pl.BlockSpec((2, CHUNK, 128), const3),
            pl.BlockSpec((128, RET_OUT), const2),
            _ORDER_SPEC,
        ],
        out_specs=(head_spec(256), head_spec(256), head_spec(256), tok_spec(SG_WIDTH),
                   tok_spec(512), tok_spec(RET_OUT),
                   pl.BlockSpec((1, cpt, 2, RET_QK, RET_OUT), lambda bi, j: (bi, j, 0, 0, 0))),
        out_shape=out_shape,
        compiler_params=_cparams(("parallel", "parallel")),
        name="input_projection",
    )(x_all, mod_tab, tab, lw["w_in"], lw["q_g"], lw["kv_g"], lw["w_uq"], lw["w_ukv"],
      lw["sg_g"], lw["sg_b"], lw["sg_w"], lw["sg_bias"], lw["kdec"], lw["bd"], after)


def _attn_kernel(q_ref, k_ref, v_ref, o_ref, *, n_main, tk, tail):
    q = q_ref[0, 0]
    tq = q.shape[0]
    chunks = [(i * tk, tk) for i in range(n_main)] + ([(n_main * tk, tail)] if tail else [])

    def scores(ci):
        start, size = chunks[ci]
        return _dot_nt(q, k_ref[0, 0, start:start + size, :])

    m = jnp.full((tq, 1), -1e30, F32)
    acc = jnp.zeros((tq, 256), F32)
    s_next = scores(0)
    for ci, (start, size) in enumerate(chunks):
        s = s_next
        if ci + 1 < len(chunks):
            s_next = scores(ci + 1)
        m_new = jnp.maximum(m, jnp.max(s, axis=-1, keepdims=True))
        p = jnp.exp(s - m_new)
        acc = jnp.exp(m - m_new) * acc + _dot(_mx(p), v_ref[0, 0, start:start + size, :])
        m = m_new
    o_ref[0] = (acc[:, 0:MLA_V] / acc[:, MLA_V:MLA_V + 1]).astype(o_ref.dtype)


def mla_attention(q, k, v, s_len, lc):
    b, hn, t, _ = q.shape
    tq = min(ATT_TQ, s_len)
    tk = min(ATT_TK, s_len)
    kv_full = pl.BlockSpec((1, 1, t, 256), lambda bi, hi, i: (bi, hi, 0, 0))
    out_lat = pl.pallas_call(
        functools.partial(_attn_kernel, n_main=s_len // tk, tk=tk, tail=lc),
        grid=(b, hn, s_len // tq),
        in_specs=[pl.BlockSpec((1, 1, tq, 256), lambda bi, hi, i: (bi, hi, i, 0)), kv_full, kv_full],
        out_specs=pl.BlockSpec((1, tq, MLA_V), lambda bi, hi, i: (bi, i, hi)),
        out_shape=jax.ShapeDtypeStruct((b, s_len, MLA_OUT), MXU_DT),
        compiler_params=_cparams(("parallel", "parallel", "arbitrary")),
        name="mla_attention_latent",
    )(q, k, v)
    cblk = s_len // lc
    ctx_spec = pl.BlockSpec((1, 1, lc, 256), lambda bi, hi: (bi, hi, cblk, 0))
    out_ctx = pl.pallas_call(
        functools.partial(_attn_kernel, n_main=0, tk=tk, tail=lc),
        grid=(b, hn),
        in_specs=[ctx_spec, ctx_spec, ctx_spec],
        out_specs=pl.BlockSpec((1, lc, MLA_V), lambda bi, hi: (bi, 0, hi)),
        out_shape=jax.ShapeDtypeStruct((b, lc, MLA_OUT), MXU_DT),
        compiler_params=_cparams(("parallel", "parallel")),
        name="mla_attention_context",
    )(q, k, v)
    return out_lat, out_ctx


def _ret_scan_kernel(a_ref, cd_ref, s_ref, *, n_lat_chunks):
    nc = a_ref.shape[1]
    ncc = nc - n_lat_chunks
    cd_f, cd_b = cd_ref[0], cd_ref[1]

    def body(n, carry):
        sf, sb = carry
        cf = jnp.where(n < ncc, n_lat_chunks + n, n - ncc)
        cb = jnp.where(n < ncc, nc - 1 - n, n_lat_chunks - 1 - (n - ncc))
        s_ref[0, cf, 0] = sf
        s_ref[0, cb, 1] = sb
        return sf * cd_f + a_ref[0, cf, 0], sb * cd_b + a_ref[0, cb, 1]

    zero = jnp.zeros((RET_QK, RET_OUT), F32)
    lax.fori_loop(0, nc, body, (zero, zero))


def retention_scan(a, cd, n_lat_chunks):
    b, nc = a.shape[:2]
    blk = pl.BlockSpec((1, nc, 2, RET_QK, RET_OUT), lambda bi: (bi, 0, 0, 0, 0))
    return pl.pallas_call(
        functools.partial(_ret_scan_kernel, n_lat_chunks=n_lat_chunks),
        grid=(b,),
        in_specs=[blk, pl.BlockSpec((2, 1, RET_OUT), lambda bi: (0, 0, 0))],
        out_specs=blk,
        out_shape=jax.ShapeDtypeStruct(a.shape, F32),
        compiler_params=_cparams(("parallel",)),
        name="retention_scan",
    )(a, cd)


def _split_dot(x, ones_bd):
    hi = x.astype(BF16)
    lo = (x - hi.astype(F32)).astype(BF16)
    return _dot(hi, ones_bd) + _dot(lo, ones_bd)


def _outproj_kernel(x_ref, mlal_ref, mlac_ref, sg_ref, retp_ref, rv_ref, st_ref, mod_ref,
                    m_ref, qdec_ref, bd_ref, seg_ref, w_o_ref, lng_ref, lnb_ref, rw_ref, rb_ref, tri_ref,
                    x1_ref, h2_ref, idx_ref, gate_ref, rank_ref, cnt_ref, carry_ref, *, n_lat_tiles):
    @pl.when(pl.program_id(0) == 0)
    def _():
        carry_ref[...] = jnp.zeros_like(carry_ref)

    for bb in range(x_ref.shape[0]):
        _outproj_tile(bb, x_ref, mlal_ref, mlac_ref, sg_ref, retp_ref, rv_ref, st_ref, mod_ref,
                      m_ref, qdec_ref, bd_ref, seg_ref, w_o_ref, lng_ref, lnb_ref, rw_ref, rb_ref, tri_ref,
                      x1_ref, h2_ref, idx_ref, gate_ref, rank_ref, carry_ref, n_lat_tiles)
    cnt_ref[...] = carry_ref[...].astype(jnp.int32)


def _outproj_tile(bb, x_ref, mlal_ref, mlac_ref, sg_ref, retp_ref, rv_ref, st_ref, mod_ref,
                  m_ref, qdec_ref, bd_ref, seg_ref, w_o_ref, lng_ref, lnb_ref, rw_ref, rb_ref, tri_ref,
                  x1_ref, h2_ref, idx_ref, gate_ref, rank_ref, carry_ref, n_lat_tiles):
    g32 = _lane_group((CHUNK, 128), RET_QK)
    g64 = _lane_group((CHUNK, RET_OUT), RET_V)
    seg = seg_ref[...]
    bd = bd_ref[...]
    ret_rows = []
    for c in range(TM // CHUNK):
        rows = slice(c * CHUNK, (c + 1) * CHUNK)
        rq = retp_ref[bb, rows, 0:128]
        rk = _mx(retp_ref[bb, rows, 128:256])
        rg = retp_ref[bb, rows, 256:512]
        rv = rv_ref[bb, rows, :]
        sf = _mx(jnp.concatenate([st_ref[bb, c, 0]] * RET_HEADS, axis=0) * bd)
        sb = _mx(jnp.concatenate([st_ref[bb, c, 1]] * RET_HEADS, axis=0) * bd)
        o = _dot(_mx(rq * qdec_ref[0]), sf) + _dot(_mx(rq * qdec_ref[1]), sb)
        for hh in range(RET_HEADS):
            s = _dot_nt(_mx(jnp.where(g32 == hh, rq, 0.0)), rk)
            oh = _dot(_mx(s * m_ref[hh]), rv)
            o = o + jnp.where(g64 == hh, oh, 0.0)
        mean = _split_dot(o, seg) * (1.0 / RET_V)
        oc = o - mean
        var = _split_dot(oc * oc, seg) * (1.0 / RET_V)
        ret_rows.append(oc * lax.rsqrt(var + LN_EPS) * (rg * _sigmoid(rg)))
    ret = jnp.concatenate(ret_rows, axis=0)

    mla = jnp.where(pl.program_id(0) >= n_lat_tiles, mlac_ref[bb], mlal_ref[bb])
    y = (_dot(mla, w_o_ref[0:MLA_OUT, :])
         + _dot(sg_ref[bb], w_o_ref[MLA_OUT:MLA_OUT + SG_WIDTH, :])
         + _dot(_mx(ret), w_o_ref[MLA_OUT + SG_WIDTH:, :]))
    mod = mod_ref[bb, 0]
    x1 = _ln(DEEPNORM_ALPHA * x_ref[bb] + mod[2:3] * y) * lng_ref[...] + lnb_ref[...]
    x1_ref[bb] = x1
    h2 = x1 * (1.0 + mod[4:5]) + mod[3:4]
    h2_hi = h2.astype(BF16)
    if PACK_ROWS:
        bits = lax.bitcast_convert_type(h2_hi.astype(F32), jnp.uint32)
        half = bits.shape[1] // 2
        h2_ref[bb] = bits[:, :half] | (bits[:, half:] >> 16)
    else:
        h2_ref[bb] = h2
    h2_lo = (h2 - h2_hi.astype(F32)).astype(BF16)

    r2 = _dot_nt(rw_ref[...], h2_hi)
    logits = (r2[0:N_EXPERTS] + r2[N_EXPERTS:2 * N_EXPERTS]
              + _dot_nt(rw_ref[0:N_EXPERTS, :], h2_lo) + rb_ref[...])
    e_iota = lax.broadcasted_iota(jnp.int32, logits.shape, 0).astype(F32)
    work = logits
    vals, idxs = [], []
    for _ in range(TOP_K):
        mval = jnp.max(work, axis=0, keepdims=True)
        midx = jnp.min(jnp.where(work == mval, e_iota, float(N_EXPERTS)), axis=0, keepdims=True)
        vals.append(mval)
        idxs.append(midx)
        work = jnp.where(e_iota == midx, -jnp.inf, work)
    ex = [jnp.exp(vv - vals[0]) for vv in vals]
    den = ex[0] + ex[1] + ex[2] + ex[3]
    onehot = jnp.zeros_like(logits)
    for kk in range(TOP_K):
        onehot = onehot + jnp.where(e_iota == idxs[kk], 1.0, 0.0)
    base = carry_ref[:, 0:1] + _dot(onehot.astype(BF16), tri_ref[...])
    for kk in range(TOP_K):
        gate_ref[bb, 0, kk:kk + 1, :] = ex[kk] / den
        idx_ref[bb, 0, kk:kk + 1, :] = idxs[kk].astype(jnp.int32)
        rk_k = jnp.sum(jnp.where(e_iota == idxs[kk], base, 0.0), axis=0, keepdims=True)
        rank_ref[bb, 0, kk:kk + 1, :] = rk_k.astype(jnp.int32)
    carry_ref[...] = carry_ref[...] + jnp.sum(onehot, axis=1, keepdims=True)


def output_projection(x_all, mla_lat, mla_ctx, sg, retp, rv, st, mod_tab, lw, n_lat_tiles):
    b, t, d = x_all.shape
    nt = t // TM
    cpt = TM // CHUNK
    const2 = lambda j: (0, 0)
    const3 = lambda j: (0, 0, 0)
    tok_spec = lambda w: pl.BlockSpec((b, TM, w), lambda j: (0, j, 0))
    route_spec = pl.BlockSpec((b, 1, TOP_K, TM), lambda j: (0, j, 0, 0))
    route_shape = lambda dt: jax.ShapeDtypeStruct((b, nt, TOP_K, TM), dt)
    h2w = d // 2 if PACK_ROWS else d
    return pl.pallas_call(
        functools.partial(_outproj_kernel, n_lat_tiles=n_lat_tiles),
        grid=(nt,),
        in_specs=[
            tok_spec(d),
            pl.BlockSpec((b, TM, MLA_OUT), lambda j: (0, jnp.minimum(j, n_lat_tiles - 1), 0)),
            pl.BlockSpec((b, TM, MLA_OUT), lambda j: (0, jnp.maximum(j - n_lat_tiles, 0), 0)),
            tok_spec(SG_WIDTH), tok_spec(512), tok_spec(RET_OUT),
            pl.BlockSpec((b, cpt, 2, RET_QK, RET_OUT), lambda j: (0, j, 0, 0, 0)),
            pl.BlockSpec((b, 1, N_MOD, d), lambda j: (0, j // n_lat_tiles, 0, 0)),
            pl.BlockSpec((RET_HEADS, CHUNK, CHUNK), const3),
            pl.BlockSpec((2, CHUNK, 128), const3),
            pl.BlockSpec((128, RET_OUT), const2),
            pl.BlockSpec((RET_OUT, RET_OUT), const2),
            pl.BlockSpec((d, d), const2),
            pl.BlockSpec((1, d), const2),
            pl.BlockSpec((1, d), const2),
            pl.BlockSpec((2 * N_EXPERTS, d), const2),
            pl.BlockSpec((N_EXPERTS, 1), const2),
            pl.BlockSpec((TM, TM), const2),
        ],
        out_specs=(tok_spec(d), tok_spec(h2w), route_spec, route_spec, route_spec,
                   pl.BlockSpec((N_EXPERTS, 128), const2)),
        out_shape=(jax.ShapeDtypeStruct((b, t, d), F32),
                   jax.ShapeDtypeStruct((b, t, h2w), jnp.uint32 if PACK_ROWS else F32),
                   route_shape(jnp.int32), route_shape(F32), route_shape(jnp.int32),
                   jax.ShapeDtypeStruct((N_EXPERTS, 128), jnp.int32)),
        scratch_shapes=[pltpu.VMEM((N_EXPERTS, 128), F32)],
        compiler_params=_cparams(("arbitrary",)),
        name="output_projection",
    )(x_all, mla_lat, mla_ctx, sg, retp, rv, st, mod_tab, lw["ret_m"], lw["qdec"], lw["bd"], lw["seg"],
      lw["w_o"], lw["ln1_g"], lw["ln1_b"], lw["router_w"], lw["router_b"], lw["tri"])


_DEINT = 256


def _expert_kernel(be_ref, nu_ref, x_ref, wgu_ref, bg_ref, bl_ref, wd_ref, bd_ref, perm_ref, y_ref,
                   wg_s, wl_s, wd_s):
    i = pl.program_id(0)
    active = i < nu_ref[0]
    fresh = jnp.logical_or(i == 0, be_ref[i] != be_ref[jnp.maximum(i - 1, 0)])

    @pl.when(jnp.logical_and(active, fresh))
    def _():
        half = _DEINT // 2
        for c in range(2 * D_EXPERT // _DEINT):
            r = _dot(_mx(wgu_ref[0, 0, :, _DEINT * c:_DEINT * (c + 1)]), perm_ref[...])
            wg_s[:, half * c:half * (c + 1)] = r[:, :half].astype(wg_s.dtype)
            wl_s[:, half * c:half * (c + 1)] = r[:, half:].astype(wl_s.dtype)
        wd_s[...] = wd_ref[0, 0].astype(wd_s.dtype)

    @pl.when(active)
    def _():
        if PACK_ROWS:
            w = x_ref[...]
            hi = lax.bitcast_convert_type(w & jnp.uint32(0xFFFF0000), F32).astype(BF16)
            lo = lax.bitcast_convert_type(w << 16, F32).astype(BF16)
            xb = jnp.concatenate([hi, lo], axis=1)
        else:
            xb = x_ref[...]
        glu = jnp.minimum(_dot(xb, wg_s[...]) + bg_ref[0], SWIGLU_LIMIT)
        lin = jnp.clip(_dot(xb, wl_s[...]) + bl_ref[0], -SWIGLU_LIMIT, SWIGLU_LIMIT)
        act = glu * _sigmoid(SWIGLU_ALPHA * glu) * (lin + 1.0)
        y_ref[...] = _dot(_mx(act), wd_s[...]) + bd_ref[0]

    @pl.when(jnp.logical_not(active))
    def _():
        y_ref[...] = jnp.zeros_like(y_ref)


def expert_ffn(xg, block_e, n_used, w_gate_up, w_down, li, lw, after):
    cap, xw = xg.shape
    d = D_MODEL
    nb = cap // MOE_BM
    de = D_EXPERT
    xmap = lambda i, be, nu: (jnp.minimum(i, nu[0] - 1), 0)
    wmap = lambda i, be, nu: (be[i], 0, 0)
    lmap = lambda i, be, nu: (li, be[i], 0, 0)
    grid_spec = pltpu.PrefetchScalarGridSpec(
        num_scalar_prefetch=2,
        grid=(nb,),
        in_specs=[pl.BlockSpec((MOE_BM, xw), xmap),
                  pl.BlockSpec((1, 1, d, 2 * de), lmap),
                  pl.BlockSpec((1, 1, de), wmap), pl.BlockSpec((1, 1, de), wmap),
                  pl.BlockSpec((1, 1, de, d), lmap), pl.BlockSpec((1, 1, d), wmap),
                  pl.BlockSpec((_DEINT, _DEINT), lambda i, be, nu: (0, 0)),
                  _ORDER_SPEC],
        out_specs=pl.BlockSpec((MOE_BM, d), lambda i, be, nu: (i, 0)),
        scratch_shapes=[pltpu.VMEM((d, de), MXU_DT), pltpu.VMEM((d, de), MXU_DT),
                        pltpu.VMEM((de, d), MXU_DT)],
    )
    return pl.pallas_call(
        _ordered_after(_expert_kernel, 9),
        grid_spec=grid_spec,
        out_shape=jax.ShapeDtypeStruct((cap, d), F32),
        compiler_params=pltpu.CompilerParams(dimension_semantics=("arbitrary",),
                                             vmem_limit_bytes=EXPERT_VMEM_LIMIT),
        name="expert_ffn",
    )(block_e, n_used, xg, w_gate_up, lw["b_glu"], lw["b_lin"], w_down, lw["b_down"], lw["deint"], after)


def _combine_ln2_kernel(x_ref, y_ref, gate_ref, mod_ref, g_ref, b_ref, o_ref):
    gates = gate_ref[0]
    f = gates[:, 0:1] * y_ref[0, 0]
    for kk in range(1, TOP_K):
        f = f + gates[:, kk:kk + 1] * y_ref[kk, 0]
    mod = mod_ref[0, 0]
    o_ref[0] = _ln(DEEPNORM_ALPHA * x_ref[0] + mod[5:6] * f) * g_ref[...] + b_ref[...]


def combine_deepnorm2(x1, yg, gates, mod_tab, lw, n_lat_tiles, after):
    b, t, d = x1.shape
    tok = pl.BlockSpec((1, TM, d), lambda bi, j: (bi, j, 0))
    vec = pl.BlockSpec((1, d), lambda bi, j: (0, 0))
    return pl.pallas_call(
        _ordered_after(_combine_ln2_kernel, 6),
        grid=(b, t // TM),
        in_specs=[tok,
                  pl.BlockSpec((TOP_K, 1, TM, d), lambda bi, j: (0, bi, j, 0)),
                  pl.BlockSpec((1, TM, TOP_K), lambda bi, j: (bi, j, 0)),
                  pl.BlockSpec((1, 1, N_MOD, d), lambda bi, j: (bi, j // n_lat_tiles, 0, 0)), vec, vec,
                  _ORDER_SPEC],
        out_specs=tok,
        out_shape=jax.ShapeDtypeStruct((b, t, d), F32),
        compiler_params=_cparams(("parallel", "parallel")),
        name="combine_deepnorm2",
    )(x1, yg, gates, mod_tab, lw["ln2_g"], lw["ln2_b"], after)


def _rotation_tables(s_len, lc):
    rows = s_len // GRID_W
    row = jnp.broadcast_to(jnp.arange(rows, dtype=F32)[:, None], (rows, GRID_W)).reshape(-1)
    col = jnp.broadcast_to(jnp.arange(GRID_W, dtype=F32)[None, :], (rows, GRID_W)).reshape(-1)
    inv = ROPE_BASE ** (-jnp.arange(ROPE_AXIS_FREQS, dtype=F32) / ROPE_AXIS_FREQS)
    ar, ac = row[:, None] * inv, col[:, None] * inv
    c64 = jnp.concatenate([jnp.cos(ar), jnp.cos(ar), jnp.cos(ac), jnp.cos(ac)], axis=1)
    s64 = jnp.concatenate([-jnp.sin(ar), jnp.sin(ar), -jnp.sin(ac), jnp.sin(ac)], axis=1)
    c64 = jnp.concatenate([c64, jnp.ones((lc, 64), F32)], axis=0)
    s64 = jnp.concatenate([s64, jnp.zeros((lc, 64), F32)], axis=0)
    half = RET_QK // 2
    pos = jnp.concatenate([lc + jnp.arange(s_len, dtype=F32), jnp.arange(lc, dtype=F32)])
    inv_r = 1.0 / (RET_ROPE_BASE ** jnp.linspace(0.0, 1.0, half, dtype=F32))
    ang = pos[:, None] * inv_r
    rc = jnp.tile(jnp.concatenate([jnp.cos(ang), jnp.cos(ang)], axis=1), (1, RET_HEADS))
    rs = jnp.tile(jnp.concatenate([-jnp.sin(ang), jnp.sin(ang)], axis=1), (1, RET_HEADS))
    qs = RET_QK ** -0.5
    return jnp.concatenate([
        jnp.tile(c64, (1, MLA_HEADS)) * MLA_SCALE, jnp.tile(s64, (1, MLA_HEADS)) * MLA_SCALE,
        c64, s64, rc * qs, rs * qs, rc, rs], axis=1)


def _in_perm():
    a = np.arange
    return np.concatenate([
        a(0, 640), a(704, 1216), a(1216, 1344), a(1344, 1472),
        1216 + _swap16(a(128)), 1344 + _swap16(a(128)), a(1472, 1984),
        640 + a(64), 640 + _swap16(a(64))])


def _uq_perm():
    a = np.arange
    nope = [h * MLA_QK + a(MLA_NOPE) for h in range(MLA_HEADS)]
    rope = [h * MLA_QK + MLA_NOPE + a(MLA_ROPE) for h in range(MLA_HEADS)]
    part = [h * MLA_QK + MLA_NOPE + _swap16(a(MLA_ROPE)) for h in range(MLA_HEADS)]
    return np.concatenate(nope + rope + part)


def _layer_weights(p):
    nl = p["w_in"].shape[0]
    lgf = jax.nn.log_sigmoid(p["ret_decay_fwd"].astype(F32))
    lgb = jax.nn.log_sigmoid(p["ret_decay_bwd"].astype(F32))
    h128 = np.arange(128) // RET_QK
    h256 = np.arange(RET_OUT) // RET_V
    a = jnp.arange(CHUNK, dtype=F32)[None, :, None]
    lf, lb = lgf[:, h128][:, None, :], lgb[:, h128][:, None, :]
    kdec = jnp.stack([jnp.exp(lf * (CHUNK - 1.0 - a)), jnp.exp(lb * a)], axis=1)
    qdec = jnp.stack([jnp.exp(lf * (a + 1.0)), jnp.exp(lb * (CHUNK - a))], axis=1)
    i = jnp.arange(CHUNK, dtype=F32)[:, None]
    j = jnp.arange(CHUNK, dtype=F32)[None, :]
    dif = (i - j)[None, None]
    ret_m = jnp.where(dif >= 0, jnp.exp(lgf[:, :, None, None] * jnp.maximum(dif, 0.0)),
                      jnp.exp(lgb[:, :, None, None] * jnp.maximum(-dif, 0.0)))
    cd = jnp.stack([jnp.exp(lgf[:, h256] * CHUNK), jnp.exp(lgb[:, h256] * CHUNK)], axis=1)[:, :, None, :]
    bd = (h128[:, None] == h256[None, :]).astype(np.float32)
    seg = (h256[:, None] == h256[None, :]).astype(np.float32)
    tri = (np.arange(TM)[:, None] < np.arange(TM)[None, :]).astype(np.float32)
    jj = np.arange(_DEINT // 2)
    deint = np.zeros((_DEINT, _DEINT), np.float32)
    deint[2 * jj, jj] = 1.0
    deint[2 * jj + 1, _DEINT // 2 + jj] = 1.0
    rw_t = jnp.swapaxes(p["router_w"], 1, 2)
    rw_hi = rw_t.astype(BF16)
    rw_lo = (rw_t - rw_hi.astype(F32)).astype(BF16)
    sg_bias = jnp.repeat(jnp.swapaxes(p["sg_b"], 1, 2), SG_WIDTH // SG_GROUPS, axis=2)
    bgu = p["b_gate_up"]
    return {
        "w_in": p["w_in"][:, :, _in_perm()].astype(MXU_DT),
        "q_g": p["mla_q_norm_g"][:, None, :], "kv_g": p["mla_kv_norm_g"][:, None, :],
        "w_uq": p["mla_w_uq"][:, :, _uq_perm()].astype(MXU_DT),
        "w_ukv": p["mla_w_ukv"].astype(MXU_DT),
        "sg_g": p["sg_norm_g"][:, None, :], "sg_b": p["sg_norm_b"][:, None, :],
        "sg_w": p["sg_w"].reshape(nl, SG_GROUPS * CHUNK, CHUNK).astype(MXU_DT),
        "sg_bias": sg_bias,
        "kdec": kdec, "qdec": qdec, "ret_m": ret_m, "cd": cd,
        "bd": jnp.broadcast_to(jnp.asarray(bd), (nl,) + bd.shape),
        "seg": jnp.broadcast_to(jnp.asarray(seg, BF16), (nl,) + seg.shape),
        "tri": jnp.broadcast_to(jnp.asarray(tri, BF16), (nl,) + tri.shape),
        "w_o": p["w_o"].astype(MXU_DT),
        "ln1_g": p["ln1_g"][:, None, :], "ln1_b": p["ln1_b"][:, None, :],
        "ln2_g": p["ln2_g"][:, None, :], "ln2_b": p["ln2_b"][:, None, :],
        "router_w": jnp.concatenate([rw_hi, rw_lo], axis=1),
        "router_b": p["router_b"][:, :, None],
        "b_glu": bgu[:, :, None, 0::2], "b_lin": bgu[:, :, None, 1::2],
        "b_down": p["b_down"][:, :, None, :],
        "deint": jnp.broadcast_to(jnp.asarray(deint, MXU_DT), (nl,) + deint.shape),
    }


def _route(idx, rank, counts):
    n_assign = idx.shape[1] * TOP_K
    nb = -(-(n_assign + N_EXPERTS * (MOE_BM - 1)) // MOE_BM)
    padded = (counts + MOE_BM - 1) // MOE_BM * MOE_BM
    pad_end = jnp.cumsum(padded)
    pad_start = pad_end - padded
    experts = jnp.arange(N_EXPERTS, dtype=jnp.int32)
    dest = rank + jnp.sum(jnp.where(idx[..., None] == experts, pad_start, 0), axis=-1)
    blk_start = jnp.arange(nb, dtype=jnp.int32) * MOE_BM
    block_e = jnp.minimum(jnp.sum((pad_end[None, :] <= blk_start[:, None]).astype(jnp.int32), axis=1),
                          N_EXPERTS - 1)
    n_used = (pad_end[-1] // MOE_BM).astype(jnp.int32).reshape(1)
    return dest.astype(jnp.int32), block_e, n_used, nb


def _sc_workers():
    info = plsc.get_sparse_core_info()
    return info.num_cores, info.num_cores * info.num_subcores


def sc_dispatch(rows, dest3, cap):
    n, w = rows.shape
    nch, kk, c = dest3.shape
    ncores, nw = _sc_workers()
    per_w = nch // nw
    mesh = plsc.VectorSubcoreMesh(core_axis_name="c", subcore_axis_name="s")

    @functools.partial(
        pl.kernel, mesh=mesh, out_type=jax.ShapeDtypeStruct((cap, w), rows.dtype),
        scratch_types=[pltpu.VMEM((kk, c), jnp.int32), pltpu.VMEM((c, w), rows.dtype)])
    def scatter_rows(h_hbm, d_hbm, o_hbm, idx_v, rows_v):
        wid = lax.axis_index("s") * ncores + lax.axis_index("c")

        @pl.loop(0, per_w)
        def _(j):
            ch = wid * per_w + j
            pltpu.sync_copy(d_hbm.at[ch], idx_v)
            pltpu.sync_copy(h_hbm.at[pl.ds(ch * c, c)], rows_v)
            for q in range(kk):
                pltpu.sync_copy(rows_v, o_hbm.at[idx_v.at[q]])

    return scatter_rows(rows, dest3)


def sc_combine_gather(y, dest3, n):
    cap, d = y.shape
    nch, kk, c = dest3.shape
    ncores, nw = _sc_workers()
    per_w = nch // nw
    mesh = plsc.VectorSubcoreMesh(core_axis_name="c", subcore_axis_name="s")

    @functools.partial(
        pl.kernel, mesh=mesh, out_type=jax.ShapeDtypeStruct((kk, n, d), y.dtype),
        scratch_types=[pltpu.VMEM((kk, c), jnp.int32), pltpu.VMEM((c, d), y.dtype)])
    def gather_rows(y_hbm, d_hbm, o_hbm, idx_v, rows_v):
        wid = lax.axis_index("s") * ncores + lax.axis_index("c")

        @pl.loop(0, per_w)
        def _(j):
            ch = wid * per_w + j
            pltpu.sync_copy(d_hbm.at[ch], idx_v)
            for q in range(kk):
                pltpu.sync_copy(y_hbm.at[idx_v.at[q]], rows_v)
                pltpu.sync_copy(rows_v, o_hbm.at[q, pl.ds(ch * c, c)])

    return gather_rows(y, dest3)


def kernel(x, c, ctx, c_ctx, ada_w, ada_b, w_in, mla_q_norm_g, mla_kv_norm_g, mla_w_uq, mla_w_ukv,
           sg_norm_g, sg_norm_b, sg_w, sg_b, ret_decay_fwd, ret_decay_bwd, w_o, ln1_g, ln1_b,
           router_w, router_b, w_gate_up, b_gate_up, w_down, b_down, ln2_g, ln2_b):
    b, s_len, d = x.shape
    lc = ctx.shape[1]
    assert d == D_MODEL and lc % TM == 0 and s_len % lc == 0 and s_len % GRID_W == 0
    assert b + 1 <= 8
    t = s_len + lc
    n_lat_tiles = s_len // TM
    params = dict(w_in=w_in, mla_q_norm_g=mla_q_norm_g, mla_kv_norm_g=mla_kv_norm_g, mla_w_uq=mla_w_uq,
                  mla_w_ukv=mla_w_ukv, sg_norm_g=sg_norm_g, sg_norm_b=sg_norm_b, sg_w=sg_w, sg_b=sg_b,
                  ret_decay_fwd=ret_decay_fwd, ret_decay_bwd=ret_decay_bwd, w_o=w_o, ln1_g=ln1_g,
                  ln1_b=ln1_b, router_w=router_w, router_b=router_b, w_gate_up=w_gate_up,
                  b_gate_up=b_gate_up, w_down=w_down, b_down=b_down, ln2_g=ln2_g, ln2_b=ln2_b)
    lws = _layer_weights(params)
    tab = _rotation_tables(s_len, lc)

    c_rows = jnp.concatenate([c, c_ctx[None, :], jnp.zeros((8 - b - 1, d), F32)], axis=0)
    mod = ada_modulation(c_rows, ada_w, ada_b).reshape(DEPTH, 8, N_MOD, d)
    mod_tab = jnp.stack([mod[:, :b], jnp.broadcast_to(mod[:, b:b + 1], (DEPTH, b, N_MOD, d))], axis=2)

    n_streams = N_STREAMS if b % N_STREAMS == 0 else 1
    bs = b // n_streams
    xs = [jnp.concatenate([x[i * bs:(i + 1) * bs], ctx[i * bs:(i + 1) * bs]], axis=1)
          for i in range(n_streams)]
    order = c_rows
    held = None
    for li in range(DEPTH):
        lw = {k: v[li] for k, v in lws.items()}
        mts = [mod_tab[li, si * bs:(si + 1) * bs] for si in range(n_streams)]
        fronts = []
        for si in range(n_streams):
            if held is not None and si == n_streams - 1:
                xs[si] = _finish(*held, after=order)
                held = None
            fr = _front(xs[si], mts[si], tab, lw, s_len, lc, after=order)
            order = fr["cnt"]
            fronts.append(fr)
        ys = []
        for si in range(n_streams):
            y = expert_ffn(fronts[si]["xg"], fronts[si]["block_e"], fronts[si]["n_used"], w_gate_up, w_down,
                           li, lw, after=order)
            order = y
            ys.append(y)
        for si in range(n_streams):
            args = (fronts[si], ys[si], mts[si], lw, s_len)
            if si == n_streams - 1 and n_streams > 1 and li + 1 < DEPTH:
                held = args
            else:
                xs[si] = _finish(*args, after=order)
                order = xs[si]
    return jnp.concatenate([xi[:, :s_len] for xi in xs], axis=0)


def _front(x_all, mt, tab, lw, s_len, lc, after):
    b, t, d = x_all.shape
    n_lat_tiles = s_len // TM
    q, k, v, sg, retp, rv, a = input_projection(x_all, mt, tab, lw, n_lat_tiles, after)
    mla_lat, mla_ctx = mla_attention(q, k, v, s_len, lc)
    st = retention_scan(a, lw["cd"], s_len // CHUNK)
    x1, h2, idx, gates, rank, cnt = output_projection(
        x_all, mla_lat, mla_ctx, sg, retp, rv, st, mt, lw, n_lat_tiles)
    to_tok = lambda z: z.transpose(2, 0, 1, 3).reshape(TOP_K, b * t)
    dest, block_e, n_used, nb = _route(to_tok(idx), to_tok(rank), cnt[:, 0])
    assert (b * t) % SC_CHUNK == 0
    dest3 = dest.reshape(TOP_K, (b * t) // SC_CHUNK, SC_CHUNK).transpose(1, 0, 2)
    xg = sc_dispatch(h2.reshape(b * t, h2.shape[-1]), dest3, nb * MOE_BM)
    return dict(x1=x1, gates=gates, cnt=cnt, dest3=dest3, xg=xg, block_e=block_e, n_used=n_used)


def _finish(fr, y, mt, lw, s_len, after):
    b, t, d = fr["x1"].shape
    yg = sc_combine_gather(y, fr["dest3"], b * t).reshape(TOP_K, b, t, d)
    gates_tok = fr["gates"].transpose(0, 1, 3, 2).reshape(b, t, TOP_K)
    return combine_deepnorm2(fr["x1"], yg, gates_tok, mt, lw, s_len // TM, after)
```

```python
import functools

import numpy as np
import jax
import jax.numpy as jnp
from jax import lax
from jax.experimental import pallas as pl
from jax.experimental.pallas import tpu as pltpu
from jax.experimental.pallas import tpu_sc as plsc

F32 = jnp.float32
BF16 = jnp.bfloat16
MXU_DT = BF16
PACK_ROWS = True

D_MODEL = 1024
DEPTH = 4
GRID_W = 64
MLA_HEADS = 4
MLA_NOPE = 128
MLA_ROPE = 64
MLA_V = 128
MLA_Q_LORA = 384
MLA_KV_LORA = 256
MLA_QK = MLA_NOPE + MLA_ROPE
MLA_SCALE = MLA_QK ** -0.5
ROPE_BASE = 10000.0
ROPE_AXIS_FREQS = MLA_ROPE // 4
SG_GROUPS = 4
SG_WIDTH = 256
SG_CHUNK = 128
RET_HEADS = 4
RET_QK = 32
RET_V = 64
RET_CHUNK = 128
RET_ROPE_BASE = 10000.0
N_EXPERTS = 32
TOP_K = 4
D_EXPERT = 1024
SWIGLU_LIMIT = 7.0
SWIGLU_ALPHA = 1.702
N_MOD = 6
LN_EPS = 1e-5
RMS_EPS = 1e-6
DEEPNORM_ALPHA = (2 * DEPTH) ** 0.25
MLA_OUT = MLA_HEADS * MLA_V
RET_OUT = RET_HEADS * RET_V

TM = 256
CHUNK = 128
MOE_BM = 256
SC_CHUNK = 48
N_STREAMS = 2
ATT_TQ = 512
ATT_TK = 1024
VMEM_LIMIT = 48 * 2 ** 20

_O_CQ, _O_CKV, _O_SGU, _O_SGV = 0, 384, 640, 896
_O_RQ, _O_RK, _O_RQS, _O_RKS, _O_RV, _O_RG, _O_KR = 1152, 1280, 1408, 1536, 1664, 1920, 2176
IN_P = 2304
_T_QC, _T_QS, _T_KCS, _T_RQC, _T_RQS, _T_RKC, _T_RKS = 0, 256, 512, 640, 768, 896, 1024
TAB_W = 1152


def _cparams(sem):
    return pltpu.CompilerParams(dimension_semantics=sem, vmem_limit_bytes=VMEM_LIMIT)


def _dot(a, b):
    return jnp.dot(a, b, preferred_element_type=F32)


def _dot_nt(a, b):
    return lax.dot_general(a, b, (((1,), (1,)), ((), ())), preferred_element_type=F32)


def _mx(a):
    return a.astype(MXU_DT)


def _swap16(j):
    return (j // 32) * 32 + ((j % 32) + 16) % 32


_ERF_ALPHA = (-2.72614225801306e-10, 2.77068142495902e-08, -2.10102402082508e-06,
              -5.69250639462346e-05, -7.34990630326855e-04, -2.95459980854025e-03,
              -1.60960333262415e-02)
_ERF_BETA = (-1.45660718464996e-05, -2.13374055278905e-04, -1.68282697438203e-03,
             -7.37332916720468e-03, -1.42647390514189e-02)


def _erf(x):
    x = jnp.clip(x, -4.0, 4.0)
    x2 = x * x
    p = jnp.full_like(x, _ERF_ALPHA[0])
    for c in _ERF_ALPHA[1:]:
        p = p * x2 + c
    q = jnp.full_like(x, _ERF_BETA[0])
    for c in _ERF_BETA[1:]:
        q = q * x2 + c
    return x * p / q


def _gelu(x):
    return 0.5 * x * (1.0 + _erf(x * 0.7071067811865476))


def _sigmoid(x):
    return 1.0 / (1.0 + jnp.exp(-x))


def _ln(x):
    xc = x - jnp.mean(x, axis=-1, keepdims=True)
    return xc * lax.rsqrt(jnp.mean(xc * xc, axis=-1, keepdims=True) + LN_EPS)


def _lane_group(shape, width):
    return lax.broadcasted_iota(jnp.int32, shape, len(shape) - 1) // width


def _ada_kernel(c_ref, w_ref, b_ref, o_ref):
    c = c_ref[...]
    o_ref[0] = _dot(c * _sigmoid(c), w_ref[0]) + b_ref[0]


def ada_modulation(c_rows, ada_w, ada_b):
    nl, d, n = ada_w.shape
    tn = 1536
    return pl.pallas_call(
        _ada_kernel,
        grid=(nl, n // tn),
        in_specs=[pl.BlockSpec((8, d), lambda l, j: (0, 0)),
                  pl.BlockSpec((1, d, tn), lambda l, j: (l, 0, j)),
                  pl.BlockSpec((1, 1, tn), lambda l, j: (l, 0, j))],
        out_specs=pl.BlockSpec((1, 8, tn), lambda l, j: (l, 0, j)),
        out_shape=jax.ShapeDtypeStruct((nl, 8, n), F32),
        compiler_params=_cparams(("arbitrary", "arbitrary")),
        name="ada_modulation",
    )(c_rows, ada_w, ada_b.reshape(nl, 1, n))


def _inproj_kernel(x_ref, mod_ref, tab_ref, w_in_ref, qg_ref, kvg_ref, w_uq_ref, w_ukv_ref,
                   sgg_ref, sgb_ref, sgw_ref, sgbias_ref, kdec_ref, bd_ref,
                   q_ref, k_ref, v_ref, sg_ref, retp_ref, rv_ref, a_ref):
    x = x_ref[0]
    mod = mod_ref[0, 0]
    h = x * (1.0 + mod[1:2]) + mod[0:1]
    p = _dot(_mx(h), w_in_ref[...])
    tab = tab_ref[...]

    cq = p[:, _O_CQ:_O_CQ + MLA_Q_LORA]
    cq = cq * lax.rsqrt(jnp.mean(cq * cq, axis=-1, keepdims=True) + RMS_EPS) * qg_ref[...]
    qa = _dot(_mx(cq), w_uq_ref[...])
    rot = (qa[:, 512:768] * tab[:, _T_QC:_T_QC + 256]
           + qa[:, 768:1024] * tab[:, _T_QS:_T_QS + 256])
    for hh in range(MLA_HEADS):
        q_ref[0, hh, :, 0:128] = (qa[:, 128 * hh:128 * hh + 128] * MLA_SCALE).astype(q_ref.dtype)
        g = hh // 2
        q_ref[0, hh, :, 128:256] = rot[:, 128 * g:128 * g + 128].astype(q_ref.dtype)

    ckv = p[:, _O_CKV:_O_CKV + MLA_KV_LORA]
    ckv = ckv * lax.rsqrt(jnp.mean(ckv * ckv, axis=-1, keepdims=True) + RMS_EPS) * kvg_ref[...]
    kv = _dot(_mx(ckv), w_ukv_ref[...])
    t = p[:, _O_KR:_O_KR + 128] * tab[:, _T_KCS:_T_KCS + 128]
    u = t + pltpu.roll(t, 64, axis=1)
    low = lax.broadcasted_iota(jnp.int32, u.shape, 1) < 64
    kx = (jnp.where(low, u, 0.0), jnp.where(low, 0.0, u))
    ones_col = jnp.where(lax.broadcasted_iota(jnp.int32, u.shape, 1) == 0, 1.0, 0.0).astype(v_ref.dtype)
    for hh in range(MLA_HEADS):
        k_ref[0, hh, :, 0:128] = kv[:, 256 * hh:256 * hh + 128].astype(k_ref.dtype)
        k_ref[0, hh, :, 128:256] = kx[hh % 2].astype(k_ref.dtype)
        v_ref[0, hh, :, 0:128] = kv[:, 256 * hh + 128:256 * hh + 256].astype(v_ref.dtype)
        v_ref[0, hh, :, 128:256] = ones_col

    gu = _gelu(p[:, _O_SGU:_O_SGU + SG_WIDTH])
    gv = _ln(_gelu(p[:, _O_SGV:_O_SGV + SG_WIDTH])) * sgg_ref[...] + sgb_ref[...]
    gvm = _mx(gv)
    grp = _lane_group((CHUNK, SG_WIDTH), SG_WIDTH // SG_GROUPS)
    for c in range(TM // CHUNK):
        rows = slice(c * CHUNK, (c + 1) * CHUNK)
        res = _dot(sgw_ref[...], gvm[rows])
        mixed = sgbias_ref[...]
        for g in range(SG_GROUPS):
            mixed = mixed + jnp.where(grp == g, res[g * CHUNK:(g + 1) * CHUNK], 0.0)
        sg_ref[0, rows, :] = (gu[rows] * mixed).astype(sg_ref.dtype)

    rq = (p[:, _O_RQ:_O_RQ + 128] * tab[:, _T_RQC:_T_RQC + 128]
          + p[:, _O_RQS:_O_RQS + 128] * tab[:, _T_RQS:_T_RQS + 128])
    rk = (p[:, _O_RK:_O_RK + 128] * tab[:, _T_RKC:_T_RKC + 128]
          + p[:, _O_RKS:_O_RKS + 128] * tab[:, _T_RKS:_T_RKS + 128])
    rv = p[:, _O_RV:_O_RV + RET_OUT]
    retp_ref[0, :, 0:128] = rq
    retp_ref[0, :, 128:256] = rk
    retp_ref[0, :, 256:512] = p[:, _O_RG:_O_RG + RET_OUT]
    rvm = _mx(rv)
    rv_ref[0] = rvm.astype(rv_ref.dtype)
    bd = bd_ref[...]
    for c in range(TM // CHUNK):
        rows = slice(c * CHUNK, (c + 1) * CHUNK)
        for d in range(2):
            kd_t = _mx((rk[rows] * kdec_ref[d]).T)
            af = _dot(kd_t, rvm[rows]) * bd
            a_ref[0, c, d] = (af[0:32] + af[32:64]) + (af[64:96] + af[96:128])


def _ordered_after(kernel_fn, pos):
    def wrapped(*refs):
        return kernel_fn(*refs[:pos], *refs[pos + 1:])
    return wrapped


_ORDER_SPEC = pl.BlockSpec(memory_space=pl.ANY)


def input_projection(x_all, mod_tab, tab, lw, n_lat_tiles, after):
    b, t, d = x_all.shape
    nt = t // TM
    nc = t // CHUNK
    cpt = TM // CHUNK
    const2 = lambda bi, j: (0, 0)
    const3 = lambda bi, j: (0, 0, 0)
    out_shape = (
        jax.ShapeDtypeStruct((b, MLA_HEADS, t, 256), MXU_DT),
        jax.ShapeDtypeStruct((b, MLA_HEADS, t, 256), MXU_DT),
        jax.ShapeDtypeStruct((b, MLA_HEADS, t, 256), MXU_DT),
        jax.ShapeDtypeStruct((b, t, SG_WIDTH), MXU_DT),
        jax.ShapeDtypeStruct((b, t, 512), F32),
        jax.ShapeDtypeStruct((b, t, RET_OUT), MXU_DT),
        jax.ShapeDtypeStruct((b, nc, 2, RET_QK, RET_OUT), F32),
    )
    head_spec = lambda w: pl.BlockSpec((1, MLA_HEADS, TM, w), lambda bi, j: (bi, 0, j, 0))
    tok_spec = lambda w: pl.BlockSpec((1, TM, w), lambda bi, j: (bi, j, 0))
    return pl.pallas_call(
        _ordered_after(_inproj_kernel, 14),
        grid=(b, nt),
        in_specs=[
            tok_spec(d),
            pl.BlockSpec((1, 1, N_MOD, d), lambda bi, j: (bi, j // n_lat_tiles, 0, 0)),
            pl.BlockSpec((TM, TAB_W), lambda bi, j: (j, 0)),
            pl.BlockSpec((d, IN_P), const2),
            pl.BlockSpec((1, MLA_Q_LORA), const2),
            pl.BlockSpec((1, MLA_KV_LORA), const2),
            pl.BlockSpec((MLA_Q_LORA, 1024), const2),
            pl.BlockSpec((MLA_KV_LORA, 1024), const2),
            pl.BlockSpec((1, SG_WIDTH), const2),
            pl.BlockSpec((1, SG_WIDTH), const2),
            pl.BlockSpec((SG_GROUPS * CHUNK, CHUNK), const2),
            pl.BlockSpec((CHUNK, SG_WIDTH), const2),
            pl.BlockSpec((2, CHUNK, 128), const3),
            pl.BlockSpec((128, RET_OUT), const2),
            _ORDER_SPEC,
        ],
        out_specs=(head_spec(256), head_spec(256), head_spec(256), tok_spec(SG_WIDTH),
                   tok_spec(512), tok_spec(RET_OUT),
                   pl.BlockSpec((1, cpt, 2, RET_QK, RET_OUT), lambda bi, j: (bi, j, 0, 0, 0))),
        out_shape=out_shape,
        compiler_params=_cparams(("parallel", "parallel")),
        name="input_projection",
    )(x_all, mod_tab, tab, lw["w_in"], lw["q_g"], lw["kv_g"], lw["w_uq"], lw["w_ukv"],
      lw["sg_g"], lw["sg_b"], lw["sg_w"], lw["sg_bias"], lw["kdec"], lw["bd"], after)


def _attn_kernel(q_ref, k_ref, v_ref, o_ref, *, n_main, tk, tail):
    q = q_ref[0, 0]
    tq = q.shape[0]
    chunks = [(i * tk, tk) for i in range(n_main)] + ([(n_main * tk, tail)] if tail else [])

    def scores(ci):
        start, size = chunks[ci]
        return _dot_nt(q, k_ref[0, 0, start:start + size, :])

    m = jnp.full((tq, 1), -1e30, F32)
    acc = jnp.zeros((tq, 256), F32)
    s_next = scores(0)
    for ci, (start, size) in enumerate(chunks):
        s = s_next
        if ci + 1 < len(chunks):
            s_next = scores(ci + 1)
        m_new = jnp.maximum(m, jnp.max(s, axis=-1, keepdims=True))
        p = jnp.exp(s - m_new)
        acc = jnp.exp(m - m_new) * acc + _dot(_mx(p), v_ref[0, 0, start:start + size, :])
        m = m_new
    o_ref[0] = (acc[:, 0:MLA_V] / acc[:, MLA_V:MLA_V + 1]).astype(o_ref.dtype)


def mla_attention(q, k, v, s_len, lc):
    b, hn, t, _ = q.shape
    tq = min(ATT_TQ, s_len)
    tk = min(ATT_TK, s_len)
    kv_full = pl.BlockSpec((1, 1, t, 256), lambda bi, hi, i: (bi, hi, 0, 0))
    out_lat = pl.pallas_call(
        functools.partial(_attn_kernel, n_main=s_len // tk, tk=tk, tail=lc),
        grid=(b, hn, s_len // tq),
        in_specs=[pl.BlockSpec((1, 1, tq, 256), lambda bi, hi, i: (bi, hi, i, 0)), kv_full, kv_full],
        out_specs=pl.BlockSpec((1, tq, MLA_V), lambda bi, hi, i: (bi, i, hi)),
        out_shape=jax.ShapeDtypeStruct((b, s_len, MLA_OUT), MXU_DT),
        compiler_params=_cparams(("parallel", "parallel", "arbitrary")),
        name="mla_attention_latent",
    )(q, k, v)
    cblk = s_len // lc
    ctx_spec = pl.BlockSpec((1, 1, lc, 256), lambda bi, hi: (bi, hi, cblk, 0))
    out_ctx = pl.pallas_call(
        functools.partial(_attn_kernel, n_main=0, tk=tk, tail=lc),
        grid=(b, hn),
        in_specs=[ctx_spec, ctx_spec, ctx_spec],
        out_specs=pl.BlockSpec((1, lc, MLA_V), lambda bi, hi: (bi, 0, hi)),
        out_shape=jax.ShapeDtypeStruct((b, lc, MLA_OUT), MXU_DT),
        compiler_params=_cparams(("parallel", "parallel")),
        name="mla_attention_context",
    )(q, k, v)
    return out_lat, out_ctx


def _ret_scan_kernel(a_ref, cd_ref, s_ref, *, n_lat_chunks):
    nc = a_ref.shape[1]
    ncc = nc - n_lat_chunks
    cd_f, cd_b = cd_ref[0], cd_ref[1]

    def body(n, carry):
        sf, sb = carry
        cf = jnp.where(n < ncc, n_lat_chunks + n, n - ncc)
        cb = jnp.where(n < ncc, nc - 1 - n, n_lat_chunks - 1 - (n - ncc))
        s_ref[0, cf, 0] = sf
        s_ref[0, cb, 1] = sb
        return sf * cd_f + a_ref[0, cf, 0], sb * cd_b + a_ref[0, cb, 1]

    zero = jnp.zeros((RET_QK, RET_OUT), F32)
    lax.fori_loop(0, nc, body, (zero, zero))


def retention_scan(a, cd, n_lat_chunks):
    b, nc = a.shape[:2]
    blk = pl.BlockSpec((1, nc, 2, RET_QK, RET_OUT), lambda bi: (bi, 0, 0, 0, 0))
    return pl.pallas_call(
        functools.partial(_ret_scan_kernel, n_lat_chunks=n_lat_chunks),
        grid=(b,),
        in_specs=[blk, pl.BlockSpec((2, 1, RET_OUT), lambda bi: (0, 0, 0))],
        out_specs=blk,
        out_shape=jax.ShapeDtypeStruct(a.shape, F32),
        compiler_params=_cparams(("parallel",)),
        name="retention_scan",
    )(a, cd)


def _split_dot(x, ones_bd):
    hi = x.astype(BF16)
    lo = (x - hi.astype(F32)).astype(BF16)
    return _dot(hi, ones_bd) + _dot(lo, ones_bd)


def _outproj_kernel(x_ref, mlal_ref, mlac_ref, sg_ref, retp_ref, rv_ref, st_ref, mod_ref,
                    m_ref, qdec_ref, bd_ref, seg_ref, w_o_ref, lng_ref, lnb_ref, rw_ref, rb_ref, tri_ref,
                    x1_ref, h2_ref, idx_ref, gate_ref, rank_ref, cnt_ref, carry_ref, *, n_lat_tiles):
    @pl.when(pl.program_id(0) == 0)
    def _():
        carry_ref[...] = jnp.zeros_like(carry_ref)

    for bb in range(x_ref.shape[0]):
        _outproj_tile(bb, x_ref, mlal_ref, mlac_ref, sg_ref, retp_ref, rv_ref, st_ref, mod_ref,
                      m_ref, qdec_ref, bd_ref, seg_ref, w_o_ref, lng_ref, lnb_ref, rw_ref, rb_ref, tri_ref,
                      x1_ref, h2_ref, idx_ref, gate_ref, rank_ref, carry_ref, n_lat_tiles)
    cnt_ref[...] = carry_ref[...].astype(jnp.int32)


def _outproj_tile(bb, x_ref, mlal_ref, mlac_ref, sg_ref, retp_ref, rv_ref, st_ref, mod_ref,
                  m_ref, qdec_ref, bd_ref, seg_ref, w_o_ref, lng_ref, lnb_ref, rw_ref, rb_ref, tri_ref,
                  x1_ref, h2_ref, idx_ref, gate_ref, rank_ref, carry_ref, n_lat_tiles):
    g32 = _lane_group((CHUNK, 128), RET_QK)
    g64 = _lane_group((CHUNK, RET_OUT), RET_V)
    seg = seg_ref[...]
    bd = bd_ref[...]
    ret_rows = []
    for c in range(TM // CHUNK):
        rows = slice(c * CHUNK, (c + 1) * CHUNK)
        rq = retp_ref[bb, rows, 0:128]
        rk = _mx(retp_ref[bb, rows, 128:256])
        rg = retp_ref[bb, rows, 256:512]
        rv = rv_ref[bb, rows, :]
        sf = _mx(jnp.concatenate([st_ref[bb, c, 0]] * RET_HEADS, axis=0) * bd)
        sb = _mx(jnp.concatenate([st_ref[bb, c, 1]] * RET_HEADS, axis=0) * bd)
        o = _dot(_mx(rq * qdec_ref[0]), sf) + _dot(_mx(rq * qdec_ref[1]), sb)
        for hh in range(RET_HEADS):
            s = _dot_nt(_mx(jnp.where(g32 == hh, rq, 0.0)), rk)
            oh = _dot(_mx(s * m_ref[hh]), rv)
            o = o + jnp.where(g64 == hh, oh, 0.0)
        mean = _split_dot(o, seg) * (1.0 / RET_V)
        oc = o - mean
        var = _split_dot(oc * oc, seg) * (1.0 / RET_V)
        ret_rows.append(oc * lax.rsqrt(var + LN_EPS) * (rg * _sigmoid(rg)))
    ret = jnp.concatenate(ret_rows, axis=0)

    mla = jnp.where(pl.program_id(0) >= n_lat_tiles, mlac_ref[bb], mlal_ref[bb])
    y = (_dot(mla, w_o_ref[0:MLA_OUT, :])
         + _dot(sg_ref[bb], w_o_ref[MLA_OUT:MLA_OUT + SG_WIDTH, :])
         + _dot(_mx(ret), w_o_ref[MLA_OUT + SG_WIDTH:, :]))
    mod = mod_ref[bb, 0]
    x1 = _ln(DEEPNORM_ALPHA * x_ref[bb] + mod[2:3] * y) * lng_ref[...] + lnb_ref[...]
    x1_ref[bb] = x1
    h2 = x1 * (1.0 + mod[4:5]) + mod[3:4]
    h2_hi = h2.astype(BF16)
    if PACK_ROWS:
        h2_ref[bb] = _pack_bf16_pairs(h2)
    else:
        h2_ref[bb] = h2
    h2_lo = (h2 - h2_hi.astype(F32)).astype(BF16)

    r2 = _dot_nt(rw_ref[...], h2_hi)
    logits = (r2[0:N_EXPERTS] + r2[N_EXPERTS:2 * N_EXPERTS]
              + _dot_nt(rw_ref[0:N_EXPERTS, :], h2_lo) + rb_ref[...])
    e_iota = lax.broadcasted_iota(jnp.int32, logits.shape, 0).astype(F32)
    work = logits
    vals, idxs = [], []
    for _ in range(TOP_K):
        mval = jnp.max(work, axis=0, keepdims=True)
        midx = jnp.min(jnp.where(work == mval, e_iota, float(N_EXPERTS)), axis=0, keepdims=True)
        vals.append(mval)
        idxs.append(midx)
        work = jnp.where(e_iota == midx, -jnp.inf, work)
    ex = [jnp.exp(vv - vals[0]) for vv in vals]
    den = ex[0] + ex[1] + ex[2] + ex[3]
    onehot = jnp.zeros_like(logits)
    for kk in range(TOP_K):
        onehot = onehot + jnp.where(e_iota == idxs[kk], 1.0, 0.0)
    base = carry_ref[:, 0:1] + _dot(onehot.astype(BF16), tri_ref[...])
    for kk in range(TOP_K):
        gate_ref[bb, 0, kk:kk + 1, :] = ex[kk] / den
        idx_ref[bb, 0, kk:kk + 1, :] = idxs[kk].astype(jnp.int32)
        rk_k = jnp.sum(jnp.where(e_iota == idxs[kk], base, 0.0), axis=0, keepdims=True)
        rank_ref[bb, 0, kk:kk + 1, :] = rk_k.astype(jnp.int32)
    carry_ref[...] = carry_ref[...] + jnp.sum(onehot, axis=1, keepdims=True)


def output_projection(x_all, mla_lat, mla_ctx, sg, retp, rv, st, mod_tab, lw, n_lat_tiles):
    b, t, d = x_all.shape
    nt = t // TM
    cpt = TM // CHUNK
    const2 = lambda j: (0, 0)
    const3 = lambda j: (0, 0, 0)
    tok_spec = lambda w: pl.BlockSpec((b, TM, w), lambda j: (0, j, 0))
    route_spec = pl.BlockSpec((b, 1, TOP_K, TM), lambda j: (0, j, 0, 0))
    route_shape = lambda dt: jax.ShapeDtypeStruct((b, nt, TOP_K, TM), dt)
    h2w = d // 2 if PACK_ROWS else d
    return pl.pallas_call(
        functools.partial(_outproj_kernel, n_lat_tiles=n_lat_tiles),
        grid=(nt,),
        in_specs=[
            tok_spec(d),
            pl.BlockSpec((b, TM, MLA_OUT), lambda j: (0, jnp.minimum(j, n_lat_tiles - 1), 0)),
            pl.BlockSpec((b, TM, MLA_OUT), lambda j: (0, jnp.maximum(j - n_lat_tiles, 0), 0)),
            tok_spec(SG_WIDTH), tok_spec(512), tok_spec(RET_OUT),
            pl.BlockSpec((b, cpt, 2, RET_QK, RET_OUT), lambda j: (0, j, 0, 0, 0)),
            pl.BlockSpec((b, 1, N_MOD, d), lambda j: (0, j // n_lat_tiles, 0, 0)),
            pl.BlockSpec((RET_HEADS, CHUNK, CHUNK), const3),
            pl.BlockSpec((2, CHUNK, 128), const3),
            pl.BlockSpec((128, RET_OUT), const2),
            pl.BlockSpec((RET_OUT, RET_OUT), const2),
            pl.BlockSpec((d, d), const2),
            pl.BlockSpec((1, d), const2),
            pl.BlockSpec((1, d), const2),
            pl.BlockSpec((2 * N_EXPERTS, d), const2),
            pl.BlockSpec((N_EXPERTS, 1), const2),
            pl.BlockSpec((TM, TM), const2),
        ],
        out_specs=(tok_spec(d), tok_spec(h2w), route_spec, route_spec, route_spec,
                   pl.BlockSpec((N_EXPERTS, 128), const2)),
        out_shape=(jax.ShapeDtypeStruct((b, t, d), F32),
                   jax.ShapeDtypeStruct((b, t, h2w), jnp.uint32 if PACK_ROWS else F32),
                   route_shape(jnp.int32), route_shape(F32), route_shape(jnp.int32),
                   jax.ShapeDtypeStruct((N_EXPERTS, 128), jnp.int32)),
        scratch_shapes=[pltpu.VMEM((N_EXPERTS, 128), F32)],
        compiler_params=_cparams(("arbitrary",)),
        name="output_projection",
    )(x_all, mla_lat, mla_ctx, sg, retp, rv, st, mod_tab, lw["ret_m"], lw["qdec"], lw["bd"], lw["seg"],
      lw["w_o"], lw["ln1_g"], lw["ln1_b"], lw["router_w"], lw["router_b"], lw["tri"])


_DEINT = 256


def _expert_weights_kernel(wgu_ref, wd_ref, perm_ref, wg_ref, wl_ref, wdo_ref):
    half = _DEINT // 2
    for c in range(2 * D_EXPERT // _DEINT):
        r = _dot(_mx(wgu_ref[0, 0, :, _DEINT * c:_DEINT * (c + 1)]), perm_ref[...])
        wg_ref[0, 0, :, half * c:half * (c + 1)] = r[:, :half].astype(wg_ref.dtype)
        wl_ref[0, 0, :, half * c:half * (c + 1)] = r[:, half:].astype(wl_ref.dtype)
    wdo_ref[0, 0] = wd_ref[0, 0].astype(wdo_ref.dtype)


def expert_weights(w_gate_up, w_down, deint):
    nl, ne, d, de2 = w_gate_up.shape
    de = de2 // 2
    blk = lambda r, c: pl.BlockSpec((1, 1, r, c), lambda l, e: (l, e, 0, 0))
    return pl.pallas_call(
        _expert_weights_kernel,
        grid=(nl, ne),
        in_specs=[blk(d, de2), blk(de, d), pl.BlockSpec((_DEINT, _DEINT), lambda l, e: (0, 0))],
        out_specs=(blk(d, de), blk(d, de), blk(de, d)),
        out_shape=(jax.ShapeDtypeStruct((nl, ne, d, de), MXU_DT), jax.ShapeDtypeStruct((nl, ne, d, de), MXU_DT),
                   jax.ShapeDtypeStruct((nl, ne, de, d), MXU_DT)),
        compiler_params=_cparams(("parallel", "parallel")),
        name="expert_weights",
    )(w_gate_up, w_down, deint)


def _pack_bf16_pairs(v):
    bits = lax.bitcast_convert_type(v.astype(BF16).astype(F32), jnp.uint32)
    half = bits.shape[1] // 2
    return bits[:, :half] | (bits[:, half:] >> 16)


def _unpack_bf16_pairs(w):
    return (lax.bitcast_convert_type(w & jnp.uint32(0xFFFF0000), F32),
            lax.bitcast_convert_type(w << 16, F32))


def _expert_kernel(be_ref, nu_ref, x_ref, wg_s, wl_s, bg_ref, bl_ref, wd_s, bd_ref, y_ref):
    del be_ref
    active = pl.program_id(0) < nu_ref[0]

    @pl.when(active)
    def _():
        if PACK_ROWS:
            hi, lo = _unpack_bf16_pairs(x_ref[...])
            xb = jnp.concatenate([hi.astype(BF16), lo.astype(BF16)], axis=1)
        else:
            xb = x_ref[...]
        glu = jnp.minimum(_dot(xb, wg_s[0, 0]) + bg_ref[0], SWIGLU_LIMIT)
        lin = jnp.clip(_dot(xb, wl_s[0, 0]) + bl_ref[0], -SWIGLU_LIMIT, SWIGLU_LIMIT)
        act = glu * _sigmoid(SWIGLU_ALPHA * glu) * (lin + 1.0)
        y = _dot(_mx(act), wd_s[0, 0]) + bd_ref[0]
        y_ref[...] = _pack_bf16_pairs(y) if PACK_ROWS else y

    @pl.when(jnp.logical_not(active))
    def _():
        y_ref[...] = jnp.zeros_like(y_ref)


def expert_ffn(xg, block_e, n_used, ew, li, lw, after):
    cap, xw = xg.shape
    d = D_MODEL
    nb = cap // MOE_BM
    de = D_EXPERT
    xmap = lambda i, be, nu: (jnp.minimum(i, nu[0] - 1), 0)
    wmap = lambda i, be, nu: (be[i], 0, 0)
    lmap = lambda i, be, nu: (li, be[i], 0, 0)
    grid_spec = pltpu.PrefetchScalarGridSpec(
        num_scalar_prefetch=2,
        grid=(nb,),
        in_specs=[pl.BlockSpec((MOE_BM, xw), xmap),
                  pl.BlockSpec((1, 1, d, de), lmap), pl.BlockSpec((1, 1, d, de), lmap),
                  pl.BlockSpec((1, 1, de), wmap), pl.BlockSpec((1, 1, de), wmap),
                  pl.BlockSpec((1, 1, de, d), lmap), pl.BlockSpec((1, 1, d), wmap),
                  _ORDER_SPEC],
        out_specs=pl.BlockSpec((MOE_BM, xw), lambda i, be, nu: (i, 0)),
    )
    return pl.pallas_call(
        _ordered_after(_expert_kernel, 9),
        grid_spec=grid_spec,
        out_shape=jax.ShapeDtypeStruct((cap, xw), xg.dtype),
        compiler_params=_cparams(("arbitrary",)),
        name="expert_ffn",
    )(block_e, n_used, xg, ew[0], ew[1], lw["b_glu"], lw["b_lin"], ew[2], lw["b_down"], after)


def _combine_ln2_kernel(x_ref, y_ref, gate_ref, mod_ref, g_ref, b_ref, o_ref):
    gates = gate_ref[0]
    if PACK_ROWS:
        f_hi, f_lo = 0.0, 0.0
        for kk in range(TOP_K):
            hi, lo = _unpack_bf16_pairs(y_ref[kk, 0])
            f_hi = f_hi + gates[:, kk:kk + 1] * hi
            f_lo = f_lo + gates[:, kk:kk + 1] * lo
        f = jnp.concatenate([f_hi, f_lo], axis=1)
    else:
        f = gates[:, 0:1] * y_ref[0, 0]
        for kk in range(1, TOP_K):
            f = f + gates[:, kk:kk + 1] * y_ref[kk, 0]
    mod = mod_ref[0, 0]
    o_ref[0] = _ln(DEEPNORM_ALPHA * x_ref[0] + mod[5:6] * f) * g_ref[...] + b_ref[...]


def combine_deepnorm2(x1, yg, gates, mod_tab, lw, n_lat_tiles, after):
    b, t, d = x1.shape
    tok = pl.BlockSpec((1, TM, d), lambda bi, j: (bi, j, 0))
    vec = pl.BlockSpec((1, d), lambda bi, j: (0, 0))
    return pl.pallas_call(
        _ordered_after(_combine_ln2_kernel, 6),
        grid=(b, t // TM),
        in_specs=[tok,
                  pl.BlockSpec((TOP_K, 1, TM, yg.shape[-1]), lambda bi, j: (0, bi, j, 0)),
                  pl.BlockSpec((1, TM, TOP_K), lambda bi, j: (bi, j, 0)),
                  pl.BlockSpec((1, 1, N_MOD, d), lambda bi, j: (bi, j // n_lat_tiles, 0, 0)), vec, vec,
                  _ORDER_SPEC],
        out_specs=tok,
        out_shape=jax.ShapeDtypeStruct((b, t, d), F32),
        compiler_params=_cparams(("parallel", "parallel")),
        name="combine_deepnorm2",
    )(x1, yg, gates, mod_tab, lw["ln2_g"], lw["ln2_b"], after)


def _rotation_tables(s_len, lc):
    rows = s_len // GRID_W
    row = jnp.broadcast_to(jnp.arange(rows, dtype=F32)[:, None], (rows, GRID_W)).reshape(-1)
    col = jnp.broadcast_to(jnp.arange(GRID_W, dtype=F32)[None, :], (rows, GRID_W)).reshape(-1)
    inv = ROPE_BASE ** (-jnp.arange(ROPE_AXIS_FREQS, dtype=F32) / ROPE_AXIS_FREQS)
    ar, ac = row[:, None] * inv, col[:, None] * inv
    c64 = jnp.concatenate([jnp.cos(ar), jnp.cos(ar), jnp.cos(ac), jnp.cos(ac)], axis=1)
    s64 = jnp.concatenate([-jnp.sin(ar), jnp.sin(ar), -jnp.sin(ac), jnp.sin(ac)], axis=1)
    c64 = jnp.concatenate([c64, jnp.ones((lc, 64), F32)], axis=0)
    s64 = jnp.concatenate([s64, jnp.zeros((lc, 64), F32)], axis=0)
    half = RET_QK // 2
    pos = jnp.concatenate([lc + jnp.arange(s_len, dtype=F32), jnp.arange(lc, dtype=F32)])
    inv_r = 1.0 / (RET_ROPE_BASE ** jnp.linspace(0.0, 1.0, half, dtype=F32))
    ang = pos[:, None] * inv_r
    rc = jnp.tile(jnp.concatenate([jnp.cos(ang), jnp.cos(ang)], axis=1), (1, RET_HEADS))
    rs = jnp.tile(jnp.concatenate([-jnp.sin(ang), jnp.sin(ang)], axis=1), (1, RET_HEADS))
    qs = RET_QK ** -0.5
    return jnp.concatenate([
        jnp.tile(c64, (1, MLA_HEADS)) * MLA_SCALE, jnp.tile(s64, (1, MLA_HEADS)) * MLA_SCALE,
        c64, s64, rc * qs, rs * qs, rc, rs], axis=1)


def _in_perm():
    a = np.arange
    return np.concatenate([
        a(0, 640), a(704, 1216), a(1216, 1344), a(1344, 1472),
        1216 + _swap16(a(128)), 1344 + _swap16(a(128)), a(1472, 1984),
        640 + a(64), 640 + _swap16(a(64))])


def _uq_perm():
    a = np.arange
    nope = [h * MLA_QK + a(MLA_NOPE) for h in range(MLA_HEADS)]
    rope = [h * MLA_QK + MLA_NOPE + a(MLA_ROPE) for h in range(MLA_HEADS)]
    part = [h * MLA_QK + MLA_NOPE + _swap16(a(MLA_ROPE)) for h in range(MLA_HEADS)]
    return np.concatenate(nope + rope + part)


def _layer_weights(p):
    nl = p["w_in"].shape[0]
    lgf = jax.nn.log_sigmoid(p["ret_decay_fwd"].astype(F32))
    lgb = jax.nn.log_sigmoid(p["ret_decay_bwd"].astype(F32))
    h128 = np.arange(128) // RET_QK
    h256 = np.arange(RET_OUT) // RET_V
    a = jnp.arange(CHUNK, dtype=F32)[None, :, None]
    lf, lb = lgf[:, h128][:, None, :], lgb[:, h128][:, None, :]
    kdec = jnp.stack([jnp.exp(lf * (CHUNK - 1.0 - a)), jnp.exp(lb * a)], axis=1)
    qdec = jnp.stack([jnp.exp(lf * (a + 1.0)), jnp.exp(lb * (CHUNK - a))], axis=1)
    i = jnp.arange(CHUNK, dtype=F32)[:, None]
    j = jnp.arange(CHUNK, dtype=F32)[None, :]
    dif = (i - j)[None, None]
    ret_m = jnp.where(dif >= 0, jnp.exp(lgf[:, :, None, None] * jnp.maximum(dif, 0.0)),
                      jnp.exp(lgb[:, :, None, None] * jnp.maximum(-dif, 0.0)))
    cd = jnp.stack([jnp.exp(lgf[:, h256] * CHUNK), jnp.exp(lgb[:, h256] * CHUNK)], axis=1)[:, :, None, :]
    bd = (h128[:, None] == h256[None, :]).astype(np.float32)
    seg = (h256[:, None] == h256[None, :]).astype(np.float32)
    tri = (np.arange(TM)[:, None] < np.arange(TM)[None, :]).astype(np.float32)
    jj = np.arange(_DEINT // 2)
    deint = np.zeros((_DEINT, _DEINT), np.float32)
    deint[2 * jj, jj] = 1.0
    deint[2 * jj + 1, _DEINT // 2 + jj] = 1.0
    rw_t = jnp.swapaxes(p["router_w"], 1, 2)
    rw_hi = rw_t.astype(BF16)
    rw_lo = (rw_t - rw_hi.astype(F32)).astype(BF16)
    sg_bias = jnp.repeat(jnp.swapaxes(p["sg_b"], 1, 2), SG_WIDTH // SG_GROUPS, axis=2)
    bgu = p["b_gate_up"]
    return {
        "w_in": p["w_in"][:, :, _in_perm()].astype(MXU_DT),
        "q_g": p["mla_q_norm_g"][:, None, :], "kv_g": p["mla_kv_norm_g"][:, None, :],
        "w_uq": p["mla_w_uq"][:, :, _uq_perm()].astype(MXU_DT),
        "w_ukv": p["mla_w_ukv"].astype(MXU_DT),
        "sg_g": p["sg_norm_g"][:, None, :], "sg_b": p["sg_norm_b"][:, None, :],
        "sg_w": p["sg_w"].reshape(nl, SG_GROUPS * CHUNK, CHUNK).astype(MXU_DT),
        "sg_bias": sg_bias,
        "kdec": kdec, "qdec": qdec, "ret_m": ret_m, "cd": cd,
        "bd": jnp.broadcast_to(jnp.asarray(bd), (nl,) + bd.shape),
        "seg": jnp.broadcast_to(jnp.asarray(seg, BF16), (nl,) + seg.shape),
        "tri": jnp.broadcast_to(jnp.asarray(tri, BF16), (nl,) + tri.shape),
        "w_o": p["w_o"].astype(MXU_DT),
        "ln1_g": p["ln1_g"][:, None, :], "ln1_b": p["ln1_b"][:, None, :],
        "ln2_g": p["ln2_g"][:, None, :], "ln2_b": p["ln2_b"][:, None, :],
        "router_w": jnp.concatenate([rw_hi, rw_lo], axis=1),
        "router_b": p["router_b"][:, :, None],
        "b_glu": bgu[:, :, None, 0::2], "b_lin": bgu[:, :, None, 1::2],
        "b_down": p["b_down"][:, :, None, :],
        "deint": jnp.broadcast_to(jnp.asarray(deint, MXU_DT), (nl,) + deint.shape),
    }


def _route(idx, rank, counts):
    n_assign = idx.shape[1] * TOP_K
    nb = -(-(n_assign + N_EXPERTS * (MOE_BM - 1)) // MOE_BM)
    padded = (counts + MOE_BM - 1) // MOE_BM * MOE_BM
    pad_end = jnp.cumsum(padded)
    pad_start = pad_end - padded
    experts = jnp.arange(N_EXPERTS, dtype=jnp.int32)
    dest = rank + jnp.sum(jnp.where(idx[..., None] == experts, pad_start, 0), axis=-1)
    blk_start = jnp.arange(nb, dtype=jnp.int32) * MOE_BM
    block_e = jnp.minimum(jnp.sum((pad_end[None, :] <= blk_start[:, None]).astype(jnp.int32), axis=1),
                          N_EXPERTS - 1)
    n_used = (pad_end[-1] // MOE_BM).astype(jnp.int32).reshape(1)
    return dest.astype(jnp.int32), block_e, n_used, nb


def _sc_workers():
    info = plsc.get_sparse_core_info()
    return info.num_cores, info.num_cores * info.num_subcores


def sc_dispatch(rows, dest3, cap):
    n, w = rows.shape
    nch, kk, c = dest3.shape
    ncores, nw = _sc_workers()
    per_w = nch // nw
    mesh = plsc.VectorSubcoreMesh(core_axis_name="c", subcore_axis_name="s")

    @functools.partial(
        pl.kernel, mesh=mesh, out_type=jax.ShapeDtypeStruct((cap, w), rows.dtype),
        scratch_types=[pltpu.VMEM((kk, c), jnp.int32), pltpu.VMEM((c, w), rows.dtype)])
    def scatter_rows(h_hbm, d_hbm, o_hbm, idx_v, rows_v):
        wid = lax.axis_index("s") * ncores + lax.axis_index("c")

        @pl.loop(0, per_w)
        def _(j):
            ch = wid * per_w + j
            pltpu.sync_copy(d_hbm.at[ch], idx_v)
            pltpu.sync_copy(h_hbm.at[pl.ds(ch * c, c)], rows_v)
            for q in range(kk):
                pltpu.sync_copy(rows_v, o_hbm.at[idx_v.at[q]])

    return scatter_rows(rows, dest3)


def sc_combine_gather(y, dest3, n):
    cap, d = y.shape
    nch, kk, c = dest3.shape
    ncores, nw = _sc_workers()
    per_w = nch // nw
    mesh = plsc.VectorSubcoreMesh(core_axis_name="c", subcore_axis_name="s")

    @functools.partial(
        pl.kernel, mesh=mesh, out_type=jax.ShapeDtypeStruct((kk, n, d), y.dtype),
        scratch_types=[pltpu.VMEM((kk, c), jnp.int32), pltpu.VMEM((c, d), y.dtype)])
    def gather_rows(y_hbm, d_hbm, o_hbm, idx_v, rows_v):
        wid = lax.axis_index("s") * ncores + lax.axis_index("c")

        @pl.loop(0, per_w)
        def _(j):
            ch = wid * per_w + j
            pltpu.sync_copy(d_hbm.at[ch], idx_v)
            for q in range(kk):
                pltpu.sync_copy(y_hbm.at[idx_v.at[q]], rows_v)
                pltpu.sync_copy(rows_v, o_hbm.at[q, pl.ds(ch * c, c)])

    return gather_rows(y, dest3)


def kernel(x, c, ctx, c_ctx, ada_w, ada_b, w_in, mla_q_norm_g, mla_kv_norm_g, mla_w_uq, mla_w_ukv,
           sg_norm_g, sg_norm_b, sg_w, sg_b, ret_decay_fwd, ret_decay_bwd, w_o, ln1_g, ln1_b,
           router_w, router_b, w_gate_up, b_gate_up, w_down, b_down, ln2_g, ln2_b):
    b, s_len, d = x.shape
    lc = ctx.shape[1]
    assert d == D_MODEL and lc % TM == 0 and s_len % lc == 0 and s_len % GRID_W == 0
    assert b + 1 <= 8
    t = s_len + lc
    n_lat_tiles = s_len // TM
    params = dict(w_in=w_in, mla_q_norm_g=mla_q_norm_g, mla_kv_norm_g=mla_kv_norm_g, mla_w_uq=mla_w_uq,
                  mla_w_ukv=mla_w_ukv, sg_norm_g=sg_norm_g, sg_norm_b=sg_norm_b, sg_w=sg_w, sg_b=sg_b,
                  ret_decay_fwd=ret_decay_fwd, ret_decay_bwd=ret_decay_bwd, w_o=w_o, ln1_g=ln1_g,
                  ln1_b=ln1_b, router_w=router_w, router_b=router_b, w_gate_up=w_gate_up,
                  b_gate_up=b_gate_up, w_down=w_down, b_down=b_down, ln2_g=ln2_g, ln2_b=ln2_b)
    lws = _layer_weights(params)
    ew = expert_weights(w_gate_up, w_down, lws["deint"][0])
    tab = _rotation_tables(s_len, lc)

    c_rows = jnp.concatenate([c, c_ctx[None, :], jnp.zeros((8 - b - 1, d), F32)], axis=0)
    mod = ada_modulation(c_rows, ada_w, ada_b).reshape(DEPTH, 8, N_MOD, d)
    mod_tab = jnp.stack([mod[:, :b], jnp.broadcast_to(mod[:, b:b + 1], (DEPTH, b, N_MOD, d))], axis=2)

    n_streams = N_STREAMS if b % N_STREAMS == 0 else 1
    bs = b // n_streams
    xs = [jnp.concatenate([x[i * bs:(i + 1) * bs], ctx[i * bs:(i + 1) * bs]], axis=1)
          for i in range(n_streams)]
    order = c_rows
    held = None
    for li in range(DEPTH):
        lw = {k: v[li] for k, v in lws.items()}
        mts = [mod_tab[li, si * bs:(si + 1) * bs] for si in range(n_streams)]
        fronts = []
        for si in range(n_streams):
            if held is not None and si == n_streams - 1:
                xs[si] = _finish(*held, after=order)
                held = None
            fr = _front(xs[si], mts[si], tab, lw, s_len, lc, after=order)
            order = fr["cnt"]
            fronts.append(fr)
        ys = []
        for si in range(n_streams):
            y = expert_ffn(fronts[si]["xg"], fronts[si]["block_e"], fronts[si]["n_used"], ew, li, lw,
                           after=order)
            order = y
            ys.append(y)
        for si in range(n_streams):
            args = (fronts[si], ys[si], mts[si], lw, s_len)
            if si == n_streams - 1 and n_streams > 1 and li + 1 < DEPTH:
                held = args
            else:
                xs[si] = _finish(*args, after=order)
                order = xs[si]
    return jnp.concatenate([xi[:, :s_len] for xi in xs], axis=0)


def _front(x_all, mt, tab, lw, s_len, lc, after):
    b, t, d = x_all.shape
    n_lat_tiles = s_len // TM
    q, k, v, sg, retp, rv, a = input_projection(x_all, mt, tab, lw, n_lat_tiles, after)
    mla_lat, mla_ctx = mla_attention(q, k, v, s_len, lc)
    st = retention_scan(a, lw["cd"], s_len // CHUNK)
    x1, h2, idx, gates, rank, cnt = output_projection(
        x_all, mla_lat, mla_ctx, sg, retp, rv, st, mt, lw, n_lat_tiles)
    to_tok = lambda z: z.transpose(2, 0, 1, 3).reshape(TOP_K, b * t)
    dest, block_e, n_used, nb = _route(to_tok(idx), to_tok(rank), cnt[:, 0])
    assert (b * t) % SC_CHUNK == 0
    dest3 = dest.reshape(TOP_K, (b * t) // SC_CHUNK, SC_CHUNK).transpose(1, 0, 2)
    xg = sc_dispatch(h2.reshape(b * t, h2.shape[-1]), dest3, nb * MOE_BM)
    return dict(x1=x1, gates=gates, cnt=cnt, dest3=dest3, xg=xg, block_e=block_e, n_used=n_used)


def _finish(fr, y, mt, lw, s_len, after):
    b, t, d = fr["x1"].shape
    yg = sc_combine_gather(y, fr["dest3"], b * t).reshape(TOP_K, b, t, y.shape[-1])
    gates_tok = fr["gates"].transpose(0, 1, 3, 2).reshape(b, t, TOP_K)
    return combine_deepnorm2(fr["x1"], yg, gates_tok, mt, lw, s_len // TM, after)
```

```python
import functools

import numpy as np
import jax
import jax.numpy as jnp
from jax import lax
from jax.experimental import pallas as pl
from jax.experimental.pallas import tpu as pltpu
from jax.experimental.pallas import tpu_sc as plsc

F32 = jnp.float32
BF16 = jnp.bfloat16
MXU_DT = BF16
PACK_ROWS = True

D_MODEL = 1024
DEPTH = 4
GRID_W = 64
MLA_HEADS = 4
MLA_NOPE = 128
MLA_ROPE = 64
MLA_V = 128
MLA_Q_LORA = 384
MLA_KV_LORA = 256
MLA_QK = MLA_NOPE + MLA_ROPE
MLA_SCALE = MLA_QK ** -0.5
ROPE_BASE = 10000.0
ROPE_AXIS_FREQS = MLA_ROPE // 4
SG_GROUPS = 4
SG_WIDTH = 256
SG_CHUNK = 128
RET_HEADS = 4
RET_QK = 32
RET_V = 64
RET_CHUNK = 128
RET_ROPE_BASE = 10000.0
N_EXPERTS = 32
TOP_K = 4
D_EXPERT = 1024
SWIGLU_LIMIT = 7.0
SWIGLU_ALPHA = 1.702
N_MOD = 6
LN_EPS = 1e-5
RMS_EPS = 1e-6
DEEPNORM_ALPHA = (2 * DEPTH) ** 0.25
MLA_OUT = MLA_HEADS * MLA_V
RET_OUT = RET_HEADS * RET_V

TM = 256
CHUNK = 128
MOE_BM = 256
SC_CHUNK = 48
N_STREAMS = 2
ATT_TQ = 1024
ATT_TK = 2048
VMEM_LIMIT = 48 * 2 ** 20

_O_CQ, _O_CKV, _O_SGU, _O_SGV = 0, 384, 640, 896
_O_RQ, _O_RK, _O_RQS, _O_RKS, _O_RV, _O_RG, _O_KR = 1152, 1280, 1408, 1536, 1664, 1920, 2176
IN_P = 2304
_T_QC, _T_QS, _T_KCS, _T_RQC, _T_RQS, _T_RKC, _T_RKS = 0, 256, 512, 640, 768, 896, 1024
TAB_W = 1152


def _cparams(sem):
    return pltpu.CompilerParams(dimension_semantics=sem, vmem_limit_bytes=VMEM_LIMIT)


def _dot(a, b):
    return jnp.dot(a, b, preferred_element_type=F32)


def _dot_nt(a, b):
    return lax.dot_general(a, b, (((1,), (1,)), ((), ())), preferred_element_type=F32)


def _mx(a):
    return a.astype(MXU_DT)


def _swap16(j):
    return (j // 32) * 32 + ((j % 32) + 16) % 32


_ERF_ALPHA = (-2.72614225801306e-10, 2.77068142495902e-08, -2.10102402082508e-06,
              -5.69250639462346e-05, -7.34990630326855e-04, -2.95459980854025e-03,
              -1.60960333262415e-02)
_ERF_BETA = (-1.45660718464996e-05, -2.13374055278905e-04, -1.68282697438203e-03,
             -7.37332916720468e-03, -1.42647390514189e-02)


def _erf(x):
    x = jnp.clip(x, -4.0, 4.0)
    x2 = x * x
    p = jnp.full_like(x, _ERF_ALPHA[0])
    for c in _ERF_ALPHA[1:]:
        p = p * x2 + c
    q = jnp.full_like(x, _ERF_BETA[0])
    for c in _ERF_BETA[1:]:
        q = q * x2 + c
    return x * p / q


def _gelu(x):
    return 0.5 * x * (1.0 + _erf(x * 0.7071067811865476))


def _sigmoid(x):
    return 1.0 / (1.0 + jnp.exp(-x))


def _ln(x):
    xc = x - jnp.mean(x, axis=-1, keepdims=True)
    return xc * lax.rsqrt(jnp.mean(xc * xc, axis=-1, keepdims=True) + LN_EPS)


def _lane_group(shape, width):
    return lax.broadcasted_iota(jnp.int32, shape, len(shape) - 1) // width


def _ada_kernel(c_ref, w_ref, b_ref, o_ref):
    c = c_ref[...]
    o_ref[0] = _dot(c * _sigmoid(c), w_ref[0]) + b_ref[0]


def ada_modulation(c_rows, ada_w, ada_b):
    nl, d, n = ada_w.shape
    tn = 1536
    return pl.pallas_call(
        _ada_kernel,
        grid=(nl, n // tn),
        in_specs=[pl.BlockSpec((8, d), lambda l, j: (0, 0)),
                  pl.BlockSpec((1, d, tn), lambda l, j: (l, 0, j)),
                  pl.BlockSpec((1, 1, tn), lambda l, j: (l, 0, j))],
        out_specs=pl.BlockSpec((1, 8, tn), lambda l, j: (l, 0, j)),
        out_shape=jax.ShapeDtypeStruct((nl, 8, n), F32),
        compiler_params=_cparams(("arbitrary", "arbitrary")),
        name="ada_modulation",
    )(c_rows, ada_w, ada_b.reshape(nl, 1, n))


def _inproj_kernel(x_ref, mod_ref, tab_ref, w_in_ref, qg_ref, kvg_ref, w_uq_ref, w_ukv_ref,
                   sgg_ref, sgb_ref, sgw_ref, sgbias_ref, kdec_ref, bd_ref,
                   q_ref, k_ref, v_ref, sg_ref, retp_ref, rv_ref, a_ref):
    x = x_ref[0]
    mod = mod_ref[0, 0]
    h = x * (1.0 + mod[1:2]) + mod[0:1]
    p = _dot(_mx(h), w_in_ref[...])
    tab = tab_ref[...]

    cq = p[:, _O_CQ:_O_CQ + MLA_Q_LORA]
    cq = cq * lax.rsqrt(jnp.mean(cq * cq, axis=-1, keepdims=True) + RMS_EPS) * qg_ref[...]
    qa = _dot(_mx(cq), w_uq_ref[...])
    rot = (qa[:, 512:768] * tab[:, _T_QC:_T_QC + 256]
           + qa[:, 768:1024] * tab[:, _T_QS:_T_QS + 256])
    for hh in range(MLA_HEADS):
        q_ref[0, hh, :, 0:128] = (qa[:, 128 * hh:128 * hh + 128] * MLA_SCALE).astype(q_ref.dtype)
        g = hh // 2
        q_ref[0, hh, :, 128:256] = rot[:, 128 * g:128 * g + 128].astype(q_ref.dtype)

    ckv = p[:, _O_CKV:_O_CKV + MLA_KV_LORA]
    ckv = ckv * lax.rsqrt(jnp.mean(ckv * ckv, axis=-1, keepdims=True) + RMS_EPS) * kvg_ref[...]
    kv = _dot(_mx(ckv), w_ukv_ref[...])
    t = p[:, _O_KR:_O_KR + 128] * tab[:, _T_KCS:_T_KCS + 128]
    u = t + pltpu.roll(t, 64, axis=1)
    low = lax.broadcasted_iota(jnp.int32, u.shape, 1) < 64
    kx = (jnp.where(low, u, 0.0), jnp.where(low, 0.0, u))
    ones_col = jnp.where(lax.broadcasted_iota(jnp.int32, u.shape, 1) == 0, 1.0, 0.0).astype(v_ref.dtype)
    for hh in range(MLA_HEADS):
        k_ref[0, hh, :, 0:128] = kv[:, 256 * hh:256 * hh + 128].astype(k_ref.dtype)
        k_ref[0, hh, :, 128:256] = kx[hh % 2].astype(k_ref.dtype)
        v_ref[0, hh, :, 0:128] = kv[:, 256 * hh + 128:256 * hh + 256].astype(v_ref.dtype)
        v_ref[0, hh, :, 128:256] = ones_col

    gu = _gelu(p[:, _O_SGU:_O_SGU + SG_WIDTH])
    gv = _ln(_gelu(p[:, _O_SGV:_O_SGV + SG_WIDTH])) * sgg_ref[...] + sgb_ref[...]
    gvm = _mx(gv)
    grp = _lane_group((CHUNK, SG_WIDTH), SG_WIDTH // SG_GROUPS)
    for c in range(TM // CHUNK):
        rows = slice(c * CHUNK, (c + 1) * CHUNK)
        res = _dot(sgw_ref[...], gvm[rows])
        mixed = sgbias_ref[...]
        for g in range(SG_GROUPS):
            mixed = mixed + jnp.where(grp == g, res[g * CHUNK:(g + 1) * CHUNK], 0.0)
        sg_ref[0, rows, :] = (gu[rows] * mixed).astype(sg_ref.dtype)

    rq = (p[:, _O_RQ:_O_RQ + 128] * tab[:, _T_RQC:_T_RQC + 128]
          + p[:, _O_RQS:_O_RQS + 128] * tab[:, _T_RQS:_T_RQS + 128])
    rk = (p[:, _O_RK:_O_RK + 128] * tab[:, _T_RKC:_T_RKC + 128]
          + p[:, _O_RKS:_O_RKS + 128] * tab[:, _T_RKS:_T_RKS + 128])
    rv = p[:, _O_RV:_O_RV + RET_OUT]
    retp_ref[0, :, 0:128] = rq
    retp_ref[0, :, 128:256] = rk
    retp_ref[0, :, 256:512] = p[:, _O_RG:_O_RG + RET_OUT]
    rvm = _mx(rv)
    rv_ref[0] = rvm.astype(rv_ref.dtype)
    bd = bd_ref[...]
    for c in range(TM // CHUNK):
        rows = slice(c * CHUNK, (c + 1) * CHUNK)
        for d in range(2):
            kd_t = _mx((rk[rows] * kdec_ref[d]).T)
            af = _dot(kd_t, rvm[rows]) * bd
            a_ref[0, c, d] = (af[0:32] + af[32:64]) + (af[64:96] + af[96:128])


def _ordered_after(kernel_fn, pos):
    def wrapped(*refs):
        return kernel_fn(*refs[:pos], *refs[pos + 1:])
    return wrapped


_ORDER_SPEC = pl.BlockSpec(memory_space=pl.ANY)


def input_projection(x_all, mod_tab, tab, lw, n_lat_tiles, after):
    b, t, d = x_all.shape
    nt = t // TM
    nc = t // CHUNK
    cpt = TM // CHUNK
    const2 = lambda bi, j: (0, 0)
    const3 = lambda bi, j: (0, 0, 0)
    out_shape = (
        jax.ShapeDtypeStruct((b, MLA_HEADS, t, 256), MXU_DT),
        jax.ShapeDtypeStruct((b, MLA_HEADS, t, 256), MXU_DT),
        jax.ShapeDtypeStruct((b, MLA_HEADS, t, 256), MXU_DT),
        jax.ShapeDtypeStruct((b, t, SG_WIDTH), MXU_DT),
        jax.ShapeDtypeStruct((b, t, 512), F32),
        jax.ShapeDtypeStruct((b, t, RET_OUT), MXU_DT),
        jax.ShapeDtypeStruct((b, nc, 2, RET_QK, RET_OUT), F32),
    )
    head_spec = lambda w: pl.BlockSpec((1, MLA_HEADS, TM, w), lambda bi, j: (bi, 0, j, 0))
    tok_spec = lambda w: pl.BlockSpec((1, TM, w), lambda bi, j: (bi, j, 0))
    return pl.pallas_call(
        _ordered_after(_inproj_kernel, 14),
        grid=(b, nt),
        in_specs=[
            tok_spec(d),
            pl.BlockSpec((1, 1, N_MOD, d), lambda bi, j: (bi, j // n_lat_tiles, 0, 0)),
            pl.BlockSpec((TM, TAB_W), lambda bi, j: (j, 0)),
            pl.BlockSpec((d, IN_P), const2),
            pl.BlockSpec((1, MLA_Q_LORA), const2),
            pl.BlockSpec((1, MLA_KV_LORA), const2),
            pl.BlockSpec((MLA_Q_LORA, 1024), const2),
            pl.BlockSpec((MLA_KV_LORA, 1024), const2),
            pl.BlockSpec((1, SG_WIDTH), const2),
            pl.BlockSpec((1, SG_WIDTH), const2),
            pl.BlockSpec((SG_GROUPS * CHUNK, CHUNK), const2),
            pl.BlockSpec((CHUNK, SG_WIDTH), const2),
            pl.BlockSpec((2, CHUNK, 128), const3),
            pl.BlockSpec((128, RET_OUT), const2),
            _ORDER_SPEC,
        ],
        out_specs=(head_spec(256), head_spec(256), head_spec(256), tok_spec(SG_WIDTH),
                   tok_spec(512), tok_spec(RET_OUT),
                   pl.BlockSpec((1, cpt, 2, RET_QK, RET_OUT), lambda bi, j: (bi, j, 0, 0, 0))),
        out_shape=out_shape,
        compiler_params=_cparams(("parallel", "parallel")),
        name="input_projection",
    )(x_all, mod_tab, tab, lw["w_in"], lw["q_g"], lw["kv_g"], lw["w_uq"], lw["w_ukv"],
      lw["sg_g"], lw["sg_b"], lw["sg_w"], lw["sg_bias"], lw["kdec"], lw["bd"], after)


def _attn_kernel(q_ref, k_ref, v_ref, o_ref, *, n_main, tk, tail):
    q = q_ref[0, 0]
    tq = q.shape[0]
    chunks = [(i * tk, tk) for i in range(n_main)] + ([(n_main * tk, tail)] if tail else [])

    def scores(ci):
        start, size = chunks[ci]
        return _dot_nt(q, k_ref[0, 0, start:start + size, :])

    m = jnp.full((tq, 1), -1e30, F32)
    acc = jnp.zeros((tq, 256), F32)
    s_next = scores(0)
    for ci, (start, size) in enumerate(chunks):
        s = s_next
        if ci + 1 < len(chunks):
            s_next = scores(ci + 1)
        m_new = jnp.maximum(m, jnp.max(s, axis=-1, keepdims=True))
        p = jnp.exp(s - m_new)
        acc = jnp.exp(m - m_new) * acc + _dot(_mx(p), v_ref[0, 0, start:start + size, :])
        m = m_new
    o_ref[0] = (acc[:, 0:MLA_V] / acc[:, MLA_V:MLA_V + 1]).astype(o_ref.dtype)


def mla_attention(q, k, v, s_len, lc):
    b, hn, t, _ = q.shape
    tq = min(ATT_TQ, s_len)
    tk = min(ATT_TK, s_len)
    kv_full = pl.BlockSpec((1, 1, t, 256), lambda bi, hi, i: (bi, hi, 0, 0))
    out_lat = pl.pallas_call(
        functools.partial(_attn_kernel, n_main=s_len // tk, tk=tk, tail=lc),
        grid=(b, hn, s_len // tq),
        in_specs=[pl.BlockSpec((1, 1, tq, 256), lambda bi, hi, i: (bi, hi, i, 0)), kv_full, kv_full],
        out_specs=pl.BlockSpec((1, tq, MLA_V), lambda bi, hi, i: (bi, i, hi)),
        out_shape=jax.ShapeDtypeStruct((b, s_len, MLA_OUT), MXU_DT),
        compiler_params=_cparams(("parallel", "parallel", "arbitrary")),
        name="mla_attention_latent",
    )(q, k, v)
    cblk = s_len // lc
    ctx_spec = pl.BlockSpec((1, 1, lc, 256), lambda bi, hi: (bi, hi, cblk, 0))
    out_ctx = pl.pallas_call(
        functools.partial(_attn_kernel, n_main=0, tk=tk, tail=lc),
        grid=(b, hn),
        in_specs=[ctx_spec, ctx_spec, ctx_spec],
        out_specs=pl.BlockSpec((1, lc, MLA_V), lambda bi, hi: (bi, 0, hi)),
        out_shape=jax.ShapeDtypeStruct((b, lc, MLA_OUT), MXU_DT),
        compiler_params=_cparams(("parallel", "parallel")),
        name="mla_attention_context",
    )(q, k, v)
    return out_lat, out_ctx


def _ret_scan_kernel(a_ref, cd_ref, s_ref, *, n_lat_chunks):
    nc = a_ref.shape[1]
    ncc = nc - n_lat_chunks
    cd_f, cd_b = cd_ref[0], cd_ref[1]

    def body(n, carry):
        sf, sb = carry
        cf = jnp.where(n < ncc, n_lat_chunks + n, n - ncc)
        cb = jnp.where(n < ncc, nc - 1 - n, n_lat_chunks - 1 - (n - ncc))
        s_ref[0, cf, 0] = sf
        s_ref[0, cb, 1] = sb
        return sf * cd_f + a_ref[0, cf, 0], sb * cd_b + a_ref[0, cb, 1]

    zero = jnp.zeros((RET_QK, RET_OUT), F32)
    lax.fori_loop(0, nc, body, (zero, zero))


def retention_scan(a, cd, n_lat_chunks):
    b, nc = a.shape[:2]
    blk = pl.BlockSpec((1, nc, 2, RET_QK, RET_OUT), lambda bi: (bi, 0, 0, 0, 0))
    return pl.pallas_call(
        functools.partial(_ret_scan_kernel, n_lat_chunks=n_lat_chunks),
        grid=(b,),
        in_specs=[blk, pl.BlockSpec((2, 1, RET_OUT), lambda bi: (0, 0, 0))],
        out_specs=blk,
        out_shape=jax.ShapeDtypeStruct(a.shape, F32),
        compiler_params=_cparams(("parallel",)),
        name="retention_scan",
    )(a, cd)


def _split_dot(x, ones2):
    hi = x.astype(BF16)
    lo = (x - hi.astype(F32)).astype(BF16)
    return _dot(jnp.concatenate([hi, lo], axis=1), ones2)


def _outproj_kernel(x_ref, mlal_ref, mlac_ref, sg_ref, retp_ref, rv_ref, st_ref, mod_ref,
                    m_ref, qdec_ref, bd_ref, seg_ref, w_o_ref, lng_ref, lnb_ref, rw_ref, rb_ref, tri_ref,
                    x1_ref, h2_ref, idx_ref, gate_ref, rank_ref, cnt_ref, carry_ref, *, n_lat_tiles):
    @pl.when(pl.program_id(0) == 0)
    def _():
        carry_ref[...] = jnp.zeros_like(carry_ref)

    nbat = x_ref.shape[0]
    units = [(bb, c) for bb in range(nbat) for c in range(TM // CHUNK)]
    rows = lambda c: slice(c * CHUNK, (c + 1) * CHUNK)
    g32 = _lane_group((CHUNK, 128), RET_QK)
    g64 = _lane_group((CHUNK, RET_OUT), RET_V)
    bd = bd_ref[...]
    seg2 = jnp.concatenate([seg_ref[...], seg_ref[...]], axis=0)
    m4 = jnp.concatenate([m_ref[hh] for hh in range(RET_HEADS)], axis=0)

    rq, s4 = {}, {}
    for u in units:
        bb, c = u
        rq[u] = retp_ref[bb, rows(c), 0:128]
        q4 = jnp.concatenate([jnp.where(g32 == hh, rq[u], 0.0) for hh in range(RET_HEADS)], axis=0)
        s4[u] = _dot_nt(_mx(q4), _mx(retp_ref[bb, rows(c), 128:256]))
    o = {}
    for u in units:
        bb, c = u
        r = _dot(_mx(s4[u] * m4), rv_ref[bb, rows(c), :])
        qd = jnp.concatenate([rq[u] * qdec_ref[0], rq[u] * qdec_ref[1]], axis=1)
        st2 = jnp.concatenate([jnp.concatenate([st_ref[bb, c, dd]] * RET_HEADS, axis=0) * bd
                               for dd in range(2)], axis=0)
        acc = _dot(_mx(qd), _mx(st2))
        for hh in range(RET_HEADS):
            acc = acc + jnp.where(g64 == hh, r[hh * CHUNK:(hh + 1) * CHUNK], 0.0)
        o[u] = acc
    oc = {}
    for u in units:
        oc[u] = o[u] - _split_dot(o[u], seg2) * (1.0 / RET_V)
    ret = {}
    for u in units:
        bb, c = u
        var = _split_dot(oc[u] * oc[u], seg2) * (1.0 / RET_V)
        rg = retp_ref[bb, rows(c), 256:512]
        ret[u] = _mx(oc[u] * lax.rsqrt(var + LN_EPS) * (rg * _sigmoid(rg)))

    is_ctx = pl.program_id(0) >= n_lat_tiles
    ys = []
    for bb in range(nbat):
        mla = jnp.where(is_ctx, mlac_ref[bb], mlal_ref[bb])
        cat = jnp.concatenate(
            [mla, sg_ref[bb], jnp.concatenate([ret[(bb, c)] for c in range(TM // CHUNK)], axis=0)], axis=1)
        ys.append(_dot(cat, w_o_ref[...]))
    logits = []
    for bb in range(nbat):
        mod = mod_ref[bb, 0]
        x1 = _ln(DEEPNORM_ALPHA * x_ref[bb] + mod[2:3] * ys[bb]) * lng_ref[...] + lnb_ref[...]
        x1_ref[bb] = x1
        h2 = x1 * (1.0 + mod[4:5]) + mod[3:4]
        h2_ref[bb] = _pack_bf16_pairs(h2) if PACK_ROWS else h2
        h2_hi = h2.astype(BF16)
        h2_lo = (h2 - h2_hi.astype(F32)).astype(BF16)
        r2 = _dot_nt(rw_ref[...], jnp.concatenate([h2_hi, h2_lo], axis=0))
        logits.append(r2[0:N_EXPERTS, 0:TM] + r2[N_EXPERTS:, 0:TM] + r2[0:N_EXPERTS, TM:] + rb_ref[...])

    e_iota = lax.broadcasted_iota(jnp.int32, logits[0].shape, 0).astype(F32)
    idxs_all, onehots, prefixes = [], [], []
    for bb in range(nbat):
        work = logits[bb]
        vals, idxs = [], []
        for _ in range(TOP_K):
            mval = jnp.max(work, axis=0, keepdims=True)
            midx = jnp.min(jnp.where(work == mval, e_iota, float(N_EXPERTS)), axis=0, keepdims=True)
            vals.append(mval)
            idxs.append(midx)
            work = jnp.where(e_iota == midx, -jnp.inf, work)
        ex = [jnp.exp(vv - vals[0]) for vv in vals]
        den = ex[0] + ex[1] + ex[2] + ex[3]
        onehot = jnp.zeros_like(work)
        for kk in range(TOP_K):
            gate_ref[bb, 0, kk:kk + 1, :] = ex[kk] / den
            idx_ref[bb, 0, kk:kk + 1, :] = idxs[kk].astype(jnp.int32)
            onehot = onehot + jnp.where(e_iota == idxs[kk], 1.0, 0.0)
        idxs_all.append(idxs)
        onehots.append(onehot)
        prefixes.append(_dot(onehot.astype(BF16), tri_ref[...]))
    count = carry_ref[:, 0:1]
    for bb in range(nbat):
        base = count + prefixes[bb]
        for kk in range(TOP_K):
            rk_k = jnp.sum(jnp.where(e_iota == idxs_all[bb][kk], base, 0.0), axis=0, keepdims=True)
            rank_ref[bb, 0, kk:kk + 1, :] = rk_k.astype(jnp.int32)
        count = count + jnp.sum(onehots[bb], axis=1, keepdims=True)
    carry_ref[...] = jnp.broadcast_to(count, carry_ref.shape)
    cnt_ref[...] = carry_ref[...].astype(jnp.int32)


def output_projection(x_all, mla_lat, mla_ctx, sg, retp, rv, st, mod_tab, lw, n_lat_tiles):
    b, t, d = x_all.shape
    nt = t // TM
    cpt = TM // CHUNK
    const2 = lambda j: (0, 0)
    const3 = lambda j: (0, 0, 0)
    tok_spec = lambda w: pl.BlockSpec((b, TM, w), lambda j: (0, j, 0))
    route_spec = pl.BlockSpec((b, 1, TOP_K, TM), lambda j: (0, j, 0, 0))
    route_shape = lambda dt: jax.ShapeDtypeStruct((b, nt, TOP_K, TM), dt)
    h2w = d // 2 if PACK_ROWS else d
    return pl.pallas_call(
        functools.partial(_outproj_kernel, n_lat_tiles=n_lat_tiles),
        grid=(nt,),
        in_specs=[
            tok_spec(d),
            pl.BlockSpec((b, TM, MLA_OUT), lambda j: (0, jnp.minimum(j, n_lat_tiles - 1), 0)),
            pl.BlockSpec((b, TM, MLA_OUT), lambda j: (0, jnp.maximum(j - n_lat_tiles, 0), 0)),
            tok_spec(SG_WIDTH), tok_spec(512), tok_spec(RET_OUT),
            pl.BlockSpec((b, cpt, 2, RET_QK, RET_OUT), lambda j: (0, j, 0, 0, 0)),
            pl.BlockSpec((b, 1, N_MOD, d), lambda j: (0, j // n_lat_tiles, 0, 0)),
            pl.BlockSpec((RET_HEADS, CHUNK, CHUNK), const3),
            pl.BlockSpec((2, CHUNK, 128), const3),
            pl.BlockSpec((128, RET_OUT), const2),
            pl.BlockSpec((RET_OUT, RET_OUT), const2),
            pl.BlockSpec((d, d), const2),
            pl.BlockSpec((1, d), const2),
            pl.BlockSpec((1, d), const2),
            pl.BlockSpec((2 * N_EXPERTS, d), const2),
            pl.BlockSpec((N_EXPERTS, 1), const2),
            pl.BlockSpec((TM, TM), const2),
        ],
        out_specs=(tok_spec(d), tok_spec(h2w), route_spec, route_spec, route_spec,
                   pl.BlockSpec((N_EXPERTS, 128), const2)),
        out_shape=(jax.ShapeDtypeStruct((b, t, d), F32),
                   jax.ShapeDtypeStruct((b, t, h2w), jnp.uint32 if PACK_ROWS else F32),
                   route_shape(jnp.int32), route_shape(F32), route_shape(jnp.int32),
                   jax.ShapeDtypeStruct((N_EXPERTS, 128), jnp.int32)),
        scratch_shapes=[pltpu.VMEM((N_EXPERTS, 128), F32)],
        compiler_params=_cparams(("arbitrary",)),
        name="output_projection",
    )(x_all, mla_lat, mla_ctx, sg, retp, rv, st, mod_tab, lw["ret_m"], lw["qdec"], lw["bd"], lw["seg"],
      lw["w_o"], lw["ln1_g"], lw["ln1_b"], lw["router_w"], lw["router_b"], lw["tri"])


_DEINT = 256


def _expert_weights_kernel(wgu_ref, wd_ref, perm_ref, wg_ref, wl_ref, wdo_ref):
    half = _DEINT // 2
    for c in range(2 * D_EXPERT // _DEINT):
        r = _dot(_mx(wgu_ref[0, 0, :, _DEINT * c:_DEINT * (c + 1)]), perm_ref[...])
        wg_ref[0, 0, :, half * c:half * (c + 1)] = r[:, :half].astype(wg_ref.dtype)
        wl_ref[0, 0, :, half * c:half * (c + 1)] = r[:, half:].astype(wl_ref.dtype)
    wdo_ref[0, 0] = wd_ref[0, 0].astype(wdo_ref.dtype)


def expert_weights(w_gate_up, w_down, deint):
    nl, ne, d, de2 = w_gate_up.shape
    de = de2 // 2
    blk = lambda r, c: pl.BlockSpec((1, 1, r, c), lambda l, e: (l, e, 0, 0))
    return pl.pallas_call(
        _expert_weights_kernel,
        grid=(nl, ne),
        in_specs=[blk(d, de2), blk(de, d), pl.BlockSpec((_DEINT, _DEINT), lambda l, e: (0, 0))],
        out_specs=(blk(d, de), blk(d, de), blk(de, d)),
        out_shape=(jax.ShapeDtypeStruct((nl, ne, d, de), MXU_DT), jax.ShapeDtypeStruct((nl, ne, d, de), MXU_DT),
                   jax.ShapeDtypeStruct((nl, ne, de, d), MXU_DT)),
        compiler_params=_cparams(("parallel", "parallel")),
        name="expert_weights",
    )(w_gate_up, w_down, deint)


def _pack_bf16_pairs(v):
    bits = lax.bitcast_convert_type(v.astype(BF16).astype(F32), jnp.uint32)
    half = bits.shape[1] // 2
    return bits[:, :half] | (bits[:, half:] >> 16)


def _unpack_bf16_pairs(w):
    return (lax.bitcast_convert_type(w & jnp.uint32(0xFFFF0000), F32),
            lax.bitcast_convert_type(w << 16, F32))


def _expert_kernel(be_ref, nu_ref, x_ref, wg_s, wl_s, bg_ref, bl_ref, wd_s, bd_ref, y_ref):
    del be_ref
    active = pl.program_id(0) < nu_ref[0]

    @pl.when(active)
    def _():
        if PACK_ROWS:
            hi, lo = _unpack_bf16_pairs(x_ref[...])
            xb = jnp.concatenate([hi.astype(BF16), lo.astype(BF16)], axis=1)
        else:
            xb = x_ref[...]
        glu = jnp.minimum(_dot(xb, wg_s[0, 0]) + bg_ref[0], SWIGLU_LIMIT)
        lin = jnp.clip(_dot(xb, wl_s[0, 0]) + bl_ref[0], -SWIGLU_LIMIT, SWIGLU_LIMIT)
        act = glu * _sigmoid(SWIGLU_ALPHA * glu) * (lin + 1.0)
        y = _dot(_mx(act), wd_s[0, 0]) + bd_ref[0]
        y_ref[...] = _pack_bf16_pairs(y) if PACK_ROWS else y

    @pl.when(jnp.logical_not(active))
    def _():
        y_ref[...] = jnp.zeros_like(y_ref)


def expert_ffn(xg, block_e, n_used, ew, li, lw, after):
    cap, xw = xg.shape
    d = D_MODEL
    nb = cap // MOE_BM
    de = D_EXPERT
    xmap = lambda i, be, nu: (jnp.minimum(i, nu[0] - 1), 0)
    wmap = lambda i, be, nu: (be[i], 0, 0)
    lmap = lambda i, be, nu: (li, be[i], 0, 0)
    grid_spec = pltpu.PrefetchScalarGridSpec(
        num_scalar_prefetch=2,
        grid=(nb,),
        in_specs=[pl.BlockSpec((MOE_BM, xw), xmap),
                  pl.BlockSpec((1, 1, d, de), lmap), pl.BlockSpec((1, 1, d, de), lmap),
                  pl.BlockSpec((1, 1, de), wmap), pl.BlockSpec((1, 1, de), wmap),
                  pl.BlockSpec((1, 1, de, d), lmap), pl.BlockSpec((1, 1, d), wmap),
                  _ORDER_SPEC],
        out_specs=pl.BlockSpec((MOE_BM, xw), lambda i, be, nu: (i, 0)),
    )
    return pl.pallas_call(
        _ordered_after(_expert_kernel, 9),
        grid_spec=grid_spec,
        out_shape=jax.ShapeDtypeStruct((cap, xw), xg.dtype),
        compiler_params=_cparams(("arbitrary",)),
        name="expert_ffn",
    )(block_e, n_used, xg, ew[0], ew[1], lw["b_glu"], lw["b_lin"], ew[2], lw["b_down"], after)


def _combine_ln2_kernel(x_ref, y_ref, gate_ref, mod_ref, g_ref, b_ref, o_ref):
    for bb in range(x_ref.shape[0]):
        gates = gate_ref[bb]
        if PACK_ROWS:
            f_hi, f_lo = 0.0, 0.0
            for kk in range(TOP_K):
                hi, lo = _unpack_bf16_pairs(y_ref[kk, bb])
                f_hi = f_hi + gates[:, kk:kk + 1] * hi
                f_lo = f_lo + gates[:, kk:kk + 1] * lo
            f = jnp.concatenate([f_hi, f_lo], axis=1)
        else:
            f = gates[:, 0:1] * y_ref[0, bb]
            for kk in range(1, TOP_K):
                f = f + gates[:, kk:kk + 1] * y_ref[kk, bb]
        mod = mod_ref[bb, 0]
        o_ref[bb] = _ln(DEEPNORM_ALPHA * x_ref[bb] + mod[5:6] * f) * g_ref[...] + b_ref[...]


def combine_deepnorm2(x1, yg, gates, mod_tab, lw, n_lat_tiles, after):
    b, t, d = x1.shape
    tok = pl.BlockSpec((b, TM, d), lambda j: (0, j, 0))
    vec = pl.BlockSpec((1, d), lambda j: (0, 0))
    return pl.pallas_call(
        _ordered_after(_combine_ln2_kernel, 6),
        grid=(t // TM,),
        in_specs=[tok,
                  pl.BlockSpec((TOP_K, b, TM, yg.shape[-1]), lambda j: (0, 0, j, 0)),
                  pl.BlockSpec((b, TM, TOP_K), lambda j: (0, j, 0)),
                  pl.BlockSpec((b, 1, N_MOD, d), lambda j: (0, j // n_lat_tiles, 0, 0)), vec, vec,
                  _ORDER_SPEC],
        out_specs=tok,
        out_shape=jax.ShapeDtypeStruct((b, t, d), F32),
        compiler_params=_cparams(("parallel",)),
        name="combine_deepnorm2",
    )(x1, yg, gates, mod_tab, lw["ln2_g"], lw["ln2_b"], after)


def _rotation_tables(s_len, lc):
    rows = s_len // GRID_W
    row = jnp.broadcast_to(jnp.arange(rows, dtype=F32)[:, None], (rows, GRID_W)).reshape(-1)
    col = jnp.broadcast_to(jnp.arange(GRID_W, dtype=F32)[None, :], (rows, GRID_W)).reshape(-1)
    inv = ROPE_BASE ** (-jnp.arange(ROPE_AXIS_FREQS, dtype=F32) / ROPE_AXIS_FREQS)
    ar, ac = row[:, None] * inv, col[:, None] * inv
    c64 = jnp.concatenate([jnp.cos(ar), jnp.cos(ar), jnp.cos(ac), jnp.cos(ac)], axis=1)
    s64 = jnp.concatenate([-jnp.sin(ar), jnp.sin(ar), -jnp.sin(ac), jnp.sin(ac)], axis=1)
    c64 = jnp.concatenate([c64, jnp.ones((lc, 64), F32)], axis=0)
    s64 = jnp.concatenate([s64, jnp.zeros((lc, 64), F32)], axis=0)
    half = RET_QK // 2
    pos = jnp.concatenate([lc + jnp.arange(s_len, dtype=F32), jnp.arange(lc, dtype=F32)])
    inv_r = 1.0 / (RET_ROPE_BASE ** jnp.linspace(0.0, 1.0, half, dtype=F32))
    ang = pos[:, None] * inv_r
    rc = jnp.tile(jnp.concatenate([jnp.cos(ang), jnp.cos(ang)], axis=1), (1, RET_HEADS))
    rs = jnp.tile(jnp.concatenate([-jnp.sin(ang), jnp.sin(ang)], axis=1), (1, RET_HEADS))
    qs = RET_QK ** -0.5
    return jnp.concatenate([
        jnp.tile(c64, (1, MLA_HEADS)) * MLA_SCALE, jnp.tile(s64, (1, MLA_HEADS)) * MLA_SCALE,
        c64, s64, rc * qs, rs * qs, rc, rs], axis=1)


def _in_perm():
    a = np.arange
    return np.concatenate([
        a(0, 640), a(704, 1216), a(1216, 1344), a(1344, 1472),
        1216 + _swap16(a(128)), 1344 + _swap16(a(128)), a(1472, 1984),
        640 + a(64), 640 + _swap16(a(64))])


def _uq_perm():
    a = np.arange
    nope = [h * MLA_QK + a(MLA_NOPE) for h in range(MLA_HEADS)]
    rope = [h * MLA_QK + MLA_NOPE + a(MLA_ROPE) for h in range(MLA_HEADS)]
    part = [h * MLA_QK + MLA_NOPE + _swap16(a(MLA_ROPE)) for h in range(MLA_HEADS)]
    return np.concatenate(nope + rope + part)


def _layer_weights(p):
    nl = p["w_in"].shape[0]
    lgf = jax.nn.log_sigmoid(p["ret_decay_fwd"].astype(F32))
    lgb = jax.nn.log_sigmoid(p["ret_decay_bwd"].astype(F32))
    h128 = np.arange(128) // RET_QK
    h256 = np.arange(RET_OUT) // RET_V
    a = jnp.arange(CHUNK, dtype=F32)[None, :, None]
    lf, lb = lgf[:, h128][:, None, :], lgb[:, h128][:, None, :]
    kdec = jnp.stack([jnp.exp(lf * (CHUNK - 1.0 - a)), jnp.exp(lb * a)], axis=1)
    qdec = jnp.stack([jnp.exp(lf * (a + 1.0)), jnp.exp(lb * (CHUNK - a))], axis=1)
    i = jnp.arange(CHUNK, dtype=F32)[:, None]
    j = jnp.arange(CHUNK, dtype=F32)[None, :]
    dif = (i - j)[None, None]
    ret_m = jnp.where(dif >= 0, jnp.exp(lgf[:, :, None, None] * jnp.maximum(dif, 0.0)),
                      jnp.exp(lgb[:, :, None, None] * jnp.maximum(-dif, 0.0)))
    cd = jnp.stack([jnp.exp(lgf[:, h256] * CHUNK), jnp.exp(lgb[:, h256] * CHUNK)], axis=1)[:, :, None, :]
    bd = (h128[:, None] == h256[None, :]).astype(np.float32)
    seg = (h256[:, None] == h256[None, :]).astype(np.float32)
    tri = (np.arange(TM)[:, None] < np.arange(TM)[None, :]).astype(np.float32)
    jj = np.arange(_DEINT // 2)
    deint = np.zeros((_DEINT, _DEINT), np.float32)
    deint[2 * jj, jj] = 1.0
    deint[2 * jj + 1, _DEINT // 2 + jj] = 1.0
    rw_t = jnp.swapaxes(p["router_w"], 1, 2)
    rw_hi = rw_t.astype(BF16)
    rw_lo = (rw_t - rw_hi.astype(F32)).astype(BF16)
    sg_bias = jnp.repeat(jnp.swapaxes(p["sg_b"], 1, 2), SG_WIDTH // SG_GROUPS, axis=2)
    bgu = p["b_gate_up"]
    return {
        "w_in": p["w_in"][:, :, _in_perm()].astype(MXU_DT),
        "q_g": p["mla_q_norm_g"][:, None, :], "kv_g": p["mla_kv_norm_g"][:, None, :],
        "w_uq": p["mla_w_uq"][:, :, _uq_perm()].astype(MXU_DT),
        "w_ukv": p["mla_w_ukv"].astype(MXU_DT),
        "sg_g": p["sg_norm_g"][:, None, :], "sg_b": p["sg_norm_b"][:, None, :],
        "sg_w": p["sg_w"].reshape(nl, SG_GROUPS * CHUNK, CHUNK).astype(MXU_DT),
        "sg_bias": sg_bias,
        "kdec": kdec, "qdec": qdec, "ret_m": ret_m, "cd": cd,
        "bd": jnp.broadcast_to(jnp.asarray(bd), (nl,) + bd.shape),
        "seg": jnp.broadcast_to(jnp.asarray(seg, BF16), (nl,) + seg.shape),
        "tri": jnp.broadcast_to(jnp.asarray(tri, BF16), (nl,) + tri.shape),
        "w_o": p["w_o"].astype(MXU_DT),
        "ln1_g": p["ln1_g"][:, None, :], "ln1_b": p["ln1_b"][:, None, :],
        "ln2_g": p["ln2_g"][:, None, :], "ln2_b": p["ln2_b"][:, None, :],
        "router_w": jnp.concatenate([rw_hi, rw_lo], axis=1),
        "router_b": p["router_b"][:, :, None],
        "b_glu": bgu[:, :, None, 0::2], "b_lin": bgu[:, :, None, 1::2],
        "b_down": p["b_down"][:, :, None, :],
        "deint": jnp.broadcast_to(jnp.asarray(deint, MXU_DT), (nl,) + deint.shape),
    }


def _route(idx, rank, counts):
    n_assign = idx.shape[1] * TOP_K
    nb = -(-(n_assign + N_EXPERTS * (MOE_BM - 1)) // MOE_BM)
    padded = (counts + MOE_BM - 1) // MOE_BM * MOE_BM
    pad_end = jnp.cumsum(padded)
    pad_start = pad_end - padded
    experts = jnp.arange(N_EXPERTS, dtype=jnp.int32)
    dest = rank + jnp.sum(jnp.where(idx[..., None] == experts, pad_start, 0), axis=-1)
    blk_start = jnp.arange(nb, dtype=jnp.int32) * MOE_BM
    block_e = jnp.minimum(jnp.sum((pad_end[None, :] <= blk_start[:, None]).astype(jnp.int32), axis=1),
                          N_EXPERTS - 1)
    n_used = (pad_end[-1] // MOE_BM).astype(jnp.int32).reshape(1)
    return dest.astype(jnp.int32), block_e, n_used, nb


def _sc_workers():
    info = plsc.get_sparse_core_info()
    return info.num_cores, info.num_cores * info.num_subcores


def sc_dispatch(rows, dest3, cap):
    n, w = rows.shape
    nch, kk, c = dest3.shape
    ncores, nw = _sc_workers()
    per_w = nch // nw
    mesh = plsc.VectorSubcoreMesh(core_axis_name="c", subcore_axis_name="s")

    @functools.partial(
        pl.kernel, mesh=mesh, out_type=jax.ShapeDtypeStruct((cap, w), rows.dtype),
        scratch_types=[pltpu.VMEM((kk, c), jnp.int32), pltpu.VMEM((c, w), rows.dtype)])
    def scatter_rows(h_hbm, d_hbm, o_hbm, idx_v, rows_v):
        wid = lax.axis_index("s") * ncores + lax.axis_index("c")

        @pl.loop(0, per_w)
        def _(j):
            ch = wid * per_w + j
            pltpu.sync_copy(d_hbm.at[ch], idx_v)
            pltpu.sync_copy(h_hbm.at[pl.ds(ch * c, c)], rows_v)
            for q in range(kk):
                pltpu.sync_copy(rows_v, o_hbm.at[idx_v.at[q]])

    return scatter_rows(rows, dest3)


def sc_combine_gather(y, dest3, n):
    cap, d = y.shape
    nch, kk, c = dest3.shape
    ncores, nw = _sc_workers()
    per_w = nch // nw
    mesh = plsc.VectorSubcoreMesh(core_axis_name="c", subcore_axis_name="s")

    @functools.partial(
        pl.kernel, mesh=mesh, out_type=jax.ShapeDtypeStruct((kk, n, d), y.dtype),
        scratch_types=[pltpu.VMEM((kk, c), jnp.int32), pltpu.VMEM((c, d), y.dtype)])
    def gather_rows(y_hbm, d_hbm, o_hbm, idx_v, rows_v):
        wid = lax.axis_index("s") * ncores + lax.axis_index("c")

        @pl.loop(0, per_w)
        def _(j):
            ch = wid * per_w + j
            pltpu.sync_copy(d_hbm.at[ch], idx_v)
            for q in range(kk):
                pltpu.sync_copy(y_hbm.at[idx_v.at[q]], rows_v)
                pltpu.sync_copy(rows_v, o_hbm.at[q, pl.ds(ch * c, c)])

    return gather_rows(y, dest3)


def kernel(x, c, ctx, c_ctx, ada_w, ada_b, w_in, mla_q_norm_g, mla_kv_norm_g, mla_w_uq, mla_w_ukv,
           sg_norm_g, sg_norm_b, sg_w, sg_b, ret_decay_fwd, ret_decay_bwd, w_o, ln1_g, ln1_b,
           router_w, router_b, w_gate_up, b_gate_up, w_down, b_down, ln2_g, ln2_b):
    b, s_len, d = x.shape
    lc = ctx.shape[1]
    assert d == D_MODEL and lc % TM == 0 and s_len % lc == 0 and s_len % GRID_W == 0
    assert b + 1 <= 8
    t = s_len + lc
    n_lat_tiles = s_len // TM
    params = dict(w_in=w_in, mla_q_norm_g=mla_q_norm_g, mla_kv_norm_g=mla_kv_norm_g, mla_w_uq=mla_w_uq,
                  mla_w_ukv=mla_w_ukv, sg_norm_g=sg_norm_g, sg_norm_b=sg_norm_b, sg_w=sg_w, sg_b=sg_b,
                  ret_decay_fwd=ret_decay_fwd, ret_decay_bwd=ret_decay_bwd, w_o=w_o, ln1_g=ln1_g,
                  ln1_b=ln1_b, router_w=router_w, router_b=router_b, w_gate_up=w_gate_up,
                  b_gate_up=b_gate_up, w_down=w_down, b_down=b_down, ln2_g=ln2_g, ln2_b=ln2_b)
    lws = _layer_weights(params)
    ew = expert_weights(w_gate_up, w_down, lws["deint"][0])
    tab = _rotation_tables(s_len, lc)

    c_rows = jnp.concatenate([c, c_ctx[None, :], jnp.zeros((8 - b - 1, d), F32)], axis=0)
    mod = ada_modulation(c_rows, ada_w, ada_b).reshape(DEPTH, 8, N_MOD, d)
    mod_tab = jnp.stack([mod[:, :b], jnp.broadcast_to(mod[:, b:b + 1], (DEPTH, b, N_MOD, d))], axis=2)

    n_streams = N_STREAMS if b % N_STREAMS == 0 else 1
    bs = b // n_streams
    xs = [jnp.concatenate([x[i * bs:(i + 1) * bs], ctx[i * bs:(i + 1) * bs]], axis=1)
          for i in range(n_streams)]
    order = c_rows
    held = None
    for li in range(DEPTH):
        lw = {k: v[li] for k, v in lws.items()}
        mts = [mod_tab[li, si * bs:(si + 1) * bs] for si in range(n_streams)]
        fronts = []
        for si in range(n_streams):
            if held is not None and si == n_streams - 1:
                xs[si] = _finish(*held, after=order)
                held = None
            fr = _front(xs[si], mts[si], tab, lw, s_len, lc, after=order)
            order = fr["cnt"]
            fronts.append(fr)
        ys = []
        for si in range(n_streams):
            y = expert_ffn(fronts[si]["xg"], fronts[si]["block_e"], fronts[si]["n_used"], ew, li, lw,
                           after=order)
            order = y
            ys.append(y)
        for si in range(n_streams):
            args = (fronts[si], ys[si], mts[si], lw, s_len)
            if si == n_streams - 1 and n_streams > 1 and li + 1 < DEPTH:
                held = args
            else:
                xs[si] = _finish(*args, after=order)
                order = xs[si]
    return jnp.concatenate([xi[:, :s_len] for xi in xs], axis=0)


def _front(x_all, mt, tab, lw, s_len, lc, after):
    b, t, d = x_all.shape
    n_lat_tiles = s_len // TM
    q, k, v, sg, retp, rv, a = input_projection(x_all, mt, tab, lw, n_lat_tiles, after)
    mla_lat, mla_ctx = mla_attention(q, k, v, s_len, lc)
    st = retention_scan(a, lw["cd"], s_len // CHUNK)
    x1, h2, idx, gates, rank, cnt = output_projection(
        x_all, mla_lat, mla_ctx, sg, retp, rv, st, mt, lw, n_lat_tiles)
    to_tok = lambda z: z.transpose(2, 0, 1, 3).reshape(TOP_K, b * t)
    dest, block_e, n_used, nb = _route(to_tok(idx), to_tok(rank), cnt[:, 0])
    assert (b * t) % SC_CHUNK == 0
    dest3 = dest.reshape(TOP_K, (b * t) // SC_CHUNK, SC_CHUNK).transpose(1, 0, 2)
    xg = sc_dispatch(h2.reshape(b * t, h2.shape[-1]), dest3, nb * MOE_BM)
    return dict(x1=x1, gates=gates, cnt=cnt, dest3=dest3, xg=xg, block_e=block_e, n_used=n_used)


def _finish(fr, y, mt, lw, s_len, after):
    b, t, d = fr["x1"].shape
    yg = sc_combine_gather(y, fr["dest3"], b * t).reshape(TOP_K, b, t, y.shape[-1])
    gates_tok = fr["gates"].transpose(0, 1, 3, 2).reshape(b, t, TOP_K)
    return combine_deepnorm2(fr["x1"], yg, gates_tok, mt, lw, s_len // TM, after)
```

```python
import functools

import numpy as np
import jax
import jax.numpy as jnp
from jax import lax
from jax.experimental import pallas as pl
from jax.experimental.pallas import tpu as pltpu
from jax.experimental.pallas import tpu_sc as plsc

F32 = jnp.float32
BF16 = jnp.bfloat16
MXU_DT = BF16
PACK_ROWS = True

D_MODEL = 1024
DEPTH = 4
GRID_W = 64
MLA_HEADS = 4
MLA_NOPE = 128
MLA_ROPE = 64
MLA_V = 128
MLA_Q_LORA = 384
MLA_KV_LORA = 256
MLA_QK = MLA_NOPE + MLA_ROPE
MLA_SCALE = MLA_QK ** -0.5
ROPE_BASE = 10000.0
ROPE_AXIS_FREQS = MLA_ROPE // 4
SG_GROUPS = 4
SG_WIDTH = 256
SG_CHUNK = 128
RET_HEADS = 4
RET_QK = 32
RET_V = 64
RET_CHUNK = 128
RET_ROPE_BASE = 10000.0
N_EXPERTS = 32
TOP_K = 4
D_EXPERT = 1024
SWIGLU_LIMIT = 7.0
SWIGLU_ALPHA = 1.702
N_MOD = 6
LN_EPS = 1e-5
RMS_EPS = 1e-6
DEEPNORM_ALPHA = (2 * DEPTH) ** 0.25
MLA_OUT = MLA_HEADS * MLA_V
RET_OUT = RET_HEADS * RET_V

TM = 256
CHUNK = 128
MOE_BM = 512
SC_CHUNK = 48
N_STREAMS = 2
ATT_TQ = 1024
ATT_TK = 2048
VMEM_LIMIT = 48 * 2 ** 20

_O_CQ, _O_CKV, _O_SGU, _O_SGV = 0, 384, 640, 896
_O_RQ, _O_RK, _O_RQS, _O_RKS, _O_RV, _O_RG, _O_KR = 1152, 1280, 1408, 1536, 1664, 1920, 2176
IN_P = 2304
_T_QC, _T_QS, _T_KCS, _T_RQC, _T_RQS, _T_RKC, _T_RKS = 0, 256, 512, 640, 768, 896, 1024
TAB_W = 1152


def _cparams(sem):
    return pltpu.CompilerParams(dimension_semantics=sem, vmem_limit_bytes=VMEM_LIMIT)


def _dot(a, b):
    return jnp.dot(a, b, preferred_element_type=F32)


def _dot_nt(a, b):
    return lax.dot_general(a, b, (((1,), (1,)), ((), ())), preferred_element_type=F32)


def _mx(a):
    return a.astype(MXU_DT)


def _swap16(j):
    return (j // 32) * 32 + ((j % 32) + 16) % 32


_ERF_ALPHA = (-2.72614225801306e-10, 2.77068142495902e-08, -2.10102402082508e-06,
              -5.69250639462346e-05, -7.34990630326855e-04, -2.95459980854025e-03,
              -1.60960333262415e-02)
_ERF_BETA = (-1.45660718464996e-05, -2.13374055278905e-04, -1.68282697438203e-03,
             -7.37332916720468e-03, -1.42647390514189e-02)


def _erf(x):
    x = jnp.clip(x, -4.0, 4.0)
    x2 = x * x
    p = jnp.full_like(x, _ERF_ALPHA[0])
    for c in _ERF_ALPHA[1:]:
        p = p * x2 + c
    q = jnp.full_like(x, _ERF_BETA[0])
    for c in _ERF_BETA[1:]:
        q = q * x2 + c
    return x * p / q


def _gelu(x):
    return 0.5 * x * (1.0 + _erf(x * 0.7071067811865476))


def _sigmoid(x):
    return 1.0 / (1.0 + jnp.exp(-x))


def _ln(x):
    xc = x - jnp.mean(x, axis=-1, keepdims=True)
    return xc * lax.rsqrt(jnp.mean(xc * xc, axis=-1, keepdims=True) + LN_EPS)


def _lane_group(shape, width):
    return lax.broadcasted_iota(jnp.int32, shape, len(shape) - 1) // width


def _ada_kernel(c_ref, w_ref, b_ref, o_ref):
    c = c_ref[...]
    o_ref[0] = _dot(c * _sigmoid(c), w_ref[0]) + b_ref[0]


def ada_modulation(c_rows, ada_w, ada_b):
    nl, d, n = ada_w.shape
    tn = 1536
    return pl.pallas_call(
        _ada_kernel,
        grid=(nl, n // tn),
        in_specs=[pl.BlockSpec((8, d), lambda l, j: (0, 0)),
                  pl.BlockSpec((1, d, tn), lambda l, j: (l, 0, j)),
                  pl.BlockSpec((1, 1, tn), lambda l, j: (l, 0, j))],
        out_specs=pl.BlockSpec((1, 8, tn), lambda l, j: (l, 0, j)),
        out_shape=jax.ShapeDtypeStruct((nl, 8, n), F32),
        compiler_params=_cparams(("arbitrary", "arbitrary")),
        name="ada_modulation",
    )(c_rows, ada_w, ada_b.reshape(nl, 1, n))


def _inproj_kernel(x_ref, mod_ref, tab_ref, w_in_ref, qg_ref, kvg_ref, w_uq_ref, w_ukv_ref,
                   sgg_ref, sgb_ref, sgw_ref, sgbias_ref, kdec_ref, bd_ref,
                   q_ref, k_ref, v_ref, sg_ref, retp_ref, rv_ref, a_ref):
    x = x_ref[0]
    mod = mod_ref[0, 0]
    h = x * (1.0 + mod[1:2]) + mod[0:1]
    p = _dot(_mx(h), w_in_ref[...])
    tab = tab_ref[...]

    cq = p[:, _O_CQ:_O_CQ + MLA_Q_LORA]
    cq = cq * lax.rsqrt(jnp.mean(cq * cq, axis=-1, keepdims=True) + RMS_EPS) * qg_ref[...]
    qa = _dot(_mx(cq), w_uq_ref[...])
    rot = (qa[:, 512:768] * tab[:, _T_QC:_T_QC + 256]
           + qa[:, 768:1024] * tab[:, _T_QS:_T_QS + 256])
    for hh in range(MLA_HEADS):
        q_ref[0, hh, :, 0:128] = (qa[:, 128 * hh:128 * hh + 128] * MLA_SCALE).astype(q_ref.dtype)
        g = hh // 2
        q_ref[0, hh, :, 128:256] = rot[:, 128 * g:128 * g + 128].astype(q_ref.dtype)

    ckv = p[:, _O_CKV:_O_CKV + MLA_KV_LORA]
    ckv = ckv * lax.rsqrt(jnp.mean(ckv * ckv, axis=-1, keepdims=True) + RMS_EPS) * kvg_ref[...]
    kv = _dot(_mx(ckv), w_ukv_ref[...])
    t = p[:, _O_KR:_O_KR + 128] * tab[:, _T_KCS:_T_KCS + 128]
    u = t + pltpu.roll(t, 64, axis=1)
    low = lax.broadcasted_iota(jnp.int32, u.shape, 1) < 64
    kx = (jnp.where(low, u, 0.0), jnp.where(low, 0.0, u))
    ones_col = jnp.where(lax.broadcasted_iota(jnp.int32, u.shape, 1) == 0, 1.0, 0.0).astype(v_ref.dtype)
    for hh in range(MLA_HEADS):
        k_ref[0, hh, :, 0:128] = kv[:, 256 * hh:256 * hh + 128].astype(k_ref.dtype)
        k_ref[0, hh, :, 128:256] = kx[hh % 2].astype(k_ref.dtype)
        v_ref[0, hh, :, 0:128] = kv[:, 256 * hh + 128:256 * hh + 256].astype(v_ref.dtype)
        v_ref[0, hh, :, 128:256] = ones_col

    gu = _gelu(p[:, _O_SGU:_O_SGU + SG_WIDTH])
    gv = _ln(_gelu(p[:, _O_SGV:_O_SGV + SG_WIDTH])) * sgg_ref[...] + sgb_ref[...]
    gvm = _mx(gv)
    grp = _lane_group((CHUNK, SG_WIDTH), SG_WIDTH // SG_GROUPS)
    for c in range(TM // CHUNK):
        rows = slice(c * CHUNK, (c + 1) * CHUNK)
        res = _dot(sgw_ref[...], gvm[rows])
        mixed = sgbias_ref[...]
        for g in range(SG_GROUPS):
            mixed = mixed + jnp.where(grp == g, res[g * CHUNK:(g + 1) * CHUNK], 0.0)
        sg_ref[0, rows, :] = (gu[rows] * mixed).astype(sg_ref.dtype)

    rq = (p[:, _O_RQ:_O_RQ + 128] * tab[:, _T_RQC:_T_RQC + 128]
          + p[:, _O_RQS:_O_RQS + 128] * tab[:, _T_RQS:_T_RQS + 128])
    rk = (p[:, _O_RK:_O_RK + 128] * tab[:, _T_RKC:_T_RKC + 128]
          + p[:, _O_RKS:_O_RKS + 128] * tab[:, _T_RKS:_T_RKS + 128])
    rv = p[:, _O_RV:_O_RV + RET_OUT]
    retp_ref[0, :, 0:128] = rq
    retp_ref[0, :, 128:256] = rk
    retp_ref[0, :, 256:512] = p[:, _O_RG:_O_RG + RET_OUT]
    rvm = _mx(rv)
    rv_ref[0] = rvm.astype(rv_ref.dtype)
    bd = bd_ref[...]
    for c in range(TM // CHUNK):
        rows = slice(c * CHUNK, (c + 1) * CHUNK)
        for d in range(2):
            kd_t = _mx((rk[rows] * kdec_ref[d]).T)
            af = _dot(kd_t, rvm[rows]) * bd
            a_ref[0, c, d] = (af[0:32] + af[32:64]) + (af[64:96] + af[96:128])


def _ordered_after(kernel_fn, pos):
    def wrapped(*refs):
        return kernel_fn(*refs[:pos], *refs[pos + 1:])
    return wrapped


_ORDER_SPEC = pl.BlockSpec(memory_space=pl.ANY)


def input_projection(x_all, mod_tab, tab, lw, n_lat_tiles, after):
    b, t, d = x_all.shape
    nt = t // TM
    nc = t // CHUNK
    cpt = TM // CHUNK
    const2 = lambda bi, j: (0, 0)
    const3 = lambda bi, j: (0, 0, 0)
    out_shape = (
        jax.ShapeDtypeStruct((b, MLA_HEADS, t, 256), MXU_DT),
        jax.ShapeDtypeStruct((b, MLA_HEADS, t, 256), MXU_DT),
        jax.ShapeDtypeStruct((b, MLA_HEADS, t, 256), MXU_DT),
        jax.ShapeDtypeStruct((b, t, SG_WIDTH), MXU_DT),
        jax.ShapeDtypeStruct((b, t, 512), F32),
        jax.ShapeDtypeStruct((b, t, RET_OUT), MXU_DT),
        jax.ShapeDtypeStruct((b, nc, 2, RET_QK, RET_OUT), F32),
    )
    head_spec = lambda w: pl.BlockSpec((1, MLA_HEADS, TM, w), lambda bi, j: (bi, 0, j, 0))
    tok_spec = lambda w: pl.BlockSpec((1, TM, w), lambda bi, j: (bi, j, 0))
    return pl.pallas_call(
        _ordered_after(_inproj_kernel, 14),
        grid=(b, nt),
        in_specs=[
            tok_spec(d),
            pl.BlockSpec((1, 1, N_MOD, d), lambda bi, j: (bi, j // n_lat_tiles, 0, 0)),
            pl.BlockSpec((TM, TAB_W), lambda bi, j: (j, 0)),
            pl.BlockSpec((d, IN_P), const2),
            pl.BlockSpec((1, MLA_Q_LORA), const2),
            pl.BlockSpec((1, MLA_KV_LORA), const2),
            pl.BlockSpec((MLA_Q_LORA, 1024), const2),
            pl.BlockSpec((MLA_KV_LORA, 1024), const2),
            pl.BlockSpec((1, SG_WIDTH), const2),
            pl.BlockSpec((1, SG_WIDTH), const2),
            pl.BlockSpec((SG_GROUPS * CHUNK, CHUNK), const2),
            pl.BlockSpec((CHUNK, SG_WIDTH), const2),
            pl.BlockSpec((2, CHUNK, 128), const3),
            pl.BlockSpec((128, RET_OUT), const2),
            _ORDER_SPEC,
        ],
        out_specs=(head_spec(256), head_spec(256), head_spec(256), tok_spec(SG_WIDTH),
                   tok_spec(512), tok_spec(RET_OUT),
                   pl.BlockSpec((1, cpt, 2, RET_QK, RET_OUT), lambda bi, j: (bi, j, 0, 0, 0))),
        out_shape=out_shape,
        compiler_params=_cparams(("parallel", "parallel")),
        name="input_projection",
    )(x_all, mod_tab, tab, lw["w_in"], lw["q_g"], lw["kv_g"], lw["w_uq"], lw["w_ukv"],
      lw["sg_g"], lw["sg_b"], lw["sg_w"], lw["sg_bias"], lw["kdec"], lw["bd"], after)


def _attn_kernel(q_ref, k_ref, v_ref, o_ref, *, n_main, tk, tail):
    q = q_ref[0, 0]
    tq = q.shape[0]
    chunks = [(i * tk, tk) for i in range(n_main)] + ([(n_main * tk, tail)] if tail else [])

    def scores(ci):
        start, size = chunks[ci]
        return _dot_nt(q, k_ref[0, 0, start:start + size, :])

    m = jnp.full((tq, 1), -1e30, F32)
    acc = jnp.zeros((tq, 256), F32)
    s_next = scores(0)
    for ci, (start, size) in enumerate(chunks):
        s = s_next
        if ci + 1 < len(chunks):
            s_next = scores(ci + 1)
        m_new = jnp.maximum(m, jnp.max(s, axis=-1, keepdims=True))
        p = jnp.exp(s - m_new)
        acc = jnp.exp(m - m_new) * acc + _dot(_mx(p), v_ref[0, 0, start:start + size, :])
        m = m_new
    o_ref[0] = (acc[:, 0:MLA_V] / acc[:, MLA_V:MLA_V + 1]).astype(o_ref.dtype)


def mla_attention(q, k, v, s_len, lc):
    b, hn, t, _ = q.shape
    tq = min(ATT_TQ, s_len)
    tk = min(ATT_TK, s_len)
    kv_full = pl.BlockSpec((1, 1, t, 256), lambda bi, hi, i: (bi, hi, 0, 0))
    out_lat = pl.pallas_call(
        functools.partial(_attn_kernel, n_main=s_len // tk, tk=tk, tail=lc),
        grid=(b, hn, s_len // tq),
        in_specs=[pl.BlockSpec((1, 1, tq, 256), lambda bi, hi, i: (bi, hi, i, 0)), kv_full, kv_full],
        out_specs=pl.BlockSpec((1, tq, MLA_V), lambda bi, hi, i: (bi, i, hi)),
        out_shape=jax.ShapeDtypeStruct((b, s_len, MLA_OUT), MXU_DT),
        compiler_params=_cparams(("parallel", "parallel", "arbitrary")),
        name="mla_attention_latent",
    )(q, k, v)
    cblk = s_len // lc
    ctx_spec = pl.BlockSpec((1, 1, lc, 256), lambda bi, hi: (bi, hi, cblk, 0))
    out_ctx = pl.pallas_call(
        functools.partial(_attn_kernel, n_main=0, tk=tk, tail=lc),
        grid=(b, hn),
        in_specs=[ctx_spec, ctx_spec, ctx_spec],
        out_specs=pl.BlockSpec((1, lc, MLA_V), lambda bi, hi: (bi, 0, hi)),
        out_shape=jax.ShapeDtypeStruct((b, lc, MLA_OUT), MXU_DT),
        compiler_params=_cparams(("parallel", "parallel")),
        name="mla_attention_context",
    )(q, k, v)
    return out_lat, out_ctx


def _ret_scan_kernel(a_ref, cd_ref, s_ref, *, n_lat_chunks):
    nc = a_ref.shape[1]
    ncc = nc - n_lat_chunks
    cd_f, cd_b = cd_ref[0], cd_ref[1]

    def body(n, carry):
        sf, sb = carry
        cf = jnp.where(n < ncc, n_lat_chunks + n, n - ncc)
        cb = jnp.where(n < ncc, nc - 1 - n, n_lat_chunks - 1 - (n - ncc))
        s_ref[0, cf, 0] = sf
        s_ref[0, cb, 1] = sb
        return sf * cd_f + a_ref[0, cf, 0], sb * cd_b + a_ref[0, cb, 1]

    zero = jnp.zeros((RET_QK, RET_OUT), F32)
    lax.fori_loop(0, nc, body, (zero, zero))


def retention_scan(a, cd, n_lat_chunks):
    b, nc = a.shape[:2]
    blk = pl.BlockSpec((1, nc, 2, RET_QK, RET_OUT), lambda bi: (bi, 0, 0, 0, 0))
    return pl.pallas_call(
        functools.partial(_ret_scan_kernel, n_lat_chunks=n_lat_chunks),
        grid=(b,),
        in_specs=[blk, pl.BlockSpec((2, 1, RET_OUT), lambda bi: (0, 0, 0))],
        out_specs=blk,
        out_shape=jax.ShapeDtypeStruct(a.shape, F32),
        compiler_params=_cparams(("parallel",)),
        name="retention_scan",
    )(a, cd)


def _split_dot(x, ones2):
    hi = x.astype(BF16)
    lo = (x - hi.astype(F32)).astype(BF16)
    return _dot(jnp.concatenate([hi, lo], axis=1), ones2)


def _outproj_kernel(x_ref, mlal_ref, mlac_ref, sg_ref, retp_ref, rv_ref, st_ref, mod_ref,
                    m_ref, qdec_ref, bd_ref, seg_ref, w_o_ref, lng_ref, lnb_ref, rw_ref, rb_ref, tri_ref,
                    x1_ref, h2_ref, idx_ref, gate_ref, rank_ref, cnt_ref, carry_ref, *, n_lat_tiles):
    @pl.when(pl.program_id(0) == 0)
    def _():
        carry_ref[...] = jnp.zeros_like(carry_ref)

    nbat = x_ref.shape[0]
    units = [(bb, c) for bb in range(nbat) for c in range(TM // CHUNK)]
    rows = lambda c: slice(c * CHUNK, (c + 1) * CHUNK)
    g32 = _lane_group((CHUNK, 128), RET_QK)
    g64 = _lane_group((CHUNK, RET_OUT), RET_V)
    bd = bd_ref[...]
    seg2 = jnp.concatenate([seg_ref[...], seg_ref[...]], axis=0)
    m4 = jnp.concatenate([m_ref[hh] for hh in range(RET_HEADS)], axis=0)

    rq, s4 = {}, {}
    for u in units:
        bb, c = u
        rq[u] = retp_ref[bb, rows(c), 0:128]
        q4 = jnp.concatenate([jnp.where(g32 == hh, rq[u], 0.0) for hh in range(RET_HEADS)], axis=0)
        s4[u] = _dot_nt(_mx(q4), _mx(retp_ref[bb, rows(c), 128:256]))
    o = {}
    for u in units:
        bb, c = u
        r = _dot(_mx(s4[u] * m4), rv_ref[bb, rows(c), :])
        qd = jnp.concatenate([rq[u] * qdec_ref[0], rq[u] * qdec_ref[1]], axis=1)
        st2 = jnp.concatenate([jnp.concatenate([st_ref[bb, c, dd]] * RET_HEADS, axis=0) * bd
                               for dd in range(2)], axis=0)
        acc = _dot(_mx(qd), _mx(st2))
        for hh in range(RET_HEADS):
            acc = acc + jnp.where(g64 == hh, r[hh * CHUNK:(hh + 1) * CHUNK], 0.0)
        o[u] = acc
    oc = {}
    for u in units:
        oc[u] = o[u] - _split_dot(o[u], seg2) * (1.0 / RET_V)
    ret = {}
    for u in units:
        bb, c = u
        var = _split_dot(oc[u] * oc[u], seg2) * (1.0 / RET_V)
        rg = retp_ref[bb, rows(c), 256:512]
        ret[u] = _mx(oc[u] * lax.rsqrt(var + LN_EPS) * (rg * _sigmoid(rg)))

    is_ctx = pl.program_id(0) >= n_lat_tiles
    ys = []
    for bb in range(nbat):
        mla = jnp.where(is_ctx, mlac_ref[bb], mlal_ref[bb])
        cat = jnp.concatenate(
            [mla, sg_ref[bb], jnp.concatenate([ret[(bb, c)] for c in range(TM // CHUNK)], axis=0)], axis=1)
        ys.append(_dot(cat, w_o_ref[...]))
    logits = []
    for bb in range(nbat):
        mod = mod_ref[bb, 0]
        x1 = _ln(DEEPNORM_ALPHA * x_ref[bb] + mod[2:3] * ys[bb]) * lng_ref[...] + lnb_ref[...]
        x1_ref[bb] = x1
        h2 = x1 * (1.0 + mod[4:5]) + mod[3:4]
        h2_ref[bb] = _pack_bf16_pairs(h2) if PACK_ROWS else h2
        h2_hi = h2.astype(BF16)
        h2_lo = (h2 - h2_hi.astype(F32)).astype(BF16)
        r2 = _dot_nt(rw_ref[...], jnp.concatenate([h2_hi, h2_lo], axis=0))
        logits.append(r2[0:N_EXPERTS, 0:TM] + r2[N_EXPERTS:, 0:TM] + r2[0:N_EXPERTS, TM:] + rb_ref[...])

    e_iota = lax.broadcasted_iota(jnp.int32, logits[0].shape, 0).astype(F32)
    idxs_all, onehots, prefixes = [], [], []
    for bb in range(nbat):
        work = logits[bb]
        vals, idxs = [], []
        for _ in range(TOP_K):
            mval = jnp.max(work, axis=0, keepdims=True)
            midx = jnp.min(jnp.where(work == mval, e_iota, float(N_EXPERTS)), axis=0, keepdims=True)
            vals.append(mval)
            idxs.append(midx)
            work = jnp.where(e_iota == midx, -jnp.inf, work)
        ex = [jnp.exp(vv - vals[0]) for vv in vals]
        den = ex[0] + ex[1] + ex[2] + ex[3]
        onehot = jnp.zeros_like(work)
        for kk in range(TOP_K):
            gate_ref[bb, 0, kk:kk + 1, :] = ex[kk] / den
            idx_ref[bb, 0, kk:kk + 1, :] = idxs[kk].astype(jnp.int32)
            onehot = onehot + jnp.where(e_iota == idxs[kk], 1.0, 0.0)
        idxs_all.append(idxs)
        onehots.append(onehot)
        prefixes.append(_dot(onehot.astype(BF16), tri_ref[...]))
    count = carry_ref[:, 0:1]
    for bb in range(nbat):
        base = count + prefixes[bb]
        for kk in range(TOP_K):
            rk_k = jnp.sum(jnp.where(e_iota == idxs_all[bb][kk], base, 0.0), axis=0, keepdims=True)
            rank_ref[bb, 0, kk:kk + 1, :] = rk_k.astype(jnp.int32)
        count = count + jnp.sum(onehots[bb], axis=1, keepdims=True)
    carry_ref[...] = jnp.broadcast_to(count, carry_ref.shape)
    cnt_ref[...] = carry_ref[...].astype(jnp.int32)


def output_projection(x_all, mla_lat, mla_ctx, sg, retp, rv, st, mod_tab, lw, n_lat_tiles):
    b, t, d = x_all.shape
    nt = t // TM
    cpt = TM // CHUNK
    const2 = lambda j: (0, 0)
    const3 = lambda j: (0, 0, 0)
    tok_spec = lambda w: pl.BlockSpec((b, TM, w), lambda j: (0, j, 0))
    route_spec = pl.BlockSpec((b, 1, TOP_K, TM), lambda j: (0, j, 0, 0))
    route_shape = lambda dt: jax.ShapeDtypeStruct((b, nt, TOP_K, TM), dt)
    h2w = d // 2 if PACK_ROWS else d
    return pl.pallas_call(
        functools.partial(_outproj_kernel, n_lat_tiles=n_lat_tiles),
        grid=(nt,),
        in_specs=[
            tok_spec(d),
            pl.BlockSpec((b, TM, MLA_OUT), lambda j: (0, jnp.minimum(j, n_lat_tiles - 1), 0)),
            pl.BlockSpec((b, TM, MLA_OUT), lambda j: (0, jnp.maximum(j - n_lat_tiles, 0), 0)),
            tok_spec(SG_WIDTH), tok_spec(512), tok_spec(RET_OUT),
            pl.BlockSpec((b, cpt, 2, RET_QK, RET_OUT), lambda j: (0, j, 0, 0, 0)),
            pl.BlockSpec((b, 1, N_MOD, d), lambda j: (0, j // n_lat_tiles, 0, 0)),
            pl.BlockSpec((RET_HEADS, CHUNK, CHUNK), const3),
            pl.BlockSpec((2, CHUNK, 128), const3),
            pl.BlockSpec((128, RET_OUT), const2),
            pl.BlockSpec((RET_OUT, RET_OUT), const2),
            pl.BlockSpec((d, d), const2),
            pl.BlockSpec((1, d), const2),
            pl.BlockSpec((1, d), const2),
            pl.BlockSpec((2 * N_EXPERTS, d), const2),
            pl.BlockSpec((N_EXPERTS, 1), const2),
            pl.BlockSpec((TM, TM), const2),
        ],
        out_specs=(tok_spec(d), tok_spec(h2w), route_spec, route_spec, route_spec,
                   pl.BlockSpec((N_EXPERTS, 128), const2)),
        out_shape=(jax.ShapeDtypeStruct((b, t, d), F32),
                   jax.ShapeDtypeStruct((b, t, h2w), jnp.uint32 if PACK_ROWS else F32),
                   route_shape(jnp.int32), route_shape(F32), route_shape(jnp.int32),
                   jax.ShapeDtypeStruct((N_EXPERTS, 128), jnp.int32)),
        scratch_shapes=[pltpu.VMEM((N_EXPERTS, 128), F32)],
        compiler_params=_cparams(("arbitrary",)),
        name="output_projection",
    )(x_all, mla_lat, mla_ctx, sg, retp, rv, st, mod_tab, lw["ret_m"], lw["qdec"], lw["bd"], lw["seg"],
      lw["w_o"], lw["ln1_g"], lw["ln1_b"], lw["router_w"], lw["router_b"], lw["tri"])


_DEINT = 256


def _expert_weights_kernel(wgu_ref, wd_ref, perm_ref, wg_ref, wl_ref, wdo_ref):
    half = _DEINT // 2
    for c in range(2 * D_EXPERT // _DEINT):
        r = _dot(_mx(wgu_ref[0, 0, :, _DEINT * c:_DEINT * (c + 1)]), perm_ref[...])
        wg_ref[0, 0, :, half * c:half * (c + 1)] = r[:, :half].astype(wg_ref.dtype)
        wl_ref[0, 0, :, half * c:half * (c + 1)] = r[:, half:].astype(wl_ref.dtype)
    wdo_ref[0, 0] = wd_ref[0, 0].astype(wdo_ref.dtype)


def expert_weights(w_gate_up, w_down, deint):
    nl, ne, d, de2 = w_gate_up.shape
    de = de2 // 2
    blk = lambda r, c: pl.BlockSpec((1, 1, r, c), lambda l, e: (l, e, 0, 0))
    return pl.pallas_call(
        _expert_weights_kernel,
        grid=(nl, ne),
        in_specs=[blk(d, de2), blk(de, d), pl.BlockSpec((_DEINT, _DEINT), lambda l, e: (0, 0))],
        out_specs=(blk(d, de), blk(d, de), blk(de, d)),
        out_shape=(jax.ShapeDtypeStruct((nl, ne, d, de), MXU_DT), jax.ShapeDtypeStruct((nl, ne, d, de), MXU_DT),
                   jax.ShapeDtypeStruct((nl, ne, de, d), MXU_DT)),
        compiler_params=_cparams(("parallel", "parallel")),
        name="expert_weights",
    )(w_gate_up, w_down, deint)


def _pack_bf16_pairs(v):
    bits = lax.bitcast_convert_type(v.astype(BF16).astype(F32), jnp.uint32)
    half = bits.shape[1] // 2
    return bits[:, :half] | (bits[:, half:] >> 16)


def _unpack_bf16_pairs(w):
    return (lax.bitcast_convert_type(w & jnp.uint32(0xFFFF0000), F32),
            lax.bitcast_convert_type(w << 16, F32))


def _expert_kernel(be_ref, nu_ref, x_ref, wg_s, wl_s, bg_ref, bl_ref, wd_s, bd_ref, y_ref):
    del be_ref
    active = pl.program_id(0) < nu_ref[0]

    @pl.when(active)
    def _():
        if PACK_ROWS:
            hi, lo = _unpack_bf16_pairs(x_ref[...])
            xb = jnp.concatenate([hi.astype(BF16), lo.astype(BF16)], axis=1)
        else:
            xb = x_ref[...]
        glu = jnp.minimum(_dot(xb, wg_s[0, 0]) + bg_ref[0], SWIGLU_LIMIT)
        lin = jnp.clip(_dot(xb, wl_s[0, 0]) + bl_ref[0], -SWIGLU_LIMIT, SWIGLU_LIMIT)
        act = glu * _sigmoid(SWIGLU_ALPHA * glu) * (lin + 1.0)
        y = _dot(_mx(act), wd_s[0, 0]) + bd_ref[0]
        y_ref[...] = _pack_bf16_pairs(y) if PACK_ROWS else y

    @pl.when(jnp.logical_not(active))
    def _():
        y_ref[...] = jnp.zeros_like(y_ref)


def expert_ffn(xg, block_e, n_used, ew, li, lw, after):
    cap, xw = xg.shape
    d = D_MODEL
    nb = cap // MOE_BM
    de = D_EXPERT
    xmap = lambda i, be, nu: (jnp.minimum(i, nu[0] - 1), 0)
    wmap = lambda i, be, nu: (be[i], 0, 0)
    lmap = lambda i, be, nu: (li, be[i], 0, 0)
    grid_spec = pltpu.PrefetchScalarGridSpec(
        num_scalar_prefetch=2,
        grid=(nb,),
        in_specs=[pl.BlockSpec((MOE_BM, xw), xmap),
                  pl.BlockSpec((1, 1, d, de), lmap),
                  pl.BlockSpec((1, 1, d, de), lmap),
                  pl.BlockSpec((1, 1, de), wmap), pl.BlockSpec((1, 1, de), wmap),
                  pl.BlockSpec((1, 1, de, d), lmap), pl.BlockSpec((1, 1, d), wmap),
                  _ORDER_SPEC],
        out_specs=pl.BlockSpec((MOE_BM, xw), lambda i, be, nu: (i, 0)),
    )
    return pl.pallas_call(
        _ordered_after(_expert_kernel, 9),
        grid_spec=grid_spec,
        out_shape=jax.ShapeDtypeStruct((cap, xw), xg.dtype),
        compiler_params=_cparams(("arbitrary",)),
        name="expert_ffn",
    )(block_e, n_used, xg, ew[0], ew[1], lw["b_glu"], lw["b_lin"], ew[2], lw["b_down"], after)


def _combine_ln2_kernel(x_ref, y_ref, gate_ref, mod_ref, g_ref, b_ref, o_ref):
    for bb in range(x_ref.shape[0]):
        gates = gate_ref[bb]
        if PACK_ROWS:
            f_hi, f_lo = 0.0, 0.0
            for kk in range(TOP_K):
                hi, lo = _unpack_bf16_pairs(y_ref[kk, bb])
                f_hi = f_hi + gates[:, kk:kk + 1] * hi
                f_lo = f_lo + gates[:, kk:kk + 1] * lo
            f = jnp.concatenate([f_hi, f_lo], axis=1)
        else:
            f = gates[:, 0:1] * y_ref[0, bb]
            for kk in range(1, TOP_K):
                f = f + gates[:, kk:kk + 1] * y_ref[kk, bb]
        mod = mod_ref[bb, 0]
        o_ref[bb] = _ln(DEEPNORM_ALPHA * x_ref[bb] + mod[5:6] * f) * g_ref[...] + b_ref[...]


def combine_deepnorm2(x1, yg, gates, mod_tab, lw, n_lat_tiles, after):
    b, t, d = x1.shape
    tok = pl.BlockSpec((b, TM, d), lambda j: (0, j, 0))
    vec = pl.BlockSpec((1, d), lambda j: (0, 0))
    return pl.pallas_call(
        _ordered_after(_combine_ln2_kernel, 6),
        grid=(t // TM,),
        in_specs=[tok,
                  pl.BlockSpec((TOP_K, b, TM, yg.shape[-1]), lambda j: (0, 0, j, 0)),
                  pl.BlockSpec((b, TM, TOP_K), lambda j: (0, j, 0)),
                  pl.BlockSpec((b, 1, N_MOD, d), lambda j: (0, j // n_lat_tiles, 0, 0)), vec, vec,
                  _ORDER_SPEC],
        out_specs=tok,
        out_shape=jax.ShapeDtypeStruct((b, t, d), F32),
        compiler_params=_cparams(("parallel",)),
        name="combine_deepnorm2",
    )(x1, yg, gates, mod_tab, lw["ln2_g"], lw["ln2_b"], after)


def _rotation_tables(s_len, lc):
    rows = s_len // GRID_W
    row = jnp.broadcast_to(jnp.arange(rows, dtype=F32)[:, None], (rows, GRID_W)).reshape(-1)
    col = jnp.broadcast_to(jnp.arange(GRID_W, dtype=F32)[None, :], (rows, GRID_W)).reshape(-1)
    inv = ROPE_BASE ** (-jnp.arange(ROPE_AXIS_FREQS, dtype=F32) / ROPE_AXIS_FREQS)
    ar, ac = row[:, None] * inv, col[:, None] * inv
    c64 = jnp.concatenate([jnp.cos(ar), jnp.cos(ar), jnp.cos(ac), jnp.cos(ac)], axis=1)
    s64 = jnp.concatenate([-jnp.sin(ar), jnp.sin(ar), -jnp.sin(ac), jnp.sin(ac)], axis=1)
    c64 = jnp.concatenate([c64, jnp.ones((lc, 64), F32)], axis=0)
    s64 = jnp.concatenate([s64, jnp.zeros((lc, 64), F32)], axis=0)
    half = RET_QK // 2
    pos = jnp.concatenate([lc + jnp.arange(s_len, dtype=F32), jnp.arange(lc, dtype=F32)])
    inv_r = 1.0 / (RET_ROPE_BASE ** jnp.linspace(0.0, 1.0, half, dtype=F32))
    ang = pos[:, None] * inv_r
    rc = jnp.tile(jnp.concatenate([jnp.cos(ang), jnp.cos(ang)], axis=1), (1, RET_HEADS))
    rs = jnp.tile(jnp.concatenate([-jnp.sin(ang), jnp.sin(ang)], axis=1), (1, RET_HEADS))
    qs = RET_QK ** -0.5
    return jnp.concatenate([
        jnp.tile(c64, (1, MLA_HEADS)) * MLA_SCALE, jnp.tile(s64, (1, MLA_HEADS)) * MLA_SCALE,
        c64, s64, rc * qs, rs * qs, rc, rs], axis=1)


def _in_perm():
    a = np.arange
    return np.concatenate([
        a(0, 640), a(704, 1216), a(1216, 1344), a(1344, 1472),
        1216 + _swap16(a(128)), 1344 + _swap16(a(128)), a(1472, 1984),
        640 + a(64), 640 + _swap16(a(64))])


def _uq_perm():
    a = np.arange
    nope = [h * MLA_QK + a(MLA_NOPE) for h in range(MLA_HEADS)]
    rope = [h * MLA_QK + MLA_NOPE + a(MLA_ROPE) for h in range(MLA_HEADS)]
    part = [h * MLA_QK + MLA_NOPE + _swap16(a(MLA_ROPE)) for h in range(MLA_HEADS)]
    return np.concatenate(nope + rope + part)


def _layer_weights(p):
    nl = p["w_in"].shape[0]
    lgf = jax.nn.log_sigmoid(p["ret_decay_fwd"].astype(F32))
    lgb = jax.nn.log_sigmoid(p["ret_decay_bwd"].astype(F32))
    h128 = np.arange(128) // RET_QK
    h256 = np.arange(RET_OUT) // RET_V
    a = jnp.arange(CHUNK, dtype=F32)[None, :, None]
    lf, lb = lgf[:, h128][:, None, :], lgb[:, h128][:, None, :]
    kdec = jnp.stack([jnp.exp(lf * (CHUNK - 1.0 - a)), jnp.exp(lb * a)], axis=1)
    qdec = jnp.stack([jnp.exp(lf * (a + 1.0)), jnp.exp(lb * (CHUNK - a))], axis=1)
    i = jnp.arange(CHUNK, dtype=F32)[:, None]
    j = jnp.arange(CHUNK, dtype=F32)[None, :]
    dif = (i - j)[None, None]
    ret_m = jnp.where(dif >= 0, jnp.exp(lgf[:, :, None, None] * jnp.maximum(dif, 0.0)),
                      jnp.exp(lgb[:, :, None, None] * jnp.maximum(-dif, 0.0)))
    cd = jnp.stack([jnp.exp(lgf[:, h256] * CHUNK), jnp.exp(lgb[:, h256] * CHUNK)], axis=1)[:, :, None, :]
    bd = (h128[:, None] == h256[None, :]).astype(np.float32)
    seg = (h256[:, None] == h256[None, :]).astype(np.float32)
    tri = (np.arange(TM)[:, None] < np.arange(TM)[None, :]).astype(np.float32)
    jj = np.arange(_DEINT // 2)
    deint = np.zeros((_DEINT, _DEINT), np.float32)
    deint[2 * jj, jj] = 1.0
    deint[2 * jj + 1, _DEINT // 2 + jj] = 1.0
    rw_t = jnp.swapaxes(p["router_w"], 1, 2)
    rw_hi = rw_t.astype(BF16)
    rw_lo = (rw_t - rw_hi.astype(F32)).astype(BF16)
    sg_bias = jnp.repeat(jnp.swapaxes(p["sg_b"], 1, 2), SG_WIDTH // SG_GROUPS, axis=2)
    bgu = p["b_gate_up"]
    return {
        "w_in": p["w_in"][:, :, _in_perm()].astype(MXU_DT),
        "q_g": p["mla_q_norm_g"][:, None, :], "kv_g": p["mla_kv_norm_g"][:, None, :],
        "w_uq": p["mla_w_uq"][:, :, _uq_perm()].astype(MXU_DT),
        "w_ukv": p["mla_w_ukv"].astype(MXU_DT),
        "sg_g": p["sg_norm_g"][:, None, :], "sg_b": p["sg_norm_b"][:, None, :],
        "sg_w": p["sg_w"].reshape(nl, SG_GROUPS * CHUNK, CHUNK).astype(MXU_DT),
        "sg_bias": sg_bias,
        "kdec": kdec, "qdec": qdec, "ret_m": ret_m, "cd": cd,
        "bd": jnp.broadcast_to(jnp.asarray(bd), (nl,) + bd.shape),
        "seg": jnp.broadcast_to(jnp.asarray(seg, BF16), (nl,) + seg.shape),
        "tri": jnp.broadcast_to(jnp.asarray(tri, BF16), (nl,) + tri.shape),
        "w_o": p["w_o"].astype(MXU_DT),
        "ln1_g": p["ln1_g"][:, None, :], "ln1_b": p["ln1_b"][:, None, :],
        "ln2_g": p["ln2_g"][:, None, :], "ln2_b": p["ln2_b"][:, None, :],
        "router_w": jnp.concatenate([rw_hi, rw_lo], axis=1),
        "router_b": p["router_b"][:, :, None],
        "b_glu": bgu[:, :, None, 0::2], "b_lin": bgu[:, :, None, 1::2],
        "b_down": p["b_down"][:, :, None, :],
        "deint": jnp.broadcast_to(jnp.asarray(deint, MXU_DT), (nl,) + deint.shape),
    }


def _route(idx, rank, counts):
    n_assign = idx.shape[1] * TOP_K
    nb = -(-(n_assign + N_EXPERTS * (MOE_BM - 1)) // MOE_BM)
    padded = (counts + MOE_BM - 1) // MOE_BM * MOE_BM
    pad_end = jnp.cumsum(padded)
    pad_start = pad_end - padded
    experts = jnp.arange(N_EXPERTS, dtype=jnp.int32)
    dest = rank + jnp.sum(jnp.where(idx[..., None] == experts, pad_start, 0), axis=-1)
    blk_start = jnp.arange(nb, dtype=jnp.int32) * MOE_BM
    block_e = jnp.minimum(jnp.sum((pad_end[None, :] <= blk_start[:, None]).astype(jnp.int32), axis=1),
                          N_EXPERTS - 1)
    n_used = (pad_end[-1] // MOE_BM).astype(jnp.int32).reshape(1)
    return dest.astype(jnp.int32), block_e, n_used, nb


def _sc_workers():
    info = plsc.get_sparse_core_info()
    return info.num_cores, info.num_cores * info.num_subcores


def sc_dispatch(rows, dest3, cap):
    n, w = rows.shape
    nch, kk, c = dest3.shape
    ncores, nw = _sc_workers()
    per_w = nch // nw
    mesh = plsc.VectorSubcoreMesh(core_axis_name="c", subcore_axis_name="s")

    @functools.partial(
        pl.kernel, mesh=mesh, out_type=jax.ShapeDtypeStruct((cap, w), rows.dtype),
        scratch_types=[pltpu.VMEM((kk, c), jnp.int32), pltpu.VMEM((c, w), rows.dtype)])
    def scatter_rows(h_hbm, d_hbm, o_hbm, idx_v, rows_v):
        wid = lax.axis_index("s") * ncores + lax.axis_index("c")

        @pl.loop(0, per_w)
        def _(j):
            ch = wid * per_w + j
            pltpu.sync_copy(d_hbm.at[ch], idx_v)
            pltpu.sync_copy(h_hbm.at[pl.ds(ch * c, c)], rows_v)
            for q in range(kk):
                pltpu.sync_copy(rows_v, o_hbm.at[idx_v.at[q]])

    return scatter_rows(rows, dest3)


def sc_combine_gather(y, dest3, n):
    cap, d = y.shape
    nch, kk, c = dest3.shape
    ncores, nw = _sc_workers()
    per_w = nch // nw
    mesh = plsc.VectorSubcoreMesh(core_axis_name="c", subcore_axis_name="s")

    @functools.partial(
        pl.kernel, mesh=mesh, out_type=jax.ShapeDtypeStruct((kk, n, d), y.dtype),
        scratch_types=[pltpu.VMEM((kk, c), jnp.int32), pltpu.VMEM((c, d), y.dtype)])
    def gather_rows(y_hbm, d_hbm, o_hbm, idx_v, rows_v):
        wid = lax.axis_index("s") * ncores + lax.axis_index("c")

        @pl.loop(0, per_w)
        def _(j):
            ch = wid * per_w + j
            pltpu.sync_copy(d_hbm.at[ch], idx_v)
            for q in range(kk):
                pltpu.sync_copy(y_hbm.at[idx_v.at[q]], rows_v)
                pltpu.sync_copy(rows_v, o_hbm.at[q, pl.ds(ch * c, c)])

    return gather_rows(y, dest3)


def kernel(x, c, ctx, c_ctx, ada_w, ada_b, w_in, mla_q_norm_g, mla_kv_norm_g, mla_w_uq, mla_w_ukv,
           sg_norm_g, sg_norm_b, sg_w, sg_b, ret_decay_fwd, ret_decay_bwd, w_o, ln1_g, ln1_b,
           router_w, router_b, w_gate_up, b_gate_up, w_down, b_down, ln2_g, ln2_b):
    b, s_len, d = x.shape
    lc = ctx.shape[1]
    assert d == D_MODEL and lc % TM == 0 and s_len % lc == 0 and s_len % GRID_W == 0
    assert b + 1 <= 8
    t = s_len + lc
    n_lat_tiles = s_len // TM
    params = dict(w_in=w_in, mla_q_norm_g=mla_q_norm_g, mla_kv_norm_g=mla_kv_norm_g, mla_w_uq=mla_w_uq,
                  mla_w_ukv=mla_w_ukv, sg_norm_g=sg_norm_g, sg_norm_b=sg_norm_b, sg_w=sg_w, sg_b=sg_b,
                  ret_decay_fwd=ret_decay_fwd, ret_decay_bwd=ret_decay_bwd, w_o=w_o, ln1_g=ln1_g,
                  ln1_b=ln1_b, router_w=router_w, router_b=router_b, w_gate_up=w_gate_up,
                  b_gate_up=b_gate_up, w_down=w_down, b_down=b_down, ln2_g=ln2_g, ln2_b=ln2_b)
    lws = _layer_weights(params)
    ew = expert_weights(w_gate_up, w_down, lws["deint"][0])
    tab = _rotation_tables(s_len, lc)

    c_rows = jnp.concatenate([c, c_ctx[None, :], jnp.zeros((8 - b - 1, d), F32)], axis=0)
    mod = ada_modulation(c_rows, ada_w, ada_b).reshape(DEPTH, 8, N_MOD, d)
    mod_tab = jnp.stack([mod[:, :b], jnp.broadcast_to(mod[:, b:b + 1], (DEPTH, b, N_MOD, d))], axis=2)

    n_streams = N_STREAMS if b % N_STREAMS == 0 else 1
    bs = b // n_streams
    xs = [jnp.concatenate([x[i * bs:(i + 1) * bs], ctx[i * bs:(i + 1) * bs]], axis=1)
          for i in range(n_streams)]
    order = c_rows
    held = None
    for li in range(DEPTH):
        lw = {k: v[li] for k, v in lws.items()}
        mts = [mod_tab[li, si * bs:(si + 1) * bs] for si in range(n_streams)]
        fronts = []
        for si in range(n_streams):
            if held is not None and si == n_streams - 1:
                xs[si] = _finish(*held, after=order)
                held = None
            fr = _front(xs[si], mts[si], tab, lw, s_len, lc, after=order)
            order = fr["cnt"]
            fronts.append(fr)
        ys = []
        for si in range(n_streams):
            y = expert_ffn(fronts[si]["xg"], fronts[si]["block_e"], fronts[si]["n_used"], ew, li, lw,
                           after=order)
            order = y
            ys.append(y)
        for si in range(n_streams):
            args = (fronts[si], ys[si], mts[si], lw, s_len)
            if si == n_streams - 1 and n_streams > 1 and li + 1 < DEPTH:
                held = args
            else:
                xs[si] = _finish(*args, after=order)
                order = xs[si]
    return jnp.concatenate([xi[:, :s_len] for xi in xs], axis=0)


def _front(x_all, mt, tab, lw, s_len, lc, after):
    b, t, d = x_all.shape
    n_lat_tiles = s_len // TM
    q, k, v, sg, retp, rv, a = input_projection(x_all, mt, tab, lw, n_lat_tiles, after)
    mla_lat, mla_ctx = mla_attention(q, k, v, s_len, lc)
    st = retention_scan(a, lw["cd"], s_len // CHUNK)
    x1, h2, idx, gates, rank, cnt = output_projection(
        x_all, mla_lat, mla_ctx, sg, retp, rv, st, mt, lw, n_lat_tiles)
    to_tok = lambda z: z.transpose(2, 0, 1, 3).reshape(TOP_K, b * t)
    dest, block_e, n_used, nb = _route(to_tok(idx), to_tok(rank), cnt[:, 0])
    assert (b * t) % SC_CHUNK == 0
    dest3 = dest.reshape(TOP_K, (b * t) // SC_CHUNK, SC_CHUNK).transpose(1, 0, 2)
    xg = sc_dispatch(h2.reshape(b * t, h2.shape[-1]), dest3, nb * MOE_BM)
    return dict(x1=x1, gates=gates, cnt=cnt, dest3=dest3, xg=xg, block_e=block_e, n_used=n_used)


def _finish(fr, y, mt, lw, s_len, after):
    b, t, d = fr["x1"].shape
    yg = sc_combine_gather(y, fr["dest3"], b * t).reshape(TOP_K, b, t, y.shape[-1])
    gates_tok = fr["gates"].transpose(0, 1, 3, 2).reshape(b, t, TOP_K)
    return combine_deepnorm2(fr["x1"], yg, gates_tok, mt, lw, s_len // TM, after)
```

```python
import functools

import numpy as np
import jax
import jax.numpy as jnp
from jax import lax
from jax.experimental import pallas as pl
from jax.experimental.pallas import tpu as pltpu
from jax.experimental.pallas import tpu_sc as plsc

F32 = jnp.float32
BF16 = jnp.bfloat16
MXU_DT = BF16
PACK_ROWS = True

D_MODEL = 1024
DEPTH = 4
GRID_W = 64
MLA_HEADS = 4
MLA_NOPE = 128
MLA_ROPE = 64
MLA_V = 128
MLA_Q_LORA = 384
MLA_KV_LORA = 256
MLA_QK = MLA_NOPE + MLA_ROPE
MLA_SCALE = MLA_QK ** -0.5
ROPE_BASE = 10000.0
ROPE_AXIS_FREQS = MLA_ROPE // 4
SG_GROUPS = 4
SG_WIDTH = 256
SG_CHUNK = 128
RET_HEADS = 4
RET_QK = 32
RET_V = 64
RET_CHUNK = 128
RET_ROPE_BASE = 10000.0
N_EXPERTS = 32
TOP_K = 4
D_EXPERT = 1024
SWIGLU_LIMIT = 7.0
SWIGLU_ALPHA = 1.702
N_MOD = 6
LN_EPS = 1e-5
RMS_EPS = 1e-6
DEEPNORM_ALPHA = (2 * DEPTH) ** 0.25
MLA_OUT = MLA_HEADS * MLA_V
RET_OUT = RET_HEADS * RET_V

TM = 256
CHUNK = 128
MOE_BM = 512
SC_CHUNK = 48
N_STREAMS = 2
ATT_TQ = 1024
ATT_TK = 2048
VMEM_LIMIT = 48 * 2 ** 20

_O_CQ, _O_CKV, _O_SGU, _O_SGV = 0, 384, 640, 896
_O_RQ, _O_RK, _O_RQS, _O_RKS, _O_RV, _O_RG, _O_KR = 1152, 1280, 1408, 1536, 1664, 1920, 2176
IN_P = 2304
_T_QC, _T_QS, _T_KCS, _T_RQC, _T_RQS, _T_RKC, _T_RKS = 0, 256, 512, 640, 768, 896, 1024
TAB_W = 1152


def _cparams(sem):
    return pltpu.CompilerParams(dimension_semantics=sem, vmem_limit_bytes=VMEM_LIMIT)


def _dot(a, b):
    return jnp.dot(a, b, preferred_element_type=F32)


def _dot_nt(a, b):
    return lax.dot_general(a, b, (((1,), (1,)), ((), ())), preferred_element_type=F32)


def _mx(a):
    return a.astype(MXU_DT)


def _swap16(j):
    return (j // 32) * 32 + ((j % 32) + 16) % 32


_ERF_ALPHA = (-2.72614225801306e-10, 2.77068142495902e-08, -2.10102402082508e-06,
              -5.69250639462346e-05, -7.34990630326855e-04, -2.95459980854025e-03,
              -1.60960333262415e-02)
_ERF_BETA = (-1.45660718464996e-05, -2.13374055278905e-04, -1.68282697438203e-03,
             -7.37332916720468e-03, -1.42647390514189e-02)


def _erf(x):
    x = jnp.clip(x, -4.0, 4.0)
    x2 = x * x
    p = jnp.full_like(x, _ERF_ALPHA[0])
    for c in _ERF_ALPHA[1:]:
        p = p * x2 + c
    q = jnp.full_like(x, _ERF_BETA[0])
    for c in _ERF_BETA[1:]:
        q = q * x2 + c
    return x * p / q


def _gelu(x):
    return 0.5 * x * (1.0 + _erf(x * 0.7071067811865476))


def _sigmoid(x):
    return 1.0 / (1.0 + jnp.exp(-x))


def _ln(x):
    xc = x - jnp.mean(x, axis=-1, keepdims=True)
    return xc * lax.rsqrt(jnp.mean(xc * xc, axis=-1, keepdims=True) + LN_EPS)


def _lane_group(shape, width):
    return lax.broadcasted_iota(jnp.int32, shape, len(shape) - 1) // width


def _ada_kernel(c_ref, w_ref, b_ref, o_ref):
    c = c_ref[...]
    o_ref[0] = _dot(c * _sigmoid(c), w_ref[0]) + b_ref[0]


def ada_modulation(c_rows, ada_w, ada_b):
    nl, d, n = ada_w.shape
    tn = 1536
    return pl.pallas_call(
        _ada_kernel,
        grid=(nl, n // tn),
        in_specs=[pl.BlockSpec((8, d), lambda l, j: (0, 0)),
                  pl.BlockSpec((1, d, tn), lambda l, j: (l, 0, j)),
                  pl.BlockSpec((1, 1, tn), lambda l, j: (l, 0, j))],
        out_specs=pl.BlockSpec((1, 8, tn), lambda l, j: (l, 0, j)),
        out_shape=jax.ShapeDtypeStruct((nl, 8, n), F32),
        compiler_params=_cparams(("arbitrary", "arbitrary")),
        name="ada_modulation",
    )(c_rows, ada_w, ada_b.reshape(nl, 1, n))


def _inproj_kernel(x_ref, mod_ref, tab_ref, w_in_ref, qg_ref, kvg_ref, w_uq_ref, w_ukv_ref,
                   sgg_ref, sgb_ref, sgw_ref, sgbias_ref, kdec_ref, bd_ref,
                   q_ref, k_ref, v_ref, sg_ref, retp_ref, rv_ref, a_ref):
    x = x_ref[0]
    mod = mod_ref[0, 0]
    h = x * (1.0 + mod[1:2]) + mod[0:1]
    p = _dot(_mx(h), w_in_ref[...])
    tab = tab_ref[...]

    cq = p[:, _O_CQ:_O_CQ + MLA_Q_LORA]
    cq = cq * lax.rsqrt(jnp.mean(cq * cq, axis=-1, keepdims=True) + RMS_EPS) * qg_ref[...]
    qa = _dot(_mx(cq), w_uq_ref[...])
    rot = (qa[:, 512:768] * tab[:, _T_QC:_T_QC + 256]
           + qa[:, 768:1024] * tab[:, _T_QS:_T_QS + 256])
    for hh in range(MLA_HEADS):
        q_ref[0, hh, :, 0:128] = (qa[:, 128 * hh:128 * hh + 128] * MLA_SCALE).astype(q_ref.dtype)
        g = hh // 2
        q_ref[0, hh, :, 128:256] = rot[:, 128 * g:128 * g + 128].astype(q_ref.dtype)

    ckv = p[:, _O_CKV:_O_CKV + MLA_KV_LORA]
    ckv = ckv * lax.rsqrt(jnp.mean(ckv * ckv, axis=-1, keepdims=True) + RMS_EPS) * kvg_ref[...]
    kv = _dot(_mx(ckv), w_ukv_ref[...])
    t = p[:, _O_KR:_O_KR + 128] * tab[:, _T_KCS:_T_KCS + 128]
    u = t + pltpu.roll(t, 64, axis=1)
    low = lax.broadcasted_iota(jnp.int32, u.shape, 1) < 64
    kx = (jnp.where(low, u, 0.0), jnp.where(low, 0.0, u))
    ones_col = jnp.where(lax.broadcasted_iota(jnp.int32, u.shape, 1) == 0, 1.0, 0.0).astype(v_ref.dtype)
    for hh in range(MLA_HEADS):
        k_ref[0, hh, :, 0:128] = kv[:, 256 * hh:256 * hh + 128].astype(k_ref.dtype)
        k_ref[0, hh, :, 128:256] = kx[hh % 2].astype(k_ref.dtype)
        v_ref[0, hh, :, 0:128] = kv[:, 256 * hh + 128:256 * hh + 256].astype(v_ref.dtype)
        v_ref[0, hh, :, 128:256] = ones_col

    gu = _gelu(p[:, _O_SGU:_O_SGU + SG_WIDTH])
    gv = _ln(_gelu(p[:, _O_SGV:_O_SGV + SG_WIDTH])) * sgg_ref[...] + sgb_ref[...]
    gvm = _mx(gv)
    grp = _lane_group((CHUNK, SG_WIDTH), SG_WIDTH // SG_GROUPS)
    for c in range(TM // CHUNK):
        rows = slice(c * CHUNK, (c + 1) * CHUNK)
        res = _dot(sgw_ref[...], gvm[rows])
        mixed = sgbias_ref[...]
        for g in range(SG_GROUPS):
            mixed = mixed + jnp.where(grp == g, res[g * CHUNK:(g + 1) * CHUNK], 0.0)
        sg_ref[0, rows, :] = (gu[rows] * mixed).astype(sg_ref.dtype)

    rq = (p[:, _O_RQ:_O_RQ + 128] * tab[:, _T_RQC:_T_RQC + 128]
          + p[:, _O_RQS:_O_RQS + 128] * tab[:, _T_RQS:_T_RQS + 128])
    rk = (p[:, _O_RK:_O_RK + 128] * tab[:, _T_RKC:_T_RKC + 128]
          + p[:, _O_RKS:_O_RKS + 128] * tab[:, _T_RKS:_T_RKS + 128])
    rv = p[:, _O_RV:_O_RV + RET_OUT]
    retp_ref[0, :, 0:128] = rq
    retp_ref[0, :, 128:256] = rk
    retp_ref[0, :, 256:512] = p[:, _O_RG:_O_RG + RET_OUT]
    rvm = _mx(rv)
    rv_ref[0] = rvm.astype(rv_ref.dtype)
    bd = bd_ref[...]
    for c in range(TM // CHUNK):
        rows = slice(c * CHUNK, (c + 1) * CHUNK)
        for d in range(2):
            kd_t = _mx((rk[rows] * kdec_ref[d]).T)
            af = _dot(kd_t, rvm[rows]) * bd
            a_ref[0, c, d] = (af[0:32] + af[32:64]) + (af[64:96] + af[96:128])


def _ordered_after(kernel_fn, pos):
    def wrapped(*refs):
        return kernel_fn(*refs[:pos], *refs[pos + 1:])
    return wrapped


_ORDER_SPEC = pl.BlockSpec(memory_space=pl.ANY)


def input_projection(x_all, mod_tab, tab, lw, n_lat_tiles, after):
    b, t, d = x_all.shape
    nt = t // TM
    nc = t // CHUNK
    cpt = TM // CHUNK
    const2 = lambda bi, j: (0, 0)
    const3 = lambda bi, j: (0, 0, 0)
    out_shape = (
        jax.ShapeDtypeStruct((b, MLA_HEADS, t, 256), MXU_DT),
        jax.ShapeDtypeStruct((b, MLA_HEADS, t, 256), MXU_DT),
        jax.ShapeDtypeStruct((b, MLA_HEADS, t, 256), MXU_DT),
        jax.ShapeDtypeStruct((b, t, SG_WIDTH), MXU_DT),
        jax.ShapeDtypeStruct((b, t, 512), F32),
        jax.ShapeDtypeStruct((b, t, RET_OUT), MXU_DT),
        jax.ShapeDtypeStruct((b, nc, 2, RET_QK, RET_OUT), F32),
    )
    head_spec = lambda w: pl.BlockSpec((1, MLA_HEADS, TM, w), lambda bi, j: (bi, 0, j, 0))
    tok_spec = lambda w: pl.BlockSpec((1, TM, w), lambda bi, j: (bi, j, 0))
    return pl.pallas_call(
        _ordered_after(_inproj_kernel, 14),
        grid=(b, nt),
        in_specs=[
            tok_spec(d),
            pl.BlockSpec((1, 1, N_MOD, d), lambda bi, j: (bi, j // n_lat_tiles, 0, 0)),
            pl.BlockSpec((TM, TAB_W), lambda bi, j: (j, 0)),
            pl.BlockSpec((d, IN_P), const2),
            pl.BlockSpec((1, MLA_Q_LORA), const2),
            pl.BlockSpec((1, MLA_KV_LORA), const2),
            pl.BlockSpec((MLA_Q_LORA, 1024), const2),
            pl.BlockSpec((MLA_KV_LORA, 1024), const2),
            pl.BlockSpec((1, SG_WIDTH), const2),
            pl.BlockSpec((1, SG_WIDTH), const2),
            pl.BlockSpec((SG_GROUPS * CHUNK, CHUNK), const2),
            pl.BlockSpec((CHUNK, SG_WIDTH), const2),
            pl.BlockSpec((2, CHUNK, 128), const3),
            pl.BlockSpec((128, RET_OUT), const2),
            _ORDER_SPEC,
        ],
        out_specs=(head_spec(256), head_spec(256), head_spec(256), tok_spec(SG_WIDTH),
                   tok_spec(512), tok_spec(RET_OUT),
                   pl.BlockSpec((1, cpt, 2, RET_QK, RET_OUT), lambda bi, j: (bi, j, 0, 0, 0))),
        out_shape=out_shape,
        compiler_params=_cparams(("parallel", "parallel")),
        name="input_projection",
    )(x_all, mod_tab, tab, lw["w_in"], lw["q_g"], lw["kv_g"], lw["w_uq"], lw["w_ukv"],
      lw["sg_g"], lw["sg_b"], lw["sg_w"], lw["sg_bias"], lw["kdec"], lw["bd"], after)


def _attn_kernel(q_ref, k_ref, v_ref, o_ref, *, n_main, tk, tail):
    q = q_ref[0, 0]
    tq = q.shape[0]
    chunks = [(i * tk, tk) for i in range(n_main)] + ([(n_main * tk, tail)] if tail else [])

    def scores(ci):
        start, size = chunks[ci]
        return _dot_nt(q, k_ref[0, 0, start:start + size, :])

    m = jnp.full((tq, 1), -1e30, F32)
    acc = jnp.zeros((tq, 256), F32)
    s_next = scores(0)
    for ci, (start, size) in enumerate(chunks):
        s = s_next
        if ci + 1 < len(chunks):
            s_next = scores(ci + 1)
        m_new = jnp.maximum(m, jnp.max(s, axis=-1, keepdims=True))
        p = jnp.exp(s - m_new)
        acc = jnp.exp(m - m_new) * acc + _dot(_mx(p), v_ref[0, 0, start:start + size, :])
        m = m_new
    o_ref[0] = (acc[:, 0:MLA_V] / acc[:, MLA_V:MLA_V + 1]).astype(o_ref.dtype)


def mla_attention(q, k, v, s_len, lc):
    b, hn, t, _ = q.shape
    tq = min(ATT_TQ, s_len)
    tk = min(ATT_TK, s_len)
    kv_full = pl.BlockSpec((1, 1, t, 256), lambda bi, hi, i: (bi, hi, 0, 0))
    out_lat = pl.pallas_call(
        functools.partial(_attn_kernel, n_main=s_len // tk, tk=tk, tail=lc),
        grid=(b, hn, s_len // tq),
        in_specs=[pl.BlockSpec((1, 1, tq, 256), lambda bi, hi, i: (bi, hi, i, 0)), kv_full, kv_full],
        out_specs=pl.BlockSpec((1, tq, MLA_V), lambda bi, hi, i: (bi, i, hi)),
        out_shape=jax.ShapeDtypeStruct((b, s_len, MLA_OUT), MXU_DT),
        compiler_params=_cparams(("parallel", "parallel", "arbitrary")),
        name="mla_attention_latent",
    )(q, k, v)
    cblk = s_len // lc
    ctx_spec = pl.BlockSpec((1, 1, lc, 256), lambda bi, hi: (bi, hi, cblk, 0))
    out_ctx = pl.pallas_call(
        functools.partial(_attn_kernel, n_main=0, tk=tk, tail=lc),
        grid=(b, hn),
        in_specs=[ctx_spec, ctx_spec, ctx_spec],
        out_specs=pl.BlockSpec((1, lc, MLA_V), lambda bi, hi: (bi, 0, hi)),
        out_shape=jax.ShapeDtypeStruct((b, lc, MLA_OUT), MXU_DT),
        compiler_params=_cparams(("parallel", "parallel")),
        name="mla_attention_context",
    )(q, k, v)
    return out_lat, out_ctx


def _ret_scan_kernel(a_ref, cd_ref, s_ref, *, n_lat_chunks):
    nc = a_ref.shape[1]
    ncc = nc - n_lat_chunks
    cd_f, cd_b = cd_ref[0], cd_ref[1]

    def body(n, carry):
        sf, sb = carry
        cf = jnp.where(n < ncc, n_lat_chunks + n, n - ncc)
        cb = jnp.where(n < ncc, nc - 1 - n, n_lat_chunks - 1 - (n - ncc))
        s_ref[0, cf, 0] = sf
        s_ref[0, cb, 1] = sb
        return sf * cd_f + a_ref[0, cf, 0], sb * cd_b + a_ref[0, cb, 1]

    zero = jnp.zeros((RET_QK, RET_OUT), F32)
    lax.fori_loop(0, nc, body, (zero, zero))


def retention_scan(a, cd, n_lat_chunks):
    b, nc = a.shape[:2]
    blk = pl.BlockSpec((1, nc, 2, RET_QK, RET_OUT), lambda bi: (bi, 0, 0, 0, 0))
    return pl.pallas_call(
        functools.partial(_ret_scan_kernel, n_lat_chunks=n_lat_chunks),
        grid=(b,),
        in_specs=[blk, pl.BlockSpec((2, 1, RET_OUT), lambda bi: (0, 0, 0))],
        out_specs=blk,
        out_shape=jax.ShapeDtypeStruct(a.shape, F32),
        compiler_params=_cparams(("parallel",)),
        name="retention_scan",
    )(a, cd)


def _split_dot(x, ones2):
    hi = x.astype(BF16)
    lo = (x - hi.astype(F32)).astype(BF16)
    return _dot(jnp.concatenate([hi, lo], axis=1), ones2)


def _outproj_kernel(x_ref, mlal_ref, mlac_ref, sg_ref, retp_ref, rv_ref, st_ref, mod_ref,
                    m_ref, qdec_ref, bd_ref, seg_ref, w_o_ref, lng_ref, lnb_ref, rw_ref, rb_ref, tri_ref,
                    x1_ref, h2_ref, idx_ref, gate_ref, rank_ref, cnt_ref, carry_ref, *, n_lat_tiles):
    @pl.when(pl.program_id(0) == 0)
    def _():
        carry_ref[...] = jnp.zeros_like(carry_ref)

    nbat = x_ref.shape[0]
    units = [(bb, c) for bb in range(nbat) for c in range(TM // CHUNK)]
    rows = lambda c: slice(c * CHUNK, (c + 1) * CHUNK)
    g32 = _lane_group((CHUNK, 128), RET_QK)
    g64 = _lane_group((CHUNK, RET_OUT), RET_V)
    bd = bd_ref[...]
    seg2 = jnp.concatenate([seg_ref[...], seg_ref[...]], axis=0)
    m4 = jnp.concatenate([m_ref[hh] for hh in range(RET_HEADS)], axis=0)

    rq, s4 = {}, {}
    for u in units:
        bb, c = u
        rq[u] = retp_ref[bb, rows(c), 0:128]
        q4 = jnp.concatenate([jnp.where(g32 == hh, rq[u], 0.0) for hh in range(RET_HEADS)], axis=0)
        s4[u] = _dot_nt(_mx(q4), _mx(retp_ref[bb, rows(c), 128:256]))
    o = {}
    for u in units:
        bb, c = u
        r = _dot(_mx(s4[u] * m4), rv_ref[bb, rows(c), :])
        qd = jnp.concatenate([rq[u] * qdec_ref[0], rq[u] * qdec_ref[1]], axis=1)
        st2 = jnp.concatenate([jnp.concatenate([st_ref[bb, c, dd]] * RET_HEADS, axis=0) * bd
                               for dd in range(2)], axis=0)
        acc = _dot(_mx(qd), _mx(st2))
        for hh in range(RET_HEADS):
            acc = acc + jnp.where(g64 == hh, r[hh * CHUNK:(hh + 1) * CHUNK], 0.0)
        o[u] = acc
    oc = {}
    for u in units:
        oc[u] = o[u] - _split_dot(o[u], seg2) * (1.0 / RET_V)
    ret = {}
    for u in units:
        bb, c = u
        var = _split_dot(oc[u] * oc[u], seg2) * (1.0 / RET_V)
        rg = retp_ref[bb, rows(c), 256:512]
        ret[u] = _mx(oc[u] * lax.rsqrt(var + LN_EPS) * (rg * _sigmoid(rg)))

    is_ctx = pl.program_id(0) >= n_lat_tiles
    ys = []
    for bb in range(nbat):
        mla = jnp.where(is_ctx, mlac_ref[bb], mlal_ref[bb])
        cat = jnp.concatenate(
            [mla, sg_ref[bb], jnp.concatenate([ret[(bb, c)] for c in range(TM // CHUNK)], axis=0)], axis=1)
        ys.append(_dot(cat, w_o_ref[...]))
    logits = []
    for bb in range(nbat):
        mod = mod_ref[bb, 0]
        x1 = _ln(DEEPNORM_ALPHA * x_ref[bb] + mod[2:3] * ys[bb]) * lng_ref[...] + lnb_ref[...]
        x1_ref[bb] = x1
        h2 = x1 * (1.0 + mod[4:5]) + mod[3:4]
        h2_ref[bb] = _pack_bf16_pairs(h2) if PACK_ROWS else h2
        h2_hi = h2.astype(BF16)
        h2_lo = (h2 - h2_hi.astype(F32)).astype(BF16)
        r2 = _dot_nt(rw_ref[...], jnp.concatenate([h2_hi, h2_lo], axis=0))
        logits.append(r2[0:N_EXPERTS, 0:TM] + r2[N_EXPERTS:, 0:TM] + r2[0:N_EXPERTS, TM:] + rb_ref[...])

    e_iota = lax.broadcasted_iota(jnp.int32, logits[0].shape, 0).astype(F32)
    idxs_all, onehots, prefixes = [], [], []
    for bb in range(nbat):
        work = logits[bb]
        vals, idxs = [], []
        for _ in range(TOP_K):
            mval = jnp.max(work, axis=0, keepdims=True)
            midx = jnp.min(jnp.where(work == mval, e_iota, float(N_EXPERTS)), axis=0, keepdims=True)
            vals.append(mval)
            idxs.append(midx)
            work = jnp.where(e_iota == midx, -jnp.inf, work)
        ex = [jnp.exp(vv - vals[0]) for vv in vals]
        den = ex[0] + ex[1] + ex[2] + ex[3]
        onehot = jnp.zeros_like(work)
        for kk in range(TOP_K):
            gate_ref[bb, 0, kk:kk + 1, :] = ex[kk] / den
            idx_ref[bb, 0, kk:kk + 1, :] = idxs[kk].astype(jnp.int32)
            onehot = onehot + jnp.where(e_iota == idxs[kk], 1.0, 0.0)
        idxs_all.append(idxs)
        onehots.append(onehot)
        prefixes.append(_dot(onehot.astype(BF16), tri_ref[...]))
    count = carry_ref[:, 0:1]
    for bb in range(nbat):
        base = count + prefixes[bb]
        for kk in range(TOP_K):
            rk_k = jnp.sum(jnp.where(e_iota == idxs_all[bb][kk], base, 0.0), axis=0, keepdims=True)
            rank_ref[bb, 0, kk:kk + 1, :] = rk_k.astype(jnp.int32)
        count = count + jnp.sum(onehots[bb], axis=1, keepdims=True)
    carry_ref[...] = jnp.broadcast_to(count, carry_ref.shape)
    cnt_ref[...] = carry_ref[...].astype(jnp.int32)


def output_projection(x_all, mla_lat, mla_ctx, sg, retp, rv, st, mod_tab, lw, n_lat_tiles):
    b, t, d = x_all.shape
    nt = t // TM
    cpt = TM // CHUNK
    const2 = lambda j: (0, 0)
    const3 = lambda j: (0, 0, 0)
    tok_spec = lambda w: pl.BlockSpec((b, TM, w), lambda j: (0, j, 0))
    route_spec = pl.BlockSpec((b, 1, TOP_K, TM), lambda j: (0, j, 0, 0))
    route_shape = lambda dt: jax.ShapeDtypeStruct((b, nt, TOP_K, TM), dt)
    h2w = d // 2 if PACK_ROWS else d
    return pl.pallas_call(
        functools.partial(_outproj_kernel, n_lat_tiles=n_lat_tiles),
        grid=(nt,),
        in_specs=[
            tok_spec(d),
            pl.BlockSpec((b, TM, MLA_OUT), lambda j: (0, jnp.minimum(j, n_lat_tiles - 1), 0)),
            pl.BlockSpec((b, TM, MLA_OUT), lambda j: (0, jnp.maximum(j - n_lat_tiles, 0), 0)),
            tok_spec(SG_WIDTH), tok_spec(512), tok_spec(RET_OUT),
            pl.BlockSpec((b, cpt, 2, RET_QK, RET_OUT), lambda j: (0, j, 0, 0, 0)),
            pl.BlockSpec((b, 1, N_MOD, d), lambda j: (0, j // n_lat_tiles, 0, 0)),
            pl.BlockSpec((RET_HEADS, CHUNK, CHUNK), const3),
            pl.BlockSpec((2, CHUNK, 128), const3),
            pl.BlockSpec((128, RET_OUT), const2),
            pl.BlockSpec((RET_OUT, RET_OUT), const2),
            pl.BlockSpec((d, d), const2),
            pl.BlockSpec((1, d), const2),
            pl.BlockSpec((1, d), const2),
            pl.BlockSpec((2 * N_EXPERTS, d), const2),
            pl.BlockSpec((N_EXPERTS, 1), const2),
            pl.BlockSpec((TM, TM), const2),
        ],
        out_specs=(tok_spec(d), tok_spec(h2w), route_spec, route_spec, route_spec,
                   pl.BlockSpec((N_EXPERTS, 128), const2)),
        out_shape=(jax.ShapeDtypeStruct((b, t, d), F32),
                   jax.ShapeDtypeStruct((b, t, h2w), jnp.uint32 if PACK_ROWS else F32),
                   route_shape(jnp.int32), route_shape(F32), route_shape(jnp.int32),
                   jax.ShapeDtypeStruct((N_EXPERTS, 128), jnp.int32)),
        scratch_shapes=[pltpu.VMEM((N_EXPERTS, 128), F32)],
        compiler_params=_cparams(("arbitrary",)),
        name="output_projection",
    )(x_all, mla_lat, mla_ctx, sg, retp, rv, st, mod_tab, lw["ret_m"], lw["qdec"], lw["bd"], lw["seg"],
      lw["w_o"], lw["ln1_g"], lw["ln1_b"], lw["router_w"], lw["router_b"], lw["tri"])


_DEINT = 256


def _expert_weights_kernel(wgu_ref, wd_ref, perm_ref, wg_ref, wl_ref, wdo_ref):
    half = _DEINT // 2
    for c in range(2 * D_EXPERT // _DEINT):
        r = _dot(_mx(wgu_ref[0, 0, :, _DEINT * c:_DEINT * (c + 1)]), perm_ref[...])
        wg_ref[0, 0, :, half * c:half * (c + 1)] = r[:, :half].astype(wg_ref.dtype)
        wl_ref[0, 0, :, half * c:half * (c + 1)] = r[:, half:].astype(wl_ref.dtype)
    wdo_ref[0, 0] = wd_ref[0, 0].astype(wdo_ref.dtype)


def expert_weights(w_gate_up, w_down, deint):
    nl, ne, d, de2 = w_gate_up.shape
    de = de2 // 2
    blk = lambda r, c: pl.BlockSpec((1, 1, r, c), lambda l, e: (l, e, 0, 0))
    return pl.pallas_call(
        _expert_weights_kernel,
        grid=(nl, ne),
        in_specs=[blk(d, de2), blk(de, d), pl.BlockSpec((_DEINT, _DEINT), lambda l, e: (0, 0))],
        out_specs=(blk(d, de), blk(d, de), blk(de, d)),
        out_shape=(jax.ShapeDtypeStruct((nl, ne, d, de), MXU_DT), jax.ShapeDtypeStruct((nl, ne, d, de), MXU_DT),
                   jax.ShapeDtypeStruct((nl, ne, de, d), MXU_DT)),
        compiler_params=_cparams(("parallel", "parallel")),
        name="expert_weights",
    )(w_gate_up, w_down, deint)


def _pack_bf16_pairs(v):
    bits = lax.bitcast_convert_type(v.astype(BF16).astype(F32), jnp.uint32)
    half = bits.shape[1] // 2
    return bits[:, :half] | (bits[:, half:] >> 16)


def _unpack_bf16_pairs(w):
    return (lax.bitcast_convert_type(w & jnp.uint32(0xFFFF0000), F32),
            lax.bitcast_convert_type(w << 16, F32))


def _expert_kernel(be_ref, nu_ref, x_ref, wg_s, wl_s, bg_ref, bl_ref, wd_s, bd_ref, y_ref):
    del be_ref
    active = pl.program_id(0) < nu_ref[0]

    @pl.when(active)
    def _():
        if PACK_ROWS:
            hi, lo = _unpack_bf16_pairs(x_ref[...])
            xb = jnp.concatenate([hi.astype(BF16), lo.astype(BF16)], axis=1)
        else:
            xb = x_ref[...]
        glu = jnp.minimum(_dot(xb, wg_s[0, 0]) + bg_ref[0], SWIGLU_LIMIT)
        lin = jnp.clip(_dot(xb, wl_s[0, 0]) + bl_ref[0], -SWIGLU_LIMIT, SWIGLU_LIMIT)
        act = glu * _sigmoid(SWIGLU_ALPHA * glu) * (lin + 1.0)
        y = _dot(_mx(act), wd_s[0, 0]) + bd_ref[0]
        y_ref[...] = _pack_bf16_pairs(y) if PACK_ROWS else y

    @pl.when(jnp.logical_not(active))
    def _():
        y_ref[...] = jnp.zeros_like(y_ref)


def expert_ffn(xg, block_e, n_used, ew, li, lw, after):
    cap, xw = xg.shape
    d = D_MODEL
    nb = cap // MOE_BM
    de = D_EXPERT
    xmap = lambda i, be, nu: (jnp.minimum(i, nu[0] - 1), 0)
    wmap = lambda i, be, nu: (be[i], 0, 0)
    lmap = lambda i, be, nu: (li, be[i], 0, 0)
    grid_spec = pltpu.PrefetchScalarGridSpec(
        num_scalar_prefetch=2,
        grid=(nb,),
        in_specs=[pl.BlockSpec((MOE_BM, xw), xmap),
                  pl.BlockSpec((1, 1, d, de), lmap),
                  pl.BlockSpec((1, 1, d, de), lmap),
                  pl.BlockSpec((1, 1, de), wmap), pl.BlockSpec((1, 1, de), wmap),
                  pl.BlockSpec((1, 1, de, d), lmap), pl.BlockSpec((1, 1, d), wmap),
                  _ORDER_SPEC],
        out_specs=pl.BlockSpec((MOE_BM, xw), lambda i, be, nu: (i, 0)),
    )
    return pl.pallas_call(
        _ordered_after(_expert_kernel, 9),
        grid_spec=grid_spec,
        out_shape=jax.ShapeDtypeStruct((cap, xw), xg.dtype),
        compiler_params=_cparams(("arbitrary",)),
        name="expert_ffn",
    )(block_e, n_used, xg, ew[0], ew[1], lw["b_glu"], lw["b_lin"], ew[2], lw["b_down"], after)


def _combine_ln2_kernel(x_ref, y_ref, gate_ref, mod_ref, g_ref, b_ref, o_ref):
    for bb in range(x_ref.shape[0]):
        gates = gate_ref[bb]
        if PACK_ROWS:
            f_hi, f_lo = 0.0, 0.0
            for kk in range(TOP_K):
                hi, lo = _unpack_bf16_pairs(y_ref[kk, bb])
                f_hi = f_hi + gates[:, kk:kk + 1] * hi
                f_lo = f_lo + gates[:, kk:kk + 1] * lo
            f = jnp.concatenate([f_hi, f_lo], axis=1)
        else:
            f = gates[:, 0:1] * y_ref[0, bb]
            for kk in range(1, TOP_K):
                f = f + gates[:, kk:kk + 1] * y_ref[kk, bb]
        mod = mod_ref[bb, 0]
        o_ref[bb] = _ln(DEEPNORM_ALPHA * x_ref[bb] + mod[5:6] * f) * g_ref[...] + b_ref[...]


def combine_deepnorm2(x1, yg, gates, mod_tab, lw, n_lat_tiles, after):
    b, t, d = x1.shape
    tok = pl.BlockSpec((b, TM, d), lambda j: (0, j, 0))
    vec = pl.BlockSpec((1, d), lambda j: (0, 0))
    return pl.pallas_call(
        _ordered_after(_combine_ln2_kernel, 6),
        grid=(t // TM,),
        in_specs=[tok,
                  pl.BlockSpec((TOP_K, b, TM, yg.shape[-1]), lambda j: (0, 0, j, 0)),
                  pl.BlockSpec((b, TM, TOP_K), lambda j: (0, j, 0)),
                  pl.BlockSpec((b, 1, N_MOD, d), lambda j: (0, j // n_lat_tiles, 0, 0)), vec, vec,
                  _ORDER_SPEC],
        out_specs=tok,
        out_shape=jax.ShapeDtypeStruct((b, t, d), F32),
        compiler_params=_cparams(("parallel",)),
        name="combine_deepnorm2",
    )(x1, yg, gates, mod_tab, lw["ln2_g"], lw["ln2_b"], after)


def _rotation_tables(s_len, lc):
    rows = s_len // GRID_W
    row = jnp.broadcast_to(jnp.arange(rows, dtype=F32)[:, None], (rows, GRID_W)).reshape(-1)
    col = jnp.broadcast_to(jnp.arange(GRID_W, dtype=F32)[None, :], (rows, GRID_W)).reshape(-1)
    inv = ROPE_BASE ** (-jnp.arange(ROPE_AXIS_FREQS, dtype=F32) / ROPE_AXIS_FREQS)
    ar, ac = row[:, None] * inv, col[:, None] * inv
    c64 = jnp.concatenate([jnp.cos(ar), jnp.cos(ar), jnp.cos(ac), jnp.cos(ac)], axis=1)
    s64 = jnp.concatenate([-jnp.sin(ar), jnp.sin(ar), -jnp.sin(ac), jnp.sin(ac)], axis=1)
    c64 = jnp.concatenate([c64, jnp.ones((lc, 64), F32)], axis=0)
    s64 = jnp.concatenate([s64, jnp.zeros((lc, 64), F32)], axis=0)
    half = RET_QK // 2
    pos = jnp.concatenate([lc + jnp.arange(s_len, dtype=F32), jnp.arange(lc, dtype=F32)])
    inv_r = 1.0 / (RET_ROPE_BASE ** jnp.linspace(0.0, 1.0, half, dtype=F32))
    ang = pos[:, None] * inv_r
    rc = jnp.tile(jnp.concatenate([jnp.cos(ang), jnp.cos(ang)], axis=1), (1, RET_HEADS))
    rs = jnp.tile(jnp.concatenate([-jnp.sin(ang), jnp.sin(ang)], axis=1), (1, RET_HEADS))
    qs = RET_QK ** -0.5
    return jnp.concatenate([
        jnp.tile(c64, (1, MLA_HEADS)) * MLA_SCALE, jnp.tile(s64, (1, MLA_HEADS)) * MLA_SCALE,
        c64, s64, rc * qs, rs * qs, rc, rs], axis=1)


def _in_perm():
    a = np.arange
    return np.concatenate([
        a(0, 640), a(704, 1216), a(1216, 1344), a(1344, 1472),
        1216 + _swap16(a(128)), 1344 + _swap16(a(128)), a(1472, 1984),
        640 + a(64), 640 + _swap16(a(64))])


def _uq_perm():
    a = np.arange
    nope = [h * MLA_QK + a(MLA_NOPE) for h in range(MLA_HEADS)]
    rope = [h * MLA_QK + MLA_NOPE + a(MLA_ROPE) for h in range(MLA_HEADS)]
    part = [h * MLA_QK + MLA_NOPE + _swap16(a(MLA_ROPE)) for h in range(MLA_HEADS)]
    return np.concatenate(nope + rope + part)


def _layer_weights(p):
    nl = p["w_in"].shape[0]
    lgf = jax.nn.log_sigmoid(p["ret_decay_fwd"].astype(F32))
    lgb = jax.nn.log_sigmoid(p["ret_decay_bwd"].astype(F32))
    h128 = np.arange(128) // RET_QK
    h256 = np.arange(RET_OUT) // RET_V
    a = jnp.arange(CHUNK, dtype=F32)[None, :, None]
    lf, lb = lgf[:, h128][:, None, :], lgb[:, h128][:, None, :]
    kdec = jnp.stack([jnp.exp(lf * (CHUNK - 1.0 - a)), jnp.exp(lb * a)], axis=1)
    qdec = jnp.stack([jnp.exp(lf * (a + 1.0)), jnp.exp(lb * (CHUNK - a))], axis=1)
    i = jnp.arange(CHUNK, dtype=F32)[:, None]
    j = jnp.arange(CHUNK, dtype=F32)[None, :]
    dif = (i - j)[None, None]
    ret_m = jnp.where(dif >= 0, jnp.exp(lgf[:, :, None, None] * jnp.maximum(dif, 0.0)),
                      jnp.exp(lgb[:, :, None, None] * jnp.maximum(-dif, 0.0)))
    cd = jnp.stack([jnp.exp(lgf[:, h256] * CHUNK), jnp.exp(lgb[:, h256] * CHUNK)], axis=1)[:, :, None, :]
    bd = (h128[:, None] == h256[None, :]).astype(np.float32)
    seg = (h256[:, None] == h256[None, :]).astype(np.float32)
    tri = (np.arange(TM)[:, None] < np.arange(TM)[None, :]).astype(np.float32)
    jj = np.arange(_DEINT // 2)
    deint = np.zeros((_DEINT, _DEINT), np.float32)
    deint[2 * jj, jj] = 1.0
    deint[2 * jj + 1, _DEINT // 2 + jj] = 1.0
    rw_t = jnp.swapaxes(p["router_w"], 1, 2)
    rw_hi = rw_t.astype(BF16)
    rw_lo = (rw_t - rw_hi.astype(F32)).astype(BF16)
    sg_bias = jnp.repeat(jnp.swapaxes(p["sg_b"], 1, 2), SG_WIDTH // SG_GROUPS, axis=2)
    bgu = p["b_gate_up"]
    return {
        "w_in": p["w_in"][:, :, _in_perm()].astype(MXU_DT),
        "q_g": p["mla_q_norm_g"][:, None, :], "kv_g": p["mla_kv_norm_g"][:, None, :],
        "w_uq": p["mla_w_uq"][:, :, _uq_perm()].astype(MXU_DT),
        "w_ukv": p["mla_w_ukv"].astype(MXU_DT),
        "sg_g": p["sg_norm_g"][:, None, :], "sg_b": p["sg_norm_b"][:, None, :],
        "sg_w": p["sg_w"].reshape(nl, SG_GROUPS * CHUNK, CHUNK).astype(MXU_DT),
        "sg_bias": sg_bias,
        "kdec": kdec, "qdec": qdec, "ret_m": ret_m, "cd": cd,
        "bd": jnp.broadcast_to(jnp.asarray(bd), (nl,) + bd.shape),
        "seg": jnp.broadcast_to(jnp.asarray(seg, BF16), (nl,) + seg.shape),
        "tri": jnp.broadcast_to(jnp.asarray(tri, BF16), (nl,) + tri.shape),
        "w_o": p["w_o"].astype(MXU_DT),
        "ln1_g": p["ln1_g"][:, None, :], "ln1_b": p["ln1_b"][:, None, :],
        "ln2_g": p["ln2_g"][:, None, :], "ln2_b": p["ln2_b"][:, None, :],
        "router_w": jnp.concatenate([rw_hi, rw_lo], axis=1),
        "router_b": p["router_b"][:, :, None],
        "b_glu": bgu[:, :, None, 0::2], "b_lin": bgu[:, :, None, 1::2],
        "b_down": p["b_down"][:, :, None, :],
        "deint": jnp.broadcast_to(jnp.asarray(deint, MXU_DT), (nl,) + deint.shape),
    }


def _route(idx, rank, counts):
    n_assign = idx.shape[1] * TOP_K
    nb = -(-(n_assign + N_EXPERTS * (MOE_BM - 1)) // MOE_BM)
    padded = (counts + MOE_BM - 1) // MOE_BM * MOE_BM
    pad_end = jnp.cumsum(padded)
    pad_start = pad_end - padded
    experts = jnp.arange(N_EXPERTS, dtype=jnp.int32)
    dest = rank + jnp.sum(jnp.where(idx[..., None] == experts, pad_start, 0), axis=-1)
    blk_start = jnp.arange(nb, dtype=jnp.int32) * MOE_BM
    block_e = jnp.minimum(jnp.sum((pad_end[None, :] <= blk_start[:, None]).astype(jnp.int32), axis=1),
                          N_EXPERTS - 1)
    n_used = (pad_end[-1] // MOE_BM).astype(jnp.int32).reshape(1)
    return dest.astype(jnp.int32), block_e, n_used, nb


def _sc_workers():
    info = plsc.get_sparse_core_info()
    return info.num_cores, info.num_cores * info.num_subcores


def sc_dispatch(rows, dest3, cap, after):
    n, w = rows.shape
    nch, kk, c = dest3.shape
    ncores, nw = _sc_workers()
    per_w = nch // nw
    mesh = plsc.VectorSubcoreMesh(core_axis_name="c", subcore_axis_name="s")

    @functools.partial(
        pl.kernel, mesh=mesh, out_type=jax.ShapeDtypeStruct((cap, w), rows.dtype),
        scratch_types=[pltpu.VMEM((kk, c), jnp.int32), pltpu.VMEM((c, w), rows.dtype)])
    def scatter_rows(h_hbm, d_hbm, after_hbm, o_hbm, idx_v, rows_v):
        del after_hbm
        wid = lax.axis_index("s") * ncores + lax.axis_index("c")

        @pl.loop(0, per_w)
        def _(j):
            ch = wid * per_w + j
            pltpu.sync_copy(d_hbm.at[ch], idx_v)
            pltpu.sync_copy(h_hbm.at[pl.ds(ch * c, c)], rows_v)
            for q in range(kk):
                pltpu.sync_copy(rows_v, o_hbm.at[idx_v.at[q]])

    return scatter_rows(rows, dest3, after)


def sc_combine_gather(y, dest3, n, after):
    cap, d = y.shape
    nch, kk, c = dest3.shape
    ncores, nw = _sc_workers()
    per_w = nch // nw
    mesh = plsc.VectorSubcoreMesh(core_axis_name="c", subcore_axis_name="s")

    @functools.partial(
        pl.kernel, mesh=mesh, out_type=jax.ShapeDtypeStruct((kk, n, d), y.dtype),
        scratch_types=[pltpu.VMEM((kk, c), jnp.int32), pltpu.VMEM((c, d), y.dtype)])
    def gather_rows(y_hbm, d_hbm, after_hbm, o_hbm, idx_v, rows_v):
        del after_hbm
        wid = lax.axis_index("s") * ncores + lax.axis_index("c")

        @pl.loop(0, per_w)
        def _(j):
            ch = wid * per_w + j
            pltpu.sync_copy(d_hbm.at[ch], idx_v)
            for q in range(kk):
                pltpu.sync_copy(y_hbm.at[idx_v.at[q]], rows_v)
                pltpu.sync_copy(rows_v, o_hbm.at[q, pl.ds(ch * c, c)])

    return gather_rows(y, dest3, after)


def kernel(x, c, ctx, c_ctx, ada_w, ada_b, w_in, mla_q_norm_g, mla_kv_norm_g, mla_w_uq, mla_w_ukv,
           sg_norm_g, sg_norm_b, sg_w, sg_b, ret_decay_fwd, ret_decay_bwd, w_o, ln1_g, ln1_b,
           router_w, router_b, w_gate_up, b_gate_up, w_down, b_down, ln2_g, ln2_b):
    b, s_len, d = x.shape
    lc = ctx.shape[1]
    assert d == D_MODEL and lc % TM == 0 and s_len % lc == 0 and s_len % GRID_W == 0
    assert b + 1 <= 8
    t = s_len + lc
    n_lat_tiles = s_len // TM
    params = dict(w_in=w_in, mla_q_norm_g=mla_q_norm_g, mla_kv_norm_g=mla_kv_norm_g, mla_w_uq=mla_w_uq,
                  mla_w_ukv=mla_w_ukv, sg_norm_g=sg_norm_g, sg_norm_b=sg_norm_b, sg_w=sg_w, sg_b=sg_b,
                  ret_decay_fwd=ret_decay_fwd, ret_decay_bwd=ret_decay_bwd, w_o=w_o, ln1_g=ln1_g,
                  ln1_b=ln1_b, router_w=router_w, router_b=router_b, w_gate_up=w_gate_up,
                  b_gate_up=b_gate_up, w_down=w_down, b_down=b_down, ln2_g=ln2_g, ln2_b=ln2_b)
    lws = _layer_weights(params)
    ew = expert_weights(w_gate_up, w_down, lws["deint"][0])
    tab = _rotation_tables(s_len, lc)

    c_rows = jnp.concatenate([c, c_ctx[None, :], jnp.zeros((8 - b - 1, d), F32)], axis=0)
    mod = ada_modulation(c_rows, ada_w, ada_b).reshape(DEPTH, 8, N_MOD, d)
    mod_tab = jnp.stack([mod[:, :b], jnp.broadcast_to(mod[:, b:b + 1], (DEPTH, b, N_MOD, d))], axis=2)

    n_streams = N_STREAMS if b % N_STREAMS == 0 else 1
    bs = b // n_streams
    xs = [jnp.concatenate([x[i * bs:(i + 1) * bs], ctx[i * bs:(i + 1) * bs]], axis=1)
          for i in range(n_streams)]
    order = c_rows
    held = None
    for li in range(DEPTH):
        lw = {k: v[li] for k, v in lws.items()}
        mts = [mod_tab[li, si * bs:(si + 1) * bs] for si in range(n_streams)]
        fronts = []
        for si in range(n_streams):
            fr = _front(xs[si], mts[si], tab, lw, s_len, lc, after=order)
            order = fr["cnt"]
            dispatch_after = order
            if si == 0 and held is not None:
                xs[-1] = _finish(*held, after=order)
                held = None
                dispatch_after = xs[-1]
            fr["xg"] = sc_dispatch(fr.pop("h2"), fr["dest3"], fr["cap"], dispatch_after)
            fronts.append(fr)
        ys = []
        for si in range(n_streams):
            y = expert_ffn(fronts[si]["xg"], fronts[si]["block_e"], fronts[si]["n_used"], ew, li, lw,
                           after=order)
            order = y
            ys.append(y)
        for si in range(n_streams):
            if si == n_streams - 1 and n_streams > 1 and li + 1 < DEPTH:
                held = (fronts[si], ys[si], mts[si], lw, s_len, order)
            else:
                xs[si] = _finish(fronts[si], ys[si], mts[si], lw, s_len, ys[si], after=order)
                order = xs[si]
    return jnp.concatenate([xi[:, :s_len] for xi in xs], axis=0)


def _front(x_all, mt, tab, lw, s_len, lc, after):
    b, t, d = x_all.shape
    n_lat_tiles = s_len // TM
    q, k, v, sg, retp, rv, a = input_projection(x_all, mt, tab, lw, n_lat_tiles, after)
    mla_lat, mla_ctx = mla_attention(q, k, v, s_len, lc)
    st = retention_scan(a, lw["cd"], s_len // CHUNK)
    x1, h2, idx, gates, rank, cnt = output_projection(
        x_all, mla_lat, mla_ctx, sg, retp, rv, st, mt, lw, n_lat_tiles)
    to_tok = lambda z: z.transpose(2, 0, 1, 3).reshape(TOP_K, b * t)
    dest, block_e, n_used, nb = _route(to_tok(idx), to_tok(rank), cnt[:, 0])
    assert (b * t) % SC_CHUNK == 0
    dest3 = dest.reshape(TOP_K, (b * t) // SC_CHUNK, SC_CHUNK).transpose(1, 0, 2)
    return dict(x1=x1, gates=gates, cnt=cnt, dest3=dest3, block_e=block_e, n_used=n_used,
                h2=h2.reshape(b * t, h2.shape[-1]), cap=nb * MOE_BM)


def _finish(fr, y, mt, lw, s_len, gather_after, after):
    b, t, d = fr["x1"].shape
    yg = sc_combine_gather(y, fr["dest3"], b * t, gather_after).reshape(TOP_K, b, t, y.shape[-1])
    gates_tok = fr["gates"].transpose(0, 1, 3, 2).reshape(b, t, TOP_K)
    return combine_deepnorm2(fr["x1"], yg, gates_tok, mt, lw, s_len // TM, after)
```

```python
import functools

import numpy as np
import jax
import jax.numpy as jnp
from jax import lax
from jax.experimental import pallas as pl
from jax.experimental.pallas import tpu as pltpu
from jax.experimental.pallas import tpu_sc as plsc

F32 = jnp.float32
BF16 = jnp.bfloat16
MXU_DT = BF16
PACK_ROWS = True

D_MODEL = 1024
DEPTH = 4
GRID_W = 64
MLA_HEADS = 4
MLA_NOPE = 128
MLA_ROPE = 64
MLA_V = 128
MLA_Q_LORA = 384
MLA_KV_LORA = 256
MLA_QK = MLA_NOPE + MLA_ROPE
MLA_SCALE = MLA_QK ** -0.5
ROPE_BASE = 10000.0
ROPE_AXIS_FREQS = MLA_ROPE // 4
SG_GROUPS = 4
SG_WIDTH = 256
SG_CHUNK = 128
RET_HEADS = 4
RET_QK = 32
RET_V = 64
RET_CHUNK = 128
RET_ROPE_BASE = 10000.0
N_EXPERTS = 32
TOP_K = 4
D_EXPERT = 1024
SWIGLU_LIMIT = 7.0
SWIGLU_ALPHA = 1.702
N_MOD = 6
LN_EPS = 1e-5
RMS_EPS = 1e-6
DEEPNORM_ALPHA = (2 * DEPTH) ** 0.25
MLA_OUT = MLA_HEADS * MLA_V
RET_OUT = RET_HEADS * RET_V

TM = 256
CHUNK = 128
MOE_BM = 512
SC_CHUNK = 48
N_STREAMS = 2
ATT_TQ = 1024
ATT_TK = 2048
VMEM_LIMIT = 48 * 2 ** 20

_O_CQ, _O_CKV, _O_SGU, _O_SGV = 0, 384, 640, 896
_O_RQ, _O_RK, _O_RQS, _O_RKS, _O_RV, _O_RG, _O_KR = 1152, 1280, 1408, 1536, 1664, 1920, 2176
IN_P = 2304
_T_QC, _T_QS, _T_KCS, _T_RQC, _T_RQS, _T_RKC, _T_RKS = 0, 256, 512, 640, 768, 896, 1024
TAB_W = 1152


def _cparams(sem):
    return pltpu.CompilerParams(dimension_semantics=sem, vmem_limit_bytes=VMEM_LIMIT)


def _dot(a, b):
    return jnp.dot(a, b, preferred_element_type=F32)


def _dot_nt(a, b):
    return lax.dot_general(a, b, (((1,), (1,)), ((), ())), preferred_element_type=F32)


def _mx(a):
    return a.astype(MXU_DT)


def _swap16(j):
    return (j // 32) * 32 + ((j % 32) + 16) % 32


_ERF_ALPHA = (-2.72614225801306e-10, 2.77068142495902e-08, -2.10102402082508e-06,
              -5.69250639462346e-05, -7.34990630326855e-04, -2.95459980854025e-03,
              -1.60960333262415e-02)
_ERF_BETA = (-1.45660718464996e-05, -2.13374055278905e-04, -1.68282697438203e-03,
             -7.37332916720468e-03, -1.42647390514189e-02)


def _erf(x):
    x = jnp.clip(x, -4.0, 4.0)
    x2 = x * x
    p = jnp.full_like(x, _ERF_ALPHA[0])
    for c in _ERF_ALPHA[1:]:
        p = p * x2 + c
    q = jnp.full_like(x, _ERF_BETA[0])
    for c in _ERF_BETA[1:]:
        q = q * x2 + c
    return x * p / q


def _gelu(x):
    return 0.5 * x * (1.0 + _erf(x * 0.7071067811865476))


def _sigmoid(x):
    return 1.0 / (1.0 + jnp.exp(-x))


def _ln(x):
    xc = x - jnp.mean(x, axis=-1, keepdims=True)
    return xc * lax.rsqrt(jnp.mean(xc * xc, axis=-1, keepdims=True) + LN_EPS)


def _lane_group(shape, width):
    return lax.broadcasted_iota(jnp.int32, shape, len(shape) - 1) // width


def _ada_kernel(c_ref, w_ref, b_ref, o_ref):
    c = c_ref[...]
    o_ref[0] = _dot(c * _sigmoid(c), w_ref[0]) + b_ref[0]


def ada_modulation(c_rows, ada_w, ada_b):
    nl, d, n = ada_w.shape
    tn = 1536
    return pl.pallas_call(
        _ada_kernel,
        grid=(nl, n // tn),
        in_specs=[pl.BlockSpec((8, d), lambda l, j: (0, 0)),
                  pl.BlockSpec((1, d, tn), lambda l, j: (l, 0, j)),
                  pl.BlockSpec((1, 1, tn), lambda l, j: (l, 0, j))],
        out_specs=pl.BlockSpec((1, 8, tn), lambda l, j: (l, 0, j)),
        out_shape=jax.ShapeDtypeStruct((nl, 8, n), F32),
        compiler_params=_cparams(("arbitrary", "arbitrary")),
        name="ada_modulation",
    )(c_rows, ada_w, ada_b.reshape(nl, 1, n))


def _inproj_kernel(x_ref, mod_ref, tab_ref, w_in_ref, qg_ref, kvg_ref, w_uq_ref, w_ukv_ref,
                   sgg_ref, sgb_ref, sgw_ref, sgbias_ref, kdec_ref, bd_ref,
                   q_ref, k_ref, v_ref, sg_ref, retp_ref, rv_ref, a_ref):
    x = x_ref[0]
    mod = mod_ref[0, 0]
    h = x * (1.0 + mod[1:2]) + mod[0:1]
    p = _dot(_mx(h), w_in_ref[...])
    tab = tab_ref[...]

    cq = p[:, _O_CQ:_O_CQ + MLA_Q_LORA]
    cq = cq * lax.rsqrt(jnp.mean(cq * cq, axis=-1, keepdims=True) + RMS_EPS) * qg_ref[...]
    qa = _dot(_mx(cq), w_uq_ref[...])
    rot = (qa[:, 512:768] * tab[:, _T_QC:_T_QC + 256]
           + qa[:, 768:1024] * tab[:, _T_QS:_T_QS + 256])
    for hh in range(MLA_HEADS):
        q_ref[0, hh, :, 0:128] = (qa[:, 128 * hh:128 * hh + 128] * MLA_SCALE).astype(q_ref.dtype)
        g = hh // 2
        q_ref[0, hh, :, 128:256] = rot[:, 128 * g:128 * g + 128].astype(q_ref.dtype)

    ckv = p[:, _O_CKV:_O_CKV + MLA_KV_LORA]
    ckv = ckv * lax.rsqrt(jnp.mean(ckv * ckv, axis=-1, keepdims=True) + RMS_EPS) * kvg_ref[...]
    kv = _dot(_mx(ckv), w_ukv_ref[...])
    t = p[:, _O_KR:_O_KR + 128] * tab[:, _T_KCS:_T_KCS + 128]
    u = t + pltpu.roll(t, 64, axis=1)
    low = lax.broadcasted_iota(jnp.int32, u.shape, 1) < 64
    kx = (jnp.where(low, u, 0.0), jnp.where(low, 0.0, u))
    ones_col = jnp.where(lax.broadcasted_iota(jnp.int32, u.shape, 1) == 0, 1.0, 0.0).astype(v_ref.dtype)
    for hh in range(MLA_HEADS):
        k_ref[0, hh, :, 0:128] = kv[:, 256 * hh:256 * hh + 128].astype(k_ref.dtype)
        k_ref[0, hh, :, 128:256] = kx[hh % 2].astype(k_ref.dtype)
        v_ref[0, hh, :, 0:128] = kv[:, 256 * hh + 128:256 * hh + 256].astype(v_ref.dtype)
        v_ref[0, hh, :, 128:256] = ones_col

    gu = _gelu(p[:, _O_SGU:_O_SGU + SG_WIDTH])
    gv = _ln(_gelu(p[:, _O_SGV:_O_SGV + SG_WIDTH])) * sgg_ref[...] + sgb_ref[...]
    gvm = _mx(gv)
    grp = _lane_group((CHUNK, SG_WIDTH), SG_WIDTH // SG_GROUPS)
    for c in range(TM // CHUNK):
        rows = slice(c * CHUNK, (c + 1) * CHUNK)
        res = _dot(sgw_ref[...], gvm[rows])
        mixed = sgbias_ref[...]
        for g in range(SG_GROUPS):
            mixed = mixed + jnp.where(grp == g, res[g * CHUNK:(g + 1) * CHUNK], 0.0)
        sg_ref[0, rows, :] = (gu[rows] * mixed).astype(sg_ref.dtype)

    rq = (p[:, _O_RQ:_O_RQ + 128] * tab[:, _T_RQC:_T_RQC + 128]
          + p[:, _O_RQS:_O_RQS + 128] * tab[:, _T_RQS:_T_RQS + 128])
    rk = (p[:, _O_RK:_O_RK + 128] * tab[:, _T_RKC:_T_RKC + 128]
          + p[:, _O_RKS:_O_RKS + 128] * tab[:, _T_RKS:_T_RKS + 128])
    rv = p[:, _O_RV:_O_RV + RET_OUT]
    retp_ref[0, :, 0:128] = rq
    retp_ref[0, :, 128:256] = rk
    retp_ref[0, :, 256:512] = p[:, _O_RG:_O_RG + RET_OUT]
    rvm = _mx(rv)
    rv_ref[0] = rvm.astype(rv_ref.dtype)
    bd = bd_ref[...]
    for c in range(TM // CHUNK):
        rows = slice(c * CHUNK, (c + 1) * CHUNK)
        for d in range(2):
            kd_t = _mx((rk[rows] * kdec_ref[d]).T)
            af = _dot(kd_t, rvm[rows]) * bd
            a_ref[0, c, d] = (af[0:32] + af[32:64]) + (af[64:96] + af[96:128])


def _ordered_after(kernel_fn, pos):
    def wrapped(*refs):
        return kernel_fn(*refs[:pos], *refs[pos + 1:])
    return wrapped


_ORDER_SPEC = pl.BlockSpec(memory_space=pl.ANY)


def input_projection(x_all, mod_tab, tab, lw, n_lat_tiles, after):
    b, t, d = x_all.shape
    nt = t // TM
    nc = t // CHUNK
    cpt = TM // CHUNK
    const2 = lambda bi, j: (0, 0)
    const3 = lambda bi, j: (0, 0, 0)
    out_shape = (
        jax.ShapeDtypeStruct((b, MLA_HEADS, t, 256), MXU_DT),
        jax.ShapeDtypeStruct((b, MLA_HEADS, t, 256), MXU_DT),
        jax.ShapeDtypeStruct((b, MLA_HEADS, t, 256), MXU_DT),
        jax.ShapeDtypeStruct((b, t, SG_WIDTH), MXU_DT),
        jax.ShapeDtypeStruct((b, t, 512), F32),
        jax.ShapeDtypeStruct((b, t, RET_OUT), MXU_DT),
        jax.ShapeDtypeStruct((b, nc, 2, RET_QK, RET_OUT), F32),
    )
    head_spec = lambda w: pl.BlockSpec((1, MLA_HEADS, TM, w), lambda bi, j: (bi, 0, j, 0))
    tok_spec = lambda w: pl.BlockSpec((1, TM, w), lambda bi, j: (bi, j, 0))
    return pl.pallas_call(
        _ordered_after(_inproj_kernel, 14),
        grid=(b, nt),
        in_specs=[
            tok_spec(d),
            pl.BlockSpec((1, 1, N_MOD, d), lambda bi, j: (bi, j // n_lat_tiles, 0, 0)),
            pl.BlockSpec((TM, TAB_W), lambda bi, j: (j, 0)),
            pl.BlockSpec((d, IN_P), const2),
            pl.BlockSpec((1, MLA_Q_LORA), const2),
            pl.BlockSpec((1, MLA_KV_LORA), const2),
            pl.BlockSpec((MLA_Q_LORA, 1024), const2),
            pl.BlockSpec((MLA_KV_LORA, 1024), const2),
            pl.BlockSpec((1, SG_WIDTH), const2),
            pl.BlockSpec((1, SG_WIDTH), const2),
            pl.BlockSpec((SG_GROUPS * CHUNK, CHUNK), const2),
            pl.BlockSpec((CHUNK, SG_WIDTH), const2),
            pl.BlockSpec((2, CHUNK, 128), const3),
            pl.BlockSpec((128, RET_OUT), const2),
            _ORDER_SPEC,
        ],
        out_specs=(head_spec(256), head_spec(256), head_spec(256), tok_spec(SG_WIDTH),
                   tok_spec(512), tok_spec(RET_OUT),
                   pl.BlockSpec((1, cpt, 2, RET_QK, RET_OUT), lambda bi, j: (bi, j, 0, 0, 0))),
        out_shape=out_shape,
        compiler_params=_cparams(("parallel", "parallel")),
        name="input_projection",
    )(x_all, mod_tab, tab, lw["w_in"], lw["q_g"], lw["kv_g"], lw["w_uq"], lw["w_ukv"],
      lw["sg_g"], lw["sg_b"], lw["sg_w"], lw["sg_bias"], lw["kdec"], lw["bd"], after)


def _attn_kernel(q_ref, k_ref, v_ref, o_ref, *, n_main, tk, tail):
    q = q_ref[0, 0]
    tq = q.shape[0]
    chunks = [(i * tk, tk) for i in range(n_main)] + ([(n_main * tk, tail)] if tail else [])

    def scores(ci):
        start, size = chunks[ci]
        return _dot_nt(q, k_ref[0, 0, start:start + size, :])

    m = jnp.full((tq, 1), -1e30, F32)
    acc = jnp.zeros((tq, 256), F32)
    s_next = scores(0)
    for ci, (start, size) in enumerate(chunks):
        s = s_next
        if ci + 1 < len(chunks):
            s_next = scores(ci + 1)
        m_new = jnp.maximum(m, jnp.max(s, axis=-1, keepdims=True))
        p = jnp.exp(s - m_new)
        acc = jnp.exp(m - m_new) * acc + _dot(_mx(p), v_ref[0, 0, start:start + size, :])
        m = m_new
    o_ref[0] = (acc[:, 0:MLA_V] / acc[:, MLA_V:MLA_V + 1]).astype(o_ref.dtype)


def mla_attention(q, k, v, s_len, lc):
    b, hn, t, _ = q.shape
    tq = min(ATT_TQ, s_len)
    tk = min(ATT_TK, s_len)
    kv_full = pl.BlockSpec((1, 1, t, 256), lambda bi, hi, i: (bi, hi, 0, 0))
    out_lat = pl.pallas_call(
        functools.partial(_attn_kernel, n_main=s_len // tk, tk=tk, tail=lc),
        grid=(b, hn, s_len // tq),
        in_specs=[pl.BlockSpec((1, 1, tq, 256), lambda bi, hi, i: (bi, hi, i, 0)), kv_full, kv_full],
        out_specs=pl.BlockSpec((1, tq, MLA_V), lambda bi, hi, i: (bi, i, hi)),
        out_shape=jax.ShapeDtypeStruct((b, s_len, MLA_OUT), MXU_DT),
        compiler_params=_cparams(("parallel", "parallel", "arbitrary")),
        name="mla_attention_latent",
    )(q, k, v)
    cblk = s_len // lc
    ctx_spec = pl.BlockSpec((1, 1, lc, 256), lambda bi, hi: (bi, hi, cblk, 0))
    out_ctx = pl.pallas_call(
        functools.partial(_attn_kernel, n_main=0, tk=tk, tail=lc),
        grid=(b, hn),
        in_specs=[ctx_spec, ctx_spec, ctx_spec],
        out_specs=pl.BlockSpec((1, lc, MLA_V), lambda bi, hi: (bi, 0, hi)),
        out_shape=jax.ShapeDtypeStruct((b, lc, MLA_OUT), MXU_DT),
        compiler_params=_cparams(("parallel", "parallel")),
        name="mla_attention_context",
    )(q, k, v)
    return out_lat, out_ctx


def _ret_scan_kernel(a_ref, cd_ref, s_ref, *, n_lat_chunks):
    nc = a_ref.shape[1]
    ncc = nc - n_lat_chunks
    cd_f, cd_b = cd_ref[0], cd_ref[1]

    def body(n, carry):
        sf, sb = carry
        cf = jnp.where(n < ncc, n_lat_chunks + n, n - ncc)
        cb = jnp.where(n < ncc, nc - 1 - n, n_lat_chunks - 1 - (n - ncc))
        s_ref[0, cf, 0] = sf
        s_ref[0, cb, 1] = sb
        return sf * cd_f + a_ref[0, cf, 0], sb * cd_b + a_ref[0, cb, 1]

    zero = jnp.zeros((RET_QK, RET_OUT), F32)
    lax.fori_loop(0, nc, body, (zero, zero))


def retention_scan(a, cd, n_lat_chunks):
    b, nc = a.shape[:2]
    blk = pl.BlockSpec((1, nc, 2, RET_QK, RET_OUT), lambda bi: (bi, 0, 0, 0, 0))
    return pl.pallas_call(
        functools.partial(_ret_scan_kernel, n_lat_chunks=n_lat_chunks),
        grid=(b,),
        in_specs=[blk, pl.BlockSpec((2, 1, RET_OUT), lambda bi: (0, 0, 0))],
        out_specs=blk,
        out_shape=jax.ShapeDtypeStruct(a.shape, F32),
        compiler_params=_cparams(("parallel",)),
        name="retention_scan",
    )(a, cd)


def _split_dot(x, ones2):
    hi = x.astype(BF16)
    lo = (x - hi.astype(F32)).astype(BF16)
    return _dot(jnp.concatenate([hi, lo], axis=1), ones2)


def _outproj_kernel(x_ref, mlal_ref, mlac_ref, sg_ref, retp_ref, rv_ref, st_ref, mod_ref,
                    m_ref, qdec_ref, bd_ref, seg_ref, w_o_ref, lng_ref, lnb_ref, rw_ref, rb_ref, tri_ref,
                    x1_ref, h2_ref, idx_ref, gate_ref, rank_ref, cnt_ref, carry_ref, *, n_lat_tiles):
    @pl.when(pl.program_id(0) == 0)
    def _():
        carry_ref[...] = jnp.zeros_like(carry_ref)

    nbat = x_ref.shape[0]
    units = [(bb, c) for bb in range(nbat) for c in range(TM // CHUNK)]
    rows = lambda c: slice(c * CHUNK, (c + 1) * CHUNK)
    g32 = _lane_group((CHUNK, 128), RET_QK)
    g64 = _lane_group((CHUNK, RET_OUT), RET_V)
    bd = bd_ref[...]
    seg2 = jnp.concatenate([seg_ref[...], seg_ref[...]], axis=0)
    m4 = jnp.concatenate([m_ref[hh] for hh in range(RET_HEADS)], axis=0)

    rq, s4 = {}, {}
    for u in units:
        bb, c = u
        rq[u] = retp_ref[bb, rows(c), 0:128]
        q4 = jnp.concatenate([jnp.where(g32 == hh, rq[u], 0.0) for hh in range(RET_HEADS)], axis=0)
        s4[u] = _dot_nt(_mx(q4), _mx(retp_ref[bb, rows(c), 128:256]))
    o = {}
    for u in units:
        bb, c = u
        r = _dot(_mx(s4[u] * m4), rv_ref[bb, rows(c), :])
        qd = jnp.concatenate([rq[u] * qdec_ref[0], rq[u] * qdec_ref[1]], axis=1)
        st2 = jnp.concatenate([jnp.concatenate([st_ref[bb, c, dd]] * RET_HEADS, axis=0) * bd
                               for dd in range(2)], axis=0)
        acc = _dot(_mx(qd), _mx(st2))
        for hh in range(RET_HEADS):
            acc = acc + jnp.where(g64 == hh, r[hh * CHUNK:(hh + 1) * CHUNK], 0.0)
        o[u] = acc
    oc = {}
    for u in units:
        oc[u] = o[u] - _split_dot(o[u], seg2) * (1.0 / RET_V)
    ret = {}
    for u in units:
        bb, c = u
        var = _split_dot(oc[u] * oc[u], seg2) * (1.0 / RET_V)
        rg = retp_ref[bb, rows(c), 256:512]
        ret[u] = _mx(oc[u] * lax.rsqrt(var + LN_EPS) * (rg * _sigmoid(rg)))

    is_ctx = pl.program_id(0) >= n_lat_tiles
    ys = []
    for bb in range(nbat):
        mla = jnp.where(is_ctx, mlac_ref[bb], mlal_ref[bb])
        cat = jnp.concatenate(
            [mla, sg_ref[bb], jnp.concatenate([ret[(bb, c)] for c in range(TM // CHUNK)], axis=0)], axis=1)
        ys.append(_dot(cat, w_o_ref[...]))
    logits = []
    for bb in range(nbat):
        mod = mod_ref[bb, 0]
        x1 = _ln(DEEPNORM_ALPHA * x_ref[bb] + mod[2:3] * ys[bb]) * lng_ref[...] + lnb_ref[...]
        x1_ref[bb] = x1
        h2 = x1 * (1.0 + mod[4:5]) + mod[3:4]
        h2_ref[bb] = _pack_bf16_pairs(h2) if PACK_ROWS else h2
        h2_hi = h2.astype(BF16)
        h2_lo = (h2 - h2_hi.astype(F32)).astype(BF16)
        r2 = _dot_nt(rw_ref[...], jnp.concatenate([h2_hi, h2_lo], axis=0))
        logits.append(r2[0:N_EXPERTS, 0:TM] + r2[N_EXPERTS:, 0:TM] + r2[0:N_EXPERTS, TM:] + rb_ref[...])

    e_iota = lax.broadcasted_iota(jnp.int32, logits[0].shape, 0).astype(F32)
    idxs_all, onehots, prefixes = [], [], []
    for bb in range(nbat):
        work = logits[bb]
        vals, idxs = [], []
        for _ in range(TOP_K):
            mval = jnp.max(work, axis=0, keepdims=True)
            midx = jnp.min(jnp.where(work == mval, e_iota, float(N_EXPERTS)), axis=0, keepdims=True)
            vals.append(mval)
            idxs.append(midx)
            work = jnp.where(e_iota == midx, -jnp.inf, work)
        ex = [jnp.exp(vv - vals[0]) for vv in vals]
        den = ex[0] + ex[1] + ex[2] + ex[3]
        onehot = jnp.zeros_like(work)
        for kk in range(TOP_K):
            gate_ref[bb, 0, kk:kk + 1, :] = ex[kk] / den
            idx_ref[bb, 0, kk:kk + 1, :] = idxs[kk].astype(jnp.int32)
            onehot = onehot + jnp.where(e_iota == idxs[kk], 1.0, 0.0)
        idxs_all.append(idxs)
        onehots.append(onehot)
        prefixes.append(_dot(onehot.astype(BF16), tri_ref[...]))
    count = carry_ref[:, 0:1]
    for bb in range(nbat):
        base = count + prefixes[bb]
        for kk in range(TOP_K):
            rk_k = jnp.sum(jnp.where(e_iota == idxs_all[bb][kk], base, 0.0), axis=0, keepdims=True)
            rank_ref[bb, 0, kk:kk + 1, :] = rk_k.astype(jnp.int32)
        count = count + jnp.sum(onehots[bb], axis=1, keepdims=True)
    carry_ref[...] = jnp.broadcast_to(count, carry_ref.shape)
    cnt_ref[...] = carry_ref[...].astype(jnp.int32)


def output_projection(x_all, mla_lat, mla_ctx, sg, retp, rv, st, mod_tab, lw, n_lat_tiles):
    b, t, d = x_all.shape
    nt = t // TM
    cpt = TM // CHUNK
    const2 = lambda j: (0, 0)
    const3 = lambda j: (0, 0, 0)
    tok_spec = lambda w: pl.BlockSpec((b, TM, w), lambda j: (0, j, 0))
    route_spec = pl.BlockSpec((b, 1, TOP_K, TM), lambda j: (0, j, 0, 0))
    route_shape = lambda dt: jax.ShapeDtypeStruct((b, nt, TOP_K, TM), dt)
    h2w = d // 2 if PACK_ROWS else d
    return pl.pallas_call(
        functools.partial(_outproj_kernel, n_lat_tiles=n_lat_tiles),
        grid=(nt,),
        in_specs=[
            tok_spec(d),
            pl.BlockSpec((b, TM, MLA_OUT), lambda j: (0, jnp.minimum(j, n_lat_tiles - 1), 0)),
            pl.BlockSpec((b, TM, MLA_OUT), lambda j: (0, jnp.maximum(j - n_lat_tiles, 0), 0)),
            tok_spec(SG_WIDTH), tok_spec(512), tok_spec(RET_OUT),
            pl.BlockSpec((b, cpt, 2, RET_QK, RET_OUT), lambda j: (0, j, 0, 0, 0)),
            pl.BlockSpec((b, 1, N_MOD, d), lambda j: (0, j // n_lat_tiles, 0, 0)),
            pl.BlockSpec((RET_HEADS, CHUNK, CHUNK), const3),
            pl.BlockSpec((2, CHUNK, 128), const3),
            pl.BlockSpec((128, RET_OUT), const2),
            pl.BlockSpec((RET_OUT, RET_OUT), const2),
            pl.BlockSpec((d, d), const2),
            pl.BlockSpec((1, d), const2),
            pl.BlockSpec((1, d), const2),
            pl.BlockSpec((2 * N_EXPERTS, d), const2),
            pl.BlockSpec((N_EXPERTS, 1), const2),
            pl.BlockSpec((TM, TM), const2),
        ],
        out_specs=(tok_spec(d), tok_spec(h2w), route_spec, route_spec, route_spec,
                   pl.BlockSpec((N_EXPERTS, 128), const2)),
        out_shape=(jax.ShapeDtypeStruct((b, t, d), F32),
                   jax.ShapeDtypeStruct((b, t, h2w), jnp.uint32 if PACK_ROWS else F32),
                   route_shape(jnp.int32), route_shape(F32), route_shape(jnp.int32),
                   jax.ShapeDtypeStruct((N_EXPERTS, 128), jnp.int32)),
        scratch_shapes=[pltpu.VMEM((N_EXPERTS, 128), F32)],
        compiler_params=_cparams(("arbitrary",)),
        name="output_projection",
    )(x_all, mla_lat, mla_ctx, sg, retp, rv, st, mod_tab, lw["ret_m"], lw["qdec"], lw["bd"], lw["seg"],
      lw["w_o"], lw["ln1_g"], lw["ln1_b"], lw["router_w"], lw["router_b"], lw["tri"])


_DEINT = 256


def _expert_weights_kernel(wgu_ref, wd_ref, perm_ref, wg_ref, wl_ref, wdo_ref):
    half = _DEINT // 2
    for c in range(2 * D_EXPERT // _DEINT):
        r = _dot(_mx(wgu_ref[0, 0, :, _DEINT * c:_DEINT * (c + 1)]), perm_ref[...])
        wg_ref[0, 0, :, half * c:half * (c + 1)] = r[:, :half].astype(wg_ref.dtype)
        wl_ref[0, 0, :, half * c:half * (c + 1)] = r[:, half:].astype(wl_ref.dtype)
    wdo_ref[0, 0] = wd_ref[0, 0].astype(wdo_ref.dtype)


def expert_weights(w_gate_up, w_down, deint):
    nl, ne, d, de2 = w_gate_up.shape
    de = de2 // 2
    blk = lambda r, c: pl.BlockSpec((1, 1, r, c), lambda l, e: (l, e, 0, 0))
    return pl.pallas_call(
        _expert_weights_kernel,
        grid=(nl, ne),
        in_specs=[blk(d, de2), blk(de, d), pl.BlockSpec((_DEINT, _DEINT), lambda l, e: (0, 0))],
        out_specs=(blk(d, de), blk(d, de), blk(de, d)),
        out_shape=(jax.ShapeDtypeStruct((nl, ne, d, de), MXU_DT), jax.ShapeDtypeStruct((nl, ne, d, de), MXU_DT),
                   jax.ShapeDtypeStruct((nl, ne, de, d), MXU_DT)),
        compiler_params=_cparams(("parallel", "parallel")),
        name="expert_weights",
    )(w_gate_up, w_down, deint)


def _pack_bf16_pairs(v):
    bits = lax.bitcast_convert_type(v.astype(BF16).astype(F32), jnp.uint32)
    half = bits.shape[1] // 2
    return bits[:, :half] | (bits[:, half:] >> 16)


def _unpack_bf16_pairs(w):
    return (lax.bitcast_convert_type(w & jnp.uint32(0xFFFF0000), F32),
            lax.bitcast_convert_type(w << 16, F32))


def _expert_kernel(be_ref, nu_ref, x_ref, wg_s, wl_s, bg_ref, bl_ref, wd_s, bd_ref, y_ref):
    del be_ref
    active = pl.program_id(0) < nu_ref[0]

    @pl.when(active)
    def _():
        if PACK_ROWS:
            hi, lo = _unpack_bf16_pairs(x_ref[...])
            xb = jnp.concatenate([hi.astype(BF16), lo.astype(BF16)], axis=1)
        else:
            xb = x_ref[...]
        glu = jnp.minimum(_dot(xb, wg_s[0, 0]) + bg_ref[0], SWIGLU_LIMIT)
        lin = jnp.clip(_dot(xb, wl_s[0, 0]) + bl_ref[0], -SWIGLU_LIMIT, SWIGLU_LIMIT)
        act = glu * _sigmoid(SWIGLU_ALPHA * glu) * (lin + 1.0)
        y = _dot(_mx(act), wd_s[0, 0]) + bd_ref[0]
        y_ref[...] = _pack_bf16_pairs(y) if PACK_ROWS else y

    @pl.when(jnp.logical_not(active))
    def _():
        y_ref[...] = jnp.zeros_like(y_ref)


def expert_ffn(xg, block_e, n_used, ew, li, lw, after):
    cap, xw = xg.shape
    d = D_MODEL
    nb = cap // MOE_BM
    de = D_EXPERT
    xmap = lambda i, be, nu: (jnp.minimum(i, nu[0] - 1), 0)
    wmap = lambda i, be, nu: (be[i], 0, 0)
    lmap = lambda i, be, nu: (li, be[i], 0, 0)
    grid_spec = pltpu.PrefetchScalarGridSpec(
        num_scalar_prefetch=2,
        grid=(nb,),
        in_specs=[pl.BlockSpec((MOE_BM, xw), xmap),
                  pl.BlockSpec((1, 1, d, de), lmap),
                  pl.BlockSpec((1, 1, d, de), lmap),
                  pl.BlockSpec((1, 1, de), wmap), pl.BlockSpec((1, 1, de), wmap),
                  pl.BlockSpec((1, 1, de, d), lmap), pl.BlockSpec((1, 1, d), wmap),
                  _ORDER_SPEC],
        out_specs=pl.BlockSpec((MOE_BM, xw), lambda i, be, nu: (i, 0)),
    )
    return pl.pallas_call(
        _ordered_after(_expert_kernel, 9),
        grid_spec=grid_spec,
        out_shape=jax.ShapeDtypeStruct((cap, xw), xg.dtype),
        compiler_params=_cparams(("arbitrary",)),
        name="expert_ffn",
    )(block_e, n_used, xg, ew[0], ew[1], lw["b_glu"], lw["b_lin"], ew[2], lw["b_down"], after)


def _combine_ln2_kernel(x_ref, y_ref, gate_ref, mod_ref, g_ref, b_ref, o_ref):
    for bb in range(x_ref.shape[0]):
        gates = gate_ref[bb]
        if PACK_ROWS:
            f_hi, f_lo = 0.0, 0.0
            for kk in range(TOP_K):
                hi, lo = _unpack_bf16_pairs(y_ref[kk, bb])
                f_hi = f_hi + gates[:, kk:kk + 1] * hi
                f_lo = f_lo + gates[:, kk:kk + 1] * lo
            f = jnp.concatenate([f_hi, f_lo], axis=1)
        else:
            f = gates[:, 0:1] * y_ref[0, bb]
            for kk in range(1, TOP_K):
                f = f + gates[:, kk:kk + 1] * y_ref[kk, bb]
        mod = mod_ref[bb, 0]
        o_ref[bb] = _ln(DEEPNORM_ALPHA * x_ref[bb] + mod[5:6] * f) * g_ref[...] + b_ref[...]


def combine_deepnorm2(x1, yg, gates, mod_tab, lw, n_lat_tiles, after):
    b, t, d = x1.shape
    tok = pl.BlockSpec((b, TM, d), lambda j: (0, j, 0))
    vec = pl.BlockSpec((1, d), lambda j: (0, 0))
    return pl.pallas_call(
        _ordered_after(_combine_ln2_kernel, 6),
        grid=(t // TM,),
        in_specs=[tok,
                  pl.BlockSpec((TOP_K, b, TM, yg.shape[-1]), lambda j: (0, 0, j, 0)),
                  pl.BlockSpec((b, TM, TOP_K), lambda j: (0, j, 0)),
                  pl.BlockSpec((b, 1, N_MOD, d), lambda j: (0, j // n_lat_tiles, 0, 0)), vec, vec,
                  _ORDER_SPEC],
        out_specs=tok,
        out_shape=jax.ShapeDtypeStruct((b, t, d), F32),
        compiler_params=_cparams(("parallel",)),
        name="combine_deepnorm2",
    )(x1, yg, gates, mod_tab, lw["ln2_g"], lw["ln2_b"], after)


def _rotation_tables(s_len, lc):
    rows = s_len // GRID_W
    row = jnp.broadcast_to(jnp.arange(rows, dtype=F32)[:, None], (rows, GRID_W)).reshape(-1)
    col = jnp.broadcast_to(jnp.arange(GRID_W, dtype=F32)[None, :], (rows, GRID_W)).reshape(-1)
    inv = ROPE_BASE ** (-jnp.arange(ROPE_AXIS_FREQS, dtype=F32) / ROPE_AXIS_FREQS)
    ar, ac = row[:, None] * inv, col[:, None] * inv
    c64 = jnp.concatenate([jnp.cos(ar), jnp.cos(ar), jnp.cos(ac), jnp.cos(ac)], axis=1)
    s64 = jnp.concatenate([-jnp.sin(ar), jnp.sin(ar), -jnp.sin(ac), jnp.sin(ac)], axis=1)
    c64 = jnp.concatenate([c64, jnp.ones((lc, 64), F32)], axis=0)
    s64 = jnp.concatenate([s64, jnp.zeros((lc, 64), F32)], axis=0)
    half = RET_QK // 2
    pos = jnp.concatenate([lc + jnp.arange(s_len, dtype=F32), jnp.arange(lc, dtype=F32)])
    inv_r = 1.0 / (RET_ROPE_BASE ** jnp.linspace(0.0, 1.0, half, dtype=F32))
    ang = pos[:, None] * inv_r
    rc = jnp.tile(jnp.concatenate([jnp.cos(ang), jnp.cos(ang)], axis=1), (1, RET_HEADS))
    rs = jnp.tile(jnp.concatenate([-jnp.sin(ang), jnp.sin(ang)], axis=1), (1, RET_HEADS))
    qs = RET_QK ** -0.5
    return jnp.concatenate([
        jnp.tile(c64, (1, MLA_HEADS)) * MLA_SCALE, jnp.tile(s64, (1, MLA_HEADS)) * MLA_SCALE,
        c64, s64, rc * qs, rs * qs, rc, rs], axis=1)


def _in_perm():
    a = np.arange
    return np.concatenate([
        a(0, 640), a(704, 1216), a(1216, 1344), a(1344, 1472),
        1216 + _swap16(a(128)), 1344 + _swap16(a(128)), a(1472, 1984),
        640 + a(64), 640 + _swap16(a(64))])


def _uq_perm():
    a = np.arange
    nope = [h * MLA_QK + a(MLA_NOPE) for h in range(MLA_HEADS)]
    rope = [h * MLA_QK + MLA_NOPE + a(MLA_ROPE) for h in range(MLA_HEADS)]
    part = [h * MLA_QK + MLA_NOPE + _swap16(a(MLA_ROPE)) for h in range(MLA_HEADS)]
    return np.concatenate(nope + rope + part)


def _layer_weights(p):
    nl = p["w_in"].shape[0]
    lgf = jax.nn.log_sigmoid(p["ret_decay_fwd"].astype(F32))
    lgb = jax.nn.log_sigmoid(p["ret_decay_bwd"].astype(F32))
    h128 = np.arange(128) // RET_QK
    h256 = np.arange(RET_OUT) // RET_V
    a = jnp.arange(CHUNK, dtype=F32)[None, :, None]
    lf, lb = lgf[:, h128][:, None, :], lgb[:, h128][:, None, :]
    kdec = jnp.stack([jnp.exp(lf * (CHUNK - 1.0 - a)), jnp.exp(lb * a)], axis=1)
    qdec = jnp.stack([jnp.exp(lf * (a + 1.0)), jnp.exp(lb * (CHUNK - a))], axis=1)
    i = jnp.arange(CHUNK, dtype=F32)[:, None]
    j = jnp.arange(CHUNK, dtype=F32)[None, :]
    dif = (i - j)[None, None]
    ret_m = jnp.where(dif >= 0, jnp.exp(lgf[:, :, None, None] * jnp.maximum(dif, 0.0)),
                      jnp.exp(lgb[:, :, None, None] * jnp.maximum(-dif, 0.0)))
    cd = jnp.stack([jnp.exp(lgf[:, h256] * CHUNK), jnp.exp(lgb[:, h256] * CHUNK)], axis=1)[:, :, None, :]
    bd = (h128[:, None] == h256[None, :]).astype(np.float32)
    seg = (h256[:, None] == h256[None, :]).astype(np.float32)
    tri = (np.arange(TM)[:, None] < np.arange(TM)[None, :]).astype(np.float32)
    jj = np.arange(_DEINT // 2)
    deint = np.zeros((_DEINT, _DEINT), np.float32)
    deint[2 * jj, jj] = 1.0
    deint[2 * jj + 1, _DEINT // 2 + jj] = 1.0
    rw_t = jnp.swapaxes(p["router_w"], 1, 2)
    rw_hi = rw_t.astype(BF16)
    rw_lo = (rw_t - rw_hi.astype(F32)).astype(BF16)
    sg_bias = jnp.repeat(jnp.swapaxes(p["sg_b"], 1, 2), SG_WIDTH // SG_GROUPS, axis=2)
    bgu = p["b_gate_up"]
    return {
        "w_in": p["w_in"][:, :, _in_perm()].astype(MXU_DT),
        "q_g": p["mla_q_norm_g"][:, None, :], "kv_g": p["mla_kv_norm_g"][:, None, :],
        "w_uq": p["mla_w_uq"][:, :, _uq_perm()].astype(MXU_DT),
        "w_ukv": p["mla_w_ukv"].astype(MXU_DT),
        "sg_g": p["sg_norm_g"][:, None, :], "sg_b": p["sg_norm_b"][:, None, :],
        "sg_w": p["sg_w"].reshape(nl, SG_GROUPS * CHUNK, CHUNK).astype(MXU_DT),
        "sg_bias": sg_bias,
        "kdec": kdec, "qdec": qdec, "ret_m": ret_m, "cd": cd,
        "bd": jnp.broadcast_to(jnp.asarray(bd), (nl,) + bd.shape),
        "seg": jnp.broadcast_to(jnp.asarray(seg, BF16), (nl,) + seg.shape),
        "tri": jnp.broadcast_to(jnp.asarray(tri, BF16), (nl,) + tri.shape),
        "w_o": p["w_o"].astype(MXU_DT),
        "ln1_g": p["ln1_g"][:, None, :], "ln1_b": p["ln1_b"][:, None, :],
        "ln2_g": p["ln2_g"][:, None, :], "ln2_b": p["ln2_b"][:, None, :],
        "router_w": jnp.concatenate([rw_hi, rw_lo], axis=1),
        "router_b": p["router_b"][:, :, None],
        "b_glu": bgu[:, :, None, 0::2], "b_lin": bgu[:, :, None, 1::2],
        "b_down": p["b_down"][:, :, None, :],
        "deint": jnp.broadcast_to(jnp.asarray(deint, MXU_DT), (nl,) + deint.shape),
    }


def _route(idx, rank, counts):
    n_assign = idx.shape[1] * TOP_K
    nb = -(-(n_assign + N_EXPERTS * (MOE_BM - 1)) // MOE_BM)
    padded = (counts + MOE_BM - 1) // MOE_BM * MOE_BM
    pad_end = jnp.cumsum(padded)
    pad_start = pad_end - padded
    experts = jnp.arange(N_EXPERTS, dtype=jnp.int32)
    dest = rank + jnp.sum(jnp.where(idx[..., None] == experts, pad_start, 0), axis=-1)
    blk_start = jnp.arange(nb, dtype=jnp.int32) * MOE_BM
    block_e = jnp.minimum(jnp.sum((pad_end[None, :] <= blk_start[:, None]).astype(jnp.int32), axis=1),
                          N_EXPERTS - 1)
    n_used = (pad_end[-1] // MOE_BM).astype(jnp.int32).reshape(1)
    return dest.astype(jnp.int32), block_e, n_used, nb


def _sc_workers():
    info = plsc.get_sparse_core_info()
    return info.num_cores, info.num_cores * info.num_subcores


def sc_dispatch(rows, dest3, cap, after):
    n, w = rows.shape
    nch, kk, c = dest3.shape
    ncores, nw = _sc_workers()
    per_w = nch // nw
    mesh = plsc.VectorSubcoreMesh(core_axis_name="c", subcore_axis_name="s")

    @functools.partial(
        pl.kernel, mesh=mesh, out_type=jax.ShapeDtypeStruct((cap, w), rows.dtype),
        scratch_types=[pltpu.VMEM((kk, c), jnp.int32), pltpu.VMEM((c, w), rows.dtype)])
    def scatter_rows(h_hbm, d_hbm, after_hbm, o_hbm, idx_v, rows_v):
        del after_hbm
        wid = lax.axis_index("s") * ncores + lax.axis_index("c")

        @pl.loop(0, per_w)
        def _(j):
            ch = wid * per_w + j
            pltpu.sync_copy(d_hbm.at[ch], idx_v)
            pltpu.sync_copy(h_hbm.at[pl.ds(ch * c, c)], rows_v)
            for q in range(kk):
                pltpu.sync_copy(rows_v, o_hbm.at[idx_v.at[q]])

    return scatter_rows(rows, dest3, after)


def sc_combine_gather(y, dest3, n, after):
    cap, d = y.shape
    nch, kk, c = dest3.shape
    ncores, nw = _sc_workers()
    per_w = nch // nw
    mesh = plsc.VectorSubcoreMesh(core_axis_name="c", subcore_axis_name="s")

    @functools.partial(
        pl.kernel, mesh=mesh, out_type=jax.ShapeDtypeStruct((kk, n, d), y.dtype),
        scratch_types=[pltpu.VMEM((kk, c), jnp.int32), pltpu.VMEM((c, d), y.dtype)])
    def gather_rows(y_hbm, d_hbm, after_hbm, o_hbm, idx_v, rows_v):
        del after_hbm
        wid = lax.axis_index("s") * ncores + lax.axis_index("c")

        @pl.loop(0, per_w)
        def _(j):
            ch = wid * per_w + j
            pltpu.sync_copy(d_hbm.at[ch], idx_v)
            for q in range(kk):
                pltpu.sync_copy(y_hbm.at[idx_v.at[q]], rows_v)
                pltpu.sync_copy(rows_v, o_hbm.at[q, pl.ds(ch * c, c)])

    return gather_rows(y, dest3, after)


def kernel(x, c, ctx, c_ctx, ada_w, ada_b, w_in, mla_q_norm_g, mla_kv_norm_g, mla_w_uq, mla_w_ukv,
           sg_norm_g, sg_norm_b, sg_w, sg_b, ret_decay_fwd, ret_decay_bwd, w_o, ln1_g, ln1_b,
           router_w, router_b, w_gate_up, b_gate_up, w_down, b_down, ln2_g, ln2_b):
    b, s_len, d = x.shape
    lc = ctx.shape[1]
    assert d == D_MODEL and lc % TM == 0 and s_len % lc == 0 and s_len % GRID_W == 0
    assert b + 1 <= 8
    t = s_len + lc
    n_lat_tiles = s_len // TM
    params = dict(w_in=w_in, mla_q_norm_g=mla_q_norm_g, mla_kv_norm_g=mla_kv_norm_g, mla_w_uq=mla_w_uq,
                  mla_w_ukv=mla_w_ukv, sg_norm_g=sg_norm_g, sg_norm_b=sg_norm_b, sg_w=sg_w, sg_b=sg_b,
                  ret_decay_fwd=ret_decay_fwd, ret_decay_bwd=ret_decay_bwd, w_o=w_o, ln1_g=ln1_g,
                  ln1_b=ln1_b, router_w=router_w, router_b=router_b, w_gate_up=w_gate_up,
                  b_gate_up=b_gate_up, w_down=w_down, b_down=b_down, ln2_g=ln2_g, ln2_b=ln2_b)
    lws = _layer_weights(params)
    ew = expert_weights(w_gate_up, w_down, lws["deint"][0])
    tab = _rotation_tables(s_len, lc)

    c_rows = jnp.concatenate([c, c_ctx[None, :], jnp.zeros((8 - b - 1, d), F32)], axis=0)
    mod = ada_modulation(c_rows, ada_w, ada_b).reshape(DEPTH, 8, N_MOD, d)
    mod_tab = jnp.stack([mod[:, :b], jnp.broadcast_to(mod[:, b:b + 1], (DEPTH, b, N_MOD, d))], axis=2)

    n_streams = N_STREAMS if b % N_STREAMS == 0 else 1
    bs = b // n_streams
    xs = [jnp.concatenate([x[i * bs:(i + 1) * bs], ctx[i * bs:(i + 1) * bs]], axis=1)
          for i in range(n_streams)]
    order = c_rows
    held = None
    for li in range(DEPTH):
        lw = {k: v[li] for k, v in lws.items()}
        mts = [mod_tab[li, si * bs:(si + 1) * bs] for si in range(n_streams)]
        fronts = []
        for si in range(n_streams):
            fr = _front(xs[si], mts[si], tab, lw, s_len, lc, after=order)
            order = fr["cnt"]
            if si == 0 and held is not None:
                xs[-1] = _finish(*held, gather_after=fr["mixed"], after=order)
                held = None
            fronts.append(fr)
        for si, fr in enumerate(fronts):
            start = fronts[si + 1]["mixed"] if si + 1 < n_streams else fr["cnt"]
            fr["xg"] = sc_dispatch(fr.pop("h2"), fr["dest3"], fr["cap"], start)
        ys = []
        for si in range(n_streams):
            y = expert_ffn(fronts[si]["xg"], fronts[si]["block_e"], fronts[si]["n_used"], ew, li, lw,
                           after=order)
            order = y
            ys.append(y)
        for si in range(n_streams):
            if si == n_streams - 1 and n_streams > 1 and li + 1 < DEPTH:
                held = (fronts[si], ys[si], mts[si], lw, s_len)
            else:
                xs[si] = _finish(fronts[si], ys[si], mts[si], lw, s_len, ys[si], after=order)
                order = xs[si]
    return jnp.concatenate([xi[:, :s_len] for xi in xs], axis=0)


def _front(x_all, mt, tab, lw, s_len, lc, after):
    b, t, d = x_all.shape
    n_lat_tiles = s_len // TM
    q, k, v, sg, retp, rv, a = input_projection(x_all, mt, tab, lw, n_lat_tiles, after)
    mla_lat, mla_ctx = mla_attention(q, k, v, s_len, lc)
    st = retention_scan(a, lw["cd"], s_len // CHUNK)
    x1, h2, idx, gates, rank, cnt = output_projection(
        x_all, mla_lat, mla_ctx, sg, retp, rv, st, mt, lw, n_lat_tiles)
    to_tok = lambda z: z.transpose(2, 0, 1, 3).reshape(TOP_K, b * t)
    dest, block_e, n_used, nb = _route(to_tok(idx), to_tok(rank), cnt[:, 0])
    assert (b * t) % SC_CHUNK == 0
    dest3 = dest.reshape(TOP_K, (b * t) // SC_CHUNK, SC_CHUNK).transpose(1, 0, 2)
    return dict(x1=x1, gates=gates, cnt=cnt, dest3=dest3, block_e=block_e, n_used=n_used,
                h2=h2.reshape(b * t, h2.shape[-1]), cap=nb * MOE_BM, mixed=a)


def _finish(fr, y, mt, lw, s_len, gather_after, after):
    b, t, d = fr["x1"].shape
    yg = sc_combine_gather(y, fr["dest3"], b * t, gather_after).reshape(TOP_K, b, t, y.shape[-1])
    gates_tok = fr["gates"].transpose(0, 1, 3, 2).reshape(b, t, TOP_K)
    return combine_deepnorm2(fr["x1"], yg, gates_tok, mt, lw, s_len // TM, after)
```

```python
import functools

import numpy as np
import jax
import jax.numpy as jnp
from jax import lax
from jax.experimental import pallas as pl
from jax.experimental.pallas import tpu as pltpu
from jax.experimental.pallas import tpu_sc as plsc

F32 = jnp.float32
BF16 = jnp.bfloat16
MXU_DT = BF16
PACK_ROWS = True

D_MODEL = 1024
DEPTH = 4
GRID_W = 64
MLA_HEADS = 4
MLA_NOPE = 128
MLA_ROPE = 64
MLA_V = 128
MLA_Q_LORA = 384
MLA_KV_LORA = 256
MLA_QK = MLA_NOPE + MLA_ROPE
MLA_SCALE = MLA_QK ** -0.5
ROPE_BASE = 10000.0
ROPE_AXIS_FREQS = MLA_ROPE // 4
SG_GROUPS = 4
SG_WIDTH = 256
SG_CHUNK = 128
RET_HEADS = 4
RET_QK = 32
RET_V = 64
RET_CHUNK = 128
RET_ROPE_BASE = 10000.0
N_EXPERTS = 32
TOP_K = 4
D_EXPERT = 1024
SWIGLU_LIMIT = 7.0
SWIGLU_ALPHA = 1.702
N_MOD = 6
LN_EPS = 1e-5
RMS_EPS = 1e-6
DEEPNORM_ALPHA = (2 * DEPTH) ** 0.25
MLA_OUT = MLA_HEADS * MLA_V
RET_OUT = RET_HEADS * RET_V

TM = 256
CHUNK = 128
MOE_BM = 512
SC_CHUNK = 48
N_STREAMS = 2
ATT_TQ = 1024
ATT_TK = 2048
VMEM_LIMIT = 48 * 2 ** 20

_O_CQ, _O_CKV, _O_SGU, _O_SGV = 0, 384, 640, 896
_O_RQ, _O_RK, _O_RQS, _O_RKS, _O_RV, _O_RG, _O_KR = 1152, 1280, 1408, 1536, 1664, 1920, 2176
IN_P = 2304
_T_QC, _T_QS, _T_KCS, _T_RQC, _T_RQS, _T_RKC, _T_RKS = 0, 256, 512, 640, 768, 896, 1024
TAB_W = 1152


def _cparams(sem):
    return pltpu.CompilerParams(dimension_semantics=sem, vmem_limit_bytes=VMEM_LIMIT)


def _dot(a, b):
    return jnp.dot(a, b, preferred_element_type=F32)


def _dot_nt(a, b):
    return lax.dot_general(a, b, (((1,), (1,)), ((), ())), preferred_element_type=F32)


def _mx(a):
    return a.astype(MXU_DT)


def _swap16(j):
    return (j // 32) * 32 + ((j % 32) + 16) % 32


_ERF_ALPHA = (-2.72614225801306e-10, 2.77068142495902e-08, -2.10102402082508e-06,
              -5.69250639462346e-05, -7.34990630326855e-04, -2.95459980854025e-03,
              -1.60960333262415e-02)
_ERF_BETA = (-1.45660718464996e-05, -2.13374055278905e-04, -1.68282697438203e-03,
             -7.37332916720468e-03, -1.42647390514189e-02)


def _erf(x):
    x = jnp.clip(x, -4.0, 4.0)
    x2 = x * x
    p = jnp.full_like(x, _ERF_ALPHA[0])
    for c in _ERF_ALPHA[1:]:
        p = p * x2 + c
    q = jnp.full_like(x, _ERF_BETA[0])
    for c in _ERF_BETA[1:]:
        q = q * x2 + c
    return x * p / q


def _gelu(x):
    return 0.5 * x * (1.0 + _erf(x * 0.7071067811865476))


def _sigmoid(x):
    return 1.0 / (1.0 + jnp.exp(-x))


def _ln(x):
    xc = x - jnp.mean(x, axis=-1, keepdims=True)
    return xc * lax.rsqrt(jnp.mean(xc * xc, axis=-1, keepdims=True) + LN_EPS)


def _lane_group(shape, width):
    return lax.broadcasted_iota(jnp.int32, shape, len(shape) - 1) // width


def _ada_kernel(c_ref, w_ref, b_ref, o_ref):
    c = c_ref[...]
    o_ref[0] = _dot(c * _sigmoid(c), w_ref[0]) + b_ref[0]


def ada_modulation(c_rows, ada_w, ada_b):
    nl, d, n = ada_w.shape
    tn = 1536
    return pl.pallas_call(
        _ada_kernel,
        grid=(nl, n // tn),
        in_specs=[pl.BlockSpec((8, d), lambda l, j: (0, 0)),
                  pl.BlockSpec((1, d, tn), lambda l, j: (l, 0, j)),
                  pl.BlockSpec((1, 1, tn), lambda l, j: (l, 0, j))],
        out_specs=pl.BlockSpec((1, 8, tn), lambda l, j: (l, 0, j)),
        out_shape=jax.ShapeDtypeStruct((nl, 8, n), F32),
        compiler_params=_cparams(("arbitrary", "arbitrary")),
        name="ada_modulation",
    )(c_rows, ada_w, ada_b.reshape(nl, 1, n))


def _inproj_kernel(x_ref, mod_ref, tab_ref, w_in_ref, qg_ref, kvg_ref, w_uq_ref, w_ukv_ref,
                   sgg_ref, sgb_ref, sgw_ref, sgbias_ref, kdec_ref, bd_ref,
                   q_ref, k_ref, v_ref, sg_ref, retp_ref, rv_ref, a_ref):
    x = x_ref[0]
    mod = mod_ref[0, 0]
    h = x * (1.0 + mod[1:2]) + mod[0:1]
    p = _dot(_mx(h), w_in_ref[...])
    tab = tab_ref[...]

    cq = p[:, _O_CQ:_O_CQ + MLA_Q_LORA]
    cq = cq * lax.rsqrt(jnp.mean(cq * cq, axis=-1, keepdims=True) + RMS_EPS) * qg_ref[...]
    qa = _dot(_mx(cq), w_uq_ref[...])
    rot = (qa[:, 512:768] * tab[:, _T_QC:_T_QC + 256]
           + qa[:, 768:1024] * tab[:, _T_QS:_T_QS + 256])
    for hh in range(MLA_HEADS):
        q_ref[0, hh, :, 0:128] = (qa[:, 128 * hh:128 * hh + 128] * MLA_SCALE).astype(q_ref.dtype)
        g = hh // 2
        q_ref[0, hh, :, 128:256] = rot[:, 128 * g:128 * g + 128].astype(q_ref.dtype)

    ckv = p[:, _O_CKV:_O_CKV + MLA_KV_LORA]
    ckv = ckv * lax.rsqrt(jnp.mean(ckv * ckv, axis=-1, keepdims=True) + RMS_EPS) * kvg_ref[...]
    kv = _dot(_mx(ckv), w_ukv_ref[...])
    t = p[:, _O_KR:_O_KR + 128] * tab[:, _T_KCS:_T_KCS + 128]
    u = t + pltpu.roll(t, 64, axis=1)
    low = lax.broadcasted_iota(jnp.int32, u.shape, 1) < 64
    kx = (jnp.where(low, u, 0.0), jnp.where(low, 0.0, u))
    ones_col = jnp.where(lax.broadcasted_iota(jnp.int32, u.shape, 1) == 0, 1.0, 0.0).astype(v_ref.dtype)
    for hh in range(MLA_HEADS):
        k_ref[0, hh, :, 0:128] = kv[:, 256 * hh:256 * hh + 128].astype(k_ref.dtype)
        k_ref[0, hh, :, 128:256] = kx[hh % 2].astype(k_ref.dtype)
        v_ref[0, hh, :, 0:128] = kv[:, 256 * hh + 128:256 * hh + 256].astype(v_ref.dtype)
        v_ref[0, hh, :, 128:256] = ones_col

    gu = _gelu(p[:, _O_SGU:_O_SGU + SG_WIDTH])
    gv = _ln(_gelu(p[:, _O_SGV:_O_SGV + SG_WIDTH])) * sgg_ref[...] + sgb_ref[...]
    gvm = _mx(gv)
    grp = _lane_group((CHUNK, SG_WIDTH), SG_WIDTH // SG_GROUPS)
    for c in range(TM // CHUNK):
        rows = slice(c * CHUNK, (c + 1) * CHUNK)
        res = _dot(sgw_ref[...], gvm[rows])
        mixed = sgbias_ref[...]
        for g in range(SG_GROUPS):
            mixed = mixed + jnp.where(grp == g, res[g * CHUNK:(g + 1) * CHUNK], 0.0)
        sg_ref[0, rows, :] = (gu[rows] * mixed).astype(sg_ref.dtype)

    rq = (p[:, _O_RQ:_O_RQ + 128] * tab[:, _T_RQC:_T_RQC + 128]
          + p[:, _O_RQS:_O_RQS + 128] * tab[:, _T_RQS:_T_RQS + 128])
    rk = (p[:, _O_RK:_O_RK + 128] * tab[:, _T_RKC:_T_RKC + 128]
          + p[:, _O_RKS:_O_RKS + 128] * tab[:, _T_RKS:_T_RKS + 128])
    rv = p[:, _O_RV:_O_RV + RET_OUT]
    retp_ref[0, :, 0:128] = rq
    retp_ref[0, :, 128:256] = rk
    retp_ref[0, :, 256:512] = p[:, _O_RG:_O_RG + RET_OUT]
    rvm = _mx(rv)
    rv_ref[0] = rvm.astype(rv_ref.dtype)
    bd = bd_ref[...]
    for c in range(TM // CHUNK):
        rows = slice(c * CHUNK, (c + 1) * CHUNK)
        for d in range(2):
            kd_t = _mx((rk[rows] * kdec_ref[d]).T)
            af = _dot(kd_t, rvm[rows]) * bd
            a_ref[0, c, d] = (af[0:32] + af[32:64]) + (af[64:96] + af[96:128])


def _ordered_after(kernel_fn, pos):
    def wrapped(*refs):
        return kernel_fn(*refs[:pos], *refs[pos + 1:])
    return wrapped


_ORDER_SPEC = pl.BlockSpec(memory_space=pl.ANY)


def input_projection(x_all, mod_tab, tab, lw, n_lat_tiles, after, x_block=0):
    b = mod_tab.shape[0]
    _, t, d = x_all.shape
    nt = t // TM
    nc = t // CHUNK
    cpt = TM // CHUNK
    const2 = lambda bi, j: (0, 0)
    const3 = lambda bi, j: (0, 0, 0)
    out_shape = (
        jax.ShapeDtypeStruct((b, MLA_HEADS, t, 256), MXU_DT),
        jax.ShapeDtypeStruct((b, MLA_HEADS, t, 256), MXU_DT),
        jax.ShapeDtypeStruct((b, MLA_HEADS, t, 256), MXU_DT),
        jax.ShapeDtypeStruct((b, t, SG_WIDTH), MXU_DT),
        jax.ShapeDtypeStruct((b, t, 512), F32),
        jax.ShapeDtypeStruct((b, t, RET_OUT), MXU_DT),
        jax.ShapeDtypeStruct((b, nc, 2, RET_QK, RET_OUT), F32),
    )
    head_spec = lambda w: pl.BlockSpec((1, MLA_HEADS, TM, w), lambda bi, j: (bi, 0, j, 0))
    tok_spec = lambda w: pl.BlockSpec((1, TM, w), lambda bi, j: (bi, j, 0))
    return pl.pallas_call(
        _ordered_after(_inproj_kernel, 14),
        grid=(b, nt),
        in_specs=[
            pl.BlockSpec((1, TM, d), lambda bi, j: (bi + x_block * b, j, 0)),
            pl.BlockSpec((1, 1, N_MOD, d), lambda bi, j: (bi, j // n_lat_tiles, 0, 0)),
            pl.BlockSpec((TM, TAB_W), lambda bi, j: (j, 0)),
            pl.BlockSpec((d, IN_P), const2),
            pl.BlockSpec((1, MLA_Q_LORA), const2),
            pl.BlockSpec((1, MLA_KV_LORA), const2),
            pl.BlockSpec((MLA_Q_LORA, 1024), const2),
            pl.BlockSpec((MLA_KV_LORA, 1024), const2),
            pl.BlockSpec((1, SG_WIDTH), const2),
            pl.BlockSpec((1, SG_WIDTH), const2),
            pl.BlockSpec((SG_GROUPS * CHUNK, CHUNK), const2),
            pl.BlockSpec((CHUNK, SG_WIDTH), const2),
            pl.BlockSpec((2, CHUNK, 128), const3),
            pl.BlockSpec((128, RET_OUT), const2),
            _ORDER_SPEC,
        ],
        out_specs=(head_spec(256), head_spec(256), head_spec(256), tok_spec(SG_WIDTH),
                   tok_spec(512), tok_spec(RET_OUT),
                   pl.BlockSpec((1, cpt, 2, RET_QK, RET_OUT), lambda bi, j: (bi, j, 0, 0, 0))),
        out_shape=out_shape,
        compiler_params=_cparams(("parallel", "parallel")),
        name="input_projection",
    )(x_all, mod_tab, tab, lw["w_in"], lw["q_g"], lw["kv_g"], lw["w_uq"], lw["w_ukv"],
      lw["sg_g"], lw["sg_b"], lw["sg_w"], lw["sg_bias"], lw["kdec"], lw["bd"], after)


def _attn_kernel(q_ref, k_ref, v_ref, o_ref, *, n_main, tk, tail):
    q = q_ref[0, 0]
    tq = q.shape[0]
    chunks = [(i * tk, tk) for i in range(n_main)] + ([(n_main * tk, tail)] if tail else [])

    def scores(ci):
        start, size = chunks[ci]
        return _dot_nt(q, k_ref[0, 0, start:start + size, :])

    m = jnp.full((tq, 1), -1e30, F32)
    acc = jnp.zeros((tq, 256), F32)
    s_next = scores(0)
    for ci, (start, size) in enumerate(chunks):
        s = s_next
        if ci + 1 < len(chunks):
            s_next = scores(ci + 1)
        m_new = jnp.maximum(m, jnp.max(s, axis=-1, keepdims=True))
        p = jnp.exp(s - m_new)
        acc = jnp.exp(m - m_new) * acc + _dot(_mx(p), v_ref[0, 0, start:start + size, :])
        m = m_new
    o_ref[0] = (acc[:, 0:MLA_V] / acc[:, MLA_V:MLA_V + 1]).astype(o_ref.dtype)


def mla_attention(q, k, v, s_len, lc):
    b, hn, t, _ = q.shape
    tq = min(ATT_TQ, s_len)
    tk = min(ATT_TK, s_len)
    kv_full = pl.BlockSpec((1, 1, t, 256), lambda bi, hi, i: (bi, hi, 0, 0))
    out_lat = pl.pallas_call(
        functools.partial(_attn_kernel, n_main=s_len // tk, tk=tk, tail=lc),
        grid=(b, hn, s_len // tq),
        in_specs=[pl.BlockSpec((1, 1, tq, 256), lambda bi, hi, i: (bi, hi, i, 0)), kv_full, kv_full],
        out_specs=pl.BlockSpec((1, tq, MLA_V), lambda bi, hi, i: (bi, i, hi)),
        out_shape=jax.ShapeDtypeStruct((b, s_len, MLA_OUT), MXU_DT),
        compiler_params=_cparams(("parallel", "parallel", "arbitrary")),
        name="mla_attention_latent",
    )(q, k, v)
    cblk = s_len // lc
    ctx_spec = pl.BlockSpec((1, 1, lc, 256), lambda bi, hi: (bi, hi, cblk, 0))
    out_ctx = pl.pallas_call(
        functools.partial(_attn_kernel, n_main=0, tk=tk, tail=lc),
        grid=(b, hn),
        in_specs=[ctx_spec, ctx_spec, ctx_spec],
        out_specs=pl.BlockSpec((1, lc, MLA_V), lambda bi, hi: (bi, 0, hi)),
        out_shape=jax.ShapeDtypeStruct((b, lc, MLA_OUT), MXU_DT),
        compiler_params=_cparams(("parallel", "parallel")),
        name="mla_attention_context",
    )(q, k, v)
    return out_lat, out_ctx


def _ret_scan_kernel(a_ref, cd_ref, s_ref, *, n_lat_chunks):
    nc = a_ref.shape[1]
    ncc = nc - n_lat_chunks
    cd_f, cd_b = cd_ref[0], cd_ref[1]

    def body(n, carry):
        sf, sb = carry
        cf = jnp.where(n < ncc, n_lat_chunks + n, n - ncc)
        cb = jnp.where(n < ncc, nc - 1 - n, n_lat_chunks - 1 - (n - ncc))
        s_ref[0, cf, 0] = sf
        s_ref[0, cb, 1] = sb
        return sf * cd_f + a_ref[0, cf, 0], sb * cd_b + a_ref[0, cb, 1]

    zero = jnp.zeros((RET_QK, RET_OUT), F32)
    lax.fori_loop(0, nc, body, (zero, zero))


def retention_scan(a, cd, n_lat_chunks):
    b, nc = a.shape[:2]
    blk = pl.BlockSpec((1, nc, 2, RET_QK, RET_OUT), lambda bi: (bi, 0, 0, 0, 0))
    return pl.pallas_call(
        functools.partial(_ret_scan_kernel, n_lat_chunks=n_lat_chunks),
        grid=(b,),
        in_specs=[blk, pl.BlockSpec((2, 1, RET_OUT), lambda bi: (0, 0, 0))],
        out_specs=blk,
        out_shape=jax.ShapeDtypeStruct(a.shape, F32),
        compiler_params=_cparams(("parallel",)),
        name="retention_scan",
    )(a, cd)


def _split_dot(x, ones2):
    hi = x.astype(BF16)
    lo = (x - hi.astype(F32)).astype(BF16)
    return _dot(jnp.concatenate([hi, lo], axis=1), ones2)


def _outproj_kernel(x_ref, mlal_ref, mlac_ref, sg_ref, retp_ref, rv_ref, st_ref, mod_ref,
                    m_ref, qdec_ref, bd_ref, seg_ref, w_o_ref, lng_ref, lnb_ref, rw_ref, rb_ref, tri_ref,
                    x1_ref, h2_ref, idx_ref, gate_ref, rank_ref, cnt_ref, carry_ref, *, n_lat_tiles):
    @pl.when(pl.program_id(0) == 0)
    def _():
        carry_ref[...] = jnp.zeros_like(carry_ref)

    nbat = x_ref.shape[0]
    units = [(bb, c) for bb in range(nbat) for c in range(TM // CHUNK)]
    rows = lambda c: slice(c * CHUNK, (c + 1) * CHUNK)
    g32 = _lane_group((CHUNK, 128), RET_QK)
    g64 = _lane_group((CHUNK, RET_OUT), RET_V)
    bd = bd_ref[...]
    seg2 = jnp.concatenate([seg_ref[...], seg_ref[...]], axis=0)
    m4 = jnp.concatenate([m_ref[hh] for hh in range(RET_HEADS)], axis=0)

    rq, s4 = {}, {}
    for u in units:
        bb, c = u
        rq[u] = retp_ref[bb, rows(c), 0:128]
        q4 = jnp.concatenate([jnp.where(g32 == hh, rq[u], 0.0) for hh in range(RET_HEADS)], axis=0)
        s4[u] = _dot_nt(_mx(q4), _mx(retp_ref[bb, rows(c), 128:256]))
    o = {}
    for u in units:
        bb, c = u
        r = _dot(_mx(s4[u] * m4), rv_ref[bb, rows(c), :])
        qd = jnp.concatenate([rq[u] * qdec_ref[0], rq[u] * qdec_ref[1]], axis=1)
        st2 = jnp.concatenate([jnp.concatenate([st_ref[bb, c, dd]] * RET_HEADS, axis=0) * bd
                               for dd in range(2)], axis=0)
        acc = _dot(_mx(qd), _mx(st2))
        for hh in range(RET_HEADS):
            acc = acc + jnp.where(g64 == hh, r[hh * CHUNK:(hh + 1) * CHUNK], 0.0)
        o[u] = acc
    oc = {}
    for u in units:
        oc[u] = o[u] - _split_dot(o[u], seg2) * (1.0 / RET_V)
    ret = {}
    for u in units:
        bb, c = u
        var = _split_dot(oc[u] * oc[u], seg2) * (1.0 / RET_V)
        rg = retp_ref[bb, rows(c), 256:512]
        ret[u] = _mx(oc[u] * lax.rsqrt(var + LN_EPS) * (rg * _sigmoid(rg)))

    is_ctx = pl.program_id(0) >= n_lat_tiles
    ys = []
    for bb in range(nbat):
        mla = jnp.where(is_ctx, mlac_ref[bb], mlal_ref[bb])
        cat = jnp.concatenate(
            [mla, sg_ref[bb], jnp.concatenate([ret[(bb, c)] for c in range(TM // CHUNK)], axis=0)], axis=1)
        ys.append(_dot(cat, w_o_ref[...]))
    logits = []
    for bb in range(nbat):
        mod = mod_ref[bb, 0]
        x1 = _ln(DEEPNORM_ALPHA * x_ref[bb] + mod[2:3] * ys[bb]) * lng_ref[...] + lnb_ref[...]
        x1_ref[bb] = x1
        h2 = x1 * (1.0 + mod[4:5]) + mod[3:4]
        h2_ref[bb] = _pack_bf16_pairs(h2) if PACK_ROWS else h2
        h2_hi = h2.astype(BF16)
        h2_lo = (h2 - h2_hi.astype(F32)).astype(BF16)
        r2 = _dot_nt(rw_ref[...], jnp.concatenate([h2_hi, h2_lo], axis=0))
        logits.append(r2[0:N_EXPERTS, 0:TM] + r2[N_EXPERTS:, 0:TM] + r2[0:N_EXPERTS, TM:] + rb_ref[...])

    work = jnp.concatenate(logits, axis=1)
    e_iota = lax.broadcasted_iota(jnp.int32, work.shape, 0).astype(F32)
    vals, idxs = [], []
    for _ in range(TOP_K):
        mval = jnp.max(work, axis=0, keepdims=True)
        midx = jnp.min(jnp.where(work == mval, e_iota, float(N_EXPERTS)), axis=0, keepdims=True)
        vals.append(mval)
        idxs.append(midx)
        work = jnp.where(e_iota == midx, -jnp.inf, work)
    ex = [jnp.exp(vv - vals[0]) for vv in vals]
    den = ex[0] + ex[1] + ex[2] + ex[3]
    onehot = jnp.zeros_like(work)
    for kk in range(TOP_K):
        onehot = onehot + jnp.where(e_iota == idxs[kk], 1.0, 0.0)
    cols = lambda bb: slice(bb * TM, (bb + 1) * TM)
    prefixes = [_dot(onehot[:, cols(bb)].astype(BF16), tri_ref[...]) for bb in range(nbat)]
    count = carry_ref[:, 0:1]
    bases = []
    for bb in range(nbat):
        bases.append(count + prefixes[bb])
        count = count + jnp.sum(onehot[:, cols(bb)], axis=1, keepdims=True)
    base = jnp.concatenate(bases, axis=1)
    for kk in range(TOP_K):
        gate = ex[kk] / den
        rank = jnp.sum(jnp.where(e_iota == idxs[kk], base, 0.0), axis=0, keepdims=True).astype(jnp.int32)
        for bb in range(nbat):
            gate_ref[bb, 0, kk:kk + 1, :] = gate[:, cols(bb)]
            idx_ref[bb, 0, kk:kk + 1, :] = idxs[kk][:, cols(bb)].astype(jnp.int32)
            rank_ref[bb, 0, kk:kk + 1, :] = rank[:, cols(bb)]
    carry_ref[...] = jnp.broadcast_to(count, carry_ref.shape)
    cnt_ref[...] = carry_ref[...].astype(jnp.int32)


def output_projection(x_all, mla_lat, mla_ctx, sg, retp, rv, st, mod_tab, lw, n_lat_tiles, x_block=0):
    b = mod_tab.shape[0]
    _, t, d = x_all.shape
    nt = t // TM
    cpt = TM // CHUNK
    const2 = lambda j: (0, 0)
    const3 = lambda j: (0, 0, 0)
    tok_spec = lambda w: pl.BlockSpec((b, TM, w), lambda j: (0, j, 0))
    route_spec = pl.BlockSpec((b, 1, TOP_K, TM), lambda j: (0, j, 0, 0))
    route_shape = lambda dt: jax.ShapeDtypeStruct((b, nt, TOP_K, TM), dt)
    h2w = d // 2 if PACK_ROWS else d
    return pl.pallas_call(
        functools.partial(_outproj_kernel, n_lat_tiles=n_lat_tiles),
        grid=(nt,),
        in_specs=[
            pl.BlockSpec((b, TM, d), lambda j: (x_block, j, 0)),
            pl.BlockSpec((b, TM, MLA_OUT), lambda j: (0, jnp.minimum(j, n_lat_tiles - 1), 0)),
            pl.BlockSpec((b, TM, MLA_OUT), lambda j: (0, jnp.maximum(j - n_lat_tiles, 0), 0)),
            tok_spec(SG_WIDTH), tok_spec(512), tok_spec(RET_OUT),
            pl.BlockSpec((b, cpt, 2, RET_QK, RET_OUT), lambda j: (0, j, 0, 0, 0)),
            pl.BlockSpec((b, 1, N_MOD, d), lambda j: (0, j // n_lat_tiles, 0, 0)),
            pl.BlockSpec((RET_HEADS, CHUNK, CHUNK), const3),
            pl.BlockSpec((2, CHUNK, 128), const3),
            pl.BlockSpec((128, RET_OUT), const2),
            pl.BlockSpec((RET_OUT, RET_OUT), const2),
            pl.BlockSpec((d, d), const2),
            pl.BlockSpec((1, d), const2),
            pl.BlockSpec((1, d), const2),
            pl.BlockSpec((2 * N_EXPERTS, d), const2),
            pl.BlockSpec((N_EXPERTS, 1), const2),
            pl.BlockSpec((TM, TM), const2),
        ],
        out_specs=(tok_spec(d), tok_spec(h2w), route_spec, route_spec, route_spec,
                   pl.BlockSpec((N_EXPERTS, 128), const2)),
        out_shape=(jax.ShapeDtypeStruct((b, t, d), F32),
                   jax.ShapeDtypeStruct((b, t, h2w), jnp.uint32 if PACK_ROWS else F32),
                   route_shape(jnp.int32), route_shape(F32), route_shape(jnp.int32),
                   jax.ShapeDtypeStruct((N_EXPERTS, 128), jnp.int32)),
        scratch_shapes=[pltpu.VMEM((N_EXPERTS, 128), F32)],
        compiler_params=_cparams(("arbitrary",)),
        name="output_projection",
    )(x_all, mla_lat, mla_ctx, sg, retp, rv, st, mod_tab, lw["ret_m"], lw["qdec"], lw["bd"], lw["seg"],
      lw["w_o"], lw["ln1_g"], lw["ln1_b"], lw["router_w"], lw["router_b"], lw["tri"])


_DEINT = 256


def _expert_weights_kernel(wgu_ref, wd_ref, perm_ref, wg_ref, wl_ref, wdo_ref):
    half = _DEINT // 2
    for c in range(2 * D_EXPERT // _DEINT):
        r = _dot(_mx(wgu_ref[0, 0, :, _DEINT * c:_DEINT * (c + 1)]), perm_ref[...])
        wg_ref[0, 0, :, half * c:half * (c + 1)] = r[:, :half].astype(wg_ref.dtype)
        wl_ref[0, 0, :, half * c:half * (c + 1)] = r[:, half:].astype(wl_ref.dtype)
    wdo_ref[0, 0] = wd_ref[0, 0].astype(wdo_ref.dtype)


def expert_weights(w_gate_up, w_down, deint):
    nl, ne, d, de2 = w_gate_up.shape
    de = de2 // 2
    blk = lambda r, c: pl.BlockSpec((1, 1, r, c), lambda l, e: (l, e, 0, 0))
    return pl.pallas_call(
        _expert_weights_kernel,
        grid=(nl, ne),
        in_specs=[blk(d, de2), blk(de, d), pl.BlockSpec((_DEINT, _DEINT), lambda l, e: (0, 0))],
        out_specs=(blk(d, de), blk(d, de), blk(de, d)),
        out_shape=(jax.ShapeDtypeStruct((nl, ne, d, de), MXU_DT), jax.ShapeDtypeStruct((nl, ne, d, de), MXU_DT),
                   jax.ShapeDtypeStruct((nl, ne, de, d), MXU_DT)),
        compiler_params=_cparams(("parallel", "parallel")),
        name="expert_weights",
    )(w_gate_up, w_down, deint)


def _pack_bf16_pairs(v):
    bits = lax.bitcast_convert_type(v.astype(BF16).astype(F32), jnp.uint32)
    half = bits.shape[1] // 2
    return bits[:, :half] | (bits[:, half:] >> 16)


def _unpack_bf16_pairs(w):
    return (lax.bitcast_convert_type(w & jnp.uint32(0xFFFF0000), F32),
            lax.bitcast_convert_type(w << 16, F32))


def _expert_kernel(be_ref, nu_ref, x_ref, wg_s, wl_s, bg_ref, bl_ref, wd_s, bd_ref, y_ref):
    del be_ref
    active = pl.program_id(0) < nu_ref[0]

    @pl.when(active)
    def _():
        if PACK_ROWS:
            hi, lo = _unpack_bf16_pairs(x_ref[...])
            xb = jnp.concatenate([hi.astype(BF16), lo.astype(BF16)], axis=1)
        else:
            xb = x_ref[...]
        glu = jnp.minimum(_dot(xb, wg_s[0, 0]) + bg_ref[0], SWIGLU_LIMIT)
        lin = jnp.clip(_dot(xb, wl_s[0, 0]) + bl_ref[0], -SWIGLU_LIMIT, SWIGLU_LIMIT)
        act = glu * _sigmoid(SWIGLU_ALPHA * glu) * (lin + 1.0)
        y = _dot(_mx(act), wd_s[0, 0]) + bd_ref[0]
        y_ref[...] = _pack_bf16_pairs(y) if PACK_ROWS else y

    @pl.when(jnp.logical_not(active))
    def _():
        y_ref[...] = jnp.zeros_like(y_ref)


def expert_ffn(xg, block_e, n_used, ew, li, lw, after):
    cap, xw = xg.shape
    d = D_MODEL
    nb = cap // MOE_BM
    de = D_EXPERT
    xmap = lambda i, be, nu: (jnp.minimum(i, nu[0] - 1), 0)
    wmap = lambda i, be, nu: (be[i], 0, 0)
    lmap = lambda i, be, nu: (li, be[i], 0, 0)
    grid_spec = pltpu.PrefetchScalarGridSpec(
        num_scalar_prefetch=2,
        grid=(nb,),
        in_specs=[pl.BlockSpec((MOE_BM, xw), xmap),
                  pl.BlockSpec((1, 1, d, de), lmap),
                  pl.BlockSpec((1, 1, d, de), lmap),
                  pl.BlockSpec((1, 1, de), wmap), pl.BlockSpec((1, 1, de), wmap),
                  pl.BlockSpec((1, 1, de, d), lmap), pl.BlockSpec((1, 1, d), wmap),
                  _ORDER_SPEC],
        out_specs=pl.BlockSpec((MOE_BM, xw), lambda i, be, nu: (i, 0)),
    )
    return pl.pallas_call(
        _ordered_after(_expert_kernel, 9),
        grid_spec=grid_spec,
        out_shape=jax.ShapeDtypeStruct((cap, xw), xg.dtype),
        compiler_params=_cparams(("arbitrary",)),
        name="expert_ffn",
    )(block_e, n_used, xg, ew[0], ew[1], lw["b_glu"], lw["b_lin"], ew[2], lw["b_down"], after)


def _combine_ln2_kernel(x_ref, y_ref, gate_ref, mod_ref, g_ref, b_ref, o_ref):
    for bb in range(x_ref.shape[0]):
        gates = gate_ref[bb]
        if PACK_ROWS:
            f_hi, f_lo = 0.0, 0.0
            for kk in range(TOP_K):
                hi, lo = _unpack_bf16_pairs(y_ref[kk, bb])
                f_hi = f_hi + gates[:, kk:kk + 1] * hi
                f_lo = f_lo + gates[:, kk:kk + 1] * lo
            f = jnp.concatenate([f_hi, f_lo], axis=1)
        else:
            f = gates[:, 0:1] * y_ref[0, bb]
            for kk in range(1, TOP_K):
                f = f + gates[:, kk:kk + 1] * y_ref[kk, bb]
        mod = mod_ref[bb, 0]
        o_ref[bb] = _ln(DEEPNORM_ALPHA * x_ref[bb] + mod[5:6] * f) * g_ref[...] + b_ref[...]


def combine_deepnorm2(x1, yg, gates, mod_tab, lw, n_lat_tiles, after):
    b, t, d = x1.shape
    tok = pl.BlockSpec((b, TM, d), lambda j: (0, j, 0))
    vec = pl.BlockSpec((1, d), lambda j: (0, 0))
    return pl.pallas_call(
        _ordered_after(_combine_ln2_kernel, 6),
        grid=(t // TM,),
        in_specs=[tok,
                  pl.BlockSpec((TOP_K, b, TM, yg.shape[-1]), lambda j: (0, 0, j, 0)),
                  pl.BlockSpec((b, TM, TOP_K), lambda j: (0, j, 0)),
                  pl.BlockSpec((b, 1, N_MOD, d), lambda j: (0, j // n_lat_tiles, 0, 0)), vec, vec,
                  _ORDER_SPEC],
        out_specs=tok,
        out_shape=jax.ShapeDtypeStruct((b, t, d), F32),
        compiler_params=_cparams(("parallel",)),
        name="combine_deepnorm2",
    )(x1, yg, gates, mod_tab, lw["ln2_g"], lw["ln2_b"], after)


def _rotation_tables(s_len, lc):
    rows = s_len // GRID_W
    row = jnp.broadcast_to(jnp.arange(rows, dtype=F32)[:, None], (rows, GRID_W)).reshape(-1)
    col = jnp.broadcast_to(jnp.arange(GRID_W, dtype=F32)[None, :], (rows, GRID_W)).reshape(-1)
    inv = ROPE_BASE ** (-jnp.arange(ROPE_AXIS_FREQS, dtype=F32) / ROPE_AXIS_FREQS)
    ar, ac = row[:, None] * inv, col[:, None] * inv
    c64 = jnp.concatenate([jnp.cos(ar), jnp.cos(ar), jnp.cos(ac), jnp.cos(ac)], axis=1)
    s64 = jnp.concatenate([-jnp.sin(ar), jnp.sin(ar), -jnp.sin(ac), jnp.sin(ac)], axis=1)
    c64 = jnp.concatenate([c64, jnp.ones((lc, 64), F32)], axis=0)
    s64 = jnp.concatenate([s64, jnp.zeros((lc, 64), F32)], axis=0)
    half = RET_QK // 2
    pos = jnp.concatenate([lc + jnp.arange(s_len, dtype=F32), jnp.arange(lc, dtype=F32)])
    inv_r = 1.0 / (RET_ROPE_BASE ** jnp.linspace(0.0, 1.0, half, dtype=F32))
    ang = pos[:, None] * inv_r
    rc = jnp.tile(jnp.concatenate([jnp.cos(ang), jnp.cos(ang)], axis=1), (1, RET_HEADS))
    rs = jnp.tile(jnp.concatenate([-jnp.sin(ang), jnp.sin(ang)], axis=1), (1, RET_HEADS))
    qs = RET_QK ** -0.5
    return jnp.concatenate([
        jnp.tile(c64, (1, MLA_HEADS)) * MLA_SCALE, jnp.tile(s64, (1, MLA_HEADS)) * MLA_SCALE,
        c64, s64, rc * qs, rs * qs, rc, rs], axis=1)


def _in_perm():
    a = np.arange
    return np.concatenate([
        a(0, 640), a(704, 1216), a(1216, 1344), a(1344, 1472),
        1216 + _swap16(a(128)), 1344 + _swap16(a(128)), a(1472, 1984),
        640 + a(64), 640 + _swap16(a(64))])


def _uq_perm():
    a = np.arange
    nope = [h * MLA_QK + a(MLA_NOPE) for h in range(MLA_HEADS)]
    rope = [h * MLA_QK + MLA_NOPE + a(MLA_ROPE) for h in range(MLA_HEADS)]
    part = [h * MLA_QK + MLA_NOPE + _swap16(a(MLA_ROPE)) for h in range(MLA_HEADS)]
    return np.concatenate(nope + rope + part)


def _layer_weights(p):
    nl = p["w_in"].shape[0]
    lgf = jax.nn.log_sigmoid(p["ret_decay_fwd"].astype(F32))
    lgb = jax.nn.log_sigmoid(p["ret_decay_bwd"].astype(F32))
    h128 = np.arange(128) // RET_QK
    h256 = np.arange(RET_OUT) // RET_V
    a = jnp.arange(CHUNK, dtype=F32)[None, :, None]
    lf, lb = lgf[:, h128][:, None, :], lgb[:, h128][:, None, :]
    kdec = jnp.stack([jnp.exp(lf * (CHUNK - 1.0 - a)), jnp.exp(lb * a)], axis=1)
    qdec = jnp.stack([jnp.exp(lf * (a + 1.0)), jnp.exp(lb * (CHUNK - a))], axis=1)
    i = jnp.arange(CHUNK, dtype=F32)[:, None]
    j = jnp.arange(CHUNK, dtype=F32)[None, :]
    dif = (i - j)[None, None]
    ret_m = jnp.where(dif >= 0, jnp.exp(lgf[:, :, None, None] * jnp.maximum(dif, 0.0)),
                      jnp.exp(lgb[:, :, None, None] * jnp.maximum(-dif, 0.0)))
    cd = jnp.stack([jnp.exp(lgf[:, h256] * CHUNK), jnp.exp(lgb[:, h256] * CHUNK)], axis=1)[:, :, None, :]
    bd = (h128[:, None] == h256[None, :]).astype(np.float32)
    seg = (h256[:, None] == h256[None, :]).astype(np.float32)
    tri = (np.arange(TM)[:, None] < np.arange(TM)[None, :]).astype(np.float32)
    jj = np.arange(_DEINT // 2)
    deint = np.zeros((_DEINT, _DEINT), np.float32)
    deint[2 * jj, jj] = 1.0
    deint[2 * jj + 1, _DEINT // 2 + jj] = 1.0
    rw_t = jnp.swapaxes(p["router_w"], 1, 2)
    rw_hi = rw_t.astype(BF16)
    rw_lo = (rw_t - rw_hi.astype(F32)).astype(BF16)
    sg_bias = jnp.repeat(jnp.swapaxes(p["sg_b"], 1, 2), SG_WIDTH // SG_GROUPS, axis=2)
    bgu = p["b_gate_up"]
    return {
        "w_in": p["w_in"][:, :, _in_perm()].astype(MXU_DT),
        "q_g": p["mla_q_norm_g"][:, None, :], "kv_g": p["mla_kv_norm_g"][:, None, :],
        "w_uq": p["mla_w_uq"][:, :, _uq_perm()].astype(MXU_DT),
        "w_ukv": p["mla_w_ukv"].astype(MXU_DT),
        "sg_g": p["sg_norm_g"][:, None, :], "sg_b": p["sg_norm_b"][:, None, :],
        "sg_w": p["sg_w"].reshape(nl, SG_GROUPS * CHUNK, CHUNK).astype(MXU_DT),
        "sg_bias": sg_bias,
        "kdec": kdec, "qdec": qdec, "ret_m": ret_m, "cd": cd,
        "bd": jnp.broadcast_to(jnp.asarray(bd), (nl,) + bd.shape),
        "seg": jnp.broadcast_to(jnp.asarray(seg, BF16), (nl,) + seg.shape),
        "tri": jnp.broadcast_to(jnp.asarray(tri, BF16), (nl,) + tri.shape),
        "w_o": p["w_o"].astype(MXU_DT),
        "ln1_g": p["ln1_g"][:, None, :], "ln1_b": p["ln1_b"][:, None, :],
        "ln2_g": p["ln2_g"][:, None, :], "ln2_b": p["ln2_b"][:, None, :],
        "router_w": jnp.concatenate([rw_hi, rw_lo], axis=1),
        "router_b": p["router_b"][:, :, None],
        "b_glu": bgu[:, :, None, 0::2], "b_lin": bgu[:, :, None, 1::2],
        "b_down": p["b_down"][:, :, None, :],
        "deint": jnp.broadcast_to(jnp.asarray(deint, MXU_DT), (nl,) + deint.shape),
    }


def _route(idx, rank, counts):
    n_assign = idx.shape[1] * TOP_K
    nb = -(-(n_assign + N_EXPERTS * (MOE_BM - 1)) // MOE_BM)
    padded = (counts + MOE_BM - 1) // MOE_BM * MOE_BM
    pad_end = jnp.cumsum(padded)
    pad_start = pad_end - padded
    experts = jnp.arange(N_EXPERTS, dtype=jnp.int32)
    dest = rank + jnp.sum(jnp.where(idx[..., None] == experts, pad_start, 0), axis=-1)
    blk_start = jnp.arange(nb, dtype=jnp.int32) * MOE_BM
    block_e = jnp.minimum(jnp.sum((pad_end[None, :] <= blk_start[:, None]).astype(jnp.int32), axis=1),
                          N_EXPERTS - 1)
    n_used = (pad_end[-1] // MOE_BM).astype(jnp.int32).reshape(1)
    return dest.astype(jnp.int32), block_e, n_used, nb


def _sc_workers():
    info = plsc.get_sparse_core_info()
    return info.num_cores, info.num_cores * info.num_subcores


def sc_dispatch(rows, dest3, cap, after):
    n, w = rows.shape
    nch, kk, c = dest3.shape
    ncores, nw = _sc_workers()
    assert nch % nw == 0, "token chunks must split evenly over the vector subcores"
    per_w = nch // nw
    mesh = plsc.VectorSubcoreMesh(core_axis_name="c", subcore_axis_name="s")

    @functools.partial(
        pl.kernel, mesh=mesh, out_type=jax.ShapeDtypeStruct((cap, w), rows.dtype),
        scratch_types=[pltpu.VMEM((kk, c), jnp.int32), pltpu.VMEM((c, w), rows.dtype)])
    def scatter_rows(h_hbm, d_hbm, after_hbm, o_hbm, idx_v, rows_v):
        del after_hbm
        wid = lax.axis_index("s") * ncores + lax.axis_index("c")

        @pl.loop(0, per_w)
        def _(j):
            ch = wid * per_w + j
            pltpu.sync_copy(d_hbm.at[ch], idx_v)
            pltpu.sync_copy(h_hbm.at[pl.ds(ch * c, c)], rows_v)
            for q in range(kk):
                pltpu.sync_copy(rows_v, o_hbm.at[idx_v.at[q]])

    return scatter_rows(rows, dest3, after)


def sc_combine_gather(y, dest3, n, after):
    cap, d = y.shape
    nch, kk, c = dest3.shape
    ncores, nw = _sc_workers()
    assert nch % nw == 0, "token chunks must split evenly over the vector subcores"
    per_w = nch // nw
    mesh = plsc.VectorSubcoreMesh(core_axis_name="c", subcore_axis_name="s")

    @functools.partial(
        pl.kernel, mesh=mesh, out_type=jax.ShapeDtypeStruct((kk, n, d), y.dtype),
        scratch_types=[pltpu.VMEM((kk, c), jnp.int32), pltpu.VMEM((c, d), y.dtype)])
    def gather_rows(y_hbm, d_hbm, after_hbm, o_hbm, idx_v, rows_v):
        del after_hbm
        wid = lax.axis_index("s") * ncores + lax.axis_index("c")

        @pl.loop(0, per_w)
        def _(j):
            ch = wid * per_w + j
            pltpu.sync_copy(d_hbm.at[ch], idx_v)
            for q in range(kk):
                pltpu.sync_copy(y_hbm.at[idx_v.at[q]], rows_v)
                pltpu.sync_copy(rows_v, o_hbm.at[q, pl.ds(ch * c, c)])

    return gather_rows(y, dest3, after)


def kernel(x, c, ctx, c_ctx, ada_w, ada_b, w_in, mla_q_norm_g, mla_kv_norm_g, mla_w_uq, mla_w_ukv,
           sg_norm_g, sg_norm_b, sg_w, sg_b, ret_decay_fwd, ret_decay_bwd, w_o, ln1_g, ln1_b,
           router_w, router_b, w_gate_up, b_gate_up, w_down, b_down, ln2_g, ln2_b):
    b, s_len, d = x.shape
    lc = ctx.shape[1]
    assert d == D_MODEL and lc % TM == 0 and s_len % lc == 0 and s_len % GRID_W == 0
    assert b + 1 <= 8
    t = s_len + lc
    n_lat_tiles = s_len // TM
    params = dict(w_in=w_in, mla_q_norm_g=mla_q_norm_g, mla_kv_norm_g=mla_kv_norm_g, mla_w_uq=mla_w_uq,
                  mla_w_ukv=mla_w_ukv, sg_norm_g=sg_norm_g, sg_norm_b=sg_norm_b, sg_w=sg_w, sg_b=sg_b,
                  ret_decay_fwd=ret_decay_fwd, ret_decay_bwd=ret_decay_bwd, w_o=w_o, ln1_g=ln1_g,
                  ln1_b=ln1_b, router_w=router_w, router_b=router_b, w_gate_up=w_gate_up,
                  b_gate_up=b_gate_up, w_down=w_down, b_down=b_down, ln2_g=ln2_g, ln2_b=ln2_b)
    lws = _layer_weights(params)
    ew = expert_weights(w_gate_up, w_down, lws["deint"][0])
    tab = _rotation_tables(s_len, lc)

    c_rows = jnp.concatenate([c, c_ctx[None, :], jnp.zeros((8 - b - 1, d), F32)], axis=0)
    mod = ada_modulation(c_rows, ada_w, ada_b).reshape(DEPTH, 8, N_MOD, d)
    mod_tab = jnp.stack([mod[:, :b], jnp.broadcast_to(mod[:, b:b + 1], (DEPTH, b, N_MOD, d))], axis=2)

    n_streams = N_STREAMS if b % N_STREAMS == 0 else 1
    bs = b // n_streams
    xs = [jnp.concatenate([x, ctx], axis=1)] * n_streams
    order = c_rows
    held = None
    for li in range(DEPTH):
        lw = {k: v[li] for k, v in lws.items()}
        mts = [mod_tab[li, si * bs:(si + 1) * bs] for si in range(n_streams)]
        fronts = []
        for si in range(n_streams):
            fr = _front(xs[si], si if li == 0 else 0, mts[si], tab, lw, s_len, lc, after=order)
            order = fr["cnt"]
            if si == 0 and held is not None:
                xs[-1] = _finish(*held, gather_after=fr["mixed"], after=order)
                held = None
            fronts.append(fr)
        for si, fr in enumerate(fronts):
            start = fronts[si + 1]["mixed"] if si + 1 < n_streams else fr["cnt"]
            fr["xg"] = sc_dispatch(fr.pop("h2"), fr["dest3"], fr["cap"], start)
        ys = []
        for si in range(n_streams):
            y = expert_ffn(fronts[si]["xg"], fronts[si]["block_e"], fronts[si]["n_used"], ew, li, lw,
                           after=order)
            order = y
            ys.append(y)
        for si in range(n_streams):
            if si == n_streams - 1 and n_streams > 1 and li + 1 < DEPTH:
                held = (fronts[si], ys[si], mts[si], lw, s_len)
            else:
                xs[si] = _finish(fronts[si], ys[si], mts[si], lw, s_len, ys[si], after=order)
                order = xs[si]
    return jnp.concatenate([xi[:, :s_len] for xi in xs], axis=0)


def _front(x_all, x_block, mt, tab, lw, s_len, lc, after):
    b = mt.shape[0]
    t = x_all.shape[1]
    n_lat_tiles = s_len // TM
    q, k, v, sg, retp, rv, a = input_projection(x_all, mt, tab, lw, n_lat_tiles, after, x_block)
    mla_lat, mla_ctx = mla_attention(q, k, v, s_len, lc)
    st = retention_scan(a, lw["cd"], s_len // CHUNK)
    x1, h2, idx, gates, rank, cnt = output_projection(
        x_all, mla_lat, mla_ctx, sg, retp, rv, st, mt, lw, n_lat_tiles, x_block)
    to_tok = lambda z: z.transpose(2, 0, 1, 3).reshape(TOP_K, b * t)
    dest, block_e, n_used, nb = _route(to_tok(idx), to_tok(rank), cnt[:, 0])
    assert (b * t) % SC_CHUNK == 0
    dest3 = dest.reshape(TOP_K, (b * t) // SC_CHUNK, SC_CHUNK).transpose(1, 0, 2)
    return dict(x1=x1, gates=gates, cnt=cnt, dest3=dest3, block_e=block_e, n_used=n_used,
                h2=h2.reshape(b * t, h2.shape[-1]), cap=nb * MOE_BM, mixed=a)


def _finish(fr, y, mt, lw, s_len, gather_after, after):
    b, t, d = fr["x1"].shape
    yg = sc_combine_gather(y, fr["dest3"], b * t, gather_after).reshape(TOP_K, b, t, y.shape[-1])
    gates_tok = fr["gates"].transpose(0, 1, 3, 2).reshape(b, t, TOP_K)
    return combine_deepnorm2(fr["x1"], yg, gates_tok, mt, lw, s_len // TM, after)
```

```python
import functools

import numpy as np
import jax
import jax.numpy as jnp
from jax import lax
from jax.experimental import pallas as pl
from jax.experimental.pallas import tpu as pltpu
from jax.experimental.pallas import tpu_sc as plsc

F32 = jnp.float32
BF16 = jnp.bfloat16
MXU_DT = BF16
PACK_ROWS = True

D_MODEL = 1024
DEPTH = 4
GRID_W = 64
MLA_HEADS = 4
MLA_NOPE = 128
MLA_ROPE = 64
MLA_V = 128
MLA_Q_LORA = 384
MLA_KV_LORA = 256
MLA_QK = MLA_NOPE + MLA_ROPE
MLA_SCALE = MLA_QK ** -0.5
ROPE_BASE = 10000.0
ROPE_AXIS_FREQS = MLA_ROPE // 4
SG_GROUPS = 4
SG_WIDTH = 256
SG_CHUNK = 128
RET_HEADS = 4
RET_QK = 32
RET_V = 64
RET_CHUNK = 128
RET_ROPE_BASE = 10000.0
N_EXPERTS = 32
TOP_K = 4
D_EXPERT = 1024
SWIGLU_LIMIT = 7.0
SWIGLU_ALPHA = 1.702
N_MOD = 6
LN_EPS = 1e-5
RMS_EPS = 1e-6
DEEPNORM_ALPHA = (2 * DEPTH) ** 0.25
MLA_OUT = MLA_HEADS * MLA_V
RET_OUT = RET_HEADS * RET_V

TM = 256
CHUNK = 128
MOE_BM = 512
SC_CHUNK = 48
N_STREAMS = 2
ATT_TQ = 1024
ATT_TK = 2048
VMEM_LIMIT = 48 * 2 ** 20
EXPERT_FIRST_VMEM_LIMIT = 56 * 2 ** 20

_O_CQ, _O_CKV, _O_SGU, _O_SGV = 0, 384, 640, 896
_O_RQ, _O_RK, _O_RQS, _O_RKS, _O_RV, _O_RG, _O_KR = 1152, 1280, 1408, 1536, 1664, 1920, 2176
IN_P = 2304
_T_QC, _T_QS, _T_KCS, _T_RQC, _T_RQS, _T_RKC, _T_RKS = 0, 256, 512, 640, 768, 896, 1024
TAB_W = 1152


def _cparams(sem):
    return pltpu.CompilerParams(dimension_semantics=sem, vmem_limit_bytes=VMEM_LIMIT)


def _dot(a, b):
    return jnp.dot(a, b, preferred_element_type=F32)


def _dot_nt(a, b):
    return lax.dot_general(a, b, (((1,), (1,)), ((), ())), preferred_element_type=F32)


def _mx(a):
    return a.astype(MXU_DT)


def _swap16(j):
    return (j // 32) * 32 + ((j % 32) + 16) % 32


_ERF_ALPHA = (-2.72614225801306e-10, 2.77068142495902e-08, -2.10102402082508e-06,
              -5.69250639462346e-05, -7.34990630326855e-04, -2.95459980854025e-03,
              -1.60960333262415e-02)
_ERF_BETA = (-1.45660718464996e-05, -2.13374055278905e-04, -1.68282697438203e-03,
             -7.37332916720468e-03, -1.42647390514189e-02)


def _erf(x):
    x = jnp.clip(x, -4.0, 4.0)
    x2 = x * x
    p = jnp.full_like(x, _ERF_ALPHA[0])
    for c in _ERF_ALPHA[1:]:
        p = p * x2 + c
    q = jnp.full_like(x, _ERF_BETA[0])
    for c in _ERF_BETA[1:]:
        q = q * x2 + c
    return x * p / q


def _gelu(x):
    return 0.5 * x * (1.0 + _erf(x * 0.7071067811865476))


def _sigmoid(x):
    return 1.0 / (1.0 + jnp.exp(-x))


def _ln(x):
    xc = x - jnp.mean(x, axis=-1, keepdims=True)
    return xc * lax.rsqrt(jnp.mean(xc * xc, axis=-1, keepdims=True) + LN_EPS)


def _lane_group(shape, width):
    return lax.broadcasted_iota(jnp.int32, shape, len(shape) - 1) // width


def _ada_kernel(c_ref, w_ref, b_ref, o_ref):
    c = c_ref[...]
    o_ref[0] = _dot(c * _sigmoid(c), w_ref[0]) + b_ref[0]


def ada_modulation(c_rows, ada_w, ada_b):
    nl, d, n = ada_w.shape
    tn = 1536
    return pl.pallas_call(
        _ada_kernel,
        grid=(nl, n // tn),
        in_specs=[pl.BlockSpec((8, d), lambda l, j: (0, 0)),
                  pl.BlockSpec((1, d, tn), lambda l, j: (l, 0, j)),
                  pl.BlockSpec((1, 1, tn), lambda l, j: (l, 0, j))],
        out_specs=pl.BlockSpec((1, 8, tn), lambda l, j: (l, 0, j)),
        out_shape=jax.ShapeDtypeStruct((nl, 8, n), F32),
        compiler_params=_cparams(("arbitrary", "arbitrary")),
        name="ada_modulation",
    )(c_rows, ada_w, ada_b.reshape(nl, 1, n))


def _inproj_kernel(x_ref, mod_ref, tab_ref, w_in_ref, qg_ref, kvg_ref, w_uq_ref, w_ukv_ref,
                   sgg_ref, sgb_ref, sgw_ref, sgbias_ref, kdec_ref, bd_ref,
                   q_ref, k_ref, v_ref, sg_ref, retp_ref, rv_ref, a_ref):
    x = x_ref[0]
    mod = mod_ref[0, 0]
    h = x * (1.0 + mod[1:2]) + mod[0:1]
    p = _dot(_mx(h), w_in_ref[...])
    tab = tab_ref[...]

    cq = p[:, _O_CQ:_O_CQ + MLA_Q_LORA]
    cq = cq * lax.rsqrt(jnp.mean(cq * cq, axis=-1, keepdims=True) + RMS_EPS) * qg_ref[...]
    qa = _dot(_mx(cq), w_uq_ref[...])
    rot = (qa[:, 512:768] * tab[:, _T_QC:_T_QC + 256]
           + qa[:, 768:1024] * tab[:, _T_QS:_T_QS + 256])
    for hh in range(MLA_HEADS):
        q_ref[0, hh, :, 0:128] = (qa[:, 128 * hh:128 * hh + 128] * MLA_SCALE).astype(q_ref.dtype)
        g = hh // 2
        q_ref[0, hh, :, 128:256] = rot[:, 128 * g:128 * g + 128].astype(q_ref.dtype)

    ckv = p[:, _O_CKV:_O_CKV + MLA_KV_LORA]
    ckv = ckv * lax.rsqrt(jnp.mean(ckv * ckv, axis=-1, keepdims=True) + RMS_EPS) * kvg_ref[...]
    kv = _dot(_mx(ckv), w_ukv_ref[...])
    t = p[:, _O_KR:_O_KR + 128] * tab[:, _T_KCS:_T_KCS + 128]
    u = t + pltpu.roll(t, 64, axis=1)
    low = lax.broadcasted_iota(jnp.int32, u.shape, 1) < 64
    kx = (jnp.where(low, u, 0.0), jnp.where(low, 0.0, u))
    ones_col = jnp.where(lax.broadcasted_iota(jnp.int32, u.shape, 1) == 0, 1.0, 0.0).astype(v_ref.dtype)
    for hh in range(MLA_HEADS):
        k_ref[0, hh, :, 0:128] = kv[:, 256 * hh:256 * hh + 128].astype(k_ref.dtype)
        k_ref[0, hh, :, 128:256] = kx[hh % 2].astype(k_ref.dtype)
        v_ref[0, hh, :, 0:128] = kv[:, 256 * hh + 128:256 * hh + 256].astype(v_ref.dtype)
        v_ref[0, hh, :, 128:256] = ones_col

    gu = _gelu(p[:, _O_SGU:_O_SGU + SG_WIDTH])
    gv = _ln(_gelu(p[:, _O_SGV:_O_SGV + SG_WIDTH])) * sgg_ref[...] + sgb_ref[...]
    gvm = _mx(gv)
    grp = _lane_group((CHUNK, SG_WIDTH), SG_WIDTH // SG_GROUPS)
    for c in range(TM // CHUNK):
        rows = slice(c * CHUNK, (c + 1) * CHUNK)
        res = _dot(sgw_ref[...], gvm[rows])
        mixed = sgbias_ref[...]
        for g in range(SG_GROUPS):
            mixed = mixed + jnp.where(grp == g, res[g * CHUNK:(g + 1) * CHUNK], 0.0)
        sg_ref[0, rows, :] = (gu[rows] * mixed).astype(sg_ref.dtype)

    rq = (p[:, _O_RQ:_O_RQ + 128] * tab[:, _T_RQC:_T_RQC + 128]
          + p[:, _O_RQS:_O_RQS + 128] * tab[:, _T_RQS:_T_RQS + 128])
    rk = (p[:, _O_RK:_O_RK + 128] * tab[:, _T_RKC:_T_RKC + 128]
          + p[:, _O_RKS:_O_RKS + 128] * tab[:, _T_RKS:_T_RKS + 128])
    rv = p[:, _O_RV:_O_RV + RET_OUT]
    retp_ref[0, :, 0:128] = rq
    retp_ref[0, :, 128:256] = rk
    retp_ref[0, :, 256:512] = p[:, _O_RG:_O_RG + RET_OUT]
    rvm = _mx(rv)
    rv_ref[0] = rvm.astype(rv_ref.dtype)
    bd = bd_ref[...]
    for c in range(TM // CHUNK):
        rows = slice(c * CHUNK, (c + 1) * CHUNK)
        for d in range(2):
            kd_t = _mx((rk[rows] * kdec_ref[d]).T)
            af = _dot(kd_t, rvm[rows]) * bd
            a_ref[0, c, d] = (af[0:32] + af[32:64]) + (af[64:96] + af[96:128])


def _ordered_after(kernel_fn, pos):
    def wrapped(*refs):
        return kernel_fn(*refs[:pos], *refs[pos + 1:])
    return wrapped


_ORDER_SPEC = pl.BlockSpec(memory_space=pl.ANY)


def input_projection(x_all, mod_tab, tab, lw, n_lat_tiles, after, x_block=0):
    b = mod_tab.shape[0]
    _, t, d = x_all.shape
    nt = t // TM
    nc = t // CHUNK
    cpt = TM // CHUNK
    const2 = lambda bi, j: (0, 0)
    const3 = lambda bi, j: (0, 0, 0)
    out_shape = (
        jax.ShapeDtypeStruct((b, MLA_HEADS, t, 256), MXU_DT),
        jax.ShapeDtypeStruct((b, MLA_HEADS, t, 256), MXU_DT),
        jax.ShapeDtypeStruct((b, MLA_HEADS, t, 256), MXU_DT),
        jax.ShapeDtypeStruct((b, t, SG_WIDTH), MXU_DT),
        jax.ShapeDtypeStruct((b, t, 512), F32),
        jax.ShapeDtypeStruct((b, t, RET_OUT), MXU_DT),
        jax.ShapeDtypeStruct((b, nc, 2, RET_QK, RET_OUT), F32),
    )
    head_spec = lambda w: pl.BlockSpec((1, MLA_HEADS, TM, w), lambda bi, j: (bi, 0, j, 0))
    tok_spec = lambda w: pl.BlockSpec((1, TM, w), lambda bi, j: (bi, j, 0))
    return pl.pallas_call(
        _ordered_after(_inproj_kernel, 14),
        grid=(b, nt),
        in_specs=[
            pl.BlockSpec((1, TM, d), lambda bi, j: (bi + x_block * b, j, 0)),
            pl.BlockSpec((1, 1, N_MOD, d), lambda bi, j: (bi, j // n_lat_tiles, 0, 0)),
            pl.BlockSpec((TM, TAB_W), lambda bi, j: (j, 0)),
            pl.BlockSpec((d, IN_P), const2),
            pl.BlockSpec((1, MLA_Q_LORA), const2),
            pl.BlockSpec((1, MLA_KV_LORA), const2),
            pl.BlockSpec((MLA_Q_LORA, 1024), const2),
            pl.BlockSpec((MLA_KV_LORA, 1024), const2),
            pl.BlockSpec((1, SG_WIDTH), const2),
            pl.BlockSpec((1, SG_WIDTH), const2),
            pl.BlockSpec((SG_GROUPS * CHUNK, CHUNK), const2),
            pl.BlockSpec((CHUNK, SG_WIDTH), const2),
            pl.BlockSpec((2, CHUNK, 128), const3),
            pl.BlockSpec((128, RET_OUT), const2),
            _ORDER_SPEC,
        ],
        out_specs=(head_spec(256), head_spec(256), head_spec(256), tok_spec(SG_WIDTH),
                   tok_spec(512), tok_spec(RET_OUT),
                   pl.BlockSpec((1, cpt, 2, RET_QK, RET_OUT), lambda bi, j: (bi, j, 0, 0, 0))),
        out_shape=out_shape,
        compiler_params=_cparams(("parallel", "parallel")),
        name="input_projection",
    )(x_all, mod_tab, tab, lw["w_in"], lw["q_g"], lw["kv_g"], lw["w_uq"], lw["w_ukv"],
      lw["sg_g"], lw["sg_b"], lw["sg_w"], lw["sg_bias"], lw["kdec"], lw["bd"], after)


def _attn_kernel(q_ref, k_ref, v_ref, o_ref, *, n_main, tk, tail):
    q = q_ref[0, 0]
    tq = q.shape[0]
    chunks = [(i * tk, tk) for i in range(n_main)] + ([(n_main * tk, tail)] if tail else [])

    def scores(ci):
        start, size = chunks[ci]
        return _dot_nt(q, k_ref[0, 0, start:start + size, :])

    m = jnp.full((tq, 1), -1e30, F32)
    acc = jnp.zeros((tq, 256), F32)
    s_next = scores(0)
    for ci, (start, size) in enumerate(chunks):
        s = s_next
        if ci + 1 < len(chunks):
            s_next = scores(ci + 1)
        m_new = jnp.maximum(m, jnp.max(s, axis=-1, keepdims=True))
        p = jnp.exp(s - m_new)
        acc = jnp.exp(m - m_new) * acc + _dot(_mx(p), v_ref[0, 0, start:start + size, :])
        m = m_new
    o_ref[0] = (acc[:, 0:MLA_V] / acc[:, MLA_V:MLA_V + 1]).astype(o_ref.dtype)


def mla_attention(q, k, v, s_len, lc):
    b, hn, t, _ = q.shape
    tq = min(ATT_TQ, s_len)
    tk = min(ATT_TK, s_len)
    kv_full = pl.BlockSpec((1, 1, t, 256), lambda bi, hi, i: (bi, hi, 0, 0))
    out_lat = pl.pallas_call(
        functools.partial(_attn_kernel, n_main=s_len // tk, tk=tk, tail=lc),
        grid=(b, hn, s_len // tq),
        in_specs=[pl.BlockSpec((1, 1, tq, 256), lambda bi, hi, i: (bi, hi, i, 0)), kv_full, kv_full],
        out_specs=pl.BlockSpec((1, tq, MLA_V), lambda bi, hi, i: (bi, i, hi)),
        out_shape=jax.ShapeDtypeStruct((b, s_len, MLA_OUT), MXU_DT),
        compiler_params=_cparams(("parallel", "parallel", "arbitrary")),
        name="mla_attention_latent",
    )(q, k, v)
    cblk = s_len // lc
    ctx_spec = pl.BlockSpec((1, 1, lc, 256), lambda bi, hi: (bi, hi, cblk, 0))
    out_ctx = pl.pallas_call(
        functools.partial(_attn_kernel, n_main=0, tk=tk, tail=lc),
        grid=(b, hn),
        in_specs=[ctx_spec, ctx_spec, ctx_spec],
        out_specs=pl.BlockSpec((1, lc, MLA_V), lambda bi, hi: (bi, 0, hi)),
        out_shape=jax.ShapeDtypeStruct((b, lc, MLA_OUT), MXU_DT),
        compiler_params=_cparams(("parallel", "parallel")),
        name="mla_attention_context",
    )(q, k, v)
    return out_lat, out_ctx


def _ret_scan_kernel(a_ref, cd_ref, s_ref, *, n_lat_chunks):
    nc = a_ref.shape[1]
    ncc = nc - n_lat_chunks
    cd_f, cd_b = cd_ref[0], cd_ref[1]

    def body(n, carry):
        sf, sb = carry
        cf = jnp.where(n < ncc, n_lat_chunks + n, n - ncc)
        cb = jnp.where(n < ncc, nc - 1 - n, n_lat_chunks - 1 - (n - ncc))
        s_ref[0, cf, 0] = sf
        s_ref[0, cb, 1] = sb
        return sf * cd_f + a_ref[0, cf, 0], sb * cd_b + a_ref[0, cb, 1]

    zero = jnp.zeros((RET_QK, RET_OUT), F32)
    lax.fori_loop(0, nc, body, (zero, zero))


def retention_scan(a, cd, n_lat_chunks):
    b, nc = a.shape[:2]
    blk = pl.BlockSpec((1, nc, 2, RET_QK, RET_OUT), lambda bi: (bi, 0, 0, 0, 0))
    return pl.pallas_call(
        functools.partial(_ret_scan_kernel, n_lat_chunks=n_lat_chunks),
        grid=(b,),
        in_specs=[blk, pl.BlockSpec((2, 1, RET_OUT), lambda bi: (0, 0, 0))],
        out_specs=blk,
        out_shape=jax.ShapeDtypeStruct(a.shape, F32),
        compiler_params=_cparams(("parallel",)),
        name="retention_scan",
    )(a, cd)


def _split_dot(x, ones2):
    hi = x.astype(BF16)
    lo = (x - hi.astype(F32)).astype(BF16)
    return _dot(jnp.concatenate([hi, lo], axis=1), ones2)


def _outproj_kernel(x_ref, mlal_ref, mlac_ref, sg_ref, retp_ref, rv_ref, st_ref, mod_ref,
                    m_ref, qdec_ref, bd_ref, seg_ref, w_o_ref, lng_ref, lnb_ref, rw_ref, rb_ref, tri_ref,
                    x1_ref, h2_ref, idx_ref, gate_ref, rank_ref, cnt_ref, carry_ref, *, n_lat_tiles):
    @pl.when(pl.program_id(0) == 0)
    def _():
        carry_ref[...] = jnp.zeros_like(carry_ref)

    nbat = x_ref.shape[0]
    units = [(bb, c) for bb in range(nbat) for c in range(TM // CHUNK)]
    rows = lambda c: slice(c * CHUNK, (c + 1) * CHUNK)
    g32 = _lane_group((CHUNK, 128), RET_QK)
    g64 = _lane_group((CHUNK, RET_OUT), RET_V)
    bd = bd_ref[...]
    seg2 = jnp.concatenate([seg_ref[...], seg_ref[...]], axis=0)
    m4 = jnp.concatenate([m_ref[hh] for hh in range(RET_HEADS)], axis=0)

    rq, s4 = {}, {}
    for u in units:
        bb, c = u
        rq[u] = retp_ref[bb, rows(c), 0:128]
        q4 = jnp.concatenate([jnp.where(g32 == hh, rq[u], 0.0) for hh in range(RET_HEADS)], axis=0)
        s4[u] = _dot_nt(_mx(q4), _mx(retp_ref[bb, rows(c), 128:256]))
    o = {}
    for u in units:
        bb, c = u
        r = _dot(_mx(s4[u] * m4), rv_ref[bb, rows(c), :])
        qd = jnp.concatenate([rq[u] * qdec_ref[0], rq[u] * qdec_ref[1]], axis=1)
        st2 = jnp.concatenate([jnp.concatenate([st_ref[bb, c, dd]] * RET_HEADS, axis=0) * bd
                               for dd in range(2)], axis=0)
        acc = _dot(_mx(qd), _mx(st2))
        for hh in range(RET_HEADS):
            acc = acc + jnp.where(g64 == hh, r[hh * CHUNK:(hh + 1) * CHUNK], 0.0)
        o[u] = acc
    oc = {}
    for u in units:
        oc[u] = o[u] - _split_dot(o[u], seg2) * (1.0 / RET_V)
    ret = {}
    for u in units:
        bb, c = u
        var = _split_dot(oc[u] * oc[u], seg2) * (1.0 / RET_V)
        rg = retp_ref[bb, rows(c), 256:512]
        ret[u] = _mx(oc[u] * lax.rsqrt(var + LN_EPS) * (rg * _sigmoid(rg)))

    is_ctx = pl.program_id(0) >= n_lat_tiles
    ys = []
    for bb in range(nbat):
        mla = jnp.where(is_ctx, mlac_ref[bb], mlal_ref[bb])
        cat = jnp.concatenate(
            [mla, sg_ref[bb], jnp.concatenate([ret[(bb, c)] for c in range(TM // CHUNK)], axis=0)], axis=1)
        ys.append(_dot(cat, w_o_ref[...]))
    logits = []
    for bb in range(nbat):
        mod = mod_ref[bb, 0]
        x1 = _ln(DEEPNORM_ALPHA * x_ref[bb] + mod[2:3] * ys[bb]) * lng_ref[...] + lnb_ref[...]
        x1_ref[bb] = x1
        h2 = x1 * (1.0 + mod[4:5]) + mod[3:4]
        h2_ref[bb] = _pack_bf16_pairs(h2) if PACK_ROWS else h2
        h2_hi = h2.astype(BF16)
        h2_lo = (h2 - h2_hi.astype(F32)).astype(BF16)
        r2 = _dot_nt(rw_ref[...], jnp.concatenate([h2_hi, h2_lo], axis=0))
        logits.append(r2[0:N_EXPERTS, 0:TM] + r2[N_EXPERTS:, 0:TM] + r2[0:N_EXPERTS, TM:] + rb_ref[...])

    work = jnp.concatenate(logits, axis=1)
    e_iota = lax.broadcasted_iota(jnp.int32, work.shape, 0).astype(F32)
    vals, idxs = [], []
    for _ in range(TOP_K):
        mval = jnp.max(work, axis=0, keepdims=True)
        midx = jnp.min(jnp.where(work == mval, e_iota, float(N_EXPERTS)), axis=0, keepdims=True)
        vals.append(mval)
        idxs.append(midx)
        work = jnp.where(e_iota == midx, -jnp.inf, work)
    ex = [jnp.exp(vv - vals[0]) for vv in vals]
    den = ex[0] + ex[1] + ex[2] + ex[3]
    onehot = jnp.zeros_like(work)
    for kk in range(TOP_K):
        onehot = onehot + jnp.where(e_iota == idxs[kk], 1.0, 0.0)
    cols = lambda bb: slice(bb * TM, (bb + 1) * TM)
    prefixes = [_dot(onehot[:, cols(bb)].astype(BF16), tri_ref[...]) for bb in range(nbat)]
    count = carry_ref[:, 0:1]
    bases = []
    for bb in range(nbat):
        bases.append(count + prefixes[bb])
        count = count + jnp.sum(onehot[:, cols(bb)], axis=1, keepdims=True)
    base = jnp.concatenate(bases, axis=1)
    for kk in range(TOP_K):
        gate = ex[kk] / den
        rank = jnp.sum(jnp.where(e_iota == idxs[kk], base, 0.0), axis=0, keepdims=True).astype(jnp.int32)
        for bb in range(nbat):
            gate_ref[bb, 0, kk:kk + 1, :] = gate[:, cols(bb)]
            idx_ref[bb, 0, kk:kk + 1, :] = idxs[kk][:, cols(bb)].astype(jnp.int32)
            rank_ref[bb, 0, kk:kk + 1, :] = rank[:, cols(bb)]
    carry_ref[...] = jnp.broadcast_to(count, carry_ref.shape)
    cnt_ref[...] = carry_ref[...].astype(jnp.int32)


def output_projection(x_all, mla_lat, mla_ctx, sg, retp, rv, st, mod_tab, lw, n_lat_tiles, x_block=0):
    b = mod_tab.shape[0]
    _, t, d = x_all.shape
    nt = t // TM
    cpt = TM // CHUNK
    const2 = lambda j: (0, 0)
    const3 = lambda j: (0, 0, 0)
    tok_spec = lambda w: pl.BlockSpec((b, TM, w), lambda j: (0, j, 0))
    route_spec = pl.BlockSpec((b, 1, TOP_K, TM), lambda j: (0, j, 0, 0))
    route_shape = lambda dt: jax.ShapeDtypeStruct((b, nt, TOP_K, TM), dt)
    h2w = d // 2 if PACK_ROWS else d
    return pl.pallas_call(
        functools.partial(_outproj_kernel, n_lat_tiles=n_lat_tiles),
        grid=(nt,),
        in_specs=[
            pl.BlockSpec((b, TM, d), lambda j: (x_block, j, 0)),
            pl.BlockSpec((b, TM, MLA_OUT), lambda j: (0, jnp.minimum(j, n_lat_tiles - 1), 0)),
            pl.BlockSpec((b, TM, MLA_OUT), lambda j: (0, jnp.maximum(j - n_lat_tiles, 0), 0)),
            tok_spec(SG_WIDTH), tok_spec(512), tok_spec(RET_OUT),
            pl.BlockSpec((b, cpt, 2, RET_QK, RET_OUT), lambda j: (0, j, 0, 0, 0)),
            pl.BlockSpec((b, 1, N_MOD, d), lambda j: (0, j // n_lat_tiles, 0, 0)),
            pl.BlockSpec((RET_HEADS, CHUNK, CHUNK), const3),
            pl.BlockSpec((2, CHUNK, 128), const3),
            pl.BlockSpec((128, RET_OUT), const2),
            pl.BlockSpec((RET_OUT, RET_OUT), const2),
            pl.BlockSpec((d, d), const2),
            pl.BlockSpec((1, d), const2),
            pl.BlockSpec((1, d), const2),
            pl.BlockSpec((2 * N_EXPERTS, d), const2),
            pl.BlockSpec((N_EXPERTS, 1), const2),
            pl.BlockSpec((TM, TM), const2),
        ],
        out_specs=(tok_spec(d), tok_spec(h2w), route_spec, route_spec, route_spec,
                   pl.BlockSpec((N_EXPERTS, 128), const2)),
        out_shape=(jax.ShapeDtypeStruct((b, t, d), F32),
                   jax.ShapeDtypeStruct((b, t, h2w), jnp.uint32 if PACK_ROWS else F32),
                   route_shape(jnp.int32), route_shape(F32), route_shape(jnp.int32),
                   jax.ShapeDtypeStruct((N_EXPERTS, 128), jnp.int32)),
        scratch_shapes=[pltpu.VMEM((N_EXPERTS, 128), F32)],
        compiler_params=_cparams(("arbitrary",)),
        name="output_projection",
    )(x_all, mla_lat, mla_ctx, sg, retp, rv, st, mod_tab, lw["ret_m"], lw["qdec"], lw["bd"], lw["seg"],
      lw["w_o"], lw["ln1_g"], lw["ln1_b"], lw["router_w"], lw["router_b"], lw["tri"])


_DEINT = 256


def _pack_bf16_pairs(v):
    bits = lax.bitcast_convert_type(v.astype(BF16).astype(F32), jnp.uint32)
    half = bits.shape[1] // 2
    return bits[:, :half] | (bits[:, half:] >> 16)


def _unpack_bf16_pairs(w):
    return (lax.bitcast_convert_type(w & jnp.uint32(0xFFFF0000), F32),
            lax.bitcast_convert_type(w << 16, F32))


def _expert_block(x_ref, wg, wl, bg_ref, bl_ref, wd, bd_ref, y_ref):
    if PACK_ROWS:
        hi, lo = _unpack_bf16_pairs(x_ref[...])
        xb = jnp.concatenate([hi.astype(BF16), lo.astype(BF16)], axis=1)
    else:
        xb = x_ref[...]
    glu = jnp.minimum(_dot(xb, wg[0]) + bg_ref[0], SWIGLU_LIMIT)
    lin = jnp.clip(_dot(xb, wl[0]) + bl_ref[0], -SWIGLU_LIMIT, SWIGLU_LIMIT)
    act = glu * _sigmoid(SWIGLU_ALPHA * glu) * (lin + 1.0)
    y = _dot(_mx(act), wd[0]) + bd_ref[0]
    y_ref[...] = _pack_bf16_pairs(y) if PACK_ROWS else y


def _expert_first_kernel(be_ref, nu_ref, x_ref, wgu_ref, bg_ref, bl_ref, wdn_ref, bd_ref, perm_ref,
                         y_ref, wg_o, wl_o, wd_o):
    i = pl.program_id(0)
    active = i < nu_ref[0]
    fresh = jnp.logical_or(i == 0, be_ref[i] != be_ref[jnp.maximum(i - 1, 0)])

    @pl.when(jnp.logical_and(active, fresh))
    def _():
        half = _DEINT // 2
        for c in range(2 * D_EXPERT // _DEINT):
            r = _dot(_mx(wgu_ref[0, 0, :, _DEINT * c:_DEINT * (c + 1)]), perm_ref[...])
            wg_o[0, :, half * c:half * (c + 1)] = r[:, :half].astype(wg_o.dtype)
            wl_o[0, :, half * c:half * (c + 1)] = r[:, half:].astype(wl_o.dtype)
        wd_o[0] = wdn_ref[0, 0].astype(wd_o.dtype)

    @pl.when(active)
    def _():
        _expert_block(x_ref, wg_o, wl_o, bg_ref, bl_ref, wd_o, bd_ref, y_ref)

    @pl.when(jnp.logical_not(active))
    def _():
        y_ref[...] = jnp.zeros_like(y_ref)


def _expert_kernel(be_ref, nu_ref, x_ref, wg_ref, wl_ref, bg_ref, bl_ref, wd_ref, bd_ref, y_ref):
    del be_ref
    active = pl.program_id(0) < nu_ref[0]

    @pl.when(active)
    def _():
        _expert_block(x_ref, wg_ref, wl_ref, bg_ref, bl_ref, wd_ref, bd_ref, y_ref)

    @pl.when(jnp.logical_not(active))
    def _():
        y_ref[...] = jnp.zeros_like(y_ref)


def expert_ffn_first(xg, block_e, n_used, w_gate_up, w_down, li, lw, after):
    cap, xw = xg.shape
    d, de = D_MODEL, D_EXPERT
    xmap = lambda i, be, nu: (jnp.minimum(i, nu[0] - 1), 0)
    wmap = lambda i, be, nu: (be[i], 0, 0)
    lmap = lambda i, be, nu: (li, be[i], 0, 0)
    grid_spec = pltpu.PrefetchScalarGridSpec(
        num_scalar_prefetch=2,
        grid=(cap // MOE_BM,),
        in_specs=[pl.BlockSpec((MOE_BM, xw), xmap),
                  pl.BlockSpec((1, 1, d, 2 * de), lmap),
                  pl.BlockSpec((1, 1, de), wmap), pl.BlockSpec((1, 1, de), wmap),
                  pl.BlockSpec((1, 1, de, d), lmap), pl.BlockSpec((1, 1, d), wmap),
                  pl.BlockSpec((_DEINT, _DEINT), lambda i, be, nu: (0, 0)),
                  _ORDER_SPEC],
        out_specs=(pl.BlockSpec((MOE_BM, xw), lambda i, be, nu: (i, 0)),
                   pl.BlockSpec((1, d, de), wmap), pl.BlockSpec((1, d, de), wmap),
                   pl.BlockSpec((1, de, d), wmap)),
    )
    y, wg, wl, wd = pl.pallas_call(
        _ordered_after(_expert_first_kernel, 9),
        grid_spec=grid_spec,
        out_shape=(jax.ShapeDtypeStruct((cap, xw), xg.dtype),
                   jax.ShapeDtypeStruct((N_EXPERTS, d, de), MXU_DT),
                   jax.ShapeDtypeStruct((N_EXPERTS, d, de), MXU_DT),
                   jax.ShapeDtypeStruct((N_EXPERTS, de, d), MXU_DT)),
        compiler_params=pltpu.CompilerParams(dimension_semantics=("arbitrary",),
                                             vmem_limit_bytes=EXPERT_FIRST_VMEM_LIMIT),
        name="expert_ffn_first",
    )(block_e, n_used, xg, w_gate_up, lw["b_glu"], lw["b_lin"], w_down, lw["b_down"], lw["deint"], after)
    return y, (wg, wl, wd)


def expert_ffn(xg, block_e, n_used, ew, lw, after):
    cap, xw = xg.shape
    d, de = D_MODEL, D_EXPERT
    xmap = lambda i, be, nu: (jnp.minimum(i, nu[0] - 1), 0)
    wmap = lambda i, be, nu: (be[i], 0, 0)
    grid_spec = pltpu.PrefetchScalarGridSpec(
        num_scalar_prefetch=2,
        grid=(cap // MOE_BM,),
        in_specs=[pl.BlockSpec((MOE_BM, xw), xmap),
                  pl.BlockSpec((1, d, de), wmap), pl.BlockSpec((1, d, de), wmap),
                  pl.BlockSpec((1, 1, de), wmap), pl.BlockSpec((1, 1, de), wmap),
                  pl.BlockSpec((1, de, d), wmap), pl.BlockSpec((1, 1, d), wmap),
                  _ORDER_SPEC],
        out_specs=pl.BlockSpec((MOE_BM, xw), lambda i, be, nu: (i, 0)),
    )
    return pl.pallas_call(
        _ordered_after(_expert_kernel, 9),
        grid_spec=grid_spec,
        out_shape=jax.ShapeDtypeStruct((cap, xw), xg.dtype),
        compiler_params=_cparams(("arbitrary",)),
        name="expert_ffn",
    )(block_e, n_used, xg, ew[0], ew[1], lw["b_glu"], lw["b_lin"], ew[2], lw["b_down"], after)


def _combine_ln2_kernel(x_ref, y_ref, gate_ref, mod_ref, g_ref, b_ref, o_ref):
    for bb in range(x_ref.shape[0]):
        gates = gate_ref[bb]
        if PACK_ROWS:
            f_hi, f_lo = 0.0, 0.0
            for kk in range(TOP_K):
                hi, lo = _unpack_bf16_pairs(y_ref[kk, bb])
                f_hi = f_hi + gates[:, kk:kk + 1] * hi
                f_lo = f_lo + gates[:, kk:kk + 1] * lo
            f = jnp.concatenate([f_hi, f_lo], axis=1)
        else:
            f = gates[:, 0:1] * y_ref[0, bb]
            for kk in range(1, TOP_K):
                f = f + gates[:, kk:kk + 1] * y_ref[kk, bb]
        mod = mod_ref[bb, 0]
        o_ref[bb] = _ln(DEEPNORM_ALPHA * x_ref[bb] + mod[5:6] * f) * g_ref[...] + b_ref[...]


def combine_deepnorm2(x1, yg, gates, mod_tab, lw, n_lat_tiles, after):
    b, t, d = x1.shape
    tok = pl.BlockSpec((b, TM, d), lambda j: (0, j, 0))
    vec = pl.BlockSpec((1, d), lambda j: (0, 0))
    return pl.pallas_call(
        _ordered_after(_combine_ln2_kernel, 6),
        grid=(t // TM,),
        in_specs=[tok,
                  pl.BlockSpec((TOP_K, b, TM, yg.shape[-1]), lambda j: (0, 0, j, 0)),
                  pl.BlockSpec((b, TM, TOP_K), lambda j: (0, j, 0)),
                  pl.BlockSpec((b, 1, N_MOD, d), lambda j: (0, j // n_lat_tiles, 0, 0)), vec, vec,
                  _ORDER_SPEC],
        out_specs=tok,
        out_shape=jax.ShapeDtypeStruct((b, t, d), F32),
        compiler_params=_cparams(("parallel",)),
        name="combine_deepnorm2",
    )(x1, yg, gates, mod_tab, lw["ln2_g"], lw["ln2_b"], after)


def _rotation_tables(s_len, lc):
    rows = s_len // GRID_W
    row = jnp.broadcast_to(jnp.arange(rows, dtype=F32)[:, None], (rows, GRID_W)).reshape(-1)
    col = jnp.broadcast_to(jnp.arange(GRID_W, dtype=F32)[None, :], (rows, GRID_W)).reshape(-1)
    inv = ROPE_BASE ** (-jnp.arange(ROPE_AXIS_FREQS, dtype=F32) / ROPE_AXIS_FREQS)
    ar, ac = row[:, None] * inv, col[:, None] * inv
    c64 = jnp.concatenate([jnp.cos(ar), jnp.cos(ar), jnp.cos(ac), jnp.cos(ac)], axis=1)
    s64 = jnp.concatenate([-jnp.sin(ar), jnp.sin(ar), -jnp.sin(ac), jnp.sin(ac)], axis=1)
    c64 = jnp.concatenate([c64, jnp.ones((lc, 64), F32)], axis=0)
    s64 = jnp.concatenate([s64, jnp.zeros((lc, 64), F32)], axis=0)
    half = RET_QK // 2
    pos = jnp.concatenate([lc + jnp.arange(s_len, dtype=F32), jnp.arange(lc, dtype=F32)])
    inv_r = 1.0 / (RET_ROPE_BASE ** jnp.linspace(0.0, 1.0, half, dtype=F32))
    ang = pos[:, None] * inv_r
    rc = jnp.tile(jnp.concatenate([jnp.cos(ang), jnp.cos(ang)], axis=1), (1, RET_HEADS))
    rs = jnp.tile(jnp.concatenate([-jnp.sin(ang), jnp.sin(ang)], axis=1), (1, RET_HEADS))
    qs = RET_QK ** -0.5
    return jnp.concatenate([
        jnp.tile(c64, (1, MLA_HEADS)) * MLA_SCALE, jnp.tile(s64, (1, MLA_HEADS)) * MLA_SCALE,
        c64, s64, rc * qs, rs * qs, rc, rs], axis=1)


def _in_perm():
    a = np.arange
    return np.concatenate([
        a(0, 640), a(704, 1216), a(1216, 1344), a(1344, 1472),
        1216 + _swap16(a(128)), 1344 + _swap16(a(128)), a(1472, 1984),
        640 + a(64), 640 + _swap16(a(64))])


def _uq_perm():
    a = np.arange
    nope = [h * MLA_QK + a(MLA_NOPE) for h in range(MLA_HEADS)]
    rope = [h * MLA_QK + MLA_NOPE + a(MLA_ROPE) for h in range(MLA_HEADS)]
    part = [h * MLA_QK + MLA_NOPE + _swap16(a(MLA_ROPE)) for h in range(MLA_HEADS)]
    return np.concatenate(nope + rope + part)


def _layer_weights(p):
    nl = p["w_in"].shape[0]
    lgf = jax.nn.log_sigmoid(p["ret_decay_fwd"].astype(F32))
    lgb = jax.nn.log_sigmoid(p["ret_decay_bwd"].astype(F32))
    h128 = np.arange(128) // RET_QK
    h256 = np.arange(RET_OUT) // RET_V
    a = jnp.arange(CHUNK, dtype=F32)[None, :, None]
    lf, lb = lgf[:, h128][:, None, :], lgb[:, h128][:, None, :]
    kdec = jnp.stack([jnp.exp(lf * (CHUNK - 1.0 - a)), jnp.exp(lb * a)], axis=1)
    qdec = jnp.stack([jnp.exp(lf * (a + 1.0)), jnp.exp(lb * (CHUNK - a))], axis=1)
    i = jnp.arange(CHUNK, dtype=F32)[:, None]
    j = jnp.arange(CHUNK, dtype=F32)[None, :]
    dif = (i - j)[None, None]
    ret_m = jnp.where(dif >= 0, jnp.exp(lgf[:, :, None, None] * jnp.maximum(dif, 0.0)),
                      jnp.exp(lgb[:, :, None, None] * jnp.maximum(-dif, 0.0)))
    cd = jnp.stack([jnp.exp(lgf[:, h256] * CHUNK), jnp.exp(lgb[:, h256] * CHUNK)], axis=1)[:, :, None, :]
    bd = (h128[:, None] == h256[None, :]).astype(np.float32)
    seg = (h256[:, None] == h256[None, :]).astype(np.float32)
    tri = (np.arange(TM)[:, None] < np.arange(TM)[None, :]).astype(np.float32)
    jj = np.arange(_DEINT // 2)
    deint = np.zeros((_DEINT, _DEINT), np.float32)
    deint[2 * jj, jj] = 1.0
    deint[2 * jj + 1, _DEINT // 2 + jj] = 1.0
    rw_t = jnp.swapaxes(p["router_w"], 1, 2)
    rw_hi = rw_t.astype(BF16)
    rw_lo = (rw_t - rw_hi.astype(F32)).astype(BF16)
    sg_bias = jnp.repeat(jnp.swapaxes(p["sg_b"], 1, 2), SG_WIDTH // SG_GROUPS, axis=2)
    bgu = p["b_gate_up"]
    return {
        "w_in": p["w_in"][:, :, _in_perm()].astype(MXU_DT),
        "q_g": p["mla_q_norm_g"][:, None, :], "kv_g": p["mla_kv_norm_g"][:, None, :],
        "w_uq": p["mla_w_uq"][:, :, _uq_perm()].astype(MXU_DT),
        "w_ukv": p["mla_w_ukv"].astype(MXU_DT),
        "sg_g": p["sg_norm_g"][:, None, :], "sg_b": p["sg_norm_b"][:, None, :],
        "sg_w": p["sg_w"].reshape(nl, SG_GROUPS * CHUNK, CHUNK).astype(MXU_DT),
        "sg_bias": sg_bias,
        "kdec": kdec, "qdec": qdec, "ret_m": ret_m, "cd": cd,
        "bd": jnp.broadcast_to(jnp.asarray(bd), (nl,) + bd.shape),
        "seg": jnp.broadcast_to(jnp.asarray(seg, BF16), (nl,) + seg.shape),
        "tri": jnp.broadcast_to(jnp.asarray(tri, BF16), (nl,) + tri.shape),
        "w_o": p["w_o"].astype(MXU_DT),
        "ln1_g": p["ln1_g"][:, None, :], "ln1_b": p["ln1_b"][:, None, :],
        "ln2_g": p["ln2_g"][:, None, :], "ln2_b": p["ln2_b"][:, None, :],
        "router_w": jnp.concatenate([rw_hi, rw_lo], axis=1),
        "router_b": p["router_b"][:, :, None],
        "b_glu": bgu[:, :, None, 0::2], "b_lin": bgu[:, :, None, 1::2],
        "b_down": p["b_down"][:, :, None, :],
        "deint": jnp.broadcast_to(jnp.asarray(deint, MXU_DT), (nl,) + deint.shape),
    }


def _route(idx, rank, counts, every_expert):
    n_assign = idx.shape[1] * TOP_K
    nb = -(-n_assign // MOE_BM) + N_EXPERTS
    blocks = (counts + MOE_BM - 1) // MOE_BM
    padded = (jnp.maximum(blocks, 1) if every_expert else blocks) * MOE_BM
    pad_end = jnp.cumsum(padded)
    pad_start = pad_end - padded
    experts = jnp.arange(N_EXPERTS, dtype=jnp.int32)
    dest = rank + jnp.sum(jnp.where(idx[..., None] == experts, pad_start, 0), axis=-1)
    blk_start = jnp.arange(nb, dtype=jnp.int32) * MOE_BM
    block_e = jnp.minimum(jnp.sum((pad_end[None, :] <= blk_start[:, None]).astype(jnp.int32), axis=1),
                          N_EXPERTS - 1)
    n_used = (pad_end[-1] // MOE_BM).astype(jnp.int32).reshape(1)
    return dest.astype(jnp.int32), block_e, n_used, nb


def _sc_workers():
    info = plsc.get_sparse_core_info()
    return info.num_cores, info.num_cores * info.num_subcores


def sc_dispatch(rows, dest3, cap, after):
    n, w = rows.shape
    nch, kk, c = dest3.shape
    ncores, nw = _sc_workers()
    assert nch % nw == 0, "token chunks must split evenly over the vector subcores"
    per_w = nch // nw
    mesh = plsc.VectorSubcoreMesh(core_axis_name="c", subcore_axis_name="s")

    @functools.partial(
        pl.kernel, mesh=mesh, out_type=jax.ShapeDtypeStruct((cap, w), rows.dtype),
        scratch_types=[pltpu.VMEM((kk, c), jnp.int32), pltpu.VMEM((c, w), rows.dtype)])
    def scatter_rows(h_hbm, d_hbm, after_hbm, o_hbm, idx_v, rows_v):
        del after_hbm
        wid = lax.axis_index("s") * ncores + lax.axis_index("c")

        @pl.loop(0, per_w)
        def _(j):
            ch = wid * per_w + j
            pltpu.sync_copy(d_hbm.at[ch], idx_v)
            pltpu.sync_copy(h_hbm.at[pl.ds(ch * c, c)], rows_v)
            for q in range(kk):
                pltpu.sync_copy(rows_v, o_hbm.at[idx_v.at[q]])

    return scatter_rows(rows, dest3, after)


def sc_combine_gather(y, dest3, n, after):
    cap, d = y.shape
    nch, kk, c = dest3.shape
    ncores, nw = _sc_workers()
    assert nch % nw == 0, "token chunks must split evenly over the vector subcores"
    per_w = nch // nw
    mesh = plsc.VectorSubcoreMesh(core_axis_name="c", subcore_axis_name="s")

    @functools.partial(
        pl.kernel, mesh=mesh, out_type=jax.ShapeDtypeStruct((kk, n, d), y.dtype),
        scratch_types=[pltpu.VMEM((kk, c), jnp.int32), pltpu.VMEM((c, d), y.dtype)])
    def gather_rows(y_hbm, d_hbm, after_hbm, o_hbm, idx_v, rows_v):
        del after_hbm
        wid = lax.axis_index("s") * ncores + lax.axis_index("c")

        @pl.loop(0, per_w)
        def _(j):
            ch = wid * per_w + j
            pltpu.sync_copy(d_hbm.at[ch], idx_v)
            for q in range(kk):
                pltpu.sync_copy(y_hbm.at[idx_v.at[q]], rows_v)
                pltpu.sync_copy(rows_v, o_hbm.at[q, pl.ds(ch * c, c)])

    return gather_rows(y, dest3, after)


def kernel(x, c, ctx, c_ctx, ada_w, ada_b, w_in, mla_q_norm_g, mla_kv_norm_g, mla_w_uq, mla_w_ukv,
           sg_norm_g, sg_norm_b, sg_w, sg_b, ret_decay_fwd, ret_decay_bwd, w_o, ln1_g, ln1_b,
           router_w, router_b, w_gate_up, b_gate_up, w_down, b_down, ln2_g, ln2_b):
    b, s_len, d = x.shape
    lc = ctx.shape[1]
    assert d == D_MODEL and lc % TM == 0 and s_len % lc == 0 and s_len % GRID_W == 0
    assert b + 1 <= 8
    t = s_len + lc
    n_lat_tiles = s_len // TM
    params = dict(w_in=w_in, mla_q_norm_g=mla_q_norm_g, mla_kv_norm_g=mla_kv_norm_g, mla_w_uq=mla_w_uq,
                  mla_w_ukv=mla_w_ukv, sg_norm_g=sg_norm_g, sg_norm_b=sg_norm_b, sg_w=sg_w, sg_b=sg_b,
                  ret_decay_fwd=ret_decay_fwd, ret_decay_bwd=ret_decay_bwd, w_o=w_o, ln1_g=ln1_g,
                  ln1_b=ln1_b, router_w=router_w, router_b=router_b, w_gate_up=w_gate_up,
                  b_gate_up=b_gate_up, w_down=w_down, b_down=b_down, ln2_g=ln2_g, ln2_b=ln2_b)
    lws = _layer_weights(params)
    tab = _rotation_tables(s_len, lc)

    c_rows = jnp.concatenate([c, c_ctx[None, :], jnp.zeros((8 - b - 1, d), F32)], axis=0)
    mod = ada_modulation(c_rows, ada_w, ada_b).reshape(DEPTH, 8, N_MOD, d)
    mod_tab = jnp.stack([mod[:, :b], jnp.broadcast_to(mod[:, b:b + 1], (DEPTH, b, N_MOD, d))], axis=2)

    n_streams = N_STREAMS if b % N_STREAMS == 0 else 1
    bs = b // n_streams
    xs = [jnp.concatenate([x, ctx], axis=1)] * n_streams
    order = c_rows
    held = None
    for li in range(DEPTH):
        lw = {k: v[li] for k, v in lws.items()}
        mts = [mod_tab[li, si * bs:(si + 1) * bs] for si in range(n_streams)]
        fronts = []
        for si in range(n_streams):
            fr = _front(xs[si], si if li == 0 else 0, mts[si], tab, lw, s_len, lc, every_expert=si == 0,
                        after=order)
            order = fr["cnt"]
            if si == 0 and held is not None:
                xs[-1] = _finish(*held, gather_after=fr["mixed"], after=order)
                held = None
            fronts.append(fr)
        for si, fr in enumerate(fronts):
            start = fronts[si + 1]["mixed"] if si + 1 < n_streams else fr["cnt"]
            fr["xg"] = sc_dispatch(fr.pop("h2"), fr["dest3"], fr["cap"], start)
        ys = []
        for si in range(n_streams):
            fr = fronts[si]
            if si == 0:
                y, ew = expert_ffn_first(fr["xg"], fr["block_e"], fr["n_used"], w_gate_up, w_down, li, lw,
                                         after=order)
            else:
                y = expert_ffn(fr["xg"], fr["block_e"], fr["n_used"], ew, lw, after=order)
            order = y
            ys.append(y)
        for si in range(n_streams):
            if si == n_streams - 1 and n_streams > 1 and li + 1 < DEPTH:
                held = (fronts[si], ys[si], mts[si], lw, s_len)
            else:
                xs[si] = _finish(fronts[si], ys[si], mts[si], lw, s_len, ys[si], after=order)
                order = xs[si]
    return jnp.concatenate([xi[:, :s_len] for xi in xs], axis=0)


def _front(x_all, x_block, mt, tab, lw, s_len, lc, every_expert, after):
    b = mt.shape[0]
    t = x_all.shape[1]
    n_lat_tiles = s_len // TM
    q, k, v, sg, retp, rv, a = input_projection(x_all, mt, tab, lw, n_lat_tiles, after, x_block)
    mla_lat, mla_ctx = mla_attention(q, k, v, s_len, lc)
    st = retention_scan(a, lw["cd"], s_len // CHUNK)
    x1, h2, idx, gates, rank, cnt = output_projection(
        x_all, mla_lat, mla_ctx, sg, retp, rv, st, mt, lw, n_lat_tiles, x_block)
    to_tok = lambda z: z.transpose(2, 0, 1, 3).reshape(TOP_K, b * t)
    dest, block_e, n_used, nb = _route(to_tok(idx), to_tok(rank), cnt[:, 0], every_expert)
    assert (b * t) % SC_CHUNK == 0
    dest3 = dest.reshape(TOP_K, (b * t) // SC_CHUNK, SC_CHUNK).transpose(1, 0, 2)
    return dict(x1=x1, gates=gates, cnt=cnt, dest3=dest3, block_e=block_e, n_used=n_used,
                h2=h2.reshape(b * t, h2.shape[-1]), cap=nb * MOE_BM, mixed=a)


def _finish(fr, y, mt, lw, s_len, gather_after, after):
    b, t, d = fr["x1"].shape
    yg = sc_combine_gather(y, fr["dest3"], b * t, gather_after).reshape(TOP_K, b, t, y.shape[-1])
    gates_tok = fr["gates"].transpose(0, 1, 3, 2).reshape(b, t, TOP_K)
    return combine_deepnorm2(fr["x1"], yg, gates_tok, mt, lw, s_len // TM, after)
```

```python
import functools

import numpy as np
import jax
import jax.numpy as jnp
from jax import lax
from jax.experimental import pallas as pl
from jax.experimental.pallas import tpu as pltpu
from jax.experimental.pallas import tpu_sc as plsc

F32 = jnp.float32
BF16 = jnp.bfloat16
MXU_DT = BF16
PACK_ROWS = True

D_MODEL = 1024
DEPTH = 4
GRID_W = 64
MLA_HEADS = 4
MLA_NOPE = 128
MLA_ROPE = 64
MLA_V = 128
MLA_Q_LORA = 384
MLA_KV_LORA = 256
MLA_QK = MLA_NOPE + MLA_ROPE
MLA_SCALE = MLA_QK ** -0.5
ROPE_BASE = 10000.0
ROPE_AXIS_FREQS = MLA_ROPE // 4
SG_GROUPS = 4
SG_WIDTH = 256
SG_CHUNK = 128
RET_HEADS = 4
RET_QK = 32
RET_V = 64
RET_CHUNK = 128
RET_ROPE_BASE = 10000.0
N_EXPERTS = 32
TOP_K = 4
D_EXPERT = 1024
SWIGLU_LIMIT = 7.0
SWIGLU_ALPHA = 1.702
N_MOD = 6
LN_EPS = 1e-5
RMS_EPS = 1e-6
DEEPNORM_ALPHA = (2 * DEPTH) ** 0.25
MLA_OUT = MLA_HEADS * MLA_V
RET_OUT = RET_HEADS * RET_V

TM = 256
CHUNK = 128
MOE_BM = 512
SC_CHUNK = 48
N_STREAMS = 2
ATT_TQ = 1024
ATT_TK = 2048
VMEM_LIMIT = 48 * 2 ** 20
EXPERT_FIRST_VMEM_LIMIT = 56 * 2 ** 20

_O_CQ, _O_CKV, _O_SGU, _O_SGV = 0, 384, 640, 896
_O_RQ, _O_RK, _O_RQS, _O_RKS, _O_RV, _O_RG, _O_KR = 1152, 1280, 1408, 1536, 1664, 1920, 2176
IN_P = 2304
_T_QC, _T_QS, _T_KCS, _T_RQC, _T_RQS, _T_RKC, _T_RKS = 0, 256, 512, 640, 768, 896, 1024
TAB_W = 1152


def _cparams(sem):
    return pltpu.CompilerParams(dimension_semantics=sem, vmem_limit_bytes=VMEM_LIMIT)


def _dot(a, b):
    return jnp.dot(a, b, preferred_element_type=F32)


def _dot_nt(a, b):
    return lax.dot_general(a, b, (((1,), (1,)), ((), ())), preferred_element_type=F32)


def _mx(a):
    return a.astype(MXU_DT)


def _swap16(j):
    return (j // 32) * 32 + ((j % 32) + 16) % 32


_ERF_ALPHA = (-2.72614225801306e-10, 2.77068142495902e-08, -2.10102402082508e-06,
              -5.69250639462346e-05, -7.34990630326855e-04, -2.95459980854025e-03,
              -1.60960333262415e-02)
_ERF_BETA = (-1.45660718464996e-05, -2.13374055278905e-04, -1.68282697438203e-03,
             -7.37332916720468e-03, -1.42647390514189e-02)


def _erf(x):
    x = jnp.clip(x, -4.0, 4.0)
    x2 = x * x
    p = jnp.full_like(x, _ERF_ALPHA[0])
    for c in _ERF_ALPHA[1:]:
        p = p * x2 + c
    q = jnp.full_like(x, _ERF_BETA[0])
    for c in _ERF_BETA[1:]:
        q = q * x2 + c
    return x * p / q


def _gelu(x):
    return 0.5 * x * (1.0 + _erf(x * 0.7071067811865476))


def _sigmoid(x):
    return 1.0 / (1.0 + jnp.exp(-x))


def _ln(x):
    xc = x - jnp.mean(x, axis=-1, keepdims=True)
    return xc * lax.rsqrt(jnp.mean(xc * xc, axis=-1, keepdims=True) + LN_EPS)


def _lane_group(shape, width):
    return lax.broadcasted_iota(jnp.int32, shape, len(shape) - 1) // width


def _ada_kernel(c_ref, w_ref, b_ref, o_ref):
    c = c_ref[...]
    o_ref[0] = _dot(c * _sigmoid(c), w_ref[0]) + b_ref[0]


def ada_modulation(c_rows, ada_w, ada_b):
    nl, d, n = ada_w.shape
    tn = 1536
    return pl.pallas_call(
        _ada_kernel,
        grid=(nl, n // tn),
        in_specs=[pl.BlockSpec((8, d), lambda l, j: (0, 0)),
                  pl.BlockSpec((1, d, tn), lambda l, j: (l, 0, j)),
                  pl.BlockSpec((1, 1, tn), lambda l, j: (l, 0, j))],
        out_specs=pl.BlockSpec((1, 8, tn), lambda l, j: (l, 0, j)),
        out_shape=jax.ShapeDtypeStruct((nl, 8, n), F32),
        compiler_params=_cparams(("arbitrary", "arbitrary")),
        name="ada_modulation",
    )(c_rows, ada_w, ada_b.reshape(nl, 1, n))


def _inproj_kernel(xl_ref, xc_ref, mod_ref, tab_ref, w_in_ref, qg_ref, kvg_ref, w_uq_ref, w_ukv_ref,
                   sgg_ref, sgb_ref, sgw_ref, sgbias_ref, kdec_ref, bd_ref,
                   q_ref, k_ref, v_ref, sg_ref, retp_ref, rv_ref, a_ref, *, n_lat_tiles):
    x = jnp.where(pl.program_id(1) >= n_lat_tiles, xc_ref[0], xl_ref[0])
    mod = mod_ref[0, 0]
    h = x * (1.0 + mod[1:2]) + mod[0:1]
    p = _dot(_mx(h), w_in_ref[...])
    tab = tab_ref[...]

    cq = p[:, _O_CQ:_O_CQ + MLA_Q_LORA]
    cq = cq * lax.rsqrt(jnp.mean(cq * cq, axis=-1, keepdims=True) + RMS_EPS) * qg_ref[...]
    qa = _dot(_mx(cq), w_uq_ref[...])
    rot = (qa[:, 512:768] * tab[:, _T_QC:_T_QC + 256]
           + qa[:, 768:1024] * tab[:, _T_QS:_T_QS + 256])
    for hh in range(MLA_HEADS):
        q_ref[0, hh, :, 0:128] = (qa[:, 128 * hh:128 * hh + 128] * MLA_SCALE).astype(q_ref.dtype)
        g = hh // 2
        q_ref[0, hh, :, 128:256] = rot[:, 128 * g:128 * g + 128].astype(q_ref.dtype)

    ckv = p[:, _O_CKV:_O_CKV + MLA_KV_LORA]
    ckv = ckv * lax.rsqrt(jnp.mean(ckv * ckv, axis=-1, keepdims=True) + RMS_EPS) * kvg_ref[...]
    kv = _dot(_mx(ckv), w_ukv_ref[...])
    t = p[:, _O_KR:_O_KR + 128] * tab[:, _T_KCS:_T_KCS + 128]
    u = t + pltpu.roll(t, 64, axis=1)
    low = lax.broadcasted_iota(jnp.int32, u.shape, 1) < 64
    kx = (jnp.where(low, u, 0.0), jnp.where(low, 0.0, u))
    ones_col = jnp.where(lax.broadcasted_iota(jnp.int32, u.shape, 1) == 0, 1.0, 0.0).astype(v_ref.dtype)
    for hh in range(MLA_HEADS):
        k_ref[0, hh, :, 0:128] = kv[:, 256 * hh:256 * hh + 128].astype(k_ref.dtype)
        k_ref[0, hh, :, 128:256] = kx[hh % 2].astype(k_ref.dtype)
        v_ref[0, hh, :, 0:128] = kv[:, 256 * hh + 128:256 * hh + 256].astype(v_ref.dtype)
        v_ref[0, hh, :, 128:256] = ones_col

    gu = _gelu(p[:, _O_SGU:_O_SGU + SG_WIDTH])
    gv = _ln(_gelu(p[:, _O_SGV:_O_SGV + SG_WIDTH])) * sgg_ref[...] + sgb_ref[...]
    gvm = _mx(gv)
    grp = _lane_group((CHUNK, SG_WIDTH), SG_WIDTH // SG_GROUPS)
    for c in range(TM // CHUNK):
        rows = slice(c * CHUNK, (c + 1) * CHUNK)
        res = _dot(sgw_ref[...], gvm[rows])
        mixed = sgbias_ref[...]
        for g in range(SG_GROUPS):
            mixed = mixed + jnp.where(grp == g, res[g * CHUNK:(g + 1) * CHUNK], 0.0)
        sg_ref[0, rows, :] = (gu[rows] * mixed).astype(sg_ref.dtype)

    rq = (p[:, _O_RQ:_O_RQ + 128] * tab[:, _T_RQC:_T_RQC + 128]
          + p[:, _O_RQS:_O_RQS + 128] * tab[:, _T_RQS:_T_RQS + 128])
    rk = (p[:, _O_RK:_O_RK + 128] * tab[:, _T_RKC:_T_RKC + 128]
          + p[:, _O_RKS:_O_RKS + 128] * tab[:, _T_RKS:_T_RKS + 128])
    rv = p[:, _O_RV:_O_RV + RET_OUT]
    retp_ref[0, :, 0:128] = rq
    retp_ref[0, :, 128:256] = rk
    retp_ref[0, :, 256:512] = p[:, _O_RG:_O_RG + RET_OUT]
    rvm = _mx(rv)
    rv_ref[0] = rvm.astype(rv_ref.dtype)
    bd = bd_ref[...]
    for c in range(TM // CHUNK):
        rows = slice(c * CHUNK, (c + 1) * CHUNK)
        for d in range(2):
            kd_t = _mx((rk[rows] * kdec_ref[d]).T)
            af = _dot(kd_t, rvm[rows]) * bd
            a_ref[0, c, d] = (af[0:32] + af[32:64]) + (af[64:96] + af[96:128])


def _ordered_after(kernel_fn, pos):
    def wrapped(*refs):
        return kernel_fn(*refs[:pos], *refs[pos + 1:])
    return wrapped


_ORDER_SPEC = pl.BlockSpec(memory_space=pl.ANY)


def input_projection(x_lat, x_ctx, mod_tab, tab, lw, n_lat_tiles, after, x_block=0):
    b = mod_tab.shape[0]
    d = x_lat.shape[2]
    t = x_lat.shape[1] + x_ctx.shape[1]
    nt = t // TM
    nc = t // CHUNK
    cpt = TM // CHUNK
    const2 = lambda bi, j: (0, 0)
    const3 = lambda bi, j: (0, 0, 0)
    out_shape = (
        jax.ShapeDtypeStruct((b, MLA_HEADS, t, 256), MXU_DT),
        jax.ShapeDtypeStruct((b, MLA_HEADS, t, 256), MXU_DT),
        jax.ShapeDtypeStruct((b, MLA_HEADS, t, 256), MXU_DT),
        jax.ShapeDtypeStruct((b, t, SG_WIDTH), MXU_DT),
        jax.ShapeDtypeStruct((b, t, 512), F32),
        jax.ShapeDtypeStruct((b, t, RET_OUT), MXU_DT),
        jax.ShapeDtypeStruct((b, nc, 2, RET_QK, RET_OUT), F32),
    )
    head_spec = lambda w: pl.BlockSpec((1, MLA_HEADS, TM, w), lambda bi, j: (bi, 0, j, 0))
    tok_spec = lambda w: pl.BlockSpec((1, TM, w), lambda bi, j: (bi, j, 0))
    return pl.pallas_call(
        _ordered_after(functools.partial(_inproj_kernel, n_lat_tiles=n_lat_tiles), 15),
        grid=(b, nt),
        in_specs=[
            pl.BlockSpec((1, TM, d), lambda bi, j: (bi + x_block * b, jnp.minimum(j, n_lat_tiles - 1), 0)),
            pl.BlockSpec((1, TM, d), lambda bi, j: (bi + x_block * b, jnp.maximum(j - n_lat_tiles, 0), 0)),
            pl.BlockSpec((1, 1, N_MOD, d), lambda bi, j: (bi, j // n_lat_tiles, 0, 0)),
            pl.BlockSpec((TM, TAB_W), lambda bi, j: (j, 0)),
            pl.BlockSpec((d, IN_P), const2),
            pl.BlockSpec((1, MLA_Q_LORA), const2),
            pl.BlockSpec((1, MLA_KV_LORA), const2),
            pl.BlockSpec((MLA_Q_LORA, 1024), const2),
            pl.BlockSpec((MLA_KV_LORA, 1024), const2),
            pl.BlockSpec((1, SG_WIDTH), const2),
            pl.BlockSpec((1, SG_WIDTH), const2),
            pl.BlockSpec((SG_GROUPS * CHUNK, CHUNK), const2),
            pl.BlockSpec((CHUNK, SG_WIDTH), const2),
            pl.BlockSpec((2, CHUNK, 128), const3),
            pl.BlockSpec((128, RET_OUT), const2),
            _ORDER_SPEC,
        ],
        out_specs=(head_spec(256), head_spec(256), head_spec(256), tok_spec(SG_WIDTH),
                   tok_spec(512), tok_spec(RET_OUT),
                   pl.BlockSpec((1, cpt, 2, RET_QK, RET_OUT), lambda bi, j: (bi, j, 0, 0, 0))),
        out_shape=out_shape,
        compiler_params=_cparams(("parallel", "parallel")),
        name="input_projection",
    )(x_lat, x_ctx, mod_tab, tab, lw["w_in"], lw["q_g"], lw["kv_g"], lw["w_uq"], lw["w_ukv"],
      lw["sg_g"], lw["sg_b"], lw["sg_w"], lw["sg_bias"], lw["kdec"], lw["bd"], after)


def _attn_kernel(q_ref, k_ref, v_ref, o_ref, *, n_main, tk, tail):
    q = q_ref[0, 0]
    tq = q.shape[0]
    chunks = [(i * tk, tk) for i in range(n_main)] + ([(n_main * tk, tail)] if tail else [])

    def scores(ci):
        start, size = chunks[ci]
        return _dot_nt(q, k_ref[0, 0, start:start + size, :])

    m = jnp.full((tq, 1), -1e30, F32)
    acc = jnp.zeros((tq, 256), F32)
    s_next = scores(0)
    for ci, (start, size) in enumerate(chunks):
        s = s_next
        if ci + 1 < len(chunks):
            s_next = scores(ci + 1)
        m_new = jnp.maximum(m, jnp.max(s, axis=-1, keepdims=True))
        p = jnp.exp(s - m_new)
        acc = jnp.exp(m - m_new) * acc + _dot(_mx(p), v_ref[0, 0, start:start + size, :])
        m = m_new
    o_ref[0] = (acc[:, 0:MLA_V] / acc[:, MLA_V:MLA_V + 1]).astype(o_ref.dtype)


def mla_attention(q, k, v, s_len, lc):
    b, hn, t, _ = q.shape
    tq = min(ATT_TQ, s_len)
    tk = min(ATT_TK, s_len)
    kv_full = pl.BlockSpec((1, 1, t, 256), lambda bi, hi, i: (bi, hi, 0, 0))
    out_lat = pl.pallas_call(
        functools.partial(_attn_kernel, n_main=s_len // tk, tk=tk, tail=lc),
        grid=(b, hn, s_len // tq),
        in_specs=[pl.BlockSpec((1, 1, tq, 256), lambda bi, hi, i: (bi, hi, i, 0)), kv_full, kv_full],
        out_specs=pl.BlockSpec((1, tq, MLA_V), lambda bi, hi, i: (bi, i, hi)),
        out_shape=jax.ShapeDtypeStruct((b, s_len, MLA_OUT), MXU_DT),
        compiler_params=_cparams(("parallel", "parallel", "arbitrary")),
        name="mla_attention_latent",
    )(q, k, v)
    cblk = s_len // lc
    ctx_spec = pl.BlockSpec((1, 1, lc, 256), lambda bi, hi: (bi, hi, cblk, 0))
    out_ctx = pl.pallas_call(
        functools.partial(_attn_kernel, n_main=0, tk=tk, tail=lc),
        grid=(b, hn),
        in_specs=[ctx_spec, ctx_spec, ctx_spec],
        out_specs=pl.BlockSpec((1, lc, MLA_V), lambda bi, hi: (bi, 0, hi)),
        out_shape=jax.ShapeDtypeStruct((b, lc, MLA_OUT), MXU_DT),
        compiler_params=_cparams(("parallel", "parallel")),
        name="mla_attention_context",
    )(q, k, v)
    return out_lat, out_ctx


def _ret_scan_kernel(a_ref, cd_ref, s_ref, *, n_lat_chunks):
    nc = a_ref.shape[1]
    ncc = nc - n_lat_chunks
    cd_f, cd_b = cd_ref[0], cd_ref[1]

    def body(n, carry):
        sf, sb = carry
        cf = jnp.where(n < ncc, n_lat_chunks + n, n - ncc)
        cb = jnp.where(n < ncc, nc - 1 - n, n_lat_chunks - 1 - (n - ncc))
        s_ref[0, cf, 0] = sf
        s_ref[0, cb, 1] = sb
        return sf * cd_f + a_ref[0, cf, 0], sb * cd_b + a_ref[0, cb, 1]

    zero = jnp.zeros((RET_QK, RET_OUT), F32)
    lax.fori_loop(0, nc, body, (zero, zero))


def retention_scan(a, cd, n_lat_chunks):
    b, nc = a.shape[:2]
    blk = pl.BlockSpec((1, nc, 2, RET_QK, RET_OUT), lambda bi: (bi, 0, 0, 0, 0))
    return pl.pallas_call(
        functools.partial(_ret_scan_kernel, n_lat_chunks=n_lat_chunks),
        grid=(b,),
        in_specs=[blk, pl.BlockSpec((2, 1, RET_OUT), lambda bi: (0, 0, 0))],
        out_specs=blk,
        out_shape=jax.ShapeDtypeStruct(a.shape, F32),
        compiler_params=_cparams(("parallel",)),
        name="retention_scan",
    )(a, cd)


def _split_dot(x, ones2):
    hi = x.astype(BF16)
    lo = (x - hi.astype(F32)).astype(BF16)
    return _dot(jnp.concatenate([hi, lo], axis=1), ones2)


def _outproj_kernel(xl_ref, xc_ref, mlal_ref, mlac_ref, sg_ref, retp_ref, rv_ref, st_ref, mod_ref,
                    m_ref, qdec_ref, bd_ref, seg_ref, w_o_ref, lng_ref, lnb_ref, rw_ref, rb_ref, tri_ref,
                    x1_ref, h2_ref, idx_ref, gate_ref, rank_ref, cnt_ref, carry_ref, *, n_lat_tiles):
    @pl.when(pl.program_id(0) == 0)
    def _():
        carry_ref[...] = jnp.zeros_like(carry_ref)

    nbat = xl_ref.shape[0]
    units = [(bb, c) for bb in range(nbat) for c in range(TM // CHUNK)]
    rows = lambda c: slice(c * CHUNK, (c + 1) * CHUNK)
    g32 = _lane_group((CHUNK, 128), RET_QK)
    g64 = _lane_group((CHUNK, RET_OUT), RET_V)
    bd = bd_ref[...]
    seg2 = jnp.concatenate([seg_ref[...], seg_ref[...]], axis=0)
    m4 = jnp.concatenate([m_ref[hh] for hh in range(RET_HEADS)], axis=0)

    rq, s4 = {}, {}
    for u in units:
        bb, c = u
        rq[u] = retp_ref[bb, rows(c), 0:128]
        q4 = jnp.concatenate([jnp.where(g32 == hh, rq[u], 0.0) for hh in range(RET_HEADS)], axis=0)
        s4[u] = _dot_nt(_mx(q4), _mx(retp_ref[bb, rows(c), 128:256]))
    o = {}
    for u in units:
        bb, c = u
        r = _dot(_mx(s4[u] * m4), rv_ref[bb, rows(c), :])
        qd = jnp.concatenate([rq[u] * qdec_ref[0], rq[u] * qdec_ref[1]], axis=1)
        st2 = jnp.concatenate([jnp.concatenate([st_ref[bb, c, dd]] * RET_HEADS, axis=0) * bd
                               for dd in range(2)], axis=0)
        acc = _dot(_mx(qd), _mx(st2))
        for hh in range(RET_HEADS):
            acc = acc + jnp.where(g64 == hh, r[hh * CHUNK:(hh + 1) * CHUNK], 0.0)
        o[u] = acc
    oc = {}
    for u in units:
        oc[u] = o[u] - _split_dot(o[u], seg2) * (1.0 / RET_V)
    ret = {}
    for u in units:
        bb, c = u
        var = _split_dot(oc[u] * oc[u], seg2) * (1.0 / RET_V)
        rg = retp_ref[bb, rows(c), 256:512]
        ret[u] = _mx(oc[u] * lax.rsqrt(var + LN_EPS) * (rg * _sigmoid(rg)))

    is_ctx = pl.program_id(0) >= n_lat_tiles
    ys = []
    for bb in range(nbat):
        mla = jnp.where(is_ctx, mlac_ref[bb], mlal_ref[bb])
        cat = jnp.concatenate(
            [mla, sg_ref[bb], jnp.concatenate([ret[(bb, c)] for c in range(TM // CHUNK)], axis=0)], axis=1)
        ys.append(_dot(cat, w_o_ref[...]))
    logits = []
    for bb in range(nbat):
        mod = mod_ref[bb, 0]
        x = jnp.where(is_ctx, xc_ref[bb], xl_ref[bb])
        x1 = _ln(DEEPNORM_ALPHA * x + mod[2:3] * ys[bb]) * lng_ref[...] + lnb_ref[...]
        x1_ref[bb] = x1
        h2 = x1 * (1.0 + mod[4:5]) + mod[3:4]
        h2_ref[bb] = _pack_bf16_pairs(h2) if PACK_ROWS else h2
        h2_hi = h2.astype(BF16)
        h2_lo = (h2 - h2_hi.astype(F32)).astype(BF16)
        r2 = _dot_nt(rw_ref[...], jnp.concatenate([h2_hi, h2_lo], axis=0))
        logits.append(r2[0:N_EXPERTS, 0:TM] + r2[N_EXPERTS:, 0:TM] + r2[0:N_EXPERTS, TM:] + rb_ref[...])

    work = jnp.concatenate(logits, axis=1)
    e_iota = lax.broadcasted_iota(jnp.int32, work.shape, 0).astype(F32)
    vals, idxs = [], []
    for _ in range(TOP_K):
        mval = jnp.max(work, axis=0, keepdims=True)
        midx = jnp.min(jnp.where(work == mval, e_iota, float(N_EXPERTS)), axis=0, keepdims=True)
        vals.append(mval)
        idxs.append(midx)
        work = jnp.where(e_iota == midx, -jnp.inf, work)
    ex = [jnp.exp(vv - vals[0]) for vv in vals]
    den = ex[0] + ex[1] + ex[2] + ex[3]
    onehot = jnp.zeros_like(work)
    for kk in range(TOP_K):
        onehot = onehot + jnp.where(e_iota == idxs[kk], 1.0, 0.0)
    cols = lambda bb: slice(bb * TM, (bb + 1) * TM)
    prefixes = [_dot(onehot[:, cols(bb)].astype(BF16), tri_ref[...]) for bb in range(nbat)]
    count = carry_ref[:, 0:1]
    bases = []
    for bb in range(nbat):
        bases.append(count + prefixes[bb])
        count = count + jnp.sum(onehot[:, cols(bb)], axis=1, keepdims=True)
    base = jnp.concatenate(bases, axis=1)
    for kk in range(TOP_K):
        gate = ex[kk] / den
        rank = jnp.sum(jnp.where(e_iota == idxs[kk], base, 0.0), axis=0, keepdims=True).astype(jnp.int32)
        for bb in range(nbat):
            gate_ref[bb, 0, kk:kk + 1, :] = gate[:, cols(bb)]
            idx_ref[bb, 0, kk:kk + 1, :] = idxs[kk][:, cols(bb)].astype(jnp.int32)
            rank_ref[bb, 0, kk:kk + 1, :] = rank[:, cols(bb)]
    carry_ref[...] = jnp.broadcast_to(count, carry_ref.shape)
    cnt_ref[...] = carry_ref[...].astype(jnp.int32)


def output_projection(x_lat, x_ctx, mla_lat, mla_ctx, sg, retp, rv, st, mod_tab, lw, n_lat_tiles, x_block=0):
    b = mod_tab.shape[0]
    d = x_lat.shape[2]
    t = x_lat.shape[1] + x_ctx.shape[1]
    nt = t // TM
    cpt = TM // CHUNK
    const2 = lambda j: (0, 0)
    const3 = lambda j: (0, 0, 0)
    tok_spec = lambda w: pl.BlockSpec((b, TM, w), lambda j: (0, j, 0))
    route_spec = pl.BlockSpec((b, 1, TOP_K, TM), lambda j: (0, j, 0, 0))
    route_shape = lambda dt: jax.ShapeDtypeStruct((b, nt, TOP_K, TM), dt)
    h2w = d // 2 if PACK_ROWS else d
    return pl.pallas_call(
        functools.partial(_outproj_kernel, n_lat_tiles=n_lat_tiles),
        grid=(nt,),
        in_specs=[
            pl.BlockSpec((b, TM, d), lambda j: (x_block, jnp.minimum(j, n_lat_tiles - 1), 0)),
            pl.BlockSpec((b, TM, d), lambda j: (x_block, jnp.maximum(j - n_lat_tiles, 0), 0)),
            pl.BlockSpec((b, TM, MLA_OUT), lambda j: (0, jnp.minimum(j, n_lat_tiles - 1), 0)),
            pl.BlockSpec((b, TM, MLA_OUT), lambda j: (0, jnp.maximum(j - n_lat_tiles, 0), 0)),
            tok_spec(SG_WIDTH), tok_spec(512), tok_spec(RET_OUT),
            pl.BlockSpec((b, cpt, 2, RET_QK, RET_OUT), lambda j: (0, j, 0, 0, 0)),
            pl.BlockSpec((b, 1, N_MOD, d), lambda j: (0, j // n_lat_tiles, 0, 0)),
            pl.BlockSpec((RET_HEADS, CHUNK, CHUNK), const3),
            pl.BlockSpec((2, CHUNK, 128), const3),
            pl.BlockSpec((128, RET_OUT), const2),
            pl.BlockSpec((RET_OUT, RET_OUT), const2),
            pl.BlockSpec((d, d), const2),
            pl.BlockSpec((1, d), const2),
            pl.BlockSpec((1, d), const2),
            pl.BlockSpec((2 * N_EXPERTS, d), const2),
            pl.BlockSpec((N_EXPERTS, 1), const2),
            pl.BlockSpec((TM, TM), const2),
        ],
        out_specs=(tok_spec(d), tok_spec(h2w), route_spec, route_spec, route_spec,
                   pl.BlockSpec((N_EXPERTS, 128), const2)),
        out_shape=(jax.ShapeDtypeStruct((b, t, d), F32),
                   jax.ShapeDtypeStruct((b, t, h2w), jnp.uint32 if PACK_ROWS else F32),
                   route_shape(jnp.int32), route_shape(F32), route_shape(jnp.int32),
                   jax.ShapeDtypeStruct((N_EXPERTS, 128), jnp.int32)),
        scratch_shapes=[pltpu.VMEM((N_EXPERTS, 128), F32)],
        compiler_params=_cparams(("arbitrary",)),
        name="output_projection",
    )(x_lat, x_ctx, mla_lat, mla_ctx, sg, retp, rv, st, mod_tab, lw["ret_m"], lw["qdec"], lw["bd"], lw["seg"],
      lw["w_o"], lw["ln1_g"], lw["ln1_b"], lw["router_w"], lw["router_b"], lw["tri"])


_DEINT = 256


def _pack_bf16_pairs(v):
    bits = lax.bitcast_convert_type(v.astype(BF16).astype(F32), jnp.uint32)
    half = bits.shape[1] // 2
    return bits[:, :half] | (bits[:, half:] >> 16)


def _unpack_bf16_pairs(w):
    return (lax.bitcast_convert_type(w & jnp.uint32(0xFFFF0000), F32),
            lax.bitcast_convert_type(w << 16, F32))


def _expert_block(x_ref, wg, wl, bg_ref, bl_ref, wd, bd_ref, y_ref):
    if PACK_ROWS:
        hi, lo = _unpack_bf16_pairs(x_ref[...])
        xb = jnp.concatenate([hi.astype(BF16), lo.astype(BF16)], axis=1)
    else:
        xb = x_ref[...]
    glu = jnp.minimum(_dot(xb, wg[0]) + bg_ref[0], SWIGLU_LIMIT)
    lin = jnp.clip(_dot(xb, wl[0]) + bl_ref[0], -SWIGLU_LIMIT, SWIGLU_LIMIT)
    act = glu * _sigmoid(SWIGLU_ALPHA * glu) * (lin + 1.0)
    y = _dot(_mx(act), wd[0]) + bd_ref[0]
    y_ref[...] = _pack_bf16_pairs(y) if PACK_ROWS else y


def _expert_first_kernel(be_ref, nu_ref, x_ref, wgu_ref, bg_ref, bl_ref, wdn_ref, bd_ref, perm_ref,
                         y_ref, wg_o, wl_o, wd_o):
    i = pl.program_id(0)
    active = i < nu_ref[0]
    fresh = jnp.logical_or(i == 0, be_ref[i] != be_ref[jnp.maximum(i - 1, 0)])

    @pl.when(jnp.logical_and(active, fresh))
    def _():
        half = _DEINT // 2
        for c in range(2 * D_EXPERT // _DEINT):
            r = _dot(_mx(wgu_ref[0, 0, :, _DEINT * c:_DEINT * (c + 1)]), perm_ref[...])
            wg_o[0, :, half * c:half * (c + 1)] = r[:, :half].astype(wg_o.dtype)
            wl_o[0, :, half * c:half * (c + 1)] = r[:, half:].astype(wl_o.dtype)
        wd_o[0] = wdn_ref[0, 0].astype(wd_o.dtype)

    @pl.when(active)
    def _():
        _expert_block(x_ref, wg_o, wl_o, bg_ref, bl_ref, wd_o, bd_ref, y_ref)

    @pl.when(jnp.logical_not(active))
    def _():
        y_ref[...] = jnp.zeros_like(y_ref)


def _expert_kernel(be_ref, nu_ref, x_ref, wg_ref, wl_ref, bg_ref, bl_ref, wd_ref, bd_ref, y_ref):
    del be_ref
    active = pl.program_id(0) < nu_ref[0]

    @pl.when(active)
    def _():
        _expert_block(x_ref, wg_ref, wl_ref, bg_ref, bl_ref, wd_ref, bd_ref, y_ref)

    @pl.when(jnp.logical_not(active))
    def _():
        y_ref[...] = jnp.zeros_like(y_ref)


def expert_ffn_first(xg, block_e, n_used, w_gate_up, w_down, li, lw, after):
    cap, xw = xg.shape
    d, de = D_MODEL, D_EXPERT
    xmap = lambda i, be, nu: (jnp.minimum(i, nu[0] - 1), 0)
    wmap = lambda i, be, nu: (be[i], 0, 0)
    lmap = lambda i, be, nu: (li, be[i], 0, 0)
    grid_spec = pltpu.PrefetchScalarGridSpec(
        num_scalar_prefetch=2,
        grid=(cap // MOE_BM,),
        in_specs=[pl.BlockSpec((MOE_BM, xw), xmap),
                  pl.BlockSpec((1, 1, d, 2 * de), lmap),
                  pl.BlockSpec((1, 1, de), wmap), pl.BlockSpec((1, 1, de), wmap),
                  pl.BlockSpec((1, 1, de, d), lmap), pl.BlockSpec((1, 1, d), wmap),
                  pl.BlockSpec((_DEINT, _DEINT), lambda i, be, nu: (0, 0)),
                  _ORDER_SPEC],
        out_specs=(pl.BlockSpec((MOE_BM, xw), lambda i, be, nu: (i, 0)),
                   pl.BlockSpec((1, d, de), wmap), pl.BlockSpec((1, d, de), wmap),
                   pl.BlockSpec((1, de, d), wmap)),
    )
    y, wg, wl, wd = pl.pallas_call(
        _ordered_after(_expert_first_kernel, 9),
        grid_spec=grid_spec,
        out_shape=(jax.ShapeDtypeStruct((cap, xw), xg.dtype),
                   jax.ShapeDtypeStruct((N_EXPERTS, d, de), MXU_DT),
                   jax.ShapeDtypeStruct((N_EXPERTS, d, de), MXU_DT),
                   jax.ShapeDtypeStruct((N_EXPERTS, de, d), MXU_DT)),
        compiler_params=pltpu.CompilerParams(dimension_semantics=("arbitrary",),
                                             vmem_limit_bytes=EXPERT_FIRST_VMEM_LIMIT),
        name="expert_ffn_first",
    )(block_e, n_used, xg, w_gate_up, lw["b_glu"], lw["b_lin"], w_down, lw["b_down"], lw["deint"], after)
    return y, (wg, wl, wd)


def expert_ffn(xg, block_e, n_used, ew, lw, after):
    cap, xw = xg.shape
    d, de = D_MODEL, D_EXPERT
    xmap = lambda i, be, nu: (jnp.minimum(i, nu[0] - 1), 0)
    wmap = lambda i, be, nu: (be[i], 0, 0)
    grid_spec = pltpu.PrefetchScalarGridSpec(
        num_scalar_prefetch=2,
        grid=(cap // MOE_BM,),
        in_specs=[pl.BlockSpec((MOE_BM, xw), xmap),
                  pl.BlockSpec((1, d, de), wmap), pl.BlockSpec((1, d, de), wmap),
                  pl.BlockSpec((1, 1, de), wmap), pl.BlockSpec((1, 1, de), wmap),
                  pl.BlockSpec((1, de, d), wmap), pl.BlockSpec((1, 1, d), wmap),
                  _ORDER_SPEC],
        out_specs=pl.BlockSpec((MOE_BM, xw), lambda i, be, nu: (i, 0)),
    )
    return pl.pallas_call(
        _ordered_after(_expert_kernel, 9),
        grid_spec=grid_spec,
        out_shape=jax.ShapeDtypeStruct((cap, xw), xg.dtype),
        compiler_params=_cparams(("arbitrary",)),
        name="expert_ffn",
    )(block_e, n_used, xg, ew[0], ew[1], lw["b_glu"], lw["b_lin"], ew[2], lw["b_down"], after)


def _combine_ln2_kernel(x_ref, y_ref, gate_ref, mod_ref, g_ref, b_ref, ol_ref, oc_ref, *, n_lat_tiles):
    is_ctx = pl.program_id(0) >= n_lat_tiles
    for bb in range(x_ref.shape[0]):
        gates = gate_ref[bb]
        if PACK_ROWS:
            f_hi, f_lo = 0.0, 0.0
            for kk in range(TOP_K):
                hi, lo = _unpack_bf16_pairs(y_ref[kk, bb])
                f_hi = f_hi + gates[:, kk:kk + 1] * hi
                f_lo = f_lo + gates[:, kk:kk + 1] * lo
            f = jnp.concatenate([f_hi, f_lo], axis=1)
        else:
            f = gates[:, 0:1] * y_ref[0, bb]
            for kk in range(1, TOP_K):
                f = f + gates[:, kk:kk + 1] * y_ref[kk, bb]
        mod = mod_ref[bb, 0]
        res = _ln(DEEPNORM_ALPHA * x_ref[bb] + mod[5:6] * f) * g_ref[...] + b_ref[...]

        @pl.when(is_ctx)
        def _():
            oc_ref[bb] = res

        @pl.when(jnp.logical_not(is_ctx))
        def _():
            ol_ref[bb] = res


def combine_deepnorm2(x1, yg, gates, mod_tab, lw, n_lat_tiles, after):
    b, t, d = x1.shape
    s_len = n_lat_tiles * TM
    tok = pl.BlockSpec((b, TM, d), lambda j: (0, j, 0))
    vec = pl.BlockSpec((1, d), lambda j: (0, 0))
    return pl.pallas_call(
        _ordered_after(functools.partial(_combine_ln2_kernel, n_lat_tiles=n_lat_tiles), 6),
        grid=(t // TM,),
        in_specs=[tok,
                  pl.BlockSpec((TOP_K, b, TM, yg.shape[-1]), lambda j: (0, 0, j, 0)),
                  pl.BlockSpec((b, TM, TOP_K), lambda j: (0, j, 0)),
                  pl.BlockSpec((b, 1, N_MOD, d), lambda j: (0, j // n_lat_tiles, 0, 0)), vec, vec,
                  _ORDER_SPEC],
        out_specs=(pl.BlockSpec((b, TM, d), lambda j: (0, jnp.minimum(j, n_lat_tiles - 1), 0)),
                   pl.BlockSpec((b, TM, d), lambda j: (0, jnp.maximum(j - n_lat_tiles, 0), 0))),
        out_shape=(jax.ShapeDtypeStruct((b, s_len, d), F32), jax.ShapeDtypeStruct((b, t - s_len, d), F32)),
        compiler_params=_cparams(("arbitrary",)),
        name="combine_deepnorm2",
    )(x1, yg, gates, mod_tab, lw["ln2_g"], lw["ln2_b"], after)


def _rotation_tables(s_len, lc):
    rows = s_len // GRID_W
    row = jnp.broadcast_to(jnp.arange(rows, dtype=F32)[:, None], (rows, GRID_W)).reshape(-1)
    col = jnp.broadcast_to(jnp.arange(GRID_W, dtype=F32)[None, :], (rows, GRID_W)).reshape(-1)
    inv = ROPE_BASE ** (-jnp.arange(ROPE_AXIS_FREQS, dtype=F32) / ROPE_AXIS_FREQS)
    ar, ac = row[:, None] * inv, col[:, None] * inv
    c64 = jnp.concatenate([jnp.cos(ar), jnp.cos(ar), jnp.cos(ac), jnp.cos(ac)], axis=1)
    s64 = jnp.concatenate([-jnp.sin(ar), jnp.sin(ar), -jnp.sin(ac), jnp.sin(ac)], axis=1)
    c64 = jnp.concatenate([c64, jnp.ones((lc, 64), F32)], axis=0)
    s64 = jnp.concatenate([s64, jnp.zeros((lc, 64), F32)], axis=0)
    half = RET_QK // 2
    pos = jnp.concatenate([lc + jnp.arange(s_len, dtype=F32), jnp.arange(lc, dtype=F32)])
    inv_r = 1.0 / (RET_ROPE_BASE ** jnp.linspace(0.0, 1.0, half, dtype=F32))
    ang = pos[:, None] * inv_r
    rc = jnp.tile(jnp.concatenate([jnp.cos(ang), jnp.cos(ang)], axis=1), (1, RET_HEADS))
    rs = jnp.tile(jnp.concatenate([-jnp.sin(ang), jnp.sin(ang)], axis=1), (1, RET_HEADS))
    qs = RET_QK ** -0.5
    return jnp.concatenate([
        jnp.tile(c64, (1, MLA_HEADS)) * MLA_SCALE, jnp.tile(s64, (1, MLA_HEADS)) * MLA_SCALE,
        c64, s64, rc * qs, rs * qs, rc, rs], axis=1)


def _in_perm():
    a = np.arange
    return np.concatenate([
        a(0, 640), a(704, 1216), a(1216, 1344), a(1344, 1472),
        1216 + _swap16(a(128)), 1344 + _swap16(a(128)), a(1472, 1984),
        640 + a(64), 640 + _swap16(a(64))])


def _uq_perm():
    a = np.arange
    nope = [h * MLA_QK + a(MLA_NOPE) for h in range(MLA_HEADS)]
    rope = [h * MLA_QK + MLA_NOPE + a(MLA_ROPE) for h in range(MLA_HEADS)]
    part = [h * MLA_QK + MLA_NOPE + _swap16(a(MLA_ROPE)) for h in range(MLA_HEADS)]
    return np.concatenate(nope + rope + part)


def _layer_weights(p):
    nl = p["w_in"].shape[0]
    lgf = jax.nn.log_sigmoid(p["ret_decay_fwd"].astype(F32))
    lgb = jax.nn.log_sigmoid(p["ret_decay_bwd"].astype(F32))
    h128 = np.arange(128) // RET_QK
    h256 = np.arange(RET_OUT) // RET_V
    a = jnp.arange(CHUNK, dtype=F32)[None, :, None]
    lf, lb = lgf[:, h128][:, None, :], lgb[:, h128][:, None, :]
    kdec = jnp.stack([jnp.exp(lf * (CHUNK - 1.0 - a)), jnp.exp(lb * a)], axis=1)
    qdec = jnp.stack([jnp.exp(lf * (a + 1.0)), jnp.exp(lb * (CHUNK - a))], axis=1)
    i = jnp.arange(CHUNK, dtype=F32)[:, None]
    j = jnp.arange(CHUNK, dtype=F32)[None, :]
    dif = (i - j)[None, None]
    ret_m = jnp.where(dif >= 0, jnp.exp(lgf[:, :, None, None] * jnp.maximum(dif, 0.0)),
                      jnp.exp(lgb[:, :, None, None] * jnp.maximum(-dif, 0.0)))
    cd = jnp.stack([jnp.exp(lgf[:, h256] * CHUNK), jnp.exp(lgb[:, h256] * CHUNK)], axis=1)[:, :, None, :]
    bd = (h128[:, None] == h256[None, :]).astype(np.float32)
    seg = (h256[:, None] == h256[None, :]).astype(np.float32)
    tri = (np.arange(TM)[:, None] < np.arange(TM)[None, :]).astype(np.float32)
    jj = np.arange(_DEINT // 2)
    deint = np.zeros((_DEINT, _DEINT), np.float32)
    deint[2 * jj, jj] = 1.0
    deint[2 * jj + 1, _DEINT // 2 + jj] = 1.0
    rw_t = jnp.swapaxes(p["router_w"], 1, 2)
    rw_hi = rw_t.astype(BF16)
    rw_lo = (rw_t - rw_hi.astype(F32)).astype(BF16)
    sg_bias = jnp.repeat(jnp.swapaxes(p["sg_b"], 1, 2), SG_WIDTH // SG_GROUPS, axis=2)
    bgu = p["b_gate_up"]
    return {
        "w_in": p["w_in"][:, :, _in_perm()].astype(MXU_DT),
        "q_g": p["mla_q_norm_g"][:, None, :], "kv_g": p["mla_kv_norm_g"][:, None, :],
        "w_uq": p["mla_w_uq"][:, :, _uq_perm()].astype(MXU_DT),
        "w_ukv": p["mla_w_ukv"].astype(MXU_DT),
        "sg_g": p["sg_norm_g"][:, None, :], "sg_b": p["sg_norm_b"][:, None, :],
        "sg_w": p["sg_w"].reshape(nl, SG_GROUPS * CHUNK, CHUNK).astype(MXU_DT),
        "sg_bias": sg_bias,
        "kdec": kdec, "qdec": qdec, "ret_m": ret_m, "cd": cd,
        "bd": jnp.broadcast_to(jnp.asarray(bd), (nl,) + bd.shape),
        "seg": jnp.broadcast_to(jnp.asarray(seg, BF16), (nl,) + seg.shape),
        "tri": jnp.broadcast_to(jnp.asarray(tri, BF16), (nl,) + tri.shape),
        "w_o": p["w_o"].astype(MXU_DT),
        "ln1_g": p["ln1_g"][:, None, :], "ln1_b": p["ln1_b"][:, None, :],
        "ln2_g": p["ln2_g"][:, None, :], "ln2_b": p["ln2_b"][:, None, :],
        "router_w": jnp.concatenate([rw_hi, rw_lo], axis=1),
        "router_b": p["router_b"][:, :, None],
        "b_glu": bgu[:, :, None, 0::2], "b_lin": bgu[:, :, None, 1::2],
        "b_down": p["b_down"][:, :, None, :],
        "deint": jnp.broadcast_to(jnp.asarray(deint, MXU_DT), (nl,) + deint.shape),
    }


def _route(idx, rank, counts, every_expert):
    n_assign = idx.shape[1] * TOP_K
    nb = -(-n_assign // MOE_BM) + N_EXPERTS
    blocks = (counts + MOE_BM - 1) // MOE_BM
    padded = (jnp.maximum(blocks, 1) if every_expert else blocks) * MOE_BM
    pad_end = jnp.cumsum(padded)
    pad_start = pad_end - padded
    experts = jnp.arange(N_EXPERTS, dtype=jnp.int32)
    dest = rank + jnp.sum(jnp.where(idx[..., None] == experts, pad_start, 0), axis=-1)
    blk_start = jnp.arange(nb, dtype=jnp.int32) * MOE_BM
    block_e = jnp.minimum(jnp.sum((pad_end[None, :] <= blk_start[:, None]).astype(jnp.int32), axis=1),
                          N_EXPERTS - 1)
    n_used = (pad_end[-1] // MOE_BM).astype(jnp.int32).reshape(1)
    return dest.astype(jnp.int32), block_e, n_used, nb


def _sc_workers():
    info = plsc.get_sparse_core_info()
    return info.num_cores, info.num_cores * info.num_subcores


def sc_dispatch(rows, dest3, cap, after):
    n, w = rows.shape
    nch, kk, c = dest3.shape
    ncores, nw = _sc_workers()
    assert nch % nw == 0, "token chunks must split evenly over the vector subcores"
    per_w = nch // nw
    mesh = plsc.VectorSubcoreMesh(core_axis_name="c", subcore_axis_name="s")

    @functools.partial(
        pl.kernel, mesh=mesh, out_type=jax.ShapeDtypeStruct((cap, w), rows.dtype),
        scratch_types=[pltpu.VMEM((kk, c), jnp.int32), pltpu.VMEM((c, w), rows.dtype)])
    def scatter_rows(h_hbm, d_hbm, after_hbm, o_hbm, idx_v, rows_v):
        del after_hbm
        wid = lax.axis_index("s") * ncores + lax.axis_index("c")

        @pl.loop(0, per_w)
        def _(j):
            ch = wid * per_w + j
            pltpu.sync_copy(d_hbm.at[ch], idx_v)
            pltpu.sync_copy(h_hbm.at[pl.ds(ch * c, c)], rows_v)
            for q in range(kk):
                pltpu.sync_copy(rows_v, o_hbm.at[idx_v.at[q]])

    return scatter_rows(rows, dest3, after)


def sc_combine_gather(y, dest3, n, after):
    cap, d = y.shape
    nch, kk, c = dest3.shape
    ncores, nw = _sc_workers()
    assert nch % nw == 0, "token chunks must split evenly over the vector subcores"
    per_w = nch // nw
    mesh = plsc.VectorSubcoreMesh(core_axis_name="c", subcore_axis_name="s")

    @functools.partial(
        pl.kernel, mesh=mesh, out_type=jax.ShapeDtypeStruct((kk, n, d), y.dtype),
        scratch_types=[pltpu.VMEM((kk, c), jnp.int32), pltpu.VMEM((c, d), y.dtype)])
    def gather_rows(y_hbm, d_hbm, after_hbm, o_hbm, idx_v, rows_v):
        del after_hbm
        wid = lax.axis_index("s") * ncores + lax.axis_index("c")

        @pl.loop(0, per_w)
        def _(j):
            ch = wid * per_w + j
            pltpu.sync_copy(d_hbm.at[ch], idx_v)
            for q in range(kk):
                pltpu.sync_copy(y_hbm.at[idx_v.at[q]], rows_v)
                pltpu.sync_copy(rows_v, o_hbm.at[q, pl.ds(ch * c, c)])

    return gather_rows(y, dest3, after)


def kernel(x, c, ctx, c_ctx, ada_w, ada_b, w_in, mla_q_norm_g, mla_kv_norm_g, mla_w_uq, mla_w_ukv,
           sg_norm_g, sg_norm_b, sg_w, sg_b, ret_decay_fwd, ret_decay_bwd, w_o, ln1_g, ln1_b,
           router_w, router_b, w_gate_up, b_gate_up, w_down, b_down, ln2_g, ln2_b):
    b, s_len, d = x.shape
    lc = ctx.shape[1]
    assert d == D_MODEL and lc % TM == 0 and s_len % lc == 0 and s_len % GRID_W == 0
    assert b + 1 <= 8
    t = s_len + lc
    n_lat_tiles = s_len // TM
    params = dict(w_in=w_in, mla_q_norm_g=mla_q_norm_g, mla_kv_norm_g=mla_kv_norm_g, mla_w_uq=mla_w_uq,
                  mla_w_ukv=mla_w_ukv, sg_norm_g=sg_norm_g, sg_norm_b=sg_norm_b, sg_w=sg_w, sg_b=sg_b,
                  ret_decay_fwd=ret_decay_fwd, ret_decay_bwd=ret_decay_bwd, w_o=w_o, ln1_g=ln1_g,
                  ln1_b=ln1_b, router_w=router_w, router_b=router_b, w_gate_up=w_gate_up,
                  b_gate_up=b_gate_up, w_down=w_down, b_down=b_down, ln2_g=ln2_g, ln2_b=ln2_b)
    lws = _layer_weights(params)
    tab = _rotation_tables(s_len, lc)

    c_rows = jnp.concatenate([c, c_ctx[None, :], jnp.zeros((8 - b - 1, d), F32)], axis=0)
    mod = ada_modulation(c_rows, ada_w, ada_b).reshape(DEPTH, 8, N_MOD, d)
    mod_tab = jnp.stack([mod[:, :b], jnp.broadcast_to(mod[:, b:b + 1], (DEPTH, b, N_MOD, d))], axis=2)

    n_streams = N_STREAMS if b % N_STREAMS == 0 else 1
    bs = b // n_streams
    xs = [(x, ctx)] * n_streams
    order = c_rows
    held = None
    for li in range(DEPTH):
        lw = {k: v[li] for k, v in lws.items()}
        mts = [mod_tab[li, si * bs:(si + 1) * bs] for si in range(n_streams)]
        fronts = []
        for si in range(n_streams):
            fr = _front(xs[si], si if li == 0 else 0, mts[si], tab, lw, s_len, lc, every_expert=si == 0,
                        after=order)
            order = fr["cnt"]
            if si == 0 and held is not None:
                xs[-1] = _finish(*held, gather_after=fr["mixed"], after=order)
                held = None
            fronts.append(fr)
        for si, fr in enumerate(fronts):
            start = fronts[si + 1]["mixed"] if si + 1 < n_streams else fr["cnt"]
            fr["xg"] = sc_dispatch(fr.pop("h2"), fr["dest3"], fr["cap"], start)
        ys = []
        for si in range(n_streams):
            fr = fronts[si]
            if si == 0:
                y, ew = expert_ffn_first(fr["xg"], fr["block_e"], fr["n_used"], w_gate_up, w_down, li, lw,
                                         after=order)
            else:
                y = expert_ffn(fr["xg"], fr["block_e"], fr["n_used"], ew, lw, after=order)
            order = y
            ys.append(y)
        for si in range(n_streams):
            if si == n_streams - 1 and n_streams > 1 and li + 1 < DEPTH:
                held = (fronts[si], ys[si], mts[si], lw, s_len)
            else:
                xs[si] = _finish(fronts[si], ys[si], mts[si], lw, s_len, ys[si], after=order)
                order = xs[si][0]
    return jnp.concatenate([x_lat for x_lat, _ in xs], axis=0)


def _front(x_pair, x_block, mt, tab, lw, s_len, lc, every_expert, after):
    x_lat, x_ctx = x_pair
    b = mt.shape[0]
    t = s_len + lc
    n_lat_tiles = s_len // TM
    q, k, v, sg, retp, rv, a = input_projection(x_lat, x_ctx, mt, tab, lw, n_lat_tiles, after, x_block)
    mla_lat, mla_ctx = mla_attention(q, k, v, s_len, lc)
    st = retention_scan(a, lw["cd"], s_len // CHUNK)
    x1, h2, idx, gates, rank, cnt = output_projection(
        x_lat, x_ctx, mla_lat, mla_ctx, sg, retp, rv, st, mt, lw, n_lat_tiles, x_block)
    to_tok = lambda z: z.transpose(2, 0, 1, 3).reshape(TOP_K, b * t)
    dest, block_e, n_used, nb = _route(to_tok(idx), to_tok(rank), cnt[:, 0], every_expert)
    assert (b * t) % SC_CHUNK == 0
    dest3 = dest.reshape(TOP_K, (b * t) // SC_CHUNK, SC_CHUNK).transpose(1, 0, 2)
    return dict(x1=x1, gates=gates, cnt=cnt, dest3=dest3, block_e=block_e, n_used=n_used,
                h2=h2.reshape(b * t, h2.shape[-1]), cap=nb * MOE_BM, mixed=a)


def _finish(fr, y, mt, lw, s_len, gather_after, after):
    b, t, d = fr["x1"].shape
    yg = sc_combine_gather(y, fr["dest3"], b * t, gather_after).reshape(TOP_K, b, t, y.shape[-1])
    gates_tok = fr["gates"].transpose(0, 1, 3, 2).reshape(b, t, TOP_K)
    return combine_deepnorm2(fr["x1"], yg, gates_tok, mt, lw, s_len // TM, after)
```

```python
import functools

import numpy as np
import jax
import jax.numpy as jnp
from jax import lax
from jax.experimental import pallas as pl
from jax.experimental.pallas import tpu as pltpu
from jax.experimental.pallas import tpu_sc as plsc

F32 = jnp.float32
BF16 = jnp.bfloat16
MXU_DT = BF16
PACK_ROWS = True

D_MODEL = 1024
DEPTH = 4
GRID_W = 64
MLA_HEADS = 4
MLA_NOPE = 128
MLA_ROPE = 64
MLA_V = 128
MLA_Q_LORA = 384
MLA_KV_LORA = 256
MLA_QK = MLA_NOPE + MLA_ROPE
MLA_SCALE = MLA_QK ** -0.5
ROPE_BASE = 10000.0
ROPE_AXIS_FREQS = MLA_ROPE // 4
SG_GROUPS = 4
SG_WIDTH = 256
SG_CHUNK = 128
RET_HEADS = 4
RET_QK = 32
RET_V = 64
RET_CHUNK = 128
RET_ROPE_BASE = 10000.0
N_EXPERTS = 32
TOP_K = 4
D_EXPERT = 1024
SWIGLU_LIMIT = 7.0
SWIGLU_ALPHA = 1.702
N_MOD = 6
LN_EPS = 1e-5
RMS_EPS = 1e-6
DEEPNORM_ALPHA = (2 * DEPTH) ** 0.25
MLA_OUT = MLA_HEADS * MLA_V
RET_OUT = RET_HEADS * RET_V

TM = 256
CHUNK = 128
MOE_BM = 512
SC_CHUNK = 48
N_STREAMS = 2
ATT_TQ = 1024
ATT_TK = 2048
VMEM_LIMIT = 48 * 2 ** 20
EXPERT_FIRST_VMEM_LIMIT = 56 * 2 ** 20

_O_CQ, _O_CKV, _O_SGU, _O_SGV = 0, 384, 640, 896
_O_RQ, _O_RK, _O_RQS, _O_RKS, _O_RV, _O_RG, _O_KR = 1152, 1280, 1408, 1536, 1664, 1920, 2176
IN_P = 2304
_T_QC, _T_QS, _T_KCS, _T_RQC, _T_RQS, _T_RKC, _T_RKS = 0, 256, 512, 640, 768, 896, 1024
TAB_W = 1152


def _cparams(sem):
    return pltpu.CompilerParams(dimension_semantics=sem, vmem_limit_bytes=VMEM_LIMIT)


def _dot(a, b):
    return jnp.dot(a, b, preferred_element_type=F32)


def _dot_nt(a, b):
    return lax.dot_general(a, b, (((1,), (1,)), ((), ())), preferred_element_type=F32)


def _mx(a):
    return a.astype(MXU_DT)


def _swap16(j):
    return (j // 32) * 32 + ((j % 32) + 16) % 32


_ERF_ALPHA = (-2.72614225801306e-10, 2.77068142495902e-08, -2.10102402082508e-06,
              -5.69250639462346e-05, -7.34990630326855e-04, -2.95459980854025e-03,
              -1.60960333262415e-02)
_ERF_BETA = (-1.45660718464996e-05, -2.13374055278905e-04, -1.68282697438203e-03,
             -7.37332916720468e-03, -1.42647390514189e-02)


def _erf(x):
    x = jnp.clip(x, -4.0, 4.0)
    x2 = x * x
    p = jnp.full_like(x, _ERF_ALPHA[0])
    for c in _ERF_ALPHA[1:]:
        p = p * x2 + c
    q = jnp.full_like(x, _ERF_BETA[0])
    for c in _ERF_BETA[1:]:
        q = q * x2 + c
    return x * p / q


def _gelu(x):
    return 0.5 * x * (1.0 + _erf(x * 0.7071067811865476))


def _sigmoid(x):
    return 1.0 / (1.0 + jnp.exp(-x))


def _ln(x):
    xc = x - jnp.mean(x, axis=-1, keepdims=True)
    return xc * lax.rsqrt(jnp.mean(xc * xc, axis=-1, keepdims=True) + LN_EPS)


def _lane_group(shape, width):
    return lax.broadcasted_iota(jnp.int32, shape, len(shape) - 1) // width


def _ada_kernel(c_ref, w_ref, b_ref, o_ref):
    c = c_ref[...]
    o_ref[0] = _dot(c * _sigmoid(c), w_ref[0]) + b_ref[0]


def ada_modulation(c_rows, ada_w, ada_b):
    nl, d, n = ada_w.shape
    tn = 1536
    return pl.pallas_call(
        _ada_kernel,
        grid=(nl, n // tn),
        in_specs=[pl.BlockSpec((8, d), lambda l, j: (0, 0)),
                  pl.BlockSpec((1, d, tn), lambda l, j: (l, 0, j)),
                  pl.BlockSpec((1, 1, tn), lambda l, j: (l, 0, j))],
        out_specs=pl.BlockSpec((1, 8, tn), lambda l, j: (l, 0, j)),
        out_shape=jax.ShapeDtypeStruct((nl, 8, n), F32),
        compiler_params=_cparams(("arbitrary", "arbitrary")),
        name="ada_modulation",
    )(c_rows, ada_w, ada_b.reshape(nl, 1, n))


def _inproj_kernel(xl_ref, xc_ref, mod_ref, tab_ref, w_in_ref, qg_ref, kvg_ref, w_uq_ref, w_ukv_ref,
                   sgg_ref, sgb_ref, sgw_ref, sgbias_ref, kdec_ref, bd_ref,
                   q_ref, k_ref, v_ref, sg_ref, retp_ref, rv_ref, a_ref, *, n_lat_tiles):
    x = jnp.where(pl.program_id(1) >= n_lat_tiles, xc_ref[0], xl_ref[0])
    mod = mod_ref[0, 0]
    h = x * (1.0 + mod[1:2]) + mod[0:1]
    p = _dot(_mx(h), w_in_ref[...])
    tab = tab_ref[...]

    cq = p[:, _O_CQ:_O_CQ + MLA_Q_LORA]
    cq = cq * lax.rsqrt(jnp.mean(cq * cq, axis=-1, keepdims=True) + RMS_EPS) * qg_ref[...]
    qa = _dot(_mx(cq), w_uq_ref[...])
    rot = (qa[:, 512:768] * tab[:, _T_QC:_T_QC + 256]
           + qa[:, 768:1024] * tab[:, _T_QS:_T_QS + 256])
    for hh in range(MLA_HEADS):
        q_ref[0, hh, :, 0:128] = (qa[:, 128 * hh:128 * hh + 128] * MLA_SCALE).astype(q_ref.dtype)
        g = hh // 2
        q_ref[0, hh, :, 128:256] = rot[:, 128 * g:128 * g + 128].astype(q_ref.dtype)

    ckv = p[:, _O_CKV:_O_CKV + MLA_KV_LORA]
    ckv = ckv * lax.rsqrt(jnp.mean(ckv * ckv, axis=-1, keepdims=True) + RMS_EPS) * kvg_ref[...]
    kv = _dot(_mx(ckv), w_ukv_ref[...])
    t = p[:, _O_KR:_O_KR + 128] * tab[:, _T_KCS:_T_KCS + 128]
    u = t + pltpu.roll(t, 64, axis=1)
    low = lax.broadcasted_iota(jnp.int32, u.shape, 1) < 64
    kx = (jnp.where(low, u, 0.0), jnp.where(low, 0.0, u))
    ones_col = jnp.where(lax.broadcasted_iota(jnp.int32, u.shape, 1) == 0, 1.0, 0.0).astype(v_ref.dtype)
    for hh in range(MLA_HEADS):
        k_ref[0, hh, :, 0:128] = kv[:, 256 * hh:256 * hh + 128].astype(k_ref.dtype)
        k_ref[0, hh, :, 128:256] = kx[hh % 2].astype(k_ref.dtype)
        v_ref[0, hh, :, 0:128] = kv[:, 256 * hh + 128:256 * hh + 256].astype(v_ref.dtype)
        v_ref[0, hh, :, 128:256] = ones_col

    gu = _gelu(p[:, _O_SGU:_O_SGU + SG_WIDTH])
    gv = _ln(_gelu(p[:, _O_SGV:_O_SGV + SG_WIDTH])) * sgg_ref[...] + sgb_ref[...]
    gvm = _mx(gv)
    grp = _lane_group((CHUNK, SG_WIDTH), SG_WIDTH // SG_GROUPS)
    for c in range(TM // CHUNK):
        rows = slice(c * CHUNK, (c + 1) * CHUNK)
        res = _dot(sgw_ref[...], gvm[rows])
        mixed = sgbias_ref[...]
        for g in range(SG_GROUPS):
            mixed = mixed + jnp.where(grp == g, res[g * CHUNK:(g + 1) * CHUNK], 0.0)
        sg_ref[0, rows, :] = (gu[rows] * mixed).astype(sg_ref.dtype)

    rq = (p[:, _O_RQ:_O_RQ + 128] * tab[:, _T_RQC:_T_RQC + 128]
          + p[:, _O_RQS:_O_RQS + 128] * tab[:, _T_RQS:_T_RQS + 128])
    rk = (p[:, _O_RK:_O_RK + 128] * tab[:, _T_RKC:_T_RKC + 128]
          + p[:, _O_RKS:_O_RKS + 128] * tab[:, _T_RKS:_T_RKS + 128])
    rv = p[:, _O_RV:_O_RV + RET_OUT]
    retp_ref[0, :, 0:128] = rq
    retp_ref[0, :, 128:256] = rk
    retp_ref[0, :, 256:512] = p[:, _O_RG:_O_RG + RET_OUT]
    rvm = _mx(rv)
    rv_ref[0] = rvm.astype(rv_ref.dtype)
    bd = bd_ref[...]
    for c in range(TM // CHUNK):
        rows = slice(c * CHUNK, (c + 1) * CHUNK)
        for d in range(2):
            kd_t = _mx((rk[rows] * kdec_ref[d]).T)
            af = _dot(kd_t, rvm[rows]) * bd
            a_ref[0, c, d] = (af[0:32] + af[32:64]) + (af[64:96] + af[96:128])


def _ordered_after(kernel_fn, pos):
    def wrapped(*refs):
        return kernel_fn(*refs[:pos], *refs[pos + 1:])
    return wrapped


_ORDER_SPEC = pl.BlockSpec(memory_space=pl.ANY)


def input_projection(x_lat, x_ctx, mod_tab, tab, lw, n_lat_tiles, after, x_block=0):
    b = mod_tab.shape[0]
    d = x_lat.shape[2]
    t = x_lat.shape[1] + x_ctx.shape[1]
    nt = t // TM
    nc = t // CHUNK
    cpt = TM // CHUNK
    const2 = lambda bi, j: (0, 0)
    const3 = lambda bi, j: (0, 0, 0)
    out_shape = (
        jax.ShapeDtypeStruct((b, MLA_HEADS, t, 256), MXU_DT),
        jax.ShapeDtypeStruct((b, MLA_HEADS, t, 256), MXU_DT),
        jax.ShapeDtypeStruct((b, MLA_HEADS, t, 256), MXU_DT),
        jax.ShapeDtypeStruct((b, t, SG_WIDTH), MXU_DT),
        jax.ShapeDtypeStruct((b, t, 512), F32),
        jax.ShapeDtypeStruct((b, t, RET_OUT), MXU_DT),
        jax.ShapeDtypeStruct((b, nc, 2, RET_QK, RET_OUT), F32),
    )
    head_spec = lambda w: pl.BlockSpec((1, MLA_HEADS, TM, w), lambda bi, j: (bi, 0, j, 0))
    tok_spec = lambda w: pl.BlockSpec((1, TM, w), lambda bi, j: (bi, j, 0))
    return pl.pallas_call(
        _ordered_after(functools.partial(_inproj_kernel, n_lat_tiles=n_lat_tiles), 15),
        grid=(b, nt),
        in_specs=[
            pl.BlockSpec((1, TM, d), lambda bi, j: (bi + x_block * b, jnp.minimum(j, n_lat_tiles - 1), 0)),
            pl.BlockSpec((1, TM, d), lambda bi, j: (bi + x_block * b, jnp.maximum(j - n_lat_tiles, 0), 0)),
            pl.BlockSpec((1, 1, N_MOD, d), lambda bi, j: (bi, j // n_lat_tiles, 0, 0)),
            pl.BlockSpec((TM, TAB_W), lambda bi, j: (j, 0)),
            pl.BlockSpec((d, IN_P), const2),
            pl.BlockSpec((1, MLA_Q_LORA), const2),
            pl.BlockSpec((1, MLA_KV_LORA), const2),
            pl.BlockSpec((MLA_Q_LORA, 1024), const2),
            pl.BlockSpec((MLA_KV_LORA, 1024), const2),
            pl.BlockSpec((1, SG_WIDTH), const2),
            pl.BlockSpec((1, SG_WIDTH), const2),
            pl.BlockSpec((SG_GROUPS * CHUNK, CHUNK), const2),
            pl.BlockSpec((CHUNK, SG_WIDTH), const2),
            pl.BlockSpec((2, CHUNK, 128), const3),
            pl.BlockSpec((128, RET_OUT), const2),
            _ORDER_SPEC,
        ],
        out_specs=(head_spec(256), head_spec(256), head_spec(256), tok_spec(SG_WIDTH),
                   tok_spec(512), tok_spec(RET_OUT),
                   pl.BlockSpec((1, cpt, 2, RET_QK, RET_OUT), lambda bi, j: (bi, j, 0, 0, 0))),
        out_shape=out_shape,
        compiler_params=_cparams(("parallel", "parallel")),
        name="input_projection",
    )(x_lat, x_ctx, mod_tab, tab, lw["w_in"], lw["q_g"], lw["kv_g"], lw["w_uq"], lw["w_ukv"],
      lw["sg_g"], lw["sg_b"], lw["sg_w"], lw["sg_bias"], lw["kdec"], lw["bd"], after)


def _attn_kernel(q_ref, k_ref, v_ref, o_ref, *, n_main, tk, tail):
    q = q_ref[0, 0]
    tq = q.shape[0]
    chunks = [(i * tk, tk) for i in range(n_main)] + ([(n_main * tk, tail)] if tail else [])

    def scores(ci):
        start, size = chunks[ci]
        return _dot_nt(q, k_ref[0, 0, start:start + size, :])

    m = jnp.full((tq, 1), -1e30, F32)
    acc = jnp.zeros((tq, 256), F32)
    s_next = scores(0)
    for ci, (start, size) in enumerate(chunks):
        s = s_next
        if ci + 1 < len(chunks):
            s_next = scores(ci + 1)
        m_new = jnp.maximum(m, jnp.max(s, axis=-1, keepdims=True))
        p = jnp.exp(s - m_new)
        acc = jnp.exp(m - m_new) * acc + _dot(_mx(p), v_ref[0, 0, start:start + size, :])
        m = m_new
    o_ref[0] = (acc[:, 0:MLA_V] / acc[:, MLA_V:MLA_V + 1]).astype(o_ref.dtype)


def mla_attention(q, k, v, s_len, lc):
    b, hn, t, _ = q.shape
    tq = min(ATT_TQ, s_len)
    tk = min(ATT_TK, s_len)
    kv_full = pl.BlockSpec((1, 1, t, 256), lambda bi, hi, i: (bi, hi, 0, 0))
    out_lat = pl.pallas_call(
        functools.partial(_attn_kernel, n_main=s_len // tk, tk=tk, tail=lc),
        grid=(b, hn, s_len // tq),
        in_specs=[pl.BlockSpec((1, 1, tq, 256), lambda bi, hi, i: (bi, hi, i, 0)), kv_full, kv_full],
        out_specs=pl.BlockSpec((1, tq, MLA_V), lambda bi, hi, i: (bi, i, hi)),
        out_shape=jax.ShapeDtypeStruct((b, s_len, MLA_OUT), MXU_DT),
        compiler_params=_cparams(("parallel", "parallel", "arbitrary")),
        name="mla_attention_latent",
    )(q, k, v)
    cblk = s_len // lc
    ctx_spec = pl.BlockSpec((1, 1, lc, 256), lambda bi, hi: (bi, hi, cblk, 0))
    out_ctx = pl.pallas_call(
        functools.partial(_attn_kernel, n_main=0, tk=tk, tail=lc),
        grid=(b, hn),
        in_specs=[ctx_spec, ctx_spec, ctx_spec],
        out_specs=pl.BlockSpec((1, lc, MLA_V), lambda bi, hi: (bi, 0, hi)),
        out_shape=jax.ShapeDtypeStruct((b, lc, MLA_OUT), MXU_DT),
        compiler_params=_cparams(("parallel", "parallel")),
        name="mla_attention_context",
    )(q, k, v)
    return out_lat, out_ctx


def _ret_scan_kernel(a_ref, cd_ref, s_ref, *, n_lat_chunks):
    nc = a_ref.shape[1]
    ncc = nc - n_lat_chunks
    cd_f, cd_b = cd_ref[0], cd_ref[1]

    def body(n, carry):
        sf, sb = carry
        cf = jnp.where(n < ncc, n_lat_chunks + n, n - ncc)
        cb = jnp.where(n < ncc, nc - 1 - n, n_lat_chunks - 1 - (n - ncc))
        s_ref[0, cf, 0] = sf
        s_ref[0, cb, 1] = sb
        return sf * cd_f + a_ref[0, cf, 0], sb * cd_b + a_ref[0, cb, 1]

    zero = jnp.zeros((RET_QK, RET_OUT), F32)
    lax.fori_loop(0, nc, body, (zero, zero))


def retention_scan(a, cd, n_lat_chunks):
    b, nc = a.shape[:2]
    blk = pl.BlockSpec((1, nc, 2, RET_QK, RET_OUT), lambda bi: (bi, 0, 0, 0, 0))
    return pl.pallas_call(
        functools.partial(_ret_scan_kernel, n_lat_chunks=n_lat_chunks),
        grid=(b,),
        in_specs=[blk, pl.BlockSpec((2, 1, RET_OUT), lambda bi: (0, 0, 0))],
        out_specs=blk,
        out_shape=jax.ShapeDtypeStruct(a.shape, F32),
        compiler_params=_cparams(("parallel",)),
        name="retention_scan",
    )(a, cd)


def _split_dot(x, ones2):
    hi = x.astype(BF16)
    lo = (x - hi.astype(F32)).astype(BF16)
    return _dot(jnp.concatenate([hi, lo], axis=1), ones2)


def _outproj_kernel(xl_ref, xc_ref, mlal_ref, mlac_ref, sg_ref, retp_ref, rv_ref, st_ref, mod_ref,
                    m_ref, qdec_ref, bd_ref, seg_ref, w_o_ref, lng_ref, lnb_ref, rw_ref, rb_ref, tri_ref,
                    x1_ref, h2_ref, idx_ref, gate_ref, rank_ref, cnt_ref, carry_ref, *, n_lat_tiles):
    @pl.when(pl.program_id(0) == 0)
    def _():
        carry_ref[...] = jnp.zeros_like(carry_ref)

    nbat = xl_ref.shape[0]
    units = [(bb, c) for bb in range(nbat) for c in range(TM // CHUNK)]
    rows = lambda c: slice(c * CHUNK, (c + 1) * CHUNK)
    g32 = _lane_group((CHUNK, 128), RET_QK)
    g64 = _lane_group((CHUNK, RET_OUT), RET_V)
    bd = bd_ref[...]
    seg2 = jnp.concatenate([seg_ref[...], seg_ref[...]], axis=0)
    m4 = jnp.concatenate([m_ref[hh] for hh in range(RET_HEADS)], axis=0)

    rq, s4 = {}, {}
    for u in units:
        bb, c = u
        rq[u] = retp_ref[bb, rows(c), 0:128]
        q4 = jnp.concatenate([jnp.where(g32 == hh, rq[u], 0.0) for hh in range(RET_HEADS)], axis=0)
        s4[u] = _dot_nt(_mx(q4), _mx(retp_ref[bb, rows(c), 128:256]))
    o = {}
    for u in units:
        bb, c = u
        r = _dot(_mx(s4[u] * m4), rv_ref[bb, rows(c), :])
        qd = jnp.concatenate([rq[u] * qdec_ref[0], rq[u] * qdec_ref[1]], axis=1)
        st2 = jnp.concatenate([jnp.concatenate([st_ref[bb, c, dd]] * RET_HEADS, axis=0) * bd
                               for dd in range(2)], axis=0)
        acc = _dot(_mx(qd), _mx(st2))
        for hh in range(RET_HEADS):
            acc = acc + jnp.where(g64 == hh, r[hh * CHUNK:(hh + 1) * CHUNK], 0.0)
        o[u] = acc
    oc = {}
    for u in units:
        oc[u] = o[u] - _split_dot(o[u], seg2) * (1.0 / RET_V)
    ret = {}
    for u in units:
        bb, c = u
        var = _split_dot(oc[u] * oc[u], seg2) * (1.0 / RET_V)
        rg = retp_ref[bb, rows(c), 256:512]
        ret[u] = _mx(oc[u] * lax.rsqrt(var + LN_EPS) * (rg * _sigmoid(rg)))

    is_ctx = pl.program_id(0) >= n_lat_tiles
    ys = []
    for bb in range(nbat):
        mla = jnp.where(is_ctx, mlac_ref[bb], mlal_ref[bb])
        cat = jnp.concatenate(
            [mla, sg_ref[bb], jnp.concatenate([ret[(bb, c)] for c in range(TM // CHUNK)], axis=0)], axis=1)
        ys.append(_dot(cat, w_o_ref[...]))
    logits = []
    for bb in range(nbat):
        mod = mod_ref[bb, 0]
        x = jnp.where(is_ctx, xc_ref[bb], xl_ref[bb])
        x1 = _ln(DEEPNORM_ALPHA * x + mod[2:3] * ys[bb]) * lng_ref[...] + lnb_ref[...]
        x1_ref[bb] = x1
        h2 = x1 * (1.0 + mod[4:5]) + mod[3:4]
        h2_ref[bb] = _pack_bf16_pairs(h2) if PACK_ROWS else h2
        h2_hi = h2.astype(BF16)
        h2_lo = (h2 - h2_hi.astype(F32)).astype(BF16)
        r2 = _dot_nt(rw_ref[...], jnp.concatenate([h2_hi, h2_lo], axis=0))
        logits.append(r2[0:N_EXPERTS, 0:TM] + r2[N_EXPERTS:, 0:TM] + r2[0:N_EXPERTS, TM:] + rb_ref[...])

    work = jnp.concatenate(logits, axis=1)
    e_iota = lax.broadcasted_iota(jnp.int32, work.shape, 0).astype(F32)
    vals, idxs = [], []
    for _ in range(TOP_K):
        mval = jnp.max(work, axis=0, keepdims=True)
        midx = jnp.min(jnp.where(work == mval, e_iota, float(N_EXPERTS)), axis=0, keepdims=True)
        vals.append(mval)
        idxs.append(midx)
        work = jnp.where(e_iota == midx, -jnp.inf, work)
    ex = [jnp.exp(vv - vals[0]) for vv in vals]
    den = ex[0] + ex[1] + ex[2] + ex[3]
    onehot = jnp.zeros_like(work)
    for kk in range(TOP_K):
        onehot = onehot + jnp.where(e_iota == idxs[kk], 1.0, 0.0)
    cols = lambda bb: slice(bb * TM, (bb + 1) * TM)
    prefixes = [_dot(onehot[:, cols(bb)].astype(BF16), tri_ref[...]) for bb in range(nbat)]
    count = carry_ref[:, 0:1]
    bases = []
    for bb in range(nbat):
        bases.append(count + prefixes[bb])
        count = count + jnp.sum(onehot[:, cols(bb)], axis=1, keepdims=True)
    base = jnp.concatenate(bases, axis=1)
    for kk in range(TOP_K):
        gate = ex[kk] / den
        rank = jnp.sum(jnp.where(e_iota == idxs[kk], base, 0.0), axis=0, keepdims=True).astype(jnp.int32)
        for bb in range(nbat):
            gate_ref[bb, 0, kk:kk + 1, :] = gate[:, cols(bb)]
            idx_ref[bb, 0, kk:kk + 1, :] = idxs[kk][:, cols(bb)].astype(jnp.int32)
            rank_ref[bb, 0, kk:kk + 1, :] = rank[:, cols(bb)]
    carry_ref[...] = jnp.broadcast_to(count, carry_ref.shape)
    cnt_ref[...] = carry_ref[...].astype(jnp.int32)


def output_projection(x_lat, x_ctx, mla_lat, mla_ctx, sg, retp, rv, st, mod_tab, lw, n_lat_tiles, x_block=0):
    b = mod_tab.shape[0]
    d = x_lat.shape[2]
    t = x_lat.shape[1] + x_ctx.shape[1]
    nt = t // TM
    cpt = TM // CHUNK
    const2 = lambda j: (0, 0)
    const3 = lambda j: (0, 0, 0)
    tok_spec = lambda w: pl.BlockSpec((b, TM, w), lambda j: (0, j, 0))
    route_spec = pl.BlockSpec((b, 1, TOP_K, TM), lambda j: (0, j, 0, 0))
    route_shape = lambda dt: jax.ShapeDtypeStruct((b, nt, TOP_K, TM), dt)
    h2w = d // 2 if PACK_ROWS else d
    return pl.pallas_call(
        functools.partial(_outproj_kernel, n_lat_tiles=n_lat_tiles),
        grid=(nt,),
        in_specs=[
            pl.BlockSpec((b, TM, d), lambda j: (x_block, jnp.minimum(j, n_lat_tiles - 1), 0)),
            pl.BlockSpec((b, TM, d), lambda j: (x_block, jnp.maximum(j - n_lat_tiles, 0), 0)),
            pl.BlockSpec((b, TM, MLA_OUT), lambda j: (0, jnp.minimum(j, n_lat_tiles - 1), 0)),
            pl.BlockSpec((b, TM, MLA_OUT), lambda j: (0, jnp.maximum(j - n_lat_tiles, 0), 0)),
            tok_spec(SG_WIDTH), tok_spec(512), tok_spec(RET_OUT),
            pl.BlockSpec((b, cpt, 2, RET_QK, RET_OUT), lambda j: (0, j, 0, 0, 0)),
            pl.BlockSpec((b, 1, N_MOD, d), lambda j: (0, j // n_lat_tiles, 0, 0)),
            pl.BlockSpec((RET_HEADS, CHUNK, CHUNK), const3),
            pl.BlockSpec((2, CHUNK, 128), const3),
            pl.BlockSpec((128, RET_OUT), const2),
            pl.BlockSpec((RET_OUT, RET_OUT), const2),
            pl.BlockSpec((d, d), const2),
            pl.BlockSpec((1, d), const2),
            pl.BlockSpec((1, d), const2),
            pl.BlockSpec((2 * N_EXPERTS, d), const2),
            pl.BlockSpec((N_EXPERTS, 1), const2),
            pl.BlockSpec((TM, TM), const2),
        ],
        out_specs=(tok_spec(d), tok_spec(h2w), route_spec, route_spec, route_spec,
                   pl.BlockSpec((N_EXPERTS, 128), const2)),
        out_shape=(jax.ShapeDtypeStruct((b, t, d), F32),
                   jax.ShapeDtypeStruct((b, t, h2w), jnp.uint32 if PACK_ROWS else F32),
                   route_shape(jnp.int32), route_shape(F32), route_shape(jnp.int32),
                   jax.ShapeDtypeStruct((N_EXPERTS, 128), jnp.int32)),
        scratch_shapes=[pltpu.VMEM((N_EXPERTS, 128), F32)],
        compiler_params=_cparams(("arbitrary",)),
        name="output_projection",
    )(x_lat, x_ctx, mla_lat, mla_ctx, sg, retp, rv, st, mod_tab, lw["ret_m"], lw["qdec"], lw["bd"], lw["seg"],
      lw["w_o"], lw["ln1_g"], lw["ln1_b"], lw["router_w"], lw["router_b"], lw["tri"])


_DEINT = 256


def _pack_bf16_pairs(v):
    bits = lax.bitcast_convert_type(v.astype(BF16).astype(F32), jnp.uint32)
    half = bits.shape[1] // 2
    return bits[:, :half] | (bits[:, half:] >> 16)


def _unpack_bf16_pairs(w):
    return (lax.bitcast_convert_type(w & jnp.uint32(0xFFFF0000), F32),
            lax.bitcast_convert_type(w << 16, F32))


def _expert_block(x_ref, wg, wl, bg_ref, bl_ref, wd, bd_ref, y_ref):
    if PACK_ROWS:
        hi, lo = _unpack_bf16_pairs(x_ref[...])
        xb = jnp.concatenate([hi.astype(BF16), lo.astype(BF16)], axis=1)
    else:
        xb = x_ref[...]
    glu = jnp.minimum(_dot(xb, wg[0]) + bg_ref[0], SWIGLU_LIMIT)
    lin = jnp.clip(_dot(xb, wl[0]) + bl_ref[0], -SWIGLU_LIMIT, SWIGLU_LIMIT)
    act = glu * _sigmoid(SWIGLU_ALPHA * glu) * (lin + 1.0)
    y = _dot(_mx(act), wd[0]) + bd_ref[0]
    y_ref[...] = _pack_bf16_pairs(y) if PACK_ROWS else y


def _expert_first_kernel(be_ref, nu_ref, x_ref, wgu_hbm, bg_ref, bl_ref, wdn_hbm, bd_ref, perm_ref,
                         y_ref, wg_o, wl_o, wd_o, gu_buf, dn_buf, sem, *, li):
    i = pl.program_id(0)
    active = i < nu_ref[0]
    e = be_ref[i]
    fresh = jnp.logical_or(i == 0, e != be_ref[jnp.maximum(i - 1, 0)])

    def weight_copies(ex, slot):
        return (pltpu.make_async_copy(wgu_hbm.at[li, ex], gu_buf.at[slot], sem.at[0, slot]),
                pltpu.make_async_copy(wdn_hbm.at[li, ex], dn_buf.at[slot], sem.at[1, slot]))

    @pl.when(i == 0)
    def _():
        for cp in weight_copies(e, e % 2):
            cp.start()

    @pl.when(jnp.logical_and(active, fresh))
    def _():
        slot = e % 2
        for cp in weight_copies(e, slot):
            cp.wait()

        @pl.when(e + 1 < N_EXPERTS)
        def _():
            for cp in weight_copies(e + 1, 1 - slot):
                cp.start()

        half = _DEINT // 2
        for c in range(2 * D_EXPERT // _DEINT):
            r = _dot(_mx(gu_buf[slot, :, _DEINT * c:_DEINT * (c + 1)]), perm_ref[...])
            wg_o[0, :, half * c:half * (c + 1)] = r[:, :half].astype(wg_o.dtype)
            wl_o[0, :, half * c:half * (c + 1)] = r[:, half:].astype(wl_o.dtype)
        wd_o[0] = dn_buf[slot].astype(wd_o.dtype)

    @pl.when(active)
    def _():
        _expert_block(x_ref, wg_o, wl_o, bg_ref, bl_ref, wd_o, bd_ref, y_ref)

    @pl.when(jnp.logical_not(active))
    def _():
        y_ref[...] = jnp.zeros_like(y_ref)


def _expert_kernel(be_ref, nu_ref, x_ref, wg_ref, wl_ref, bg_ref, bl_ref, wd_ref, bd_ref, y_ref):
    del be_ref
    active = pl.program_id(0) < nu_ref[0]

    @pl.when(active)
    def _():
        _expert_block(x_ref, wg_ref, wl_ref, bg_ref, bl_ref, wd_ref, bd_ref, y_ref)

    @pl.when(jnp.logical_not(active))
    def _():
        y_ref[...] = jnp.zeros_like(y_ref)


def expert_ffn_first(xg, block_e, n_used, w_gate_up, w_down, li, lw, after):
    cap, xw = xg.shape
    d, de = D_MODEL, D_EXPERT
    xmap = lambda i, be, nu: (jnp.minimum(i, nu[0] - 1), 0)
    wmap = lambda i, be, nu: (be[i], 0, 0)
    hbm = pl.BlockSpec(memory_space=pl.ANY)
    grid_spec = pltpu.PrefetchScalarGridSpec(
        num_scalar_prefetch=2,
        grid=(cap // MOE_BM,),
        in_specs=[pl.BlockSpec((MOE_BM, xw), xmap),
                  hbm,
                  pl.BlockSpec((1, 1, de), wmap), pl.BlockSpec((1, 1, de), wmap),
                  hbm, pl.BlockSpec((1, 1, d), wmap),
                  pl.BlockSpec((_DEINT, _DEINT), lambda i, be, nu: (0, 0)),
                  _ORDER_SPEC],
        out_specs=(pl.BlockSpec((MOE_BM, xw), lambda i, be, nu: (i, 0)),
                   pl.BlockSpec((1, d, de), wmap), pl.BlockSpec((1, d, de), wmap),
                   pl.BlockSpec((1, de, d), wmap)),
        scratch_shapes=[pltpu.VMEM((2, d, 2 * de), F32), pltpu.VMEM((2, de, d), F32),
                        pltpu.SemaphoreType.DMA((2, 2))],
    )
    y, wg, wl, wd = pl.pallas_call(
        _ordered_after(functools.partial(_expert_first_kernel, li=li), 9),
        grid_spec=grid_spec,
        out_shape=(jax.ShapeDtypeStruct((cap, xw), xg.dtype),
                   jax.ShapeDtypeStruct((N_EXPERTS, d, de), MXU_DT),
                   jax.ShapeDtypeStruct((N_EXPERTS, d, de), MXU_DT),
                   jax.ShapeDtypeStruct((N_EXPERTS, de, d), MXU_DT)),
        compiler_params=pltpu.CompilerParams(dimension_semantics=("arbitrary",),
                                             vmem_limit_bytes=EXPERT_FIRST_VMEM_LIMIT),
        name="expert_ffn_first",
    )(block_e, n_used, xg, w_gate_up, lw["b_glu"], lw["b_lin"], w_down, lw["b_down"], lw["deint"], after)
    return y, (wg, wl, wd)


def expert_ffn(xg, block_e, n_used, ew, lw, after):
    cap, xw = xg.shape
    d, de = D_MODEL, D_EXPERT
    xmap = lambda i, be, nu: (jnp.minimum(i, nu[0] - 1), 0)
    wmap = lambda i, be, nu: (be[i], 0, 0)
    grid_spec = pltpu.PrefetchScalarGridSpec(
        num_scalar_prefetch=2,
        grid=(cap // MOE_BM,),
        in_specs=[pl.BlockSpec((MOE_BM, xw), xmap),
                  pl.BlockSpec((1, d, de), wmap), pl.BlockSpec((1, d, de), wmap),
                  pl.BlockSpec((1, 1, de), wmap), pl.BlockSpec((1, 1, de), wmap),
                  pl.BlockSpec((1, de, d), wmap), pl.BlockSpec((1, 1, d), wmap),
                  _ORDER_SPEC],
        out_specs=pl.BlockSpec((MOE_BM, xw), lambda i, be, nu: (i, 0)),
    )
    return pl.pallas_call(
        _ordered_after(_expert_kernel, 9),
        grid_spec=grid_spec,
        out_shape=jax.ShapeDtypeStruct((cap, xw), xg.dtype),
        compiler_params=_cparams(("arbitrary",)),
        name="expert_ffn",
    )(block_e, n_used, xg, ew[0], ew[1], lw["b_glu"], lw["b_lin"], ew[2], lw["b_down"], after)


def _combine_ln2_kernel(x_ref, y_ref, gate_ref, mod_ref, g_ref, b_ref, ol_ref, oc_ref, *, n_lat_tiles):
    is_ctx = pl.program_id(0) >= n_lat_tiles
    for bb in range(x_ref.shape[0]):
        gates = gate_ref[bb]
        if PACK_ROWS:
            f_hi, f_lo = 0.0, 0.0
            for kk in range(TOP_K):
                hi, lo = _unpack_bf16_pairs(y_ref[kk, bb])
                f_hi = f_hi + gates[:, kk:kk + 1] * hi
                f_lo = f_lo + gates[:, kk:kk + 1] * lo
            f = jnp.concatenate([f_hi, f_lo], axis=1)
        else:
            f = gates[:, 0:1] * y_ref[0, bb]
            for kk in range(1, TOP_K):
                f = f + gates[:, kk:kk + 1] * y_ref[kk, bb]
        mod = mod_ref[bb, 0]
        res = _ln(DEEPNORM_ALPHA * x_ref[bb] + mod[5:6] * f) * g_ref[...] + b_ref[...]

        @pl.when(is_ctx)
        def _():
            oc_ref[bb] = res

        @pl.when(jnp.logical_not(is_ctx))
        def _():
            ol_ref[bb] = res


def combine_deepnorm2(x1, yg, gates, mod_tab, lw, n_lat_tiles, after):
    b, t, d = x1.shape
    s_len = n_lat_tiles * TM
    tok = pl.BlockSpec((b, TM, d), lambda j: (0, j, 0))
    vec = pl.BlockSpec((1, d), lambda j: (0, 0))
    return pl.pallas_call(
        _ordered_after(functools.partial(_combine_ln2_kernel, n_lat_tiles=n_lat_tiles), 6),
        grid=(t // TM,),
        in_specs=[tok,
                  pl.BlockSpec((TOP_K, b, TM, yg.shape[-1]), lambda j: (0, 0, j, 0)),
                  pl.BlockSpec((b, TM, TOP_K), lambda j: (0, j, 0)),
                  pl.BlockSpec((b, 1, N_MOD, d), lambda j: (0, j // n_lat_tiles, 0, 0)), vec, vec,
                  _ORDER_SPEC],
        out_specs=(pl.BlockSpec((b, TM, d), lambda j: (0, jnp.minimum(j, n_lat_tiles - 1), 0)),
                   pl.BlockSpec((b, TM, d), lambda j: (0, jnp.maximum(j - n_lat_tiles, 0), 0))),
        out_shape=(jax.ShapeDtypeStruct((b, s_len, d), F32), jax.ShapeDtypeStruct((b, t - s_len, d), F32)),
        compiler_params=_cparams(("arbitrary",)),
        name="combine_deepnorm2",
    )(x1, yg, gates, mod_tab, lw["ln2_g"], lw["ln2_b"], after)


def _rotation_tables(s_len, lc):
    rows = s_len // GRID_W
    row = jnp.broadcast_to(jnp.arange(rows, dtype=F32)[:, None], (rows, GRID_W)).reshape(-1)
    col = jnp.broadcast_to(jnp.arange(GRID_W, dtype=F32)[None, :], (rows, GRID_W)).reshape(-1)
    inv = ROPE_BASE ** (-jnp.arange(ROPE_AXIS_FREQS, dtype=F32) / ROPE_AXIS_FREQS)
    ar, ac = row[:, None] * inv, col[:, None] * inv
    c64 = jnp.concatenate([jnp.cos(ar), jnp.cos(ar), jnp.cos(ac), jnp.cos(ac)], axis=1)
    s64 = jnp.concatenate([-jnp.sin(ar), jnp.sin(ar), -jnp.sin(ac), jnp.sin(ac)], axis=1)
    c64 = jnp.concatenate([c64, jnp.ones((lc, 64), F32)], axis=0)
    s64 = jnp.concatenate([s64, jnp.zeros((lc, 64), F32)], axis=0)
    half = RET_QK // 2
    pos = jnp.concatenate([lc + jnp.arange(s_len, dtype=F32), jnp.arange(lc, dtype=F32)])
    inv_r = 1.0 / (RET_ROPE_BASE ** jnp.linspace(0.0, 1.0, half, dtype=F32))
    ang = pos[:, None] * inv_r
    rc = jnp.tile(jnp.concatenate([jnp.cos(ang), jnp.cos(ang)], axis=1), (1, RET_HEADS))
    rs = jnp.tile(jnp.concatenate([-jnp.sin(ang), jnp.sin(ang)], axis=1), (1, RET_HEADS))
    qs = RET_QK ** -0.5
    return jnp.concatenate([
        jnp.tile(c64, (1, MLA_HEADS)) * MLA_SCALE, jnp.tile(s64, (1, MLA_HEADS)) * MLA_SCALE,
        c64, s64, rc * qs, rs * qs, rc, rs], axis=1)


def _in_perm():
    a = np.arange
    return np.concatenate([
        a(0, 640), a(704, 1216), a(1216, 1344), a(1344, 1472),
        1216 + _swap16(a(128)), 1344 + _swap16(a(128)), a(1472, 1984),
        640 + a(64), 640 + _swap16(a(64))])


def _uq_perm():
    a = np.arange
    nope = [h * MLA_QK + a(MLA_NOPE) for h in range(MLA_HEADS)]
    rope = [h * MLA_QK + MLA_NOPE + a(MLA_ROPE) for h in range(MLA_HEADS)]
    part = [h * MLA_QK + MLA_NOPE + _swap16(a(MLA_ROPE)) for h in range(MLA_HEADS)]
    return np.concatenate(nope + rope + part)


def _layer_weights(p):
    nl = p["w_in"].shape[0]
    lgf = jax.nn.log_sigmoid(p["ret_decay_fwd"].astype(F32))
    lgb = jax.nn.log_sigmoid(p["ret_decay_bwd"].astype(F32))
    h128 = np.arange(128) // RET_QK
    h256 = np.arange(RET_OUT) // RET_V
    a = jnp.arange(CHUNK, dtype=F32)[None, :, None]
    lf, lb = lgf[:, h128][:, None, :], lgb[:, h128][:, None, :]
    kdec = jnp.stack([jnp.exp(lf * (CHUNK - 1.0 - a)), jnp.exp(lb * a)], axis=1)
    qdec = jnp.stack([jnp.exp(lf * (a + 1.0)), jnp.exp(lb * (CHUNK - a))], axis=1)
    i = jnp.arange(CHUNK, dtype=F32)[:, None]
    j = jnp.arange(CHUNK, dtype=F32)[None, :]
    dif = (i - j)[None, None]
    ret_m = jnp.where(dif >= 0, jnp.exp(lgf[:, :, None, None] * jnp.maximum(dif, 0.0)),
                      jnp.exp(lgb[:, :, None, None] * jnp.maximum(-dif, 0.0)))
    cd = jnp.stack([jnp.exp(lgf[:, h256] * CHUNK), jnp.exp(lgb[:, h256] * CHUNK)], axis=1)[:, :, None, :]
    bd = (h128[:, None] == h256[None, :]).astype(np.float32)
    seg = (h256[:, None] == h256[None, :]).astype(np.float32)
    tri = (np.arange(TM)[:, None] < np.arange(TM)[None, :]).astype(np.float32)
    jj = np.arange(_DEINT // 2)
    deint = np.zeros((_DEINT, _DEINT), np.float32)
    deint[2 * jj, jj] = 1.0
    deint[2 * jj + 1, _DEINT // 2 + jj] = 1.0
    rw_t = jnp.swapaxes(p["router_w"], 1, 2)
    rw_hi = rw_t.astype(BF16)
    rw_lo = (rw_t - rw_hi.astype(F32)).astype(BF16)
    sg_bias = jnp.repeat(jnp.swapaxes(p["sg_b"], 1, 2), SG_WIDTH // SG_GROUPS, axis=2)
    bgu = p["b_gate_up"]
    return {
        "w_in": p["w_in"][:, :, _in_perm()].astype(MXU_DT),
        "q_g": p["mla_q_norm_g"][:, None, :], "kv_g": p["mla_kv_norm_g"][:, None, :],
        "w_uq": p["mla_w_uq"][:, :, _uq_perm()].astype(MXU_DT),
        "w_ukv": p["mla_w_ukv"].astype(MXU_DT),
        "sg_g": p["sg_norm_g"][:, None, :], "sg_b": p["sg_norm_b"][:, None, :],
        "sg_w": p["sg_w"].reshape(nl, SG_GROUPS * CHUNK, CHUNK).astype(MXU_DT),
        "sg_bias": sg_bias,
        "kdec": kdec, "qdec": qdec, "ret_m": ret_m, "cd": cd,
        "bd": jnp.broadcast_to(jnp.asarray(bd), (nl,) + bd.shape),
        "seg": jnp.broadcast_to(jnp.asarray(seg, BF16), (nl,) + seg.shape),
        "tri": jnp.broadcast_to(jnp.asarray(tri, BF16), (nl,) + tri.shape),
        "w_o": p["w_o"].astype(MXU_DT),
        "ln1_g": p["ln1_g"][:, None, :], "ln1_b": p["ln1_b"][:, None, :],
        "ln2_g": p["ln2_g"][:, None, :], "ln2_b": p["ln2_b"][:, None, :],
        "router_w": jnp.concatenate([rw_hi, rw_lo], axis=1),
        "router_b": p["router_b"][:, :, None],
        "b_glu": bgu[:, :, None, 0::2], "b_lin": bgu[:, :, None, 1::2],
        "b_down": p["b_down"][:, :, None, :],
        "deint": jnp.broadcast_to(jnp.asarray(deint, MXU_DT), (nl,) + deint.shape),
    }


def _route(idx, rank, counts, every_expert):
    n_assign = idx.shape[1] * TOP_K
    nb = -(-n_assign // MOE_BM) + N_EXPERTS
    blocks = (counts + MOE_BM - 1) // MOE_BM
    padded = (jnp.maximum(blocks, 1) if every_expert else blocks) * MOE_BM
    pad_end = jnp.cumsum(padded)
    pad_start = pad_end - padded
    experts = jnp.arange(N_EXPERTS, dtype=jnp.int32)
    dest = rank + jnp.sum(jnp.where(idx[..., None] == experts, pad_start, 0), axis=-1)
    blk_start = jnp.arange(nb, dtype=jnp.int32) * MOE_BM
    block_e = jnp.minimum(jnp.sum((pad_end[None, :] <= blk_start[:, None]).astype(jnp.int32), axis=1),
                          N_EXPERTS - 1)
    n_used = (pad_end[-1] // MOE_BM).astype(jnp.int32).reshape(1)
    return dest.astype(jnp.int32), block_e, n_used, nb


def _sc_workers():
    info = plsc.get_sparse_core_info()
    return info.num_cores, info.num_cores * info.num_subcores


def sc_dispatch(rows, dest3, cap, after):
    n, w = rows.shape
    nch, kk, c = dest3.shape
    ncores, nw = _sc_workers()
    assert nch % nw == 0, "token chunks must split evenly over the vector subcores"
    per_w = nch // nw
    mesh = plsc.VectorSubcoreMesh(core_axis_name="c", subcore_axis_name="s")

    @functools.partial(
        pl.kernel, mesh=mesh, out_type=jax.ShapeDtypeStruct((cap, w), rows.dtype),
        scratch_types=[pltpu.VMEM((kk, c), jnp.int32), pltpu.VMEM((c, w), rows.dtype)])
    def scatter_rows(h_hbm, d_hbm, after_hbm, o_hbm, idx_v, rows_v):
        del after_hbm
        wid = lax.axis_index("s") * ncores + lax.axis_index("c")

        @pl.loop(0, per_w)
        def _(j):
            ch = wid * per_w + j
            pltpu.sync_copy(d_hbm.at[ch], idx_v)
            pltpu.sync_copy(h_hbm.at[pl.ds(ch * c, c)], rows_v)
            for q in range(kk):
                pltpu.sync_copy(rows_v, o_hbm.at[idx_v.at[q]])

    return scatter_rows(rows, dest3, after)


def sc_combine_gather(y, dest3, n, after):
    cap, d = y.shape
    nch, kk, c = dest3.shape
    ncores, nw = _sc_workers()
    assert nch % nw == 0, "token chunks must split evenly over the vector subcores"
    per_w = nch // nw
    mesh = plsc.VectorSubcoreMesh(core_axis_name="c", subcore_axis_name="s")

    @functools.partial(
        pl.kernel, mesh=mesh, out_type=jax.ShapeDtypeStruct((kk, n, d), y.dtype),
        scratch_types=[pltpu.VMEM((kk, c), jnp.int32), pltpu.VMEM((c, d), y.dtype)])
    def gather_rows(y_hbm, d_hbm, after_hbm, o_hbm, idx_v, rows_v):
        del after_hbm
        wid = lax.axis_index("s") * ncores + lax.axis_index("c")

        @pl.loop(0, per_w)
        def _(j):
            ch = wid * per_w + j
            pltpu.sync_copy(d_hbm.at[ch], idx_v)
            for q in range(kk):
                pltpu.sync_copy(y_hbm.at[idx_v.at[q]], rows_v)
                pltpu.sync_copy(rows_v, o_hbm.at[q, pl.ds(ch * c, c)])

    return gather_rows(y, dest3, after)


def kernel(x, c, ctx, c_ctx, ada_w, ada_b, w_in, mla_q_norm_g, mla_kv_norm_g, mla_w_uq, mla_w_ukv,
           sg_norm_g, sg_norm_b, sg_w, sg_b, ret_decay_fwd, ret_decay_bwd, w_o, ln1_g, ln1_b,
           router_w, router_b, w_gate_up, b_gate_up, w_down, b_down, ln2_g, ln2_b):
    b, s_len, d = x.shape
    lc = ctx.shape[1]
    assert d == D_MODEL and lc % TM == 0 and s_len % lc == 0 and s_len % GRID_W == 0
    assert b + 1 <= 8
    t = s_len + lc
    n_lat_tiles = s_len // TM
    params = dict(w_in=w_in, mla_q_norm_g=mla_q_norm_g, mla_kv_norm_g=mla_kv_norm_g, mla_w_uq=mla_w_uq,
                  mla_w_ukv=mla_w_ukv, sg_norm_g=sg_norm_g, sg_norm_b=sg_norm_b, sg_w=sg_w, sg_b=sg_b,
                  ret_decay_fwd=ret_decay_fwd, ret_decay_bwd=ret_decay_bwd, w_o=w_o, ln1_g=ln1_g,
                  ln1_b=ln1_b, router_w=router_w, router_b=router_b, w_gate_up=w_gate_up,
                  b_gate_up=b_gate_up, w_down=w_down, b_down=b_down, ln2_g=ln2_g, ln2_b=ln2_b)
    lws = _layer_weights(params)
    tab = _rotation_tables(s_len, lc)

    c_rows = jnp.concatenate([c, c_ctx[None, :], jnp.zeros((8 - b - 1, d), F32)], axis=0)
    mod = ada_modulation(c_rows, ada_w, ada_b).reshape(DEPTH, 8, N_MOD, d)
    mod_tab = jnp.stack([mod[:, :b], jnp.broadcast_to(mod[:, b:b + 1], (DEPTH, b, N_MOD, d))], axis=2)

    n_streams = N_STREAMS if b % N_STREAMS == 0 else 1
    bs = b // n_streams
    xs = [(x, ctx)] * n_streams
    order = c_rows
    held = None
    for li in range(DEPTH):
        lw = {k: v[li] for k, v in lws.items()}
        mts = [mod_tab[li, si * bs:(si + 1) * bs] for si in range(n_streams)]
        fronts = []
        for si in range(n_streams):
            fr = _front(xs[si], si if li == 0 else 0, mts[si], tab, lw, s_len, lc, every_expert=si == 0,
                        after=order)
            order = fr["cnt"]
            if si == 0 and held is not None:
                xs[-1] = _finish(*held, gather_after=fr["mixed"], after=order)
                held = None
            fronts.append(fr)
        for si, fr in enumerate(fronts):
            start = fronts[si + 1]["mixed"] if si + 1 < n_streams else fr["cnt"]
            fr["xg"] = sc_dispatch(fr.pop("h2"), fr["dest3"], fr["cap"], start)
        ys = []
        for si in range(n_streams):
            fr = fronts[si]
            if si == 0:
                y, ew = expert_ffn_first(fr["xg"], fr["block_e"], fr["n_used"], w_gate_up, w_down, li, lw,
                                         after=order)
            else:
                y = expert_ffn(fr["xg"], fr["block_e"], fr["n_used"], ew, lw, after=order)
            order = y
            ys.append(y)
        for si in range(n_streams):
            if si == n_streams - 1 and n_streams > 1 and li + 1 < DEPTH:
                held = (fronts[si], ys[si], mts[si], lw, s_len)
            else:
                xs[si] = _finish(fronts[si], ys[si], mts[si], lw, s_len, ys[si], after=order)
                order = xs[si][0]
    return jnp.concatenate([x_lat for x_lat, _ in xs], axis=0)


def _front(x_pair, x_block, mt, tab, lw, s_len, lc, every_expert, after):
    x_lat, x_ctx = x_pair
    b = mt.shape[0]
    t = s_len + lc
    n_lat_tiles = s_len // TM
    q, k, v, sg, retp, rv, a = input_projection(x_lat, x_ctx, mt, tab, lw, n_lat_tiles, after, x_block)
    mla_lat, mla_ctx = mla_attention(q, k, v, s_len, lc)
    st = retention_scan(a, lw["cd"], s_len // CHUNK)
    x1, h2, idx, gates, rank, cnt = output_projection(
        x_lat, x_ctx, mla_lat, mla_ctx, sg, retp, rv, st, mt, lw, n_lat_tiles, x_block)
    to_tok = lambda z: z.transpose(2, 0, 1, 3).reshape(TOP_K, b * t)
    dest, block_e, n_used, nb = _route(to_tok(idx), to_tok(rank), cnt[:, 0], every_expert)
    assert (b * t) % SC_CHUNK == 0
    dest3 = dest.reshape(TOP_K, (b * t) // SC_CHUNK, SC_CHUNK).transpose(1, 0, 2)
    return dict(x1=x1, gates=gates, cnt=cnt, dest3=dest3, block_e=block_e, n_used=n_used,
                h2=h2.reshape(b * t, h2.shape[-1]), cap=nb * MOE_BM, mixed=a)


def _finish(fr, y, mt, lw, s_len, gather_after, after):
    b, t, d = fr["x1"].shape
    yg = sc_combine_gather(y, fr["dest3"], b * t, gather_after).reshape(TOP_K, b, t, y.shape[-1])
    gates_tok = fr["gates"].transpose(0, 1, 3, 2).reshape(b, t, TOP_K)
    return combine_deepnorm2(fr["x1"], yg, gates_tok, mt, lw, s_len // TM, after)
```
